```python
import math
import jax
import jax.numpy as jnp
from jax import lax
import numpy as np

D_MODEL = 1024
BATCH = 2
SEQ = 8192
DEPTH = 2

GRID_W = 64
CTX_LEN = 256
EPS = 1e-6
N_MOD = 6
F32 = jnp.float32

FNET_GROUPS = 4
FNET_GROUP_DIM = 64
FNET_DIM = FNET_GROUPS * FNET_GROUP_DIM
NA_HEADS = 4
NA_HEAD_DIM = 64
NA_DIM = NA_HEADS * NA_HEAD_DIM
NA_WIN_ROWS = 8
NA_WIN_COLS = 16
CONV_DIM = 256
CONV_WIDTH = 3
DIFF_HEADS = 4
DIFF_QK_DIM = 32
DIFF_V_DIM = 64
DIFF_QK_WIDTH = DIFF_HEADS * 2 * DIFF_QK_DIM
DIFF_V_WIDTH = DIFF_HEADS * DIFF_V_DIM
ROPE_BASE = 10000.0
Q_BLOCK = 128
N_BRANCHES = 4
BRANCH_DIM = 256

OFF_A = 0
OFF_B_Q = OFF_A + FNET_DIM
OFF_B_K = OFF_B_Q + NA_DIM
OFF_B_V = OFF_B_K + NA_DIM
OFF_C_B = OFF_B_V + NA_DIM
OFF_C_C = OFF_C_B + CONV_DIM
OFF_C_X = OFF_C_C + CONV_DIM
OFF_D_Q = OFF_C_X + CONV_DIM
OFF_D_K = OFF_D_Q + DIFF_QK_WIDTH
OFF_D_V = OFF_D_K + DIFF_QK_WIDTH
IN_DIM = OFF_D_V + DIFF_V_WIDTH

N_EXPERTS = 256
TOP_K = 8
N_GROUPS = 8
TOPK_GROUPS = 4
EXPERT_DIM = 256
SHARED_DIM = 256
ROUTED_SCALE = 2.5
MOE_BLOCK = 128

kernel_name = 'hybrid_parallel_mixer_moe_dit'


def rmsnorm(x, g):
    xf = x.astype(F32)
    y = xf * lax.rsqrt(jnp.mean(xf * xf, axis=-1, keepdims=True) + EPS)
    return (y * g.astype(F32)).astype(x.dtype)


def ada_chunks(cvec, w, b):
    m = jax.nn.silu(cvec) @ w + b
    return jnp.split(m, N_MOD, axis=-1)


def modulate(h, shift, scale):
    return h * (1.0 + scale) + shift


def fourier_mix(u):
    b, n, _ = u.shape
    ug = u.reshape(b, n, FNET_GROUPS, FNET_GROUP_DIM).astype(F32)
    f = jnp.fft.fft2(ug, axes=(1, 3), norm='ortho').real
    return f.reshape(b, n, FNET_DIM).astype(u.dtype)


def short_conv_mix(gate_b, gate_c, xs, w_conv):
    u = gate_c * xs
    y = lax.conv_general_dilated(
        u, w_conv[:, None, :].astype(u.dtype), window_strides=(1,),
        padding=[(CONV_WIDTH // 2, CONV_WIDTH // 2)],
        dimension_numbers=('NWC', 'WIO', 'NWC'), feature_group_count=CONV_DIM)
    return gate_b * y


def neighbourhood_attention(q, k, v, k_ctx, v_ctx, rel_bias, rows):
    b, s, h, d = q.shape
    wr = min(NA_WIN_ROWS, rows)
    scale = d ** -0.5
    qg = q.reshape(b, rows, GRID_W, h, d)
    kg = k.reshape(b, rows, GRID_W, h, d)
    vg = v.reshape(b, rows, GRID_W, h, d)
    r = jnp.arange(rows)
    row_idx = jnp.clip(r - wr // 2, 0, rows - wr)[:, None] + jnp.arange(wr)[None, :]
    k_band = kg[:, row_idx]
    v_band = vg[:, row_idx]
    cq = jnp.arange(GRID_W)
    col_lo = jnp.clip(cq - NA_WIN_COLS // 2, 0, GRID_W - NA_WIN_COLS)
    col_ok = (cq[None, :] >= col_lo[:, None]) & (cq[None, :] < col_lo[:, None] + NA_WIN_COLS)
    dr_idx = row_idx - r[:, None] + NA_WIN_ROWS - 1
    dc_idx = jnp.clip(cq[None, :] - cq[:, None] + NA_WIN_COLS - 1, 0, 2 * NA_WIN_COLS - 2)
    bias = rel_bias[:, dr_idx[:, None, :, None], dc_idx[None, :, None, :]]
    s_win = jnp.einsum('brqhd,brwkhd->bhrqwk', qg, k_band).astype(F32) * scale + bias.astype(F32)
    s_win = jnp.where(col_ok[:, None, :], s_win, -jnp.inf)
    s_ctx = jnp.einsum('brqhd,blhd->bhrql', qg, k_ctx).astype(F32) * scale
    n_win = wr * GRID_W
    scores = jnp.concatenate([s_win.reshape(b, h, rows, GRID_W, n_win), s_ctx], axis=-1)
    p = jax.nn.softmax(scores, axis=-1).astype(v.dtype)
    p_win = p[..., :n_win].reshape(b, h, rows, GRID_W, wr, GRID_W)
    p_ctx = p[..., n_win:]
    o = (jnp.einsum('bhrqwk,brwkhd->brqhd', p_win, v_band)
         + jnp.einsum('bhrql,blhd->brqhd', p_ctx, v_ctx))
    return o.reshape(b, s, h * d)


def dense_attention(q, k, v):
    b, n, h, d = q.shape
    s = jnp.einsum('bqhd,bkhd->bhqk', q, k).astype(F32) * d ** -0.5
    p = jax.nn.softmax(s, axis=-1).astype(v.dtype)
    return jnp.einsum('bhqk,bkhd->bqhd', p, v).reshape(b, n, h * d)


def axial_rope_tables(row, col):
    half = DIFF_QK_DIM // 2
    inv = 1.0 / (ROPE_BASE ** (jnp.arange(0, half, 2, dtype=F32) / half))
    ang_r = row.astype(F32)[:, None] * inv
    ang_c = col.astype(F32)[:, None] * inv
    return (jnp.cos(ang_r), jnp.sin(ang_r), jnp.cos(ang_c), jnp.sin(ang_c))


def rotate_section(x, cos, sin):
    x1, x2 = jnp.split(x, 2, axis=-1)
    cos = cos[:, None, None, :].astype(x.dtype)
    sin = sin[:, None, None, :].astype(x.dtype)
    return jnp.concatenate([x1 * cos - x2 * sin, x1 * sin + x2 * cos], axis=-1)


def apply_axial_rope(x, tables):
    cos_r, sin_r, cos_c, sin_c = tables
    half = x.shape[-1] // 2
    return jnp.concatenate([rotate_section(x[..., :half], cos_r, sin_r),
                            rotate_section(x[..., half:], cos_c, sin_c)], axis=-1)


def diff_lambda_value(lp, lam_init):
    lp = lp.astype(F32)
    return jnp.exp(jnp.sum(lp[0] * lp[1])) - jnp.exp(jnp.sum(lp[2] * lp[3])) + lam_init


def diff_attention_dense(q, k, v, lam):
    s = jnp.einsum('bqhmd,bkhmd->bhmqk', q, k).astype(F32) * DIFF_QK_DIM ** -0.5
    p = jax.nn.softmax(s, axis=-1)
    a = (p[:, :, 0] - lam * p[:, :, 1]).astype(v.dtype)
    return jnp.einsum('bhqk,bkhd->bqhd', a, v)


def diff_attention_blocked(q, k, v, lam):
    b, s = q.shape[:2]
    nb = s // Q_BLOCK
    qb = jnp.moveaxis(q.reshape(b, nb, Q_BLOCK, DIFF_HEADS, 2, DIFF_QK_DIM), 1, 0)
    o = lax.map(lambda qi: diff_attention_dense(qi, k, v, lam), qb)
    return jnp.moveaxis(o, 0, 1).reshape(b, s, DIFF_HEADS, DIFF_V_DIM)


def diff_out(o, sub_g, lam_init):
    b, n = o.shape[:2]
    return (rmsnorm(o, sub_g) * (1.0 - lam_init)).reshape(b, n, DIFF_V_WIDTH)


def latent_branches(p, kv_b_ctx, kv_d_ctx, rows, rope, conv_w, rel_bias, lam, sub_g, lam_init):
    b, s, _ = p.shape
    l = kv_b_ctx.shape[1]
    y_a = fourier_mix(p[..., OFF_A:OFF_B_Q])
    q_b = p[..., OFF_B_Q:OFF_B_K].reshape(b, s, NA_HEADS, NA_HEAD_DIM)
    k_b = p[..., OFF_B_K:OFF_B_V].reshape(b, s, NA_HEADS, NA_HEAD_DIM)
    v_b = p[..., OFF_B_V:OFF_C_B].reshape(b, s, NA_HEADS, NA_HEAD_DIM)
    k_bc = kv_b_ctx[..., :NA_DIM].reshape(b, l, NA_HEADS, NA_HEAD_DIM)
    v_bc = kv_b_ctx[..., NA_DIM:].reshape(b, l, NA_HEADS, NA_HEAD_DIM)
    y_b = neighbourhood_attention(q_b, k_b, v_b, k_bc, v_bc, rel_bias, rows)
    y_c = short_conv_mix(p[..., OFF_C_B:OFF_C_C], p[..., OFF_C_C:OFF_C_X], p[..., OFF_C_X:OFF_D_Q], conv_w)
    q_d = apply_axial_rope(p[..., OFF_D_Q:OFF_D_K].reshape(b, s, DIFF_HEADS, 2, DIFF_QK_DIM), rope)
    k_d = apply_axial_rope(p[..., OFF_D_K:OFF_D_V].reshape(b, s, DIFF_HEADS, 2, DIFF_QK_DIM), rope)
    v_d = p[..., OFF_D_V:].reshape(b, s, DIFF_HEADS, DIFF_V_DIM)
    k_dc = kv_d_ctx[..., :DIFF_QK_WIDTH].reshape(b, l, DIFF_HEADS, 2, DIFF_QK_DIM)
    v_dc = kv_d_ctx[..., DIFF_QK_WIDTH:].reshape(b, l, DIFF_HEADS, DIFF_V_DIM)
    o_d = diff_attention_blocked(q_d, jnp.concatenate([k_d, k_dc], axis=1),
                                 jnp.concatenate([v_d, v_dc], axis=1), lam)
    y_d = diff_out(o_d, sub_g, lam_init)
    return [y_a, y_b, y_c, y_d]


def context_branches(pc, conv_w, lam, sub_g, lam_init):
    b, l, _ = pc.shape
    y_a = fourier_mix(pc[..., OFF_A:OFF_B_Q])
    q_b = pc[..., OFF_B_Q:OFF_B_K].reshape(b, l, NA_HEADS, NA_HEAD_DIM)
    k_b = pc[..., OFF_B_K:OFF_B_V].reshape(b, l, NA_HEADS, NA_HEAD_DIM)
    v_b = pc[..., OFF_B_V:OFF_C_B].reshape(b, l, NA_HEADS, NA_HEAD_DIM)
    y_b = dense_attention(q_b, k_b, v_b)
    y_c = short_conv_mix(pc[..., OFF_C_B:OFF_C_C], pc[..., OFF_C_C:OFF_C_X], pc[..., OFF_C_X:OFF_D_Q], conv_w)
    q_d = pc[..., OFF_D_Q:OFF_D_K].reshape(b, l, DIFF_HEADS, 2, DIFF_QK_DIM)
    k_d = pc[..., OFF_D_K:OFF_D_V].reshape(b, l, DIFF_HEADS, 2, DIFF_QK_DIM)
    v_d = pc[..., OFF_D_V:].reshape(b, l, DIFF_HEADS, DIFF_V_DIM)
    y_d = diff_out(diff_attention_dense(q_d, k_d, v_d, lam), sub_g, lam_init)
    return [y_a, y_b, y_c, y_d]


def merge_branches(h, branches, w_gate, w_branch, w_o):
    merged = jax.nn.sigmoid(h @ w_gate[0]) * (branches[0] @ w_branch[0])
    for i in range(1, N_BRANCHES):
        merged = merged + jax.nn.sigmoid(h @ w_gate[i]) * (branches[i] @ w_branch[i])
    return merged @ w_o


def route(h, w_router, bias):
    n = h.shape[0]
    scores = jax.nn.sigmoid((h @ w_router).astype(F32))
    grp = (scores + bias.astype(F32)).reshape(n, N_GROUPS, N_EXPERTS // N_GROUPS)
    grp_score = jnp.sum(lax.top_k(grp, 2)[0], axis=-1)
    _, grp_idx = lax.top_k(grp_score, TOPK_GROUPS)
    grp_mask = jnp.any(grp_idx[..., None] == jnp.arange(N_GROUPS), axis=-2)
    choice = jnp.where(grp_mask[..., None], grp, -jnp.inf).reshape(n, N_EXPERTS)
    _, idx = lax.top_k(choice, TOP_K)
    w = jnp.take_along_axis(scores, idx, axis=-1)
    w = w / jnp.sum(w, axis=-1, keepdims=True) * ROUTED_SCALE
    return idx, w


def routed_experts(h, idx, wts, w_g, w_u, w_d):
    n, d = h.shape
    a = n * TOP_K
    flat_e = idx.reshape(a)
    order = jnp.argsort(flat_e)
    sorted_e = flat_e[order]
    counts = jnp.bincount(flat_e, length=N_EXPERTS)
    padded = (counts + MOE_BLOCK - 1) // MOE_BLOCK * MOE_BLOCK
    pad_end = jnp.cumsum(padded)
    pad_start = pad_end - padded
    grp_start = jnp.cumsum(counts) - counts
    dest = pad_start[sorted_e] + jnp.arange(a) - grp_start[sorted_e]
    n_blocks = (a + N_EXPERTS * (MOE_BLOCK - 1)) // MOE_BLOCK
    n_rows = n_blocks * MOE_BLOCK
    row_tok = jnp.full((n_rows,), n, jnp.int32).at[dest].set((order // TOP_K).astype(jnp.int32))
    row_w = jnp.zeros((n_rows,), F32).at[dest].set(wts.reshape(a)[order])
    block_e = jnp.minimum(jnp.searchsorted(pad_end, jnp.arange(n_blocks) * MOE_BLOCK, side='right'),
                          N_EXPERTS - 1)
    h_pad = jnp.concatenate([h, jnp.zeros((1, d), h.dtype)], axis=0)

    def block(acc, args):
        toks, e, w = args
        xb = h_pad[toks]
        yb = (jax.nn.silu(xb @ w_g[e]) * (xb @ w_u[e])) @ w_d[e]
        return acc.at[toks].add(yb * w[:, None].astype(yb.dtype)), None

    acc, _ = lax.scan(block, jnp.zeros((n + 1, d), h.dtype),
                      (row_tok.reshape(n_blocks, MOE_BLOCK), block_e, row_w.reshape(n_blocks, MOE_BLOCK)))
    return acc[:n]


def moe_ffn(h, w_router, router_bias, w_g, w_u, w_d, sw_g, sw_u, sw_d):
    idx, wts = route(h, w_router, router_bias)
    shared = (jax.nn.silu(h @ sw_g) * (h @ sw_u)) @ sw_d
    return shared + routed_experts(h, idx, wts, w_g, w_u, w_d)


def setup_inputs(seed: int = 0) -> dict:
    key = jax.random.key(seed)
    ks = iter(jax.random.split(key, 32))
    D = D_MODEL

    def nrm(shape, scale):
        return jax.random.normal(next(ks), shape, F32) * scale

    return {
        'x': nrm((BATCH, SEQ, D), 1.0),
        'c': nrm((BATCH, D), 1.0),
        'ctx': nrm((BATCH, CTX_LEN, D), 1.0),
        'c_ctx': nrm((D,), 1.0),
        'ada_w': nrm((DEPTH, D, N_MOD * D), 0.3 * D ** -0.5),
        'ada_b': nrm((DEPTH, N_MOD * D), 0.02),
        'norm1_g': 1.0 + nrm((DEPTH, D), 0.02),
        'w_in': nrm((DEPTH, D, IN_DIM), D ** -0.5),
        'conv_w': nrm((DEPTH, CONV_WIDTH, CONV_DIM), CONV_WIDTH ** -0.5),
        'na_rel_bias': nrm((DEPTH, NA_HEADS, 2 * NA_WIN_ROWS - 1, 2 * NA_WIN_COLS - 1), 0.1),
        'diff_lambda': nrm((DEPTH, 4, DIFF_QK_DIM), 0.1),
        'diff_subln_g': 1.0 + nrm((DEPTH, DIFF_V_DIM), 0.02),
        'w_branch_gate': nrm((DEPTH, N_BRANCHES, D, D), D ** -0.5),
        'w_branch': nrm((DEPTH, N_BRANCHES, BRANCH_DIM, D), BRANCH_DIM ** -0.5),
        'w_out': nrm((DEPTH, D, D), D ** -0.5),
        'norm2_g': 1.0 + nrm((DEPTH, D), 0.02),
        'router_w': nrm((DEPTH, D, N_EXPERTS), D ** -0.5),
        'router_bias': nrm((DEPTH, N_EXPERTS), 0.01),
        'expert_w_gate': nrm((DEPTH, N_EXPERTS, D, EXPERT_DIM), D ** -0.5),
        'expert_w_up': nrm((DEPTH, N_EXPERTS, D, EXPERT_DIM), D ** -0.5),
        'expert_w_down': nrm((DEPTH, N_EXPERTS, EXPERT_DIM, D), EXPERT_DIM ** -0.5),
        'shared_w_gate': nrm((DEPTH, D, SHARED_DIM), D ** -0.5),
        'shared_w_up': nrm((DEPTH, D, SHARED_DIM), D ** -0.5),
        'shared_w_down': nrm((DEPTH, SHARED_DIM, D), SHARED_DIM ** -0.5),
        'final_norm_g': 1.0 + nrm((D,), 0.02),
    }


def reference(x, c, ctx, c_ctx, ada_w, ada_b, norm1_g, w_in, conv_w, na_rel_bias, diff_lambda,
              diff_subln_g, w_branch_gate, w_branch, w_out, norm2_g, router_w, router_bias,
              expert_w_gate, expert_w_up, expert_w_down, shared_w_gate, shared_w_up, shared_w_down,
              final_norm_g):
    b, s, d = x.shape
    l_ctx = ctx.shape[1]
    rows = s // GRID_W
    t = jnp.arange(s)
    rope = axial_rope_tables(t // GRID_W, t % GRID_W)
    xc = ctx
    for layer in range(DEPTH):
        last = layer == DEPTH - 1
        lam_init = 0.8 - 0.6 * math.exp(-0.3 * layer)
        lam = diff_lambda_value(diff_lambda[layer], lam_init)
        m_lat = [u[:, None, :] for u in ada_chunks(c, ada_w[layer], ada_b[layer])]
        m_ctx = ada_chunks(c_ctx, ada_w[layer], ada_b[layer])
        w_in_l = w_in[layer]

        h = modulate(rmsnorm(x, norm1_g[layer]), m_lat[0], m_lat[1])
        hc = modulate(rmsnorm(xc, norm1_g[layer]), m_ctx[0], m_ctx[1])
        p = h @ w_in_l
        if last:
            kv_b_ctx = hc @ w_in_l[:, OFF_B_K:OFF_C_B]
            kv_d_ctx = hc @ w_in_l[:, OFF_D_K:]
        else:
            pc = hc @ w_in_l
            kv_b_ctx = pc[..., OFF_B_K:OFF_C_B]
            kv_d_ctx = pc[..., OFF_D_K:]
        br = latent_branches(p, kv_b_ctx, kv_d_ctx, rows, rope, conv_w[layer], na_rel_bias[layer],
                             lam, diff_subln_g[layer], lam_init)
        x_new = x + m_lat[2] * merge_branches(h, br, w_branch_gate[layer], w_branch[layer], w_out[layer])
        if not last:
            brc = context_branches(pc, conv_w[layer], lam, diff_subln_g[layer], lam_init)
            xc = xc + m_ctx[2] * merge_branches(hc, brc, w_branch_gate[layer], w_branch[layer], w_out[layer])
        x = x_new

        h2 = modulate(rmsnorm(x, norm2_g[layer]), m_lat[3], m_lat[4]).reshape(b * s, d)
        if last:
            tokens = h2
        else:
            hc2 = modulate(rmsnorm(xc, norm2_g[layer]), m_ctx[3], m_ctx[4]).reshape(b * l_ctx, d)
            tokens = jnp.concatenate([h2, hc2], axis=0)
        y = moe_ffn(tokens, router_w[layer], router_bias[layer], expert_w_gate[layer], expert_w_up[layer],
                    expert_w_down[layer], shared_w_gate[layer], shared_w_up[layer], shared_w_down[layer])
        x = x + m_lat[5] * y[:b * s].reshape(b, s, d)
        if not last:
            xc = xc + m_ctx[5] * y[b * s:].reshape(b, l_ctx, d)
    return rmsnorm(x, final_norm_g)
```

```python
import functools
import math

import numpy as np
import jax
import jax.numpy as jnp
from jax import lax
from jax.experimental import pallas as pl
from jax.experimental.pallas import tpu as pltpu

F32 = jnp.float32
BF16 = jnp.bfloat16

DEPTH = 2
GRID_W = 64
EPS = 1e-6
N_MOD = 6

FNET_GROUP_DIM = 64
NA_HEADS = 4
NA_HEAD_DIM = 64
NA_WIN_ROWS = 8
NA_WIN_COLS = 16
CONV_WIDTH = 3
DIFF_HEADS = 4
DIFF_QK_DIM = 32
DIFF_V_DIM = 64
ROPE_BASE = 10000.0
N_BRANCHES = 4
BRANCH_DIM = 256

COL_A, COL_BQ, COL_BK, COL_BV, COL_CB, COL_CC, COL_CX, COL_DQ, COL_DK, COL_DV = range(10)
N_COL_BLOCKS = 10
CB = 256
IN_DIM = N_COL_BLOCKS * CB

N_EXPERTS = 256
TOP_K = 8
N_GROUPS = 8
TOPK_GROUPS = 4
ROUTED_SCALE = 2.5

VMEM_LIMIT_BYTES = 56 * 1024 * 1024
BF16_SUBLANES = 16
FFT_N1 = 64
NA_QROWS = 8
NA_KROWS = 16
MOE_ROWS = 256


def _params(*sem):
    return pltpu.CompilerParams(dimension_semantics=sem, vmem_limit_bytes=VMEM_LIMIT_BYTES)


def _dot(a, b):
    return jnp.dot(a, b, preferred_element_type=F32)


def _dot_nt(a, b):
    return lax.dot_general(a, b, (((1,), (1,)), ((), ())), preferred_element_type=F32)


def _norm_mod(xf, g, shift, scale):
    y = xf * lax.rsqrt(jnp.mean(xf * xf, axis=-1, keepdims=True) + EPS)
    return (y * g) * (1.0 + scale) + shift


def _silu(v):
    return v * jax.nn.sigmoid(v)


def _ada_kernel(c_ref, w_ref, b_ref, o_ref):
    s = _silu(c_ref[...])
    o_ref[...] = _dot(s.astype(BF16), w_ref[...].astype(BF16)) + b_ref[...]


def _ada(cvec, w, b):
    rows, d = cvec.shape
    n = w.shape[1]
    tn = 1536
    return pl.pallas_call(
        _ada_kernel,
        grid=(n // tn,),
        in_specs=[pl.BlockSpec((rows, d), lambda j: (0, 0)),
                  pl.BlockSpec((d, tn), lambda j: (0, j)),
                  pl.BlockSpec((1, tn), lambda j: (0, j))],
        out_specs=pl.BlockSpec((rows, tn), lambda j: (0, j)),
        out_shape=jax.ShapeDtypeStruct((rows, n), F32),
        compiler_params=_params("arbitrary"),
    )(cvec, w, b.reshape(1, n))


def _inproj_kernel(x_ref, g_ref, mod_ref, w_ref, wf_ref, cos_ref, s1_ref, s2_ref,
                   p_ref, ur_ref, ui_ref, *, rope):
    mod = mod_ref[0]
    h = _norm_mod(x_ref[0], g_ref[...], mod[0:1], mod[1:2]).astype(BF16)
    for j in range(N_COL_BLOCKS):
        pj = _dot(h, w_ref[:, j * CB:(j + 1) * CB])
        if j == COL_A:
            u = _dot(pj.astype(BF16), wf_ref[...])
            ur_ref[0] = u[:, :CB].astype(BF16)
            ui_ref[0] = u[:, CB:].astype(BF16)
        if rope and j in (COL_DQ, COL_DK):
            cos = jnp.concatenate([cos_ref[...]] * 2, axis=1)
            s1 = jnp.concatenate([s1_ref[...]] * 2, axis=1)
            s2 = jnp.concatenate([s2_ref[...]] * 2, axis=1)
            pj = pj * cos + pltpu.roll(pj, CB - 8, 1) * s1 + pltpu.roll(pj, 8, 1) * s2
            if j == COL_DQ:
                pj = pj * (DIFF_QK_DIM ** -0.5)
        p_ref[0, :, j * CB:(j + 1) * CB] = pj.astype(BF16)


def _inproj(x, g, mods, w_bf, wf, rope_tabs, mod_row, rope, tm):
    b, s, d = x.shape
    mod_idx = (lambda bi, i: (bi, 0, 0)) if mod_row is None else (lambda bi, i: (mod_row, 0, 0))
    tab_spec = pl.BlockSpec((tm, 128), lambda bi, i: (i, 0))
    seq_spec = lambda width: pl.BlockSpec((1, tm, width), lambda bi, i: (bi, i, 0))
    return pl.pallas_call(
        functools.partial(_inproj_kernel, rope=rope),
        grid=(b, s // tm),
        in_specs=[seq_spec(d),
                  pl.BlockSpec((1, d), lambda bi, i: (0, 0)),
                  pl.BlockSpec((1, N_MOD, d), mod_idx),
                  pl.BlockSpec((d, IN_DIM), lambda bi, i: (0, 0)),
                  pl.BlockSpec((CB, 2 * CB), lambda bi, i: (0, 0)),
                  tab_spec, tab_spec, tab_spec],
        out_specs=[seq_spec(IN_DIM), seq_spec(CB), seq_spec(CB)],
        out_shape=[jax.ShapeDtypeStruct((b, s, IN_DIM), BF16),
                   jax.ShapeDtypeStruct((b, s, CB), BF16),
                   jax.ShapeDtypeStruct((b, s, CB), BF16)],
        compiler_params=_params("parallel", "arbitrary"),
    )(x, g.reshape(1, d), mods, w_bf, wf, *rope_tabs)


def _channel_dft_matrix():
    c = np.arange(FNET_GROUP_DIM)
    ang = 2.0 * np.pi * ((c[:, None] * c[None, :]) % FNET_GROUP_DIM) / FNET_GROUP_DIM
    eye = np.eye(CB // FNET_GROUP_DIM)
    m = np.concatenate([np.kron(eye, np.cos(ang)), -np.kron(eye, np.sin(ang))], axis=1)
    return jnp.asarray(m, BF16)


def _rope_tables(s):
    half = DIFF_QK_DIM // 2
    t = jnp.arange(s)
    inv = 1.0 / (ROPE_BASE ** (jnp.arange(0, half, 2, dtype=F32) / half))
    ang_r = (t // GRID_W).astype(F32)[:, None] * inv
    ang_c = (t % GRID_W).astype(F32)[:, None] * inv
    zero = jnp.zeros_like(ang_r)
    cos = jnp.concatenate([jnp.cos(ang_r)] * 2 + [jnp.cos(ang_c)] * 2, axis=1)
    s1 = jnp.concatenate([-jnp.sin(ang_r), zero, -jnp.sin(ang_c), zero], axis=1)
    s2 = jnp.concatenate([zero, jnp.sin(ang_r), zero, jnp.sin(ang_c)], axis=1)
    return tuple(jnp.concatenate([a] * 4, axis=1) for a in (cos, s1, s2))


def _fft1_kernel(ur_ref, ui_ref, w_ref, ct_ref, st_ref, ar_ref, ai_ref):
    n1 = ur_ref.shape[1]
    u = jnp.concatenate([ur_ref[0], ui_ref[0]], axis=0)
    a = _dot(w_ref[...], u)
    ar, ai = a[:n1], a[n1:]
    ct, st = ct_ref[...], st_ref[...]
    ar_ref[0] = (ar * ct + ai * st).astype(BF16)
    ai_ref[0] = (ai * ct - ar * st).astype(BF16)


def _fft2_kernel(ar_ref, ai_ref, w_ref, y_ref, *, norm):
    for j in range(ar_ref.shape[1]):
        a = jnp.concatenate([ar_ref[0, j], ai_ref[0, j]], axis=0)
        y_ref[0, j] = (_dot(w_ref[...], a) * norm).astype(BF16)


def _dft_cos_sin(n):
    k = np.arange(n)
    ang = 2.0 * np.pi * ((k[:, None] * k[None, :]) % n) / n
    return np.cos(ang), np.sin(ang)


def _fourier_latent(ur, ui):
    b, s, cb = ur.shape
    n1, n2 = FFT_N1, s // FFT_N1
    c1, s1 = _dft_cos_sin(n1)
    w1 = jnp.asarray(np.block([[c1, s1], [-s1, c1]]), BF16)
    c2, s2 = _dft_cos_sin(n2)
    w2 = jnp.asarray(np.concatenate([c2, s2], axis=1), BF16)
    tw = 2.0 * np.pi * (np.arange(n1)[:, None] * np.arange(n2)[None, :]) / s
    ct = jnp.broadcast_to(jnp.asarray(np.cos(tw), F32)[:, :, None], (n1, n2, cb)).reshape(n1, n2 * cb)
    st = jnp.broadcast_to(jnp.asarray(np.sin(tw), F32)[:, :, None], (n1, n2, cb)).reshape(n1, n2 * cb)
    lanes = n2 * cb
    tn = min(lanes, 4096)
    u_spec = pl.BlockSpec((1, n1, tn), lambda j, bi: (bi, 0, j))
    t_spec = pl.BlockSpec((n1, tn), lambda j, bi: (0, j))
    ar, ai = pl.pallas_call(
        _fft1_kernel,
        grid=(lanes // tn, b),
        in_specs=[u_spec, u_spec, pl.BlockSpec((2 * n1, 2 * n1), lambda j, bi: (0, 0)), t_spec, t_spec],
        out_specs=[u_spec, u_spec],
        out_shape=[jax.ShapeDtypeStruct((b, n1, lanes), BF16)] * 2,
        compiler_params=_params("arbitrary", "arbitrary"),
    )(ur.reshape(b, n1, lanes), ui.reshape(b, n1, lanes), w1, ct, st)
    kc = 8
    a_spec = pl.BlockSpec((1, kc, n2, cb), lambda bi, j: (bi, j, 0, 0))
    y = pl.pallas_call(
        functools.partial(_fft2_kernel, norm=1.0 / math.sqrt(s * FNET_GROUP_DIM)),
        grid=(b, n1 // kc),
        in_specs=[a_spec, a_spec, pl.BlockSpec((n2, 2 * n2), lambda bi, j: (0, 0))],
        out_specs=a_spec,
        out_shape=jax.ShapeDtypeStruct((b, n1, n2, cb), BF16),
        compiler_params=_params("parallel", "arbitrary"),
    )(ar.reshape(b, n1, n2, cb), ai.reshape(b, n1, n2, cb), w2)
    return jnp.transpose(y, (0, 2, 1, 3)).reshape(b, s, cb)


def _na_kernel(q_ref, k_ref, v_ref, kc_ref, vc_ref, bias_ref, o_ref, *, rows):
    rb = pl.program_id(1)
    kb = jnp.clip(rb * NA_QROWS - NA_WIN_ROWS // 2, 0, rows - NA_KROWS)
    nk = NA_KROWS * GRID_W
    tok0 = pl.multiple_of(kb * GRID_W, 256)
    scale = NA_HEAD_DIM ** -0.5
    outs = []
    for h in range(NA_HEADS):
        sl = slice(h * NA_HEAD_DIM, (h + 1) * NA_HEAD_DIM)
        q = q_ref[0, :, sl]
        s = _dot_nt(q, k_ref[0, pl.ds(tok0, nk), sl]) * scale + bias_ref[0, h]
        sc = _dot_nt(q, kc_ref[0, :, sl]) * scale
        m = jnp.maximum(jnp.max(s, axis=-1, keepdims=True), jnp.max(sc, axis=-1, keepdims=True))
        e = jnp.exp(s - m)
        ec = jnp.exp(sc - m)
        l = jnp.sum(e, axis=-1, keepdims=True) + jnp.sum(ec, axis=-1, keepdims=True)
        o = _dot(e.astype(BF16), v_ref[0, pl.ds(tok0, nk), sl]) + _dot(ec.astype(BF16), vc_ref[0, :, sl])
        outs.append(o / l)
    o_ref[0] = jnp.concatenate(outs, axis=1).astype(BF16)


def _na_bias_tables(rel_bias, rows):
    tabs = []
    cq = np.arange(GRID_W)
    col_lo = np.clip(cq - NA_WIN_COLS // 2, 0, GRID_W - NA_WIN_COLS)
    col_ok = (cq[None, :] >= col_lo[:, None]) & (cq[None, :] < col_lo[:, None] + NA_WIN_COLS)
    dc_idx = np.clip(cq[None, :] - cq[:, None] + NA_WIN_COLS - 1, 0, 2 * NA_WIN_COLS - 2)
    for r0 in (0, NA_QROWS, rows - NA_QROWS):
        kb = int(np.clip(r0 - NA_WIN_ROWS // 2, 0, rows - NA_KROWS))
        r = r0 + np.arange(NA_QROWS)
        rk = kb + np.arange(NA_KROWS)
        start = np.clip(r - NA_WIN_ROWS // 2, 0, rows - NA_WIN_ROWS)
        row_ok = (rk[None, :] >= start[:, None]) & (rk[None, :] < start[:, None] + NA_WIN_ROWS)
        dr_idx = np.clip(rk[None, :] - r[:, None] + NA_WIN_ROWS - 1, 0, 2 * NA_WIN_ROWS - 2)
        ok = row_ok[:, None, :, None] & col_ok[None, :, None, :]
        vals = rel_bias[:, dr_idx[:, None, :, None], dc_idx[None, :, None, :]].astype(F32)
        tab = jnp.where(ok[None], vals, -jnp.inf)
        tabs.append(tab.reshape(NA_HEADS, NA_QROWS * GRID_W, NA_KROWS * GRID_W))
    return jnp.stack(tabs)


def _na_latent(p, pc, bias_tabs):
    b, s, _ = p.shape
    l = pc.shape[1]
    rows = s // GRID_W
    nrb = rows // NA_QROWS
    tq = NA_QROWS * GRID_W
    nk = NA_KROWS * GRID_W
    return pl.pallas_call(
        functools.partial(_na_kernel, rows=rows),
        grid=(b, nrb),
        in_specs=[pl.BlockSpec((1, tq, CB), lambda bi, i: (bi, i, COL_BQ)),
                  pl.BlockSpec((1, s, CB), lambda bi, i: (bi, 0, COL_BK)),
                  pl.BlockSpec((1, s, CB), lambda bi, i: (bi, 0, COL_BV)),
                  pl.BlockSpec((1, l, CB), lambda bi, i: (bi, 0, COL_BK)),
                  pl.BlockSpec((1, l, CB), lambda bi, i: (bi, 0, COL_BV)),
                  pl.BlockSpec((1, NA_HEADS, tq, nk),
                               lambda bi, i: (jnp.minimum(i, 1) + (i == nrb - 1).astype(jnp.int32), 0, 0, 0))],
        out_specs=pl.BlockSpec((1, tq, CB), lambda bi, i: (bi, i, 0)),
        out_shape=jax.ShapeDtypeStruct((b, s, CB), BF16),
        compiler_params=_params("parallel", "arbitrary"),
    )(p, p, p, pc, pc, bias_tabs)


def _diff_lambda(lp, lam_init):
    return (jnp.exp(jnp.sum(lp[0:1] * lp[1:2], axis=-1, keepdims=True))
            - jnp.exp(jnp.sum(lp[2:3] * lp[3:4], axis=-1, keepdims=True)) + lam_init)


def _diff_finish(o, g, lam_init):
    y = o * lax.rsqrt(jnp.mean(o * o, axis=-1, keepdims=True) + EPS)
    return y * g * (1.0 - lam_init)


def _diff_kernel(q_ref, kt_ref, v_ref, lp_ref, g_ref, o_ref, m_sc, l_sc, acc_sc, *, tk, lam_init):
    tq = q_ref.shape[1]
    nk = kt_ref.shape[2] // tk
    n_maps = 2 * DIFF_HEADS
    m_sc[...] = jnp.full(m_sc.shape, -jnp.inf, F32)
    l_sc[...] = jnp.zeros(l_sc.shape, F32)
    acc_sc[...] = jnp.zeros(acc_sc.shape, F32)

    def body(c, carry):
        k0 = pl.multiple_of(c * tk, 128)
        for hm in range(n_maps):
            q = q_ref[0, :, hm * DIFF_QK_DIM:(hm + 1) * DIFF_QK_DIM]
            s = _dot(q, kt_ref[0, hm * DIFF_QK_DIM:(hm + 1) * DIFF_QK_DIM, pl.ds(k0, tk)])
            m_old = m_sc[hm]
            m_new = jnp.maximum(m_old, jnp.max(s, axis=-1, keepdims=True))
            alpha = jnp.exp(m_old - m_new)
            e = jnp.exp(s - m_new[:, :1])
            l_sc[hm] = alpha * l_sc[hm] + jnp.sum(e, axis=-1, keepdims=True)
            pv = _dot(e.astype(BF16), v_ref[0, hm // 2, pl.ds(k0, tk), :])
            acc_sc[hm] = alpha[:, :DIFF_V_DIM] * acc_sc[hm] + pv
            m_sc[hm] = m_new
        return carry

    lax.fori_loop(0, nk, body, 0)
    lam = _diff_lambda(lp_ref[...], lam_init)
    outs = []
    for h in range(DIFF_HEADS):
        o0 = acc_sc[2 * h] / l_sc[2 * h][:, :1]
        o1 = acc_sc[2 * h + 1] / l_sc[2 * h + 1][:, :1]
        outs.append(_diff_finish(o0 - lam * o1, g_ref[...], lam_init))
    o_ref[0] = jnp.concatenate(outs, axis=1).astype(BF16)


def _key_tile(nkeys, cap):
    return max(t for t in range(128, cap + 1, 128) if nkeys % t == 0)


def _diff_latent(p, pc, lp, sub_g, lam_init, tq, tk):
    b, s, _ = p.shape
    l = pc.shape[1]
    nkeys = s + l
    k_all = jnp.concatenate([p[:, :, COL_DK * CB:(COL_DK + 1) * CB], pc[:, :, COL_DK * CB:(COL_DK + 1) * CB]], axis=1)
    v_all = jnp.concatenate([p[:, :, COL_DV * CB:], pc[:, :, COL_DV * CB:]], axis=1)
    kt = jnp.transpose(k_all, (0, 2, 1))
    vh = jnp.transpose(v_all.reshape(b, nkeys, DIFF_HEADS, DIFF_V_DIM), (0, 2, 1, 3))
    n_maps = 2 * DIFF_HEADS
    return pl.pallas_call(
        functools.partial(_diff_kernel, tk=tk, lam_init=lam_init),
        grid=(b, s // tq),
        in_specs=[pl.BlockSpec((1, tq, CB), lambda bi, i: (bi, i, COL_DQ)),
                  pl.BlockSpec((1, CB, nkeys), lambda bi, i: (bi, 0, 0)),
                  pl.BlockSpec((1, DIFF_HEADS, nkeys, DIFF_V_DIM), lambda bi, i: (bi, 0, 0, 0)),
                  pl.BlockSpec((4, DIFF_QK_DIM), lambda bi, i: (0, 0)),
                  pl.BlockSpec((1, DIFF_V_DIM), lambda bi, i: (0, 0))],
        out_specs=pl.BlockSpec((1, tq, CB), lambda bi, i: (bi, i, 0)),
        out_shape=jax.ShapeDtypeStruct((b, s, CB), BF16),
        scratch_shapes=[pltpu.VMEM((n_maps, tq, 128), F32),
                        pltpu.VMEM((n_maps, tq, 128), F32),
                        pltpu.VMEM((n_maps, tq, DIFF_V_DIM), F32)],
        compiler_params=_params("parallel", "arbitrary"),
    )(p, kt, vh, lp, sub_g.reshape(1, DIFF_V_DIM))


def _softmax_rows(s):
    e = jnp.exp(s - jnp.max(s, axis=-1, keepdims=True))
    return e / jnp.sum(e, axis=-1, keepdims=True)


def _ctx_kernel(pc_ref, ur_ref, ui_ref, wf_ref, lp_ref, g_ref, ya_ref, yb_ref, yd_ref, *, lam_init):
    l = pc_ref.shape[1]
    col = lambda j, lo, hi: pc_ref[0, :, j * CB + lo:j * CB + hi]
    u = jnp.concatenate([ur_ref[0], ui_ref[0]], axis=0)
    ya_ref[0] = (_dot(wf_ref[...], u) * (1.0 / math.sqrt(l * FNET_GROUP_DIM))).astype(BF16)
    outs = []
    for h in range(NA_HEADS):
        lo, hi = h * NA_HEAD_DIM, (h + 1) * NA_HEAD_DIM
        pr = _softmax_rows(_dot_nt(col(COL_BQ, lo, hi), col(COL_BK, lo, hi)) * NA_HEAD_DIM ** -0.5)
        outs.append(_dot(pr.astype(BF16), col(COL_BV, lo, hi)))
    yb_ref[0] = jnp.concatenate(outs, axis=1).astype(BF16)
    lam = _diff_lambda(lp_ref[...], lam_init)
    outs = []
    for h in range(DIFF_HEADS):
        pm = []
        for m in range(2):
            lo = (2 * h + m) * DIFF_QK_DIM
            pm.append(_softmax_rows(_dot_nt(col(COL_DQ, lo, lo + DIFF_QK_DIM), col(COL_DK, lo, lo + DIFF_QK_DIM))
                                    * DIFF_QK_DIM ** -0.5))
        a = (pm[0] - lam * pm[1]).astype(BF16)
        o = _dot(a, col(COL_DV, h * DIFF_V_DIM, (h + 1) * DIFF_V_DIM))
        outs.append(_diff_finish(o, g_ref[...], lam_init))
    yd_ref[0] = jnp.concatenate(outs, axis=1).astype(BF16)


def _ctx_branches(pc, ucr, uci, lp, sub_g, lam_init):
    b, l, _ = pc.shape
    c, s = _dft_cos_sin(l)
    wf = jnp.asarray(np.concatenate([c, s], axis=1), BF16)
    y_spec = pl.BlockSpec((1, l, CB), lambda bi: (bi, 0, 0))
    return pl.pallas_call(
        functools.partial(_ctx_kernel, lam_init=lam_init),
        grid=(b,),
        in_specs=[pl.BlockSpec((1, l, IN_DIM), lambda bi: (bi, 0, 0)), y_spec, y_spec,
                  pl.BlockSpec((l, 2 * l), lambda bi: (0, 0)),
                  pl.BlockSpec((4, DIFF_QK_DIM), lambda bi: (0, 0)),
                  pl.BlockSpec((1, DIFF_V_DIM), lambda bi: (0, 0))],
        out_specs=[y_spec] * 3,
        out_shape=[jax.ShapeDtypeStruct((b, l, CB), BF16)] * 3,
        compiler_params=_params("parallel"),
    )(pc, ucr, uci, wf, lp, sub_g.reshape(1, DIFF_V_DIM))


def _merge_kernel(x_ref, mod_ref, g1_ref, g2_ref, ya_ref, yb_ref, yd_ref, pb_ref, pc_ref, px_ref,
                  cp_ref, xp_ref, cn_ref, xn_ref, cw_ref, wg_ref, wb_ref, wo_ref, wr_ref,
                  xo_ref, h2_ref, sc_ref):
    i = pl.program_id(1)
    last = pl.num_programs(1) - 1
    tm = x_ref.shape[1]
    mod = mod_ref[0]
    x = x_ref[0]
    h = _norm_mod(x, g1_ref[...], mod[0:1], mod[1:2]).astype(BF16)

    u = pc_ref[0].astype(F32) * px_ref[0].astype(F32)
    up = cp_ref[0, BF16_SUBLANES - 1:, :].astype(F32) * xp_ref[0, BF16_SUBLANES - 1:, :].astype(F32)
    un = cn_ref[0, :1, :].astype(F32) * xn_ref[0, :1, :].astype(F32)
    up = jnp.where(i == 0, 0.0, up)
    un = jnp.where(i == last, 0.0, un)
    rid = lax.broadcasted_iota(jnp.int32, u.shape, 0)
    u_prev = jnp.where(rid == 0, up, pltpu.roll(u, 1, 0))
    u_next = jnp.where(rid == tm - 1, un, pltpu.roll(u, tm - 1, 0))
    cw = cw_ref[...]
    yc = pb_ref[0].astype(F32) * (cw[0:1] * u_prev + cw[1:2] * u + cw[2:3] * u_next)

    branches = (ya_ref[0], yb_ref[0], yc.astype(BF16), yd_ref[0])
    d = x.shape[1]
    out = None
    for n in range(d // CB):
        cols = slice(n * CB, (n + 1) * CB)
        merged = None
        for j in range(N_BRANCHES):
            t = jax.nn.sigmoid(_dot(h, wg_ref[j, :, cols])) * _dot(branches[j], wb_ref[j, :, cols])
            merged = t if merged is None else merged + t
        t = _dot(merged.astype(BF16), wo_ref[cols, :])
        out = t if out is None else out + t
    xn = x + mod[2:3] * out
    xo_ref[0] = xn
    h2 = _norm_mod(xn, g2_ref[...], mod[3:4], mod[4:5]).astype(BF16)
    h2_ref[0] = h2
    sc_ref[0] = jax.nn.sigmoid(_dot(h2, wr_ref[...]))


def _merge(x, mods, mod_row, g1, g2, ya, yb, yd, p, conv_w, wg, wb, wo, wr, tm):
    b, s, d = x.shape
    ne = wr.shape[1]
    hb = tm // BF16_SUBLANES
    n_halo = s // BF16_SUBLANES
    mod_idx = (lambda bi, i: (bi, 0, 0)) if mod_row is None else (lambda bi, i: (mod_row, 0, 0))
    seq = lambda width, col=0: pl.BlockSpec((1, tm, width), lambda bi, i: (bi, i, col))
    prev = lambda col: pl.BlockSpec((1, BF16_SUBLANES, CB), lambda bi, i: (bi, jnp.maximum(i * hb - 1, 0), col))
    nxt = lambda col: pl.BlockSpec((1, BF16_SUBLANES, CB),
                                   lambda bi, i: (bi, jnp.minimum((i + 1) * hb, n_halo - 1), col))
    const = lambda shape: pl.BlockSpec(shape, lambda bi, i: (0,) * len(shape))
    return pl.pallas_call(
        _merge_kernel,
        grid=(b, s // tm),
        in_specs=[seq(d), pl.BlockSpec((1, N_MOD, d), mod_idx), const((1, d)), const((1, d)),
                  seq(CB), seq(CB), seq(CB),
                  seq(CB, COL_CB), seq(CB, COL_CC), seq(CB, COL_CX),
                  prev(COL_CC), prev(COL_CX), nxt(COL_CC), nxt(COL_CX),
                  const((CONV_WIDTH, CB)),
                  const((N_BRANCHES, d, d)), const((N_BRANCHES, BRANCH_DIM, d)), const((d, d)),
                  const((d, ne))],
        out_specs=[seq(d), seq(d), seq(ne)],
        out_shape=[jax.ShapeDtypeStruct((b, s, d), F32),
                   jax.ShapeDtypeStruct((b, s, d), BF16),
                   jax.ShapeDtypeStruct((b, s, ne), F32)],
        compiler_params=_params("parallel", "arbitrary"),
    )(x, mods, g1.reshape(1, d), g2.reshape(1, d), ya, yb, yd, p, p, p, p, p, p, p,
      conv_w, wg, wb, wo, wr)


def _route(scores, bias):
    n = scores.shape[0]
    grp = (scores + bias.astype(F32)).reshape(n, N_GROUPS, N_EXPERTS // N_GROUPS)
    grp_score = jnp.sum(lax.top_k(grp, 2)[0], axis=-1)
    _, grp_idx = lax.top_k(grp_score, TOPK_GROUPS)
    grp_mask = jnp.any(grp_idx[..., None] == jnp.arange(N_GROUPS), axis=-2)
    choice = jnp.where(grp_mask[..., None], grp, -jnp.inf).reshape(n, N_EXPERTS)
    _, idx = lax.top_k(choice, TOP_K)
    w = jnp.take_along_axis(scores, idx, axis=-1)
    w = w / jnp.sum(w, axis=-1, keepdims=True) * ROUTED_SCALE
    return idx, w


def _dispatch_plan(idx, wts):
    n = idx.shape[0]
    a = n * TOP_K
    flat_e = idx.reshape(a)
    order = jnp.argsort(flat_e)
    sorted_e = flat_e[order]
    counts = jnp.bincount(flat_e, length=N_EXPERTS)
    padded = (counts + MOE_ROWS - 1) // MOE_ROWS * MOE_ROWS
    pad_end = jnp.cumsum(padded)
    pad_start = pad_end - padded
    grp_start = jnp.cumsum(counts) - counts
    dest = (pad_start[sorted_e] + jnp.arange(a) - grp_start[sorted_e]).astype(jnp.int32)
    n_blocks = (a + N_EXPERTS * (MOE_ROWS - 1)) // MOE_ROWS
    n_rows = n_blocks * MOE_ROWS
    row_tok = jnp.zeros((n_rows,), jnp.int32).at[dest].set((order // TOP_K).astype(jnp.int32))
    row_w = jnp.zeros((n_rows,), F32).at[dest].set(wts.reshape(a)[order])
    block_e = jnp.minimum(jnp.searchsorted(pad_end, jnp.arange(n_blocks) * MOE_ROWS, side='right'),
                          N_EXPERTS - 1).astype(jnp.int32)
    pos = jnp.zeros((a,), jnp.int32).at[order].set(dest)
    n_used = (pad_end[-1] // MOE_ROWS).astype(jnp.int32).reshape(1)
    return row_tok, row_w, block_e, pos, n_used


def _expert_kernel(be_ref, nu_ref, x_ref, rw_ref, wg_ref, wu_ref, wd_ref, y_ref):
    i = pl.program_id(0)

    @pl.when(i < nu_ref[0])
    def _():
        x = x_ref[...]
        g = _dot(x, wg_ref[0].astype(BF16))
        u = _dot(x, wu_ref[0].astype(BF16))
        y = _dot((_silu(g) * u).astype(BF16), wd_ref[0].astype(BF16))
        y_ref[...] = (y * rw_ref[...]).astype(y_ref.dtype)

    @pl.when(i >= nu_ref[0])
    def _():
        y_ref[...] = jnp.zeros(y_ref.shape, y_ref.dtype)


def _experts(x_sorted, row_w, block_e, n_used, w_g, w_u, w_d):
    n_rows, d = x_sorted.shape
    f = w_g.shape[2]
    n_blocks = n_rows // MOE_ROWS
    grid_spec = pltpu.PrefetchScalarGridSpec(
        num_scalar_prefetch=2,
        grid=(n_blocks,),
        in_specs=[pl.BlockSpec((MOE_ROWS, d), lambda i, be, nu: (i, 0)),
                  pl.BlockSpec((MOE_ROWS, 1), lambda i, be, nu: (i, 0)),
                  pl.BlockSpec((1, d, f), lambda i, be, nu: (be[i], 0, 0)),
                  pl.BlockSpec((1, d, f), lambda i, be, nu: (be[i], 0, 0)),
                  pl.BlockSpec((1, f, d), lambda i, be, nu: (be[i], 0, 0))],
        out_specs=pl.BlockSpec((MOE_ROWS, d), lambda i, be, nu: (i, 0)))
    return pl.pallas_call(
        _expert_kernel,
        grid_spec=grid_spec,
        out_shape=jax.ShapeDtypeStruct((n_rows, d), BF16),
        compiler_params=_params("arbitrary"),
    )(block_e, n_used, x_sorted, row_w.reshape(n_rows, 1), w_g, w_u, w_d)


def _resid_kernel(x_ref, h2_ref, r_ref, mod_ref, sg_ref, su_ref, sd_ref, gf_ref, o_ref, *, final):
    h2 = h2_ref[...]
    a = (_silu(_dot(h2, sg_ref[...])) * _dot(h2, su_ref[...])).astype(BF16)
    y = _dot(a, sd_ref[...]) + r_ref[...]
    xo = x_ref[...] + mod_ref[0][5:6] * y
    if final:
        xo = xo * lax.rsqrt(jnp.mean(xo * xo, axis=-1, keepdims=True) + EPS) * gf_ref[...]
    o_ref[...] = xo


def _resid(x_flat, h2_all, routed, row_off, mods, mod_row, rows_per_mod, sg, su, sd, gf, final, tm):
    n, d = x_flat.shape
    f = sg.shape[1]
    off = row_off // tm
    per = rows_per_mod // tm
    mod_idx = (lambda i: (i // per, 0, 0)) if mod_row is None else (lambda i: (mod_row, 0, 0))
    const = lambda shape: pl.BlockSpec(shape, lambda i: (0,) * len(shape))
    return pl.pallas_call(
        functools.partial(_resid_kernel, final=final),
        grid=(n // tm,),
        in_specs=[pl.BlockSpec((tm, d), lambda i: (i, 0)),
                  pl.BlockSpec((tm, d), lambda i: (i + off, 0)),
                  pl.BlockSpec((tm, d), lambda i: (i + off, 0)),
                  pl.BlockSpec((1, N_MOD, d), mod_idx),
                  const((d, f)), const((d, f)), const((f, d)), const((1, d))],
        out_specs=pl.BlockSpec((tm, d), lambda i: (i, 0)),
        out_shape=jax.ShapeDtypeStruct((n, d), F32),
        compiler_params=_params("parallel"),
    )(x_flat, h2_all, routed, mods, sg, su, sd, gf.reshape(1, d))


def _moe_routed(h2_all, scores, bias, w_g, w_u, w_d):
    n, d = h2_all.shape
    idx, wts = _route(scores, bias)
    row_tok, row_w, block_e, pos, n_used = _dispatch_plan(idx, wts)
    x_sorted = jnp.take(h2_all, row_tok, axis=0)
    y_sorted = _experts(x_sorted, row_w, block_e, n_used, w_g, w_u, w_d)
    return jnp.take(y_sorted, pos, axis=0).reshape(n, TOP_K, d).astype(F32).sum(axis=1)


def kernel(x, c, ctx, c_ctx, ada_w, ada_b, norm1_g, w_in, conv_w, na_rel_bias, diff_lambda,
           diff_subln_g, w_branch_gate, w_branch, w_out, norm2_g, router_w, router_bias,
           expert_w_gate, expert_w_up, expert_w_down, shared_w_gate, shared_w_up, shared_w_down,
           final_norm_g):
    b, s, d = x.shape
    l_ctx = ctx.shape[1]
    rows = s // GRID_W
    ctx_row = b
    cvec = jnp.zeros((8, d), F32).at[:b].set(c).at[ctx_row].set(c_ctx)
    rope_tabs = _rope_tables(s)
    wf = _channel_dft_matrix()
    tm = 512
    xc = ctx
    for layer in range(DEPTH):
        last = layer == DEPTH - 1
        lam_init = 0.8 - 0.6 * math.exp(-0.3 * layer)
        mods = _ada(cvec, ada_w[layer], ada_b[layer]).reshape(8, N_MOD, d)
        w_in_bf = w_in[layer].astype(BF16)
        wg_bf = w_branch_gate[layer].astype(BF16)
        wb_bf = w_branch[layer].astype(BF16)
        wo_bf = w_out[layer].astype(BF16)
        wr_bf = router_w[layer].astype(BF16)
        lp = diff_lambda[layer]
        sub_g = diff_subln_g[layer]

        p, ur, ui = _inproj(x, norm1_g[layer], mods, w_in_bf, wf, rope_tabs, None, True, tm)
        ctx_tabs = tuple(t[:l_ctx] for t in rope_tabs)
        pc, ucr, uci = _inproj(xc, norm1_g[layer], mods, w_in_bf, wf, ctx_tabs, ctx_row, False, l_ctx)

        ya = _fourier_latent(ur, ui)
        yb = _na_latent(p, pc, _na_bias_tables(na_rel_bias[layer], rows))
        yd = _diff_latent(p, pc, lp, sub_g, lam_init, 256, _key_tile(s + l_ctx, 768))
        x, h2, sc = _merge(x, mods, None, norm1_g[layer], norm2_g[layer], ya, yb, yd, p,
                           conv_w[layer], wg_bf, wb_bf, wo_bf, wr_bf, tm)
        h2_all = h2.reshape(b * s, d)
        sc_all = sc.reshape(b * s, N_EXPERTS)
        if not last:
            yac, ybc, ydc = _ctx_branches(pc, ucr, uci, lp, sub_g, lam_init)
            xc, h2c, scc = _merge(xc, mods, ctx_row, norm1_g[layer], norm2_g[layer], yac, ybc, ydc, pc,
                                  conv_w[layer], wg_bf, wb_bf, wo_bf, wr_bf, l_ctx)
            h2_all = jnp.concatenate([h2_all, h2c.reshape(b * l_ctx, d)], axis=0)
            sc_all = jnp.concatenate([sc_all, scc.reshape(b * l_ctx, N_EXPERTS)], axis=0)

        routed = _moe_routed(h2_all, sc_all, router_bias[layer], expert_w_gate[layer],
                             expert_w_up[layer], expert_w_down[layer])
        sg_bf = shared_w_gate[layer].astype(BF16)
        su_bf = shared_w_up[layer].astype(BF16)
        sd_bf = shared_w_down[layer].astype(BF16)
        x = _resid(x.reshape(b * s, d), h2_all, routed, 0, mods, None, s, sg_bf, su_bf, sd_bf,
                   final_norm_g, last, tm).reshape(b, s, d)
        if not last:
            xc = _resid(xc.reshape(b * l_ctx, d), h2_all, routed, b * s, mods, ctx_row, l_ctx,
                        sg_bf, su_bf, sd_bf, final_norm_g, False, l_ctx).reshape(b, l_ctx, d)
    return x
```

```python
import functools
import math

import numpy as np
import jax
import jax.numpy as jnp
from jax import lax
from jax.experimental import pallas as pl
from jax.experimental.pallas import tpu as pltpu

F32 = jnp.float32
BF16 = jnp.bfloat16

DEPTH = 2
GRID_W = 64
EPS = 1e-6
N_MOD = 6

FNET_GROUP_DIM = 64
NA_HEADS = 4
NA_HEAD_DIM = 64
NA_WIN_ROWS = 8
NA_WIN_COLS = 16
CONV_WIDTH = 3
DIFF_HEADS = 4
DIFF_QK_DIM = 32
DIFF_V_DIM = 64
ROPE_BASE = 10000.0
N_BRANCHES = 4
BRANCH_DIM = 256

COL_A, COL_BQ, COL_BK, COL_BV, COL_CB, COL_CC, COL_CX, COL_DQ, COL_DK, COL_DV = range(10)
N_COL_BLOCKS = 10
CB = 256
IN_DIM = N_COL_BLOCKS * CB

N_EXPERTS = 256
TOP_K = 8
N_GROUPS = 8
TOPK_GROUPS = 4
ROUTED_SCALE = 2.5

VMEM_LIMIT_BYTES = 56 * 1024 * 1024
LANES = 128
BF16_SUBLANES = 16
FFT_N1 = 64
NA_QROWS = 8
NA_KROWS = 16
MOE_ROWS = 256


def _params(*sem):
    return pltpu.CompilerParams(dimension_semantics=sem, vmem_limit_bytes=VMEM_LIMIT_BYTES)


def _dot(a, b):
    return jnp.dot(a, b, preferred_element_type=F32)


def _dot_nt(a, b):
    return lax.dot_general(a, b, (((1,), (1,)), ((), ())), preferred_element_type=F32)


def _norm_mod(xf, g, shift, scale):
    y = xf * lax.rsqrt(jnp.mean(xf * xf, axis=-1, keepdims=True) + EPS)
    return (y * g) * (1.0 + scale) + shift


def _silu(v):
    return v * jax.nn.sigmoid(v)


def _ada_kernel(c_ref, w_ref, b_ref, o_ref):
    s = _silu(c_ref[...])
    o_ref[...] = _dot(s.astype(BF16), w_ref[...].astype(BF16)) + b_ref[...]


def _ada(cvec, w, b):
    rows, d = cvec.shape
    n = w.shape[1]
    tn = 1536
    return pl.pallas_call(
        _ada_kernel,
        grid=(n // tn,),
        in_specs=[pl.BlockSpec((rows, d), lambda j: (0, 0)),
                  pl.BlockSpec((d, tn), lambda j: (0, j)),
                  pl.BlockSpec((1, tn), lambda j: (0, j))],
        out_specs=pl.BlockSpec((rows, tn), lambda j: (0, j)),
        out_shape=jax.ShapeDtypeStruct((rows, n), F32),
        compiler_params=_params("arbitrary"),
    )(cvec, w, b.reshape(1, n))


def _inproj_kernel(x_ref, g_ref, mod_ref, w_ref, wf_ref, cos_ref, s1_ref, s2_ref,
                   p_ref, ur_ref, ui_ref, *, rope):
    mod = mod_ref[0]
    h = _norm_mod(x_ref[0], g_ref[...], mod[0:1], mod[1:2]).astype(BF16)
    for j in range(N_COL_BLOCKS):
        pj = _dot(h, w_ref[:, j * CB:(j + 1) * CB])
        if j == COL_A:
            u = _dot(pj.astype(BF16), wf_ref[...])
            ur_ref[0] = u[:, :CB].astype(BF16)
            ui_ref[0] = u[:, CB:].astype(BF16)
        if rope and j in (COL_DQ, COL_DK):
            cos = jnp.concatenate([cos_ref[...]] * 2, axis=1)
            s1 = jnp.concatenate([s1_ref[...]] * 2, axis=1)
            s2 = jnp.concatenate([s2_ref[...]] * 2, axis=1)
            pj = pj * cos + pltpu.roll(pj, CB - 8, 1) * s1 + pltpu.roll(pj, 8, 1) * s2
            if j == COL_DQ:
                pj = pj * (DIFF_QK_DIM ** -0.5)
        p_ref[0, :, j * CB:(j + 1) * CB] = pj.astype(BF16)


def _inproj(x, g, mods, w_bf, wf, rope_tabs, mod_row, rope, tm):
    b, s, d = x.shape
    mod_idx = (lambda bi, i: (bi, 0, 0)) if mod_row is None else (lambda bi, i: (mod_row, 0, 0))
    tab_spec = pl.BlockSpec((tm, 128), lambda bi, i: (i, 0))
    seq_spec = lambda width: pl.BlockSpec((1, tm, width), lambda bi, i: (bi, i, 0))
    return pl.pallas_call(
        functools.partial(_inproj_kernel, rope=rope),
        grid=(b, s // tm),
        in_specs=[seq_spec(d),
                  pl.BlockSpec((1, d), lambda bi, i: (0, 0)),
                  pl.BlockSpec((1, N_MOD, d), mod_idx),
                  pl.BlockSpec((d, IN_DIM), lambda bi, i: (0, 0)),
                  pl.BlockSpec((CB, 2 * CB), lambda bi, i: (0, 0)),
                  tab_spec, tab_spec, tab_spec],
        out_specs=[seq_spec(IN_DIM), seq_spec(CB), seq_spec(CB)],
        out_shape=[jax.ShapeDtypeStruct((b, s, IN_DIM), BF16),
                   jax.ShapeDtypeStruct((b, s, CB), BF16),
                   jax.ShapeDtypeStruct((b, s, CB), BF16)],
        compiler_params=_params("parallel", "arbitrary"),
    )(x, g.reshape(1, d), mods, w_bf, wf, *rope_tabs)


def _channel_dft_matrix():
    c = np.arange(FNET_GROUP_DIM)
    ang = 2.0 * np.pi * ((c[:, None] * c[None, :]) % FNET_GROUP_DIM) / FNET_GROUP_DIM
    eye = np.eye(CB // FNET_GROUP_DIM)
    m = np.concatenate([np.kron(eye, np.cos(ang)), -np.kron(eye, np.sin(ang))], axis=1)
    return jnp.asarray(m, BF16)


def _rope_tables(s):
    half = DIFF_QK_DIM // 2
    t = jnp.arange(s)
    inv = 1.0 / (ROPE_BASE ** (jnp.arange(0, half, 2, dtype=F32) / half))
    ang_r = (t // GRID_W).astype(F32)[:, None] * inv
    ang_c = (t % GRID_W).astype(F32)[:, None] * inv
    zero = jnp.zeros_like(ang_r)
    cos = jnp.concatenate([jnp.cos(ang_r)] * 2 + [jnp.cos(ang_c)] * 2, axis=1)
    s1 = jnp.concatenate([-jnp.sin(ang_r), zero, -jnp.sin(ang_c), zero], axis=1)
    s2 = jnp.concatenate([zero, jnp.sin(ang_r), zero, jnp.sin(ang_c)], axis=1)
    return tuple(jnp.concatenate([a] * 4, axis=1) for a in (cos, s1, s2))


def _fft1_kernel(ur_ref, ui_ref, w_ref, ct_ref, st_ref, ar_ref, ai_ref):
    n1 = ur_ref.shape[1]
    u = jnp.concatenate([ur_ref[0], ui_ref[0]], axis=0)
    a = _dot(w_ref[...], u)
    ar, ai = a[:n1], a[n1:]
    ct, st = ct_ref[...], st_ref[...]
    ar_ref[0] = (ar * ct + ai * st).astype(BF16)
    ai_ref[0] = (ai * ct - ar * st).astype(BF16)


def _fft2_kernel(ar_ref, ai_ref, w_ref, y_ref, *, norm):
    for j in range(ar_ref.shape[1]):
        a = jnp.concatenate([ar_ref[0, j], ai_ref[0, j]], axis=0)
        y_ref[0, j] = (_dot(w_ref[...], a) * norm).astype(BF16)


def _dft_cos_sin(n):
    k = np.arange(n)
    ang = 2.0 * np.pi * ((k[:, None] * k[None, :]) % n) / n
    return np.cos(ang), np.sin(ang)


def _fourier_latent(ur, ui):
    b, s, cb = ur.shape
    n1, n2 = FFT_N1, s // FFT_N1
    c1, s1 = _dft_cos_sin(n1)
    w1 = jnp.asarray(np.block([[c1, s1], [-s1, c1]]), BF16)
    c2, s2 = _dft_cos_sin(n2)
    w2 = jnp.asarray(np.concatenate([c2, s2], axis=1), BF16)
    tw = 2.0 * np.pi * (np.arange(n1)[:, None] * np.arange(n2)[None, :]) / s
    ct = jnp.broadcast_to(jnp.asarray(np.cos(tw), F32)[:, :, None], (n1, n2, cb)).reshape(n1, n2 * cb)
    st = jnp.broadcast_to(jnp.asarray(np.sin(tw), F32)[:, :, None], (n1, n2, cb)).reshape(n1, n2 * cb)
    lanes = n2 * cb
    tn = min(lanes, 4096)
    u_spec = pl.BlockSpec((1, n1, tn), lambda j, bi: (bi, 0, j))
    t_spec = pl.BlockSpec((n1, tn), lambda j, bi: (0, j))
    ar, ai = pl.pallas_call(
        _fft1_kernel,
        grid=(lanes // tn, b),
        in_specs=[u_spec, u_spec, pl.BlockSpec((2 * n1, 2 * n1), lambda j, bi: (0, 0)), t_spec, t_spec],
        out_specs=[u_spec, u_spec],
        out_shape=[jax.ShapeDtypeStruct((b, n1, lanes), BF16)] * 2,
        compiler_params=_params("arbitrary", "arbitrary"),
    )(ur.reshape(b, n1, lanes), ui.reshape(b, n1, lanes), w1, ct, st)
    kc = 8
    a_spec = pl.BlockSpec((1, kc, n2, cb), lambda bi, j: (bi, j, 0, 0))
    y = pl.pallas_call(
        functools.partial(_fft2_kernel, norm=1.0 / math.sqrt(s * FNET_GROUP_DIM)),
        grid=(b, n1 // kc),
        in_specs=[a_spec, a_spec, pl.BlockSpec((n2, 2 * n2), lambda bi, j: (0, 0))],
        out_specs=a_spec,
        out_shape=jax.ShapeDtypeStruct((b, n1, n2, cb), BF16),
        compiler_params=_params("parallel", "arbitrary"),
    )(ar.reshape(b, n1, n2, cb), ai.reshape(b, n1, n2, cb), w2)
    return jnp.transpose(y, (0, 2, 1, 3)).reshape(b, s, cb)


def _na_kernel(q_ref, k_ref, v_ref, kc_ref, vc_ref, bias_ref, o_ref, *, rows):
    rb = pl.program_id(1)
    kb = jnp.clip(rb * NA_QROWS - NA_WIN_ROWS // 2, 0, rows - NA_KROWS)
    nk = NA_KROWS * GRID_W
    tok0 = pl.multiple_of(kb * GRID_W, 256)
    scale = NA_HEAD_DIM ** -0.5
    outs = []
    for h in range(NA_HEADS):
        sl = slice(h * NA_HEAD_DIM, (h + 1) * NA_HEAD_DIM)
        q = q_ref[0, :, sl]
        s = _dot_nt(q, k_ref[0, pl.ds(tok0, nk), sl]) * scale + bias_ref[0, h]
        sc = _dot_nt(q, kc_ref[0, :, sl]) * scale
        m = jnp.maximum(jnp.max(s, axis=-1, keepdims=True), jnp.max(sc, axis=-1, keepdims=True))
        e = jnp.exp(s - m)
        ec = jnp.exp(sc - m)
        l = jnp.sum(e, axis=-1, keepdims=True) + jnp.sum(ec, axis=-1, keepdims=True)
        o = _dot(e.astype(BF16), v_ref[0, pl.ds(tok0, nk), sl]) + _dot(ec.astype(BF16), vc_ref[0, :, sl])
        outs.append(o / l)
    o_ref[0] = jnp.concatenate(outs, axis=1).astype(BF16)


def _na_bias_tables(rel_bias, rows):
    tabs = []
    cq = np.arange(GRID_W)
    col_lo = np.clip(cq - NA_WIN_COLS // 2, 0, GRID_W - NA_WIN_COLS)
    col_ok = (cq[None, :] >= col_lo[:, None]) & (cq[None, :] < col_lo[:, None] + NA_WIN_COLS)
    dc_idx = np.clip(cq[None, :] - cq[:, None] + NA_WIN_COLS - 1, 0, 2 * NA_WIN_COLS - 2)
    for r0 in (0, NA_QROWS, rows - NA_QROWS):
        kb = int(np.clip(r0 - NA_WIN_ROWS // 2, 0, rows - NA_KROWS))
        r = r0 + np.arange(NA_QROWS)
        rk = kb + np.arange(NA_KROWS)
        start = np.clip(r - NA_WIN_ROWS // 2, 0, rows - NA_WIN_ROWS)
        row_ok = (rk[None, :] >= start[:, None]) & (rk[None, :] < start[:, None] + NA_WIN_ROWS)
        dr_idx = np.clip(rk[None, :] - r[:, None] + NA_WIN_ROWS - 1, 0, 2 * NA_WIN_ROWS - 2)
        ok = row_ok[:, None, :, None] & col_ok[None, :, None, :]
        vals = rel_bias[:, dr_idx[:, None, :, None], dc_idx[None, :, None, :]].astype(F32)
        tab = jnp.where(ok[None], vals, -jnp.inf)
        tabs.append(tab.reshape(NA_HEADS, NA_QROWS * GRID_W, NA_KROWS * GRID_W))
    return jnp.stack(tabs)


def _na_latent(p, pc, bias_tabs):
    b, s, _ = p.shape
    l = pc.shape[1]
    rows = s // GRID_W
    nrb = rows // NA_QROWS
    tq = NA_QROWS * GRID_W
    nk = NA_KROWS * GRID_W
    return pl.pallas_call(
        functools.partial(_na_kernel, rows=rows),
        grid=(b, nrb),
        in_specs=[pl.BlockSpec((1, tq, CB), lambda bi, i: (bi, i, COL_BQ)),
                  pl.BlockSpec((1, s, CB), lambda bi, i: (bi, 0, COL_BK)),
                  pl.BlockSpec((1, s, CB), lambda bi, i: (bi, 0, COL_BV)),
                  pl.BlockSpec((1, l, CB), lambda bi, i: (bi, 0, COL_BK)),
                  pl.BlockSpec((1, l, CB), lambda bi, i: (bi, 0, COL_BV)),
                  pl.BlockSpec((1, NA_HEADS, tq, nk),
                               lambda bi, i: (jnp.minimum(i, 1) + (i == nrb - 1).astype(jnp.int32), 0, 0, 0))],
        out_specs=pl.BlockSpec((1, tq, CB), lambda bi, i: (bi, i, 0)),
        out_shape=jax.ShapeDtypeStruct((b, s, CB), BF16),
        compiler_params=_params("parallel", "arbitrary"),
    )(p, p, p, pc, pc, bias_tabs)


def _diff_lambda(lp, lam_init):
    return (jnp.exp(jnp.sum(lp[0:1] * lp[1:2], axis=-1, keepdims=True))
            - jnp.exp(jnp.sum(lp[2:3] * lp[3:4], axis=-1, keepdims=True)) + lam_init)


def _diff_finish(o, g, lam_init):
    y = o * lax.rsqrt(jnp.mean(o * o, axis=-1, keepdims=True) + EPS)
    return y * g * (1.0 - lam_init)


def _diff_kernel(q_ref, kt_ref, v_ref, lp_ref, g_ref, o_ref, m_sc, l_sc, acc_sc, *, tk, lam_init):
    tq = q_ref.shape[1]
    nk = kt_ref.shape[2] // tk
    n_maps = 2 * DIFF_HEADS
    m_sc[...] = jnp.full(m_sc.shape, -jnp.inf, F32)
    l_sc[...] = jnp.zeros(l_sc.shape, F32)
    acc_sc[...] = jnp.zeros(acc_sc.shape, F32)

    def body(c, carry):
        k0 = pl.multiple_of(c * tk, 128)
        for hm in range(n_maps):
            q = q_ref[0, :, hm * DIFF_QK_DIM:(hm + 1) * DIFF_QK_DIM]
            s = _dot(q, kt_ref[0, hm * DIFF_QK_DIM:(hm + 1) * DIFF_QK_DIM, pl.ds(k0, tk)])
            m_old = m_sc[hm]
            m_new = jnp.maximum(m_old, jnp.max(s, axis=-1, keepdims=True))
            alpha = jnp.exp(m_old - m_new)
            e = jnp.exp(s - m_new[:, :1])
            l_sc[hm] = alpha * l_sc[hm] + jnp.sum(e, axis=-1, keepdims=True)
            pv = _dot(e.astype(BF16), v_ref[0, hm // 2, pl.ds(k0, tk), :])
            acc_sc[hm] = alpha[:, :DIFF_V_DIM] * acc_sc[hm] + pv
            m_sc[hm] = m_new
        return carry

    lax.fori_loop(0, nk, body, 0)
    lam = _diff_lambda(lp_ref[...], lam_init)
    outs = []
    for h in range(DIFF_HEADS):
        o0 = acc_sc[2 * h] / l_sc[2 * h][:, :1]
        o1 = acc_sc[2 * h + 1] / l_sc[2 * h + 1][:, :1]
        outs.append(_diff_finish(o0 - lam * o1, g_ref[...], lam_init))
    o_ref[0] = jnp.concatenate(outs, axis=1).astype(BF16)


def _key_tile(nkeys, cap):
    return max(t for t in range(128, cap + 1, 128) if nkeys % t == 0)


def _diff_latent(p, pc, lp, sub_g, lam_init, tq, tk):
    b, s, _ = p.shape
    l = pc.shape[1]
    nkeys = s + l
    k_all = jnp.concatenate([p[:, :, COL_DK * CB:(COL_DK + 1) * CB], pc[:, :, COL_DK * CB:(COL_DK + 1) * CB]], axis=1)
    v_all = jnp.concatenate([p[:, :, COL_DV * CB:], pc[:, :, COL_DV * CB:]], axis=1)
    kt = jnp.transpose(k_all, (0, 2, 1))
    vh = jnp.transpose(v_all.reshape(b, nkeys, DIFF_HEADS, DIFF_V_DIM), (0, 2, 1, 3))
    n_maps = 2 * DIFF_HEADS
    return pl.pallas_call(
        functools.partial(_diff_kernel, tk=tk, lam_init=lam_init),
        grid=(b, s // tq),
        in_specs=[pl.BlockSpec((1, tq, CB), lambda bi, i: (bi, i, COL_DQ)),
                  pl.BlockSpec((1, CB, nkeys), lambda bi, i: (bi, 0, 0)),
                  pl.BlockSpec((1, DIFF_HEADS, nkeys, DIFF_V_DIM), lambda bi, i: (bi, 0, 0, 0)),
                  pl.BlockSpec((4, DIFF_QK_DIM), lambda bi, i: (0, 0)),
                  pl.BlockSpec((1, DIFF_V_DIM), lambda bi, i: (0, 0))],
        out_specs=pl.BlockSpec((1, tq, CB), lambda bi, i: (bi, i, 0)),
        out_shape=jax.ShapeDtypeStruct((b, s, CB), BF16),
        scratch_shapes=[pltpu.VMEM((n_maps, tq, 128), F32),
                        pltpu.VMEM((n_maps, tq, 128), F32),
                        pltpu.VMEM((n_maps, tq, DIFF_V_DIM), F32)],
        compiler_params=_params("parallel", "arbitrary"),
    )(p, kt, vh, lp, sub_g.reshape(1, DIFF_V_DIM))


def _softmax_rows(s):
    e = jnp.exp(s - jnp.max(s, axis=-1, keepdims=True))
    return e / jnp.sum(e, axis=-1, keepdims=True)


def _ctx_kernel(pc_ref, ur_ref, ui_ref, wf_ref, lp_ref, g_ref, ya_ref, yb_ref, yd_ref, *, lam_init):
    l = pc_ref.shape[1]
    col = lambda j, lo, hi: pc_ref[0, :, j * CB + lo:j * CB + hi]
    u = jnp.concatenate([ur_ref[0], ui_ref[0]], axis=0)
    ya_ref[0] = (_dot(wf_ref[...], u) * (1.0 / math.sqrt(l * FNET_GROUP_DIM))).astype(BF16)
    outs = []
    for h in range(NA_HEADS):
        lo, hi = h * NA_HEAD_DIM, (h + 1) * NA_HEAD_DIM
        pr = _softmax_rows(_dot_nt(col(COL_BQ, lo, hi), col(COL_BK, lo, hi)) * NA_HEAD_DIM ** -0.5)
        outs.append(_dot(pr.astype(BF16), col(COL_BV, lo, hi)))
    yb_ref[0] = jnp.concatenate(outs, axis=1).astype(BF16)
    lam = _diff_lambda(lp_ref[...], lam_init)
    outs = []
    for h in range(DIFF_HEADS):
        pm = []
        for m in range(2):
            lo = (2 * h + m) * DIFF_QK_DIM
            pm.append(_softmax_rows(_dot_nt(col(COL_DQ, lo, lo + DIFF_QK_DIM), col(COL_DK, lo, lo + DIFF_QK_DIM))
                                    * DIFF_QK_DIM ** -0.5))
        a = (pm[0] - lam * pm[1]).astype(BF16)
        o = _dot(a, col(COL_DV, h * DIFF_V_DIM, (h + 1) * DIFF_V_DIM))
        outs.append(_diff_finish(o, g_ref[...], lam_init))
    yd_ref[0] = jnp.concatenate(outs, axis=1).astype(BF16)


def _ctx_branches(pc, ucr, uci, lp, sub_g, lam_init):
    b, l, _ = pc.shape
    c, s = _dft_cos_sin(l)
    wf = jnp.asarray(np.concatenate([c, s], axis=1), BF16)
    y_spec = pl.BlockSpec((1, l, CB), lambda bi: (bi, 0, 0))
    return pl.pallas_call(
        functools.partial(_ctx_kernel, lam_init=lam_init),
        grid=(b,),
        in_specs=[pl.BlockSpec((1, l, IN_DIM), lambda bi: (bi, 0, 0)), y_spec, y_spec,
                  pl.BlockSpec((l, 2 * l), lambda bi: (0, 0)),
                  pl.BlockSpec((4, DIFF_QK_DIM), lambda bi: (0, 0)),
                  pl.BlockSpec((1, DIFF_V_DIM), lambda bi: (0, 0))],
        out_specs=[y_spec] * 3,
        out_shape=[jax.ShapeDtypeStruct((b, l, CB), BF16)] * 3,
        compiler_params=_params("parallel"),
    )(pc, ucr, uci, wf, lp, sub_g.reshape(1, DIFF_V_DIM))


def _merge_kernel(x_ref, mod_ref, g1_ref, g2_ref, ya_ref, yb_ref, yd_ref, pb_ref, pc_ref, px_ref,
                  cp_ref, xp_ref, cn_ref, xn_ref, cw_ref, wg_ref, wb_ref, wo_ref,
                  xo_ref, h2_ref):
    i = pl.program_id(1)
    last = pl.num_programs(1) - 1
    tm = x_ref.shape[1]
    mod = mod_ref[0]
    x = x_ref[0]
    h = _norm_mod(x, g1_ref[...], mod[0:1], mod[1:2]).astype(BF16)

    u = pc_ref[0].astype(F32) * px_ref[0].astype(F32)
    up = cp_ref[0, BF16_SUBLANES - 1:, :].astype(F32) * xp_ref[0, BF16_SUBLANES - 1:, :].astype(F32)
    un = cn_ref[0, :1, :].astype(F32) * xn_ref[0, :1, :].astype(F32)
    up = jnp.where(i == 0, 0.0, up)
    un = jnp.where(i == last, 0.0, un)
    rid = lax.broadcasted_iota(jnp.int32, u.shape, 0)
    u_prev = jnp.where(rid == 0, up, pltpu.roll(u, 1, 0))
    u_next = jnp.where(rid == tm - 1, un, pltpu.roll(u, tm - 1, 0))
    cw = cw_ref[...]
    yc = pb_ref[0].astype(F32) * (cw[0:1] * u_prev + cw[1:2] * u + cw[2:3] * u_next)

    branches = (ya_ref[0], yb_ref[0], yc.astype(BF16), yd_ref[0])
    d = x.shape[1]
    out = None
    for n in range(d // CB):
        cols = slice(n * CB, (n + 1) * CB)
        merged = None
        for j in range(N_BRANCHES):
            t = jax.nn.sigmoid(_dot(h, wg_ref[j, :, cols])) * _dot(branches[j], wb_ref[j, :, cols])
            merged = t if merged is None else merged + t
        t = _dot(merged.astype(BF16), wo_ref[cols, :])
        out = t if out is None else out + t
    xn = x + mod[2:3] * out
    xo_ref[0] = xn
    h2_ref[0] = _norm_mod(xn, g2_ref[...], mod[3:4], mod[4:5]).astype(BF16)


def _merge(x, mods, mod_row, g1, g2, ya, yb, yd, p, conv_w, wg, wb, wo, tm):
    b, s, d = x.shape
    hb = tm // BF16_SUBLANES
    n_halo = s // BF16_SUBLANES
    mod_idx = (lambda bi, i: (bi, 0, 0)) if mod_row is None else (lambda bi, i: (mod_row, 0, 0))
    seq = lambda width, col=0: pl.BlockSpec((1, tm, width), lambda bi, i: (bi, i, col))
    prev = lambda col: pl.BlockSpec((1, BF16_SUBLANES, CB), lambda bi, i: (bi, jnp.maximum(i * hb - 1, 0), col))
    nxt = lambda col: pl.BlockSpec((1, BF16_SUBLANES, CB),
                                   lambda bi, i: (bi, jnp.minimum((i + 1) * hb, n_halo - 1), col))
    const = lambda shape: pl.BlockSpec(shape, lambda bi, i: (0,) * len(shape))
    return pl.pallas_call(
        _merge_kernel,
        grid=(b, s // tm),
        in_specs=[seq(d), pl.BlockSpec((1, N_MOD, d), mod_idx), const((1, d)), const((1, d)),
                  seq(CB), seq(CB), seq(CB),
                  seq(CB, COL_CB), seq(CB, COL_CC), seq(CB, COL_CX),
                  prev(COL_CC), prev(COL_CX), nxt(COL_CC), nxt(COL_CX),
                  const((CONV_WIDTH, CB)),
                  const((N_BRANCHES, d, d)), const((N_BRANCHES, BRANCH_DIM, d)), const((d, d))],
        out_specs=[seq(d), seq(d)],
        out_shape=[jax.ShapeDtypeStruct((b, s, d), F32),
                   jax.ShapeDtypeStruct((b, s, d), BF16)],
        compiler_params=_params("parallel", "arbitrary"),
    )(x, mods, g1.reshape(1, d), g2.reshape(1, d), ya, yb, yd, p, p, p, p, p, p, p,
      conv_w, wg, wb, wo)


def _route_kernel(h2_ref, wr_ref, bias_ref, tri_ref, ones_ref, idx_ref, w_ref, rank_ref, cnt_ref,
                  score_sc, sel_sc, carry_sc):
    i = pl.program_id(0)
    tm = h2_ref.shape[0]
    ne = wr_ref.shape[0]
    gsz = ne // N_GROUPS
    n_chunks = tm // LANES

    @pl.when(i == 0)
    def _():
        carry_sc[...] = jnp.zeros(carry_sc.shape, F32)

    score_sc[...] = jax.nn.sigmoid(_dot_nt(wr_ref[...], h2_ref[...]))

    def select(cidx, carry):
        c0 = pl.multiple_of(cidx * LANES, LANES)
        scores = score_sc[:, pl.ds(c0, LANES)]
        biased = scores + bias_ref[...]
        liota = lax.broadcasted_iota(jnp.int32, (gsz, LANES), 0)
        gs = []
        for g in range(N_GROUPS):
            v = biased[g * gsz:(g + 1) * gsz]
            m1 = jnp.max(v, axis=0, keepdims=True)
            i1 = jnp.min(jnp.where(v == m1, liota, gsz), axis=0, keepdims=True)
            m2 = jnp.max(jnp.where(liota == i1, -jnp.inf, v), axis=0, keepdims=True)
            gs.append(m1 + m2)
        gsm = jnp.concatenate(gs, axis=0)
        giota = lax.broadcasted_iota(jnp.int32, gsm.shape, 0)
        keep = jnp.zeros(gsm.shape, F32)
        for _ in range(TOPK_GROUPS):
            m = jnp.max(gsm, axis=0, keepdims=True)
            gi = jnp.min(jnp.where(gsm == m, giota, N_GROUPS), axis=0, keepdims=True)
            hit = giota == gi
            keep = jnp.where(hit, 1.0, keep)
            gsm = jnp.where(hit, -jnp.inf, gsm)
        cur = jnp.concatenate(
            [jnp.where(jnp.broadcast_to(keep[g:g + 1], (gsz, LANES)) > 0.0, biased[g * gsz:(g + 1) * gsz], -jnp.inf)
             for g in range(N_GROUPS)], axis=0)
        eiota = lax.broadcasted_iota(jnp.int32, (ne, LANES), 0)
        sel = jnp.zeros((ne, LANES), F32)
        idxs, ws = [], []
        for _ in range(TOP_K):
            m = jnp.max(cur, axis=0, keepdims=True)
            ik = jnp.min(jnp.where(cur == m, eiota, ne), axis=0, keepdims=True)
            hit = eiota == ik
            cur = jnp.where(hit, -jnp.inf, cur)
            ws.append(jnp.sum(jnp.where(hit, scores, 0.0), axis=0, keepdims=True))
            idxs.append(ik)
            sel = jnp.where(hit, 1.0, sel)
        w = jnp.concatenate(ws, axis=0)
        w_ref[:, pl.ds(c0, LANES)] = w / jnp.sum(w, axis=0, keepdims=True) * ROUTED_SCALE
        idx_ref[:, pl.ds(c0, LANES)] = jnp.concatenate(idxs, axis=0)
        sel_sc[:, pl.ds(c0, LANES)] = sel.astype(BF16)
        return carry

    lax.fori_loop(0, n_chunks, select, 0)

    sel_all = sel_sc[...]
    score_sc[...] = _dot(sel_all, tri_ref[...]) + jnp.concatenate([carry_sc[...]] * n_chunks, axis=1)

    def ranks(cidx, carry):
        c0 = pl.multiple_of(cidx * LANES, LANES)
        before = score_sc[:, pl.ds(c0, LANES)]
        idx = idx_ref[:, pl.ds(c0, LANES)]
        eiota = lax.broadcasted_iota(jnp.int32, (ne, LANES), 0)
        rows = [jnp.sum(jnp.where(eiota == idx[k:k + 1], before, 0.0), axis=0, keepdims=True)
                for k in range(TOP_K)]
        rank_ref[:, pl.ds(c0, LANES)] = jnp.concatenate(rows, axis=0).astype(jnp.int32)
        return carry

    lax.fori_loop(0, n_chunks, ranks, 0)
    carry_sc[...] += _dot(sel_all, ones_ref[...])
    cnt_ref[...] = carry_sc[...]


def _route(h2_all, wr_t, bias, tm):
    n, d = h2_all.shape
    ne = wr_t.shape[0]
    tri = jnp.asarray(np.triu(np.ones((tm, tm), np.float32), 1), BF16)
    ones = jnp.ones((tm, LANES), BF16)
    bias_b = jnp.broadcast_to(bias.astype(F32)[:, None], (ne, LANES))
    const = lambda shape: pl.BlockSpec(shape, lambda i: (0,) * len(shape))
    tok = pl.BlockSpec((TOP_K, tm), lambda i: (0, i))
    return pl.pallas_call(
        _route_kernel,
        grid=(n // tm,),
        in_specs=[pl.BlockSpec((tm, d), lambda i: (i, 0)), const((ne, d)), const((ne, LANES)),
                  const((tm, tm)), const((tm, LANES))],
        out_specs=[tok, tok, tok, const((ne, LANES))],
        out_shape=[jax.ShapeDtypeStruct((TOP_K, n), jnp.int32),
                   jax.ShapeDtypeStruct((TOP_K, n), F32),
                   jax.ShapeDtypeStruct((TOP_K, n), jnp.int32),
                   jax.ShapeDtypeStruct((ne, LANES), F32)],
        scratch_shapes=[pltpu.VMEM((ne, tm), F32), pltpu.VMEM((ne, tm), BF16), pltpu.VMEM((ne, LANES), F32)],
        compiler_params=_params("arbitrary"),
    )(h2_all, wr_t, bias_b, tri, ones)


def _pos_kernel(idx_ref, rank_ref, start_ref, pos_ref):
    ne = start_ref.shape[0]
    start = start_ref[...]

    def body(cidx, carry):
        c0 = pl.multiple_of(cidx * LANES, LANES)
        idx = idx_ref[:, pl.ds(c0, LANES)]
        eiota = lax.broadcasted_iota(jnp.int32, (ne, LANES), 0)
        rows = [jnp.sum(jnp.where(eiota == idx[k:k + 1], start, 0.0), axis=0, keepdims=True)
                for k in range(TOP_K)]
        pos_ref[:, pl.ds(c0, LANES)] = jnp.concatenate(rows, axis=0).astype(jnp.int32) + rank_ref[:, pl.ds(c0, LANES)]
        return carry

    lax.fori_loop(0, idx_ref.shape[1] // LANES, body, 0)


def _positions(idx, rank, start_rows, tm):
    k, n = idx.shape
    ne = start_rows.shape[0]
    start_b = jnp.broadcast_to(start_rows.astype(F32)[:, None], (ne, LANES))
    tok = pl.BlockSpec((k, tm), lambda i: (0, i))
    return pl.pallas_call(
        _pos_kernel,
        grid=(n // tm,),
        in_specs=[tok, tok, pl.BlockSpec((ne, LANES), lambda i: (0, 0))],
        out_specs=tok,
        out_shape=jax.ShapeDtypeStruct((k, n), jnp.int32),
        compiler_params=_params("parallel"),
    )(idx, rank, start_b)


def _expert_kernel(be_ref, nu_ref, x_ref, wg_ref, wu_ref, wd_ref, y_ref):
    i = pl.program_id(0)

    @pl.when(i < nu_ref[0])
    def _():
        x = x_ref[...]
        g = _dot(x, wg_ref[0].astype(BF16))
        u = _dot(x, wu_ref[0].astype(BF16))
        y_ref[...] = _dot((_silu(g) * u).astype(BF16), wd_ref[0].astype(BF16)).astype(y_ref.dtype)

    @pl.when(i >= nu_ref[0])
    def _():
        y_ref[...] = jnp.zeros(y_ref.shape, y_ref.dtype)


def _experts(x_sorted, block_e, n_used, w_g, w_u, w_d):
    n_rows, d = x_sorted.shape
    f = w_g.shape[2]
    n_blocks = n_rows // MOE_ROWS
    grid_spec = pltpu.PrefetchScalarGridSpec(
        num_scalar_prefetch=2,
        grid=(n_blocks,),
        in_specs=[pl.BlockSpec((MOE_ROWS, d), lambda i, be, nu: (i, 0)),
                  pl.BlockSpec((1, d, f), lambda i, be, nu: (be[i], 0, 0)),
                  pl.BlockSpec((1, d, f), lambda i, be, nu: (be[i], 0, 0)),
                  pl.BlockSpec((1, f, d), lambda i, be, nu: (be[i], 0, 0))],
        out_specs=pl.BlockSpec((MOE_ROWS, d), lambda i, be, nu: (i, 0)))
    return pl.pallas_call(
        _expert_kernel,
        grid_spec=grid_spec,
        out_shape=jax.ShapeDtypeStruct((n_rows, d), BF16),
        compiler_params=_params("arbitrary"),
    )(block_e, n_used, x_sorted, w_g, w_u, w_d)


def _resid_kernel(x_ref, h2_ref, y_ref, w_ref, mod_ref, sg_ref, su_ref, sd_ref, gf_ref, o_ref, *, final):
    h2 = h2_ref[...]
    a = (_silu(_dot(h2, sg_ref[...])) * _dot(h2, su_ref[...])).astype(BF16)
    y = _dot(a, sd_ref[...])
    w = w_ref[...]
    for k in range(TOP_K):
        y = y + w[:, k:k + 1] * y_ref[k].astype(F32)
    xo = x_ref[...] + mod_ref[0][5:6] * y
    if final:
        xo = xo * lax.rsqrt(jnp.mean(xo * xo, axis=-1, keepdims=True) + EPS) * gf_ref[...]
    o_ref[...] = xo


def _resid(x_flat, h2_all, y_tok, w_tok, row_off, mods, mod_row, rows_per_mod, sg, su, sd, gf, final, tm):
    n, d = x_flat.shape
    f = sg.shape[1]
    off = row_off // tm
    per = rows_per_mod // tm
    mod_idx = (lambda i: (i // per, 0, 0)) if mod_row is None else (lambda i: (mod_row, 0, 0))
    const = lambda shape: pl.BlockSpec(shape, lambda i: (0,) * len(shape))
    return pl.pallas_call(
        functools.partial(_resid_kernel, final=final),
        grid=(n // tm,),
        in_specs=[pl.BlockSpec((tm, d), lambda i: (i, 0)),
                  pl.BlockSpec((tm, d), lambda i: (i + off, 0)),
                  pl.BlockSpec((TOP_K, tm, d), lambda i: (0, i + off, 0)),
                  pl.BlockSpec((tm, TOP_K), lambda i: (i + off, 0)),
                  pl.BlockSpec((1, N_MOD, d), mod_idx),
                  const((d, f)), const((d, f)), const((f, d)), const((1, d))],
        out_specs=pl.BlockSpec((tm, d), lambda i: (i, 0)),
        out_shape=jax.ShapeDtypeStruct((n, d), F32),
        compiler_params=_params("parallel"),
    )(x_flat, h2_all, y_tok, w_tok, mods, sg, su, sd, gf.reshape(1, d))


def _moe_routed(h2_all, wr_t, bias, w_g, w_u, w_d):
    n, d = h2_all.shape
    idx, wts, rank, cnt = _route(h2_all, wr_t, bias, 512)
    counts = cnt[:, 0].astype(jnp.int32)
    padded = (counts + MOE_ROWS - 1) // MOE_ROWS * MOE_ROWS
    pad_end = jnp.cumsum(padded)
    n_blocks = (n * TOP_K + N_EXPERTS * (MOE_ROWS - 1)) // MOE_ROWS
    block_e = jnp.minimum(jnp.searchsorted(pad_end, jnp.arange(n_blocks) * MOE_ROWS, side='right'),
                          N_EXPERTS - 1).astype(jnp.int32)
    n_used = (pad_end[-1] // MOE_ROWS).astype(jnp.int32).reshape(1)
    pos = _positions(idx, rank, pad_end - padded, 512).reshape(TOP_K * n)
    tok = jnp.tile(jnp.arange(n, dtype=jnp.int32), TOP_K)
    row_tok = jnp.zeros((n_blocks * MOE_ROWS,), jnp.int32).at[pos].set(tok)
    x_sorted = jnp.take(h2_all, row_tok, axis=0)
    y_sorted = _experts(x_sorted, block_e, n_used, w_g, w_u, w_d)
    return jnp.take(y_sorted, pos, axis=0).reshape(TOP_K, n, d), wts.T


def kernel(x, c, ctx, c_ctx, ada_w, ada_b, norm1_g, w_in, conv_w, na_rel_bias, diff_lambda,
           diff_subln_g, w_branch_gate, w_branch, w_out, norm2_g, router_w, router_bias,
           expert_w_gate, expert_w_up, expert_w_down, shared_w_gate, shared_w_up, shared_w_down,
           final_norm_g):
    b, s, d = x.shape
    l_ctx = ctx.shape[1]
    rows = s // GRID_W
    ctx_row = b
    cvec = jnp.zeros((8, d), F32).at[:b].set(c).at[ctx_row].set(c_ctx)
    rope_tabs = _rope_tables(s)
    wf = _channel_dft_matrix()
    tm = 512
    xc = ctx
    for layer in range(DEPTH):
        last = layer == DEPTH - 1
        lam_init = 0.8 - 0.6 * math.exp(-0.3 * layer)
        mods = _ada(cvec, ada_w[layer], ada_b[layer]).reshape(8, N_MOD, d)
        w_in_bf = w_in[layer].astype(BF16)
        wg_bf = w_branch_gate[layer].astype(BF16)
        wb_bf = w_branch[layer].astype(BF16)
        wo_bf = w_out[layer].astype(BF16)
        wr_bf = router_w[layer].T.astype(BF16)
        lp = diff_lambda[layer]
        sub_g = diff_subln_g[layer]

        p, ur, ui = _inproj(x, norm1_g[layer], mods, w_in_bf, wf, rope_tabs, None, True, tm)
        ctx_tabs = tuple(t[:l_ctx] for t in rope_tabs)
        pc, ucr, uci = _inproj(xc, norm1_g[layer], mods, w_in_bf, wf, ctx_tabs, ctx_row, False, l_ctx)

        ya = _fourier_latent(ur, ui)
        yb = _na_latent(p, pc, _na_bias_tables(na_rel_bias[layer], rows))
        yd = _diff_latent(p, pc, lp, sub_g, lam_init, 256, _key_tile(s + l_ctx, 768))
        x, h2 = _merge(x, mods, None, norm1_g[layer], norm2_g[layer], ya, yb, yd, p,
                       conv_w[layer], wg_bf, wb_bf, wo_bf, tm)
        h2_all = h2.reshape(b * s, d)
        if not last:
            yac, ybc, ydc = _ctx_branches(pc, ucr, uci, lp, sub_g, lam_init)
            xc, h2c = _merge(xc, mods, ctx_row, norm1_g[layer], norm2_g[layer], yac, ybc, ydc, pc,
                             conv_w[layer], wg_bf, wb_bf, wo_bf, l_ctx)
            h2_all = jnp.concatenate([h2_all, h2c.reshape(b * l_ctx, d)], axis=0)

        y_tok, w_tok = _moe_routed(h2_all, wr_bf, router_bias[layer], expert_w_gate[layer],
                                   expert_w_up[layer], expert_w_down[layer])
        sg_bf = shared_w_gate[layer].astype(BF16)
        su_bf = shared_w_up[layer].astype(BF16)
        sd_bf = shared_w_down[layer].astype(BF16)
        x = _resid(x.reshape(b * s, d), h2_all, y_tok, w_tok, 0, mods, None, s, sg_bf, su_bf, sd_bf,
                   final_norm_g, last, tm).reshape(b, s, d)
        if not last:
            xc = _resid(xc.reshape(b * l_ctx, d), h2_all, y_tok, w_tok, b * s, mods, ctx_row, l_ctx,
                        sg_bf, su_bf, sd_bf, final_norm_g, False, l_ctx).reshape(b, l_ctx, d)
    return x
```

```python
import functools
import math

import numpy as np
import jax
import jax.numpy as jnp
from jax import lax
from jax.experimental import pallas as pl
from jax.experimental.pallas import tpu as pltpu

F32 = jnp.float32
BF16 = jnp.bfloat16

DEPTH = 2
GRID_W = 64
EPS = 1e-6
N_MOD = 6

FNET_GROUP_DIM = 64
NA_HEADS = 4
NA_HEAD_DIM = 64
NA_WIN_ROWS = 8
NA_WIN_COLS = 16
CONV_WIDTH = 3
DIFF_HEADS = 4
DIFF_QK_DIM = 32
DIFF_V_DIM = 64
ROPE_BASE = 10000.0
N_BRANCHES = 4
BRANCH_DIM = 256

COL_A, COL_BQ, COL_BK, COL_BV, COL_CB, COL_CC, COL_CX, COL_DQ, COL_DK, COL_DV = range(10)
N_COL_BLOCKS = 10
CB = 256
IN_DIM = N_COL_BLOCKS * CB

N_EXPERTS = 256
TOP_K = 8
N_GROUPS = 8
TOPK_GROUPS = 4
ROUTED_SCALE = 2.5

VMEM_LIMIT_BYTES = 56 * 1024 * 1024
LANES = 128
BF16_SUBLANES = 16
FFT_N1 = 64
NA_QROWS = 8
NA_KROWS = 16
MOE_ROWS = 256


def _params(*sem):
    return pltpu.CompilerParams(dimension_semantics=sem, vmem_limit_bytes=VMEM_LIMIT_BYTES)


def _dot(a, b):
    return jnp.dot(a, b, preferred_element_type=F32)


def _dot_nt(a, b):
    return lax.dot_general(a, b, (((1,), (1,)), ((), ())), preferred_element_type=F32)


def _norm_mod(xf, g, shift, scale):
    y = xf * lax.rsqrt(jnp.mean(xf * xf, axis=-1, keepdims=True) + EPS)
    return (y * g) * (1.0 + scale) + shift


def _silu(v):
    return v * jax.nn.sigmoid(v)


def _ada_kernel(c_ref, w_ref, b_ref, o_ref):
    s = _silu(c_ref[...])
    o_ref[...] = _dot(s.astype(BF16), w_ref[...].astype(BF16)) + b_ref[...]


def _ada(cvec, w, b):
    rows, d = cvec.shape
    n = w.shape[1]
    tn = 1536
    return pl.pallas_call(
        _ada_kernel,
        grid=(n // tn,),
        in_specs=[pl.BlockSpec((rows, d), lambda j: (0, 0)),
                  pl.BlockSpec((d, tn), lambda j: (0, j)),
                  pl.BlockSpec((1, tn), lambda j: (0, j))],
        out_specs=pl.BlockSpec((rows, tn), lambda j: (0, j)),
        out_shape=jax.ShapeDtypeStruct((rows, n), F32),
        compiler_params=_params("arbitrary"),
    )(cvec, w, b.reshape(1, n))


def _inproj_kernel(x_ref, g_ref, mod_ref, w_ref, wf_ref, cos_ref, s1_ref, s2_ref,
                   p_ref, ur_ref, ui_ref, *, rope):
    mod = mod_ref[0]
    h = _norm_mod(x_ref[0], g_ref[...], mod[0:1], mod[1:2]).astype(BF16)
    for j in range(N_COL_BLOCKS):
        pj = _dot(h, w_ref[:, j * CB:(j + 1) * CB])
        if j == COL_A:
            u = _dot(pj.astype(BF16), wf_ref[...])
            ur_ref[0] = u[:, :CB].astype(BF16)
            ui_ref[0] = u[:, CB:].astype(BF16)
        if rope and j in (COL_DQ, COL_DK):
            cos = jnp.concatenate([cos_ref[...]] * 2, axis=1)
            s1 = jnp.concatenate([s1_ref[...]] * 2, axis=1)
            s2 = jnp.concatenate([s2_ref[...]] * 2, axis=1)
            pj = pj * cos + pltpu.roll(pj, CB - 8, 1) * s1 + pltpu.roll(pj, 8, 1) * s2
            if j == COL_DQ:
                pj = pj * (DIFF_QK_DIM ** -0.5)
        p_ref[0, :, j * CB:(j + 1) * CB] = pj.astype(BF16)


def _inproj(x, g, mods, w_bf, wf, rope_tabs, mod_row, rope, tm):
    b, s, d = x.shape
    mod_idx = (lambda bi, i: (bi, 0, 0)) if mod_row is None else (lambda bi, i: (mod_row, 0, 0))
    tab_spec = pl.BlockSpec((tm, 128), lambda bi, i: (i, 0))
    seq_spec = lambda width: pl.BlockSpec((1, tm, width), lambda bi, i: (bi, i, 0))
    return pl.pallas_call(
        functools.partial(_inproj_kernel, rope=rope),
        grid=(b, s // tm),
        in_specs=[seq_spec(d),
                  pl.BlockSpec((1, d), lambda bi, i: (0, 0)),
                  pl.BlockSpec((1, N_MOD, d), mod_idx),
                  pl.BlockSpec((d, IN_DIM), lambda bi, i: (0, 0)),
                  pl.BlockSpec((CB, 2 * CB), lambda bi, i: (0, 0)),
                  tab_spec, tab_spec, tab_spec],
        out_specs=[seq_spec(IN_DIM), seq_spec(CB), seq_spec(CB)],
        out_shape=[jax.ShapeDtypeStruct((b, s, IN_DIM), BF16),
                   jax.ShapeDtypeStruct((b, s, CB), BF16),
                   jax.ShapeDtypeStruct((b, s, CB), BF16)],
        compiler_params=_params("parallel", "arbitrary"),
    )(x, g.reshape(1, d), mods, w_bf, wf, *rope_tabs)


def _channel_dft_matrix():
    c = np.arange(FNET_GROUP_DIM)
    ang = 2.0 * np.pi * ((c[:, None] * c[None, :]) % FNET_GROUP_DIM) / FNET_GROUP_DIM
    eye = np.eye(CB // FNET_GROUP_DIM)
    m = np.concatenate([np.kron(eye, np.cos(ang)), -np.kron(eye, np.sin(ang))], axis=1)
    return jnp.asarray(m, BF16)


def _rope_tables(s):
    half = DIFF_QK_DIM // 2
    t = jnp.arange(s)
    inv = 1.0 / (ROPE_BASE ** (jnp.arange(0, half, 2, dtype=F32) / half))
    ang_r = (t // GRID_W).astype(F32)[:, None] * inv
    ang_c = (t % GRID_W).astype(F32)[:, None] * inv
    zero = jnp.zeros_like(ang_r)
    cos = jnp.concatenate([jnp.cos(ang_r)] * 2 + [jnp.cos(ang_c)] * 2, axis=1)
    s1 = jnp.concatenate([-jnp.sin(ang_r), zero, -jnp.sin(ang_c), zero], axis=1)
    s2 = jnp.concatenate([zero, jnp.sin(ang_r), zero, jnp.sin(ang_c)], axis=1)
    return tuple(jnp.concatenate([a] * 4, axis=1) for a in (cos, s1, s2))


def _fft1_kernel(ur_ref, ui_ref, w_ref, ct_ref, st_ref, ar_ref, ai_ref):
    n1 = ur_ref.shape[1]
    u = jnp.concatenate([ur_ref[0], ui_ref[0]], axis=0)
    a = _dot(w_ref[...], u)
    ar, ai = a[:n1], a[n1:]
    ct, st = ct_ref[...], st_ref[...]
    ar_ref[0] = (ar * ct + ai * st).astype(BF16)
    ai_ref[0] = (ai * ct - ar * st).astype(BF16)


def _fft2_kernel(ar_ref, ai_ref, w_ref, y_ref, *, norm):
    for j in range(ar_ref.shape[1]):
        a = jnp.concatenate([ar_ref[0, j], ai_ref[0, j]], axis=0)
        y_ref[0, j] = (_dot(w_ref[...], a) * norm).astype(BF16)


def _dft_cos_sin(n):
    k = np.arange(n)
    ang = 2.0 * np.pi * ((k[:, None] * k[None, :]) % n) / n
    return np.cos(ang), np.sin(ang)


def _fourier_latent(ur, ui):
    b, s, cb = ur.shape
    n1, n2 = FFT_N1, s // FFT_N1
    c1, s1 = _dft_cos_sin(n1)
    w1 = jnp.asarray(np.block([[c1, s1], [-s1, c1]]), BF16)
    c2, s2 = _dft_cos_sin(n2)
    w2 = jnp.asarray(np.concatenate([c2, s2], axis=1), BF16)
    tw = 2.0 * np.pi * (np.arange(n1)[:, None] * np.arange(n2)[None, :]) / s
    ct = jnp.broadcast_to(jnp.asarray(np.cos(tw), F32)[:, :, None], (n1, n2, cb)).reshape(n1, n2 * cb)
    st = jnp.broadcast_to(jnp.asarray(np.sin(tw), F32)[:, :, None], (n1, n2, cb)).reshape(n1, n2 * cb)
    lanes = n2 * cb
    tn = min(lanes, 4096)
    u_spec = pl.BlockSpec((1, n1, tn), lambda j, bi: (bi, 0, j))
    t_spec = pl.BlockSpec((n1, tn), lambda j, bi: (0, j))
    ar, ai = pl.pallas_call(
        _fft1_kernel,
        grid=(lanes // tn, b),
        in_specs=[u_spec, u_spec, pl.BlockSpec((2 * n1, 2 * n1), lambda j, bi: (0, 0)), t_spec, t_spec],
        out_specs=[u_spec, u_spec],
        out_shape=[jax.ShapeDtypeStruct((b, n1, lanes), BF16)] * 2,
        compiler_params=_params("arbitrary", "arbitrary"),
    )(ur.reshape(b, n1, lanes), ui.reshape(b, n1, lanes), w1, ct, st)
    kc = 8
    a_spec = pl.BlockSpec((1, kc, n2, cb), lambda bi, j: (bi, j, 0, 0))
    y = pl.pallas_call(
        functools.partial(_fft2_kernel, norm=1.0 / math.sqrt(s * FNET_GROUP_DIM)),
        grid=(b, n1 // kc),
        in_specs=[a_spec, a_spec, pl.BlockSpec((n2, 2 * n2), lambda bi, j: (0, 0))],
        out_specs=a_spec,
        out_shape=jax.ShapeDtypeStruct((b, n1, n2, cb), BF16),
        compiler_params=_params("parallel", "arbitrary"),
    )(ar.reshape(b, n1, n2, cb), ai.reshape(b, n1, n2, cb), w2)
    return jnp.transpose(y, (0, 2, 1, 3)).reshape(b, s, cb)


def _na_kernel(q_ref, k_ref, v_ref, kc_ref, vc_ref, bias_ref, o_ref, *, rows):
    rb = pl.program_id(1)
    kb = jnp.clip(rb * NA_QROWS - NA_WIN_ROWS // 2, 0, rows - NA_KROWS)
    nk = NA_KROWS * GRID_W
    tok0 = pl.multiple_of(kb * GRID_W, 256)
    scale = NA_HEAD_DIM ** -0.5
    outs = []
    for h in range(NA_HEADS):
        sl = slice(h * NA_HEAD_DIM, (h + 1) * NA_HEAD_DIM)
        q = q_ref[0, :, sl]
        s = _dot_nt(q, k_ref[0, pl.ds(tok0, nk), sl]) * scale + bias_ref[0, h]
        sc = _dot_nt(q, kc_ref[0, :, sl]) * scale
        m = jnp.maximum(jnp.max(s, axis=-1, keepdims=True), jnp.max(sc, axis=-1, keepdims=True))
        e = jnp.exp(s - m)
        ec = jnp.exp(sc - m)
        l = jnp.sum(e, axis=-1, keepdims=True) + jnp.sum(ec, axis=-1, keepdims=True)
        o = _dot(e.astype(BF16), v_ref[0, pl.ds(tok0, nk), sl]) + _dot(ec.astype(BF16), vc_ref[0, :, sl])
        outs.append(o / l)
    o_ref[0] = jnp.concatenate(outs, axis=1).astype(BF16)


def _na_bias_tables(rel_bias, rows):
    tabs = []
    cq = np.arange(GRID_W)
    col_lo = np.clip(cq - NA_WIN_COLS // 2, 0, GRID_W - NA_WIN_COLS)
    col_ok = (cq[None, :] >= col_lo[:, None]) & (cq[None, :] < col_lo[:, None] + NA_WIN_COLS)
    dc_idx = np.clip(cq[None, :] - cq[:, None] + NA_WIN_COLS - 1, 0, 2 * NA_WIN_COLS - 2)
    for r0 in (0, NA_QROWS, rows - NA_QROWS):
        kb = int(np.clip(r0 - NA_WIN_ROWS // 2, 0, rows - NA_KROWS))
        r = r0 + np.arange(NA_QROWS)
        rk = kb + np.arange(NA_KROWS)
        start = np.clip(r - NA_WIN_ROWS // 2, 0, rows - NA_WIN_ROWS)
        row_ok = (rk[None, :] >= start[:, None]) & (rk[None, :] < start[:, None] + NA_WIN_ROWS)
        dr_idx = np.clip(rk[None, :] - r[:, None] + NA_WIN_ROWS - 1, 0, 2 * NA_WIN_ROWS - 2)
        ok = row_ok[:, None, :, None] & col_ok[None, :, None, :]
        oh_r = jnp.asarray(dr_idx[:, :, None] == np.arange(2 * NA_WIN_ROWS - 1), F32)
        oh_c = jnp.asarray(dc_idx[:, :, None] == np.arange(2 * NA_WIN_COLS - 1), F32)
        by_row = jnp.einsum('qka,hab->hqkb', oh_r, rel_bias.astype(F32), precision=lax.Precision.HIGHEST)
        vals = jnp.einsum('hqkb,cdb->hqckd', by_row, oh_c, precision=lax.Precision.HIGHEST)
        tab = jnp.where(ok[None], vals, -jnp.inf)
        tabs.append(tab.reshape(NA_HEADS, NA_QROWS * GRID_W, NA_KROWS * GRID_W))
    return jnp.stack(tabs)


def _na_latent(p, pc, bias_tabs):
    b, s, _ = p.shape
    l = pc.shape[1]
    rows = s // GRID_W
    nrb = rows // NA_QROWS
    tq = NA_QROWS * GRID_W
    nk = NA_KROWS * GRID_W
    return pl.pallas_call(
        functools.partial(_na_kernel, rows=rows),
        grid=(b, nrb),
        in_specs=[pl.BlockSpec((1, tq, CB), lambda bi, i: (bi, i, COL_BQ)),
                  pl.BlockSpec((1, s, CB), lambda bi, i: (bi, 0, COL_BK)),
                  pl.BlockSpec((1, s, CB), lambda bi, i: (bi, 0, COL_BV)),
                  pl.BlockSpec((1, l, CB), lambda bi, i: (bi, 0, COL_BK)),
                  pl.BlockSpec((1, l, CB), lambda bi, i: (bi, 0, COL_BV)),
                  pl.BlockSpec((1, NA_HEADS, tq, nk),
                               lambda bi, i: (jnp.minimum(i, 1) + (i == nrb - 1).astype(jnp.int32), 0, 0, 0))],
        out_specs=pl.BlockSpec((1, tq, CB), lambda bi, i: (bi, i, 0)),
        out_shape=jax.ShapeDtypeStruct((b, s, CB), BF16),
        compiler_params=_params("parallel", "arbitrary"),
    )(p, p, p, pc, pc, bias_tabs)


def _diff_lambda(lp, lam_init):
    return (jnp.exp(jnp.sum(lp[0:1] * lp[1:2], axis=-1, keepdims=True))
            - jnp.exp(jnp.sum(lp[2:3] * lp[3:4], axis=-1, keepdims=True)) + lam_init)


def _diff_finish(o, g, lam_init):
    y = o * lax.rsqrt(jnp.mean(o * o, axis=-1, keepdims=True) + EPS)
    return y * g * (1.0 - lam_init)


def _diff_kernel(q_ref, kt_ref, v_ref, lp_ref, g_ref, o_ref, m_sc, l_sc, acc_sc, *, tk, lam_init):
    tq = q_ref.shape[1]
    nk = kt_ref.shape[2] // tk
    n_maps = 2 * DIFF_HEADS
    m_sc[...] = jnp.full(m_sc.shape, -jnp.inf, F32)
    l_sc[...] = jnp.zeros(l_sc.shape, F32)
    acc_sc[...] = jnp.zeros(acc_sc.shape, F32)

    def body(c, carry):
        k0 = pl.multiple_of(c * tk, 128)
        for hm in range(n_maps):
            q = q_ref[0, :, hm * DIFF_QK_DIM:(hm + 1) * DIFF_QK_DIM]
            s = _dot(q, kt_ref[0, hm * DIFF_QK_DIM:(hm + 1) * DIFF_QK_DIM, pl.ds(k0, tk)])
            m_old = m_sc[hm]
            m_new = jnp.maximum(m_old, jnp.max(s, axis=-1, keepdims=True))
            alpha = jnp.exp(m_old - m_new)
            e = jnp.exp(s - m_new[:, :1])
            l_sc[hm] = alpha * l_sc[hm] + jnp.sum(e, axis=-1, keepdims=True)
            pv = _dot(e.astype(BF16), v_ref[0, hm // 2, pl.ds(k0, tk), :])
            acc_sc[hm] = alpha[:, :DIFF_V_DIM] * acc_sc[hm] + pv
            m_sc[hm] = m_new
        return carry

    lax.fori_loop(0, nk, body, 0)
    lam = _diff_lambda(lp_ref[...], lam_init)
    outs = []
    for h in range(DIFF_HEADS):
        o0 = acc_sc[2 * h] / l_sc[2 * h][:, :1]
        o1 = acc_sc[2 * h + 1] / l_sc[2 * h + 1][:, :1]
        outs.append(_diff_finish(o0 - lam * o1, g_ref[...], lam_init))
    o_ref[0] = jnp.concatenate(outs, axis=1).astype(BF16)


def _key_tile(nkeys, cap):
    return max(t for t in range(128, cap + 1, 128) if nkeys % t == 0)


def _diff_latent(p, pc, lp, sub_g, lam_init, tq, tk):
    b, s, _ = p.shape
    l = pc.shape[1]
    nkeys = s + l
    k_all = jnp.concatenate([p[:, :, COL_DK * CB:(COL_DK + 1) * CB], pc[:, :, COL_DK * CB:(COL_DK + 1) * CB]], axis=1)
    v_all = jnp.concatenate([p[:, :, COL_DV * CB:], pc[:, :, COL_DV * CB:]], axis=1)
    kt = jnp.transpose(k_all, (0, 2, 1))
    vh = jnp.transpose(v_all.reshape(b, nkeys, DIFF_HEADS, DIFF_V_DIM), (0, 2, 1, 3))
    n_maps = 2 * DIFF_HEADS
    return pl.pallas_call(
        functools.partial(_diff_kernel, tk=tk, lam_init=lam_init),
        grid=(b, s // tq),
        in_specs=[pl.BlockSpec((1, tq, CB), lambda bi, i: (bi, i, COL_DQ)),
                  pl.BlockSpec((1, CB, nkeys), lambda bi, i: (bi, 0, 0)),
                  pl.BlockSpec((1, DIFF_HEADS, nkeys, DIFF_V_DIM), lambda bi, i: (bi, 0, 0, 0)),
                  pl.BlockSpec((4, DIFF_QK_DIM), lambda bi, i: (0, 0)),
                  pl.BlockSpec((1, DIFF_V_DIM), lambda bi, i: (0, 0))],
        out_specs=pl.BlockSpec((1, tq, CB), lambda bi, i: (bi, i, 0)),
        out_shape=jax.ShapeDtypeStruct((b, s, CB), BF16),
        scratch_shapes=[pltpu.VMEM((n_maps, tq, 128), F32),
                        pltpu.VMEM((n_maps, tq, 128), F32),
                        pltpu.VMEM((n_maps, tq, DIFF_V_DIM), F32)],
        compiler_params=_params("parallel", "arbitrary"),
    )(p, kt, vh, lp, sub_g.reshape(1, DIFF_V_DIM))


def _softmax_rows(s):
    e = jnp.exp(s - jnp.max(s, axis=-1, keepdims=True))
    return e / jnp.sum(e, axis=-1, keepdims=True)


def _ctx_kernel(pc_ref, ur_ref, ui_ref, wf_ref, lp_ref, g_ref, ya_ref, yb_ref, yd_ref, *, lam_init):
    l = pc_ref.shape[1]
    col = lambda j, lo, hi: pc_ref[0, :, j * CB + lo:j * CB + hi]
    u = jnp.concatenate([ur_ref[0], ui_ref[0]], axis=0)
    ya_ref[0] = (_dot(wf_ref[...], u) * (1.0 / math.sqrt(l * FNET_GROUP_DIM))).astype(BF16)
    outs = []
    for h in range(NA_HEADS):
        lo, hi = h * NA_HEAD_DIM, (h + 1) * NA_HEAD_DIM
        pr = _softmax_rows(_dot_nt(col(COL_BQ, lo, hi), col(COL_BK, lo, hi)) * NA_HEAD_DIM ** -0.5)
        outs.append(_dot(pr.astype(BF16), col(COL_BV, lo, hi)))
    yb_ref[0] = jnp.concatenate(outs, axis=1).astype(BF16)
    lam = _diff_lambda(lp_ref[...], lam_init)
    outs = []
    for h in range(DIFF_HEADS):
        pm = []
        for m in range(2):
            lo = (2 * h + m) * DIFF_QK_DIM
            pm.append(_softmax_rows(_dot_nt(col(COL_DQ, lo, lo + DIFF_QK_DIM), col(COL_DK, lo, lo + DIFF_QK_DIM))
                                    * DIFF_QK_DIM ** -0.5))
        a = (pm[0] - lam * pm[1]).astype(BF16)
        o = _dot(a, col(COL_DV, h * DIFF_V_DIM, (h + 1) * DIFF_V_DIM))
        outs.append(_diff_finish(o, g_ref[...], lam_init))
    yd_ref[0] = jnp.concatenate(outs, axis=1).astype(BF16)


def _ctx_branches(pc, ucr, uci, lp, sub_g, lam_init):
    b, l, _ = pc.shape
    c, s = _dft_cos_sin(l)
    wf = jnp.asarray(np.concatenate([c, s], axis=1), BF16)
    y_spec = pl.BlockSpec((1, l, CB), lambda bi: (bi, 0, 0))
    return pl.pallas_call(
        functools.partial(_ctx_kernel, lam_init=lam_init),
        grid=(b,),
        in_specs=[pl.BlockSpec((1, l, IN_DIM), lambda bi: (bi, 0, 0)), y_spec, y_spec,
                  pl.BlockSpec((l, 2 * l), lambda bi: (0, 0)),
                  pl.BlockSpec((4, DIFF_QK_DIM), lambda bi: (0, 0)),
                  pl.BlockSpec((1, DIFF_V_DIM), lambda bi: (0, 0))],
        out_specs=[y_spec] * 3,
        out_shape=[jax.ShapeDtypeStruct((b, l, CB), BF16)] * 3,
        compiler_params=_params("parallel"),
    )(pc, ucr, uci, wf, lp, sub_g.reshape(1, DIFF_V_DIM))


def _merge_kernel(x_ref, mod_ref, g1_ref, g2_ref, ya_ref, yb_ref, yd_ref, pb_ref, pc_ref, px_ref,
                  cp_ref, xp_ref, cn_ref, xn_ref, cw_ref, wg_ref, wb_ref, wo_ref,
                  xo_ref, h2_ref):
    i = pl.program_id(1)
    last = pl.num_programs(1) - 1
    tm = x_ref.shape[1]
    mod = mod_ref[0]
    x = x_ref[0]
    h = _norm_mod(x, g1_ref[...], mod[0:1], mod[1:2]).astype(BF16)

    u = pc_ref[0].astype(F32) * px_ref[0].astype(F32)
    up = cp_ref[0, BF16_SUBLANES - 1:, :].astype(F32) * xp_ref[0, BF16_SUBLANES - 1:, :].astype(F32)
    un = cn_ref[0, :1, :].astype(F32) * xn_ref[0, :1, :].astype(F32)
    up = jnp.where(i == 0, 0.0, up)
    un = jnp.where(i == last, 0.0, un)
    rid = lax.broadcasted_iota(jnp.int32, u.shape, 0)
    u_prev = jnp.where(rid == 0, up, pltpu.roll(u, 1, 0))
    u_next = jnp.where(rid == tm - 1, un, pltpu.roll(u, tm - 1, 0))
    cw = cw_ref[...]
    yc = pb_ref[0].astype(F32) * (cw[0:1] * u_prev + cw[1:2] * u + cw[2:3] * u_next)

    branches = (ya_ref[0], yb_ref[0], yc.astype(BF16), yd_ref[0])
    d = x.shape[1]
    out = None
    for n in range(d // CB):
        cols = slice(n * CB, (n + 1) * CB)
        merged = None
        for j in range(N_BRANCHES):
            t = jax.nn.sigmoid(_dot(h, wg_ref[j, :, cols])) * _dot(branches[j], wb_ref[j, :, cols])
            merged = t if merged is None else merged + t
        t = _dot(merged.astype(BF16), wo_ref[cols, :])
        out = t if out is None else out + t
    xn = x + mod[2:3] * out
    xo_ref[0] = xn
    h2_ref[0] = _norm_mod(xn, g2_ref[...], mod[3:4], mod[4:5]).astype(BF16)


def _merge(x, mods, mod_row, g1, g2, ya, yb, yd, p, conv_w, wg, wb, wo, tm):
    b, s, d = x.shape
    hb = tm // BF16_SUBLANES
    n_halo = s // BF16_SUBLANES
    mod_idx = (lambda bi, i: (bi, 0, 0)) if mod_row is None else (lambda bi, i: (mod_row, 0, 0))
    seq = lambda width, col=0: pl.BlockSpec((1, tm, width), lambda bi, i: (bi, i, col))
    prev = lambda col: pl.BlockSpec((1, BF16_SUBLANES, CB), lambda bi, i: (bi, jnp.maximum(i * hb - 1, 0), col))
    nxt = lambda col: pl.BlockSpec((1, BF16_SUBLANES, CB),
                                   lambda bi, i: (bi, jnp.minimum((i + 1) * hb, n_halo - 1), col))
    const = lambda shape: pl.BlockSpec(shape, lambda bi, i: (0,) * len(shape))
    return pl.pallas_call(
        _merge_kernel,
        grid=(b, s // tm),
        in_specs=[seq(d), pl.BlockSpec((1, N_MOD, d), mod_idx), const((1, d)), const((1, d)),
                  seq(CB), seq(CB), seq(CB),
                  seq(CB, COL_CB), seq(CB, COL_CC), seq(CB, COL_CX),
                  prev(COL_CC), prev(COL_CX), nxt(COL_CC), nxt(COL_CX),
                  const((CONV_WIDTH, CB)),
                  const((N_BRANCHES, d, d)), const((N_BRANCHES, BRANCH_DIM, d)), const((d, d))],
        out_specs=[seq(d), seq(d)],
        out_shape=[jax.ShapeDtypeStruct((b, s, d), F32),
                   jax.ShapeDtypeStruct((b, s, d), BF16)],
        compiler_params=_params("parallel", "arbitrary"),
    )(x, mods, g1.reshape(1, d), g2.reshape(1, d), ya, yb, yd, p, p, p, p, p, p, p,
      conv_w, wg, wb, wo)


def _route_kernel(h2_ref, wr_ref, bias_ref, tri_ref, ones_ref, idx_ref, w_ref, rank_ref, cnt_ref,
                  score_sc, sel_sc, carry_sc):
    i = pl.program_id(0)
    tm = h2_ref.shape[0]
    ne = wr_ref.shape[0]
    gsz = ne // N_GROUPS
    n_chunks = tm // LANES

    @pl.when(i == 0)
    def _():
        carry_sc[...] = jnp.zeros(carry_sc.shape, F32)

    score_sc[...] = jax.nn.sigmoid(_dot_nt(wr_ref[...], h2_ref[...]))

    def select(cidx, carry):
        c0 = pl.multiple_of(cidx * LANES, LANES)
        scores = score_sc[:, pl.ds(c0, LANES)]
        biased = scores + bias_ref[...]
        liota = lax.broadcasted_iota(jnp.int32, (gsz, LANES), 0)
        gs = []
        for g in range(N_GROUPS):
            v = biased[g * gsz:(g + 1) * gsz]
            m1 = jnp.max(v, axis=0, keepdims=True)
            i1 = jnp.min(jnp.where(v == m1, liota, gsz), axis=0, keepdims=True)
            m2 = jnp.max(jnp.where(liota == i1, -jnp.inf, v), axis=0, keepdims=True)
            gs.append(m1 + m2)
        gsm = jnp.concatenate(gs, axis=0)
        giota = lax.broadcasted_iota(jnp.int32, gsm.shape, 0)
        keep = jnp.zeros(gsm.shape, F32)
        for _ in range(TOPK_GROUPS):
            m = jnp.max(gsm, axis=0, keepdims=True)
            gi = jnp.min(jnp.where(gsm == m, giota, N_GROUPS), axis=0, keepdims=True)
            hit = giota == gi
            keep = jnp.where(hit, 1.0, keep)
            gsm = jnp.where(hit, -jnp.inf, gsm)
        cur = jnp.concatenate(
            [jnp.where(jnp.broadcast_to(keep[g:g + 1], (gsz, LANES)) > 0.0, biased[g * gsz:(g + 1) * gsz], -jnp.inf)
             for g in range(N_GROUPS)], axis=0)
        eiota = lax.broadcasted_iota(jnp.int32, (ne, LANES), 0)
        sel = jnp.zeros((ne, LANES), F32)
        idxs, ws = [], []
        for _ in range(TOP_K):
            m = jnp.max(cur, axis=0, keepdims=True)
            ik = jnp.min(jnp.where(cur == m, eiota, ne), axis=0, keepdims=True)
            hit = eiota == ik
            cur = jnp.where(hit, -jnp.inf, cur)
            ws.append(jnp.sum(jnp.where(hit, scores, 0.0), axis=0, keepdims=True))
            idxs.append(ik)
            sel = jnp.where(hit, 1.0, sel)
        w = jnp.concatenate(ws, axis=0)
        w_ref[:, pl.ds(c0, LANES)] = w / jnp.sum(w, axis=0, keepdims=True) * ROUTED_SCALE
        idx_ref[:, pl.ds(c0, LANES)] = jnp.concatenate(idxs, axis=0)
        sel_sc[:, pl.ds(c0, LANES)] = sel.astype(BF16)
        return carry

    lax.fori_loop(0, n_chunks, select, 0)

    sel_all = sel_sc[...]
    score_sc[...] = _dot(sel_all, tri_ref[...]) + jnp.concatenate([carry_sc[...]] * n_chunks, axis=1)

    def ranks(cidx, carry):
        c0 = pl.multiple_of(cidx * LANES, LANES)
        before = score_sc[:, pl.ds(c0, LANES)]
        idx = idx_ref[:, pl.ds(c0, LANES)]
        eiota = lax.broadcasted_iota(jnp.int32, (ne, LANES), 0)
        rows = [jnp.sum(jnp.where(eiota == idx[k:k + 1], before, 0.0), axis=0, keepdims=True)
                for k in range(TOP_K)]
        rank_ref[:, pl.ds(c0, LANES)] = jnp.concatenate(rows, axis=0).astype(jnp.int32)
        return carry

    lax.fori_loop(0, n_chunks, ranks, 0)
    carry_sc[...] += _dot(sel_all, ones_ref[...])
    cnt_ref[...] = carry_sc[...]


def _route(h2_all, wr_t, bias, tm):
    n, d = h2_all.shape
    ne = wr_t.shape[0]
    tri = jnp.asarray(np.triu(np.ones((tm, tm), np.float32), 1), BF16)
    ones = jnp.ones((tm, LANES), BF16)
    bias_b = jnp.broadcast_to(bias.astype(F32)[:, None], (ne, LANES))
    const = lambda shape: pl.BlockSpec(shape, lambda i: (0,) * len(shape))
    tok = pl.BlockSpec((TOP_K, tm), lambda i: (0, i))
    return pl.pallas_call(
        _route_kernel,
        grid=(n // tm,),
        in_specs=[pl.BlockSpec((tm, d), lambda i: (i, 0)), const((ne, d)), const((ne, LANES)),
                  const((tm, tm)), const((tm, LANES))],
        out_specs=[tok, tok, tok, const((ne, LANES))],
        out_shape=[jax.ShapeDtypeStruct((TOP_K, n), jnp.int32),
                   jax.ShapeDtypeStruct((TOP_K, n), F32),
                   jax.ShapeDtypeStruct((TOP_K, n), jnp.int32),
                   jax.ShapeDtypeStruct((ne, LANES), F32)],
        scratch_shapes=[pltpu.VMEM((ne, tm), F32), pltpu.VMEM((ne, tm), BF16), pltpu.VMEM((ne, LANES), F32)],
        compiler_params=_params("arbitrary"),
    )(h2_all, wr_t, bias_b, tri, ones)


def _pos_kernel(idx_ref, rank_ref, start_ref, pos_ref):
    ne = start_ref.shape[0]
    start = start_ref[...]

    def body(cidx, carry):
        c0 = pl.multiple_of(cidx * LANES, LANES)
        idx = idx_ref[:, pl.ds(c0, LANES)]
        eiota = lax.broadcasted_iota(jnp.int32, (ne, LANES), 0)
        rows = [jnp.sum(jnp.where(eiota == idx[k:k + 1], start, 0.0), axis=0, keepdims=True)
                for k in range(TOP_K)]
        pos_ref[:, pl.ds(c0, LANES)] = jnp.concatenate(rows, axis=0).astype(jnp.int32) + rank_ref[:, pl.ds(c0, LANES)]
        return carry

    lax.fori_loop(0, idx_ref.shape[1] // LANES, body, 0)


def _positions(idx, rank, start_rows, tm):
    k, n = idx.shape
    ne = start_rows.shape[0]
    start_b = jnp.broadcast_to(start_rows.astype(F32)[:, None], (ne, LANES))
    tok = pl.BlockSpec((k, tm), lambda i: (0, i))
    return pl.pallas_call(
        _pos_kernel,
        grid=(n // tm,),
        in_specs=[tok, tok, pl.BlockSpec((ne, LANES), lambda i: (0, 0))],
        out_specs=tok,
        out_shape=jax.ShapeDtypeStruct((k, n), jnp.int32),
        compiler_params=_params("parallel"),
    )(idx, rank, start_b)


def _expert_kernel(be_ref, nu_ref, x_ref, wg_ref, wu_ref, wd_ref, y_ref):
    i = pl.program_id(0)

    @pl.when(i < nu_ref[0])
    def _():
        x = x_ref[...]
        g = _dot(x, wg_ref[0].astype(BF16))
        u = _dot(x, wu_ref[0].astype(BF16))
        y_ref[...] = _dot((_silu(g) * u).astype(BF16), wd_ref[0].astype(BF16)).astype(y_ref.dtype)

    @pl.when(i >= nu_ref[0])
    def _():
        y_ref[...] = jnp.zeros(y_ref.shape, y_ref.dtype)


def _experts(x_sorted, block_e, n_used, w_g, w_u, w_d):
    n_rows, d = x_sorted.shape
    f = w_g.shape[2]
    n_blocks = n_rows // MOE_ROWS
    grid_spec = pltpu.PrefetchScalarGridSpec(
        num_scalar_prefetch=2,
        grid=(n_blocks,),
        in_specs=[pl.BlockSpec((MOE_ROWS, d), lambda i, be, nu: (i, 0)),
                  pl.BlockSpec((1, d, f), lambda i, be, nu: (be[i], 0, 0)),
                  pl.BlockSpec((1, d, f), lambda i, be, nu: (be[i], 0, 0)),
                  pl.BlockSpec((1, f, d), lambda i, be, nu: (be[i], 0, 0))],
        out_specs=pl.BlockSpec((MOE_ROWS, d), lambda i, be, nu: (i, 0)))
    return pl.pallas_call(
        _expert_kernel,
        grid_spec=grid_spec,
        out_shape=jax.ShapeDtypeStruct((n_rows, d), BF16),
        compiler_params=_params("arbitrary"),
    )(block_e, n_used, x_sorted, w_g, w_u, w_d)


def _resid_kernel(x_ref, h2_ref, y_ref, w_ref, mod_ref, sg_ref, su_ref, sd_ref, gf_ref, o_ref, *, final):
    h2 = h2_ref[...]
    a = (_silu(_dot(h2, sg_ref[...])) * _dot(h2, su_ref[...])).astype(BF16)
    y = _dot(a, sd_ref[...])
    w = w_ref[...]
    for k in range(TOP_K):
        y = y + w[:, k:k + 1] * y_ref[k].astype(F32)
    xo = x_ref[...] + mod_ref[0][5:6] * y
    if final:
        xo = xo * lax.rsqrt(jnp.mean(xo * xo, axis=-1, keepdims=True) + EPS) * gf_ref[...]
    o_ref[...] = xo


def _resid(x_flat, h2_all, y_tok, w_tok, row_off, mods, mod_row, rows_per_mod, sg, su, sd, gf, final, tm):
    n, d = x_flat.shape
    f = sg.shape[1]
    off = row_off // tm
    per = rows_per_mod // tm
    mod_idx = (lambda i: (i // per, 0, 0)) if mod_row is None else (lambda i: (mod_row, 0, 0))
    const = lambda shape: pl.BlockSpec(shape, lambda i: (0,) * len(shape))
    return pl.pallas_call(
        functools.partial(_resid_kernel, final=final),
        grid=(n // tm,),
        in_specs=[pl.BlockSpec((tm, d), lambda i: (i, 0)),
                  pl.BlockSpec((tm, d), lambda i: (i + off, 0)),
                  pl.BlockSpec((TOP_K, tm, d), lambda i: (0, i + off, 0)),
                  pl.BlockSpec((tm, TOP_K), lambda i: (i + off, 0)),
                  pl.BlockSpec((1, N_MOD, d), mod_idx),
                  const((d, f)), const((d, f)), const((f, d)), const((1, d))],
        out_specs=pl.BlockSpec((tm, d), lambda i: (i, 0)),
        out_shape=jax.ShapeDtypeStruct((n, d), F32),
        compiler_params=_params("parallel"),
    )(x_flat, h2_all, y_tok, w_tok, mods, sg, su, sd, gf.reshape(1, d))


def _moe_routed(h2_all, wr_t, bias, w_g, w_u, w_d):
    n, d = h2_all.shape
    idx, wts, rank, cnt = _route(h2_all, wr_t, bias, 512)
    counts = cnt[:, 0].astype(jnp.int32)
    padded = (counts + MOE_ROWS - 1) // MOE_ROWS * MOE_ROWS
    pad_end = jnp.cumsum(padded)
    n_blocks = (n * TOP_K + N_EXPERTS * (MOE_ROWS - 1)) // MOE_ROWS
    block_e = jnp.minimum(jnp.searchsorted(pad_end, jnp.arange(n_blocks) * MOE_ROWS, side='right'),
                          N_EXPERTS - 1).astype(jnp.int32)
    n_used = (pad_end[-1] // MOE_ROWS).astype(jnp.int32).reshape(1)
    pos = _positions(idx, rank, pad_end - padded, 512).reshape(TOP_K * n)
    tok = jnp.tile(jnp.arange(n, dtype=jnp.int32), TOP_K)
    row_tok = jnp.zeros((n_blocks * MOE_ROWS,), jnp.int32).at[pos].set(tok)
    x_sorted = jnp.take(jnp.concatenate([h2_all, h2_all], axis=0), row_tok, axis=0)
    y_sorted = _experts(x_sorted, block_e, n_used, w_g, w_u, w_d)
    return jnp.take(y_sorted, pos, axis=0).reshape(TOP_K, n, d), wts.T


def kernel(x, c, ctx, c_ctx, ada_w, ada_b, norm1_g, w_in, conv_w, na_rel_bias, diff_lambda,
           diff_subln_g, w_branch_gate, w_branch, w_out, norm2_g, router_w, router_bias,
           expert_w_gate, expert_w_up, expert_w_down, shared_w_gate, shared_w_up, shared_w_down,
           final_norm_g):
    b, s, d = x.shape
    l_ctx = ctx.shape[1]
    rows = s // GRID_W
    ctx_row = b
    cvec = jnp.zeros((8, d), F32).at[:b].set(c).at[ctx_row].set(c_ctx)
    rope_tabs = _rope_tables(s)
    wf = _channel_dft_matrix()
    tm = 512
    xc = ctx
    for layer in range(DEPTH):
        last = layer == DEPTH - 1
        lam_init = 0.8 - 0.6 * math.exp(-0.3 * layer)
        mods = _ada(cvec, ada_w[layer], ada_b[layer]).reshape(8, N_MOD, d)
        w_in_bf = w_in[layer].astype(BF16)
        wg_bf = w_branch_gate[layer].astype(BF16)
        wb_bf = w_branch[layer].astype(BF16)
        wo_bf = w_out[layer].astype(BF16)
        wr_bf = router_w[layer].T.astype(BF16)
        lp = diff_lambda[layer]
        sub_g = diff_subln_g[layer]

        p, ur, ui = _inproj(x, norm1_g[layer], mods, w_in_bf, wf, rope_tabs, None, True, tm)
        ctx_tabs = tuple(t[:l_ctx] for t in rope_tabs)
        pc, ucr, uci = _inproj(xc, norm1_g[layer], mods, w_in_bf, wf, ctx_tabs, ctx_row, False, l_ctx)

        ya = _fourier_latent(ur, ui)
        yb = _na_latent(p, pc, _na_bias_tables(na_rel_bias[layer], rows))
        yd = _diff_latent(p, pc, lp, sub_g, lam_init, 256, _key_tile(s + l_ctx, 768))
        x, h2 = _merge(x, mods, None, norm1_g[layer], norm2_g[layer], ya, yb, yd, p,
                       conv_w[layer], wg_bf, wb_bf, wo_bf, tm)
        h2_all = h2.reshape(b * s, d)
        if not last:
            yac, ybc, ydc = _ctx_branches(pc, ucr, uci, lp, sub_g, lam_init)
            xc, h2c = _merge(xc, mods, ctx_row, norm1_g[layer], norm2_g[layer], yac, ybc, ydc, pc,
                             conv_w[layer], wg_bf, wb_bf, wo_bf, l_ctx)
            h2_all = jnp.concatenate([h2_all, h2c.reshape(b * l_ctx, d)], axis=0)

        y_tok, w_tok = _moe_routed(h2_all, wr_bf, router_bias[layer], expert_w_gate[layer],
                                   expert_w_up[layer], expert_w_down[layer])
        sg_bf = shared_w_gate[layer].astype(BF16)
        su_bf = shared_w_up[layer].astype(BF16)
        sd_bf = shared_w_down[layer].astype(BF16)
        x = _resid(x.reshape(b * s, d), h2_all, y_tok, w_tok, 0, mods, None, s, sg_bf, su_bf, sd_bf,
                   final_norm_g, last, tm).reshape(b, s, d)
        if not last:
            xc = _resid(xc.reshape(b * l_ctx, d), h2_all, y_tok, w_tok, b * s, mods, ctx_row, l_ctx,
                        sg_bf, su_bf, sd_bf, final_norm_g, False, l_ctx).reshape(b, l_ctx, d)
    return x
```

```python
import functools
import math

import numpy as np
import jax
import jax.numpy as jnp
from jax import lax
from jax.experimental import pallas as pl
from jax.experimental.pallas import tpu as pltpu

F32 = jnp.float32
BF16 = jnp.bfloat16

DEPTH = 2
GRID_W = 64
EPS = 1e-6
N_MOD = 6

FNET_GROUP_DIM = 64
NA_HEADS = 4
NA_HEAD_DIM = 64
NA_WIN_ROWS = 8
NA_WIN_COLS = 16
CONV_WIDTH = 3
DIFF_HEADS = 4
DIFF_QK_DIM = 32
DIFF_V_DIM = 64
ROPE_BASE = 10000.0
N_BRANCHES = 4
BRANCH_DIM = 256

COL_A, COL_BQ, COL_BK, COL_BV, COL_CB, COL_CC, COL_CX, COL_DQ, COL_DK, COL_DV = range(10)
N_COL_BLOCKS = 10
CB = 256
IN_DIM = N_COL_BLOCKS * CB

N_EXPERTS = 256
TOP_K = 8
N_GROUPS = 8
TOPK_GROUPS = 4
ROUTED_SCALE = 2.5

VMEM_LIMIT_BYTES = 56 * 1024 * 1024
LANES = 128
BF16_SUBLANES = 16
FFT_N1 = 64
NA_QROWS = 8
NA_KROWS = 16
MOE_ROWS = 256


def _params(*sem):
    return pltpu.CompilerParams(dimension_semantics=sem, vmem_limit_bytes=VMEM_LIMIT_BYTES)


def _dot(a, b):
    return jnp.dot(a, b, preferred_element_type=F32)


def _dot_nt(a, b):
    return lax.dot_general(a, b, (((1,), (1,)), ((), ())), preferred_element_type=F32)


def _norm_mod(xf, g, shift, scale):
    y = xf * lax.rsqrt(jnp.mean(xf * xf, axis=-1, keepdims=True) + EPS)
    return (y * g) * (1.0 + scale) + shift


def _silu(v):
    return v * jax.nn.sigmoid(v)


def _ada_kernel(c_ref, w_ref, b_ref, o_ref):
    s = _silu(c_ref[...])
    o_ref[...] = _dot(s.astype(BF16), w_ref[0].astype(BF16)) + b_ref[0]


def _ada(cvec, w, b, layer):
    rows, d = cvec.shape
    depth, _, n = w.shape
    tn = 1536
    return pl.pallas_call(
        _ada_kernel,
        grid=(n // tn,),
        in_specs=[pl.BlockSpec((rows, d), lambda j: (0, 0)),
                  pl.BlockSpec((1, d, tn), lambda j: (layer, 0, j)),
                  pl.BlockSpec((1, 1, tn), lambda j: (layer, 0, j))],
        out_specs=pl.BlockSpec((rows, tn), lambda j: (0, j)),
        out_shape=jax.ShapeDtypeStruct((rows, n), F32),
        compiler_params=_params("arbitrary"),
    )(cvec, w, b.reshape(depth, 1, n))


def _inproj_kernel(x_ref, g_ref, mod_ref, w_ref, wf_ref, cos_ref, s1_ref, s2_ref,
                   p_ref, ur_ref, ui_ref, *, rope):
    mod = mod_ref[0]
    h = _norm_mod(x_ref[0], g_ref[...], mod[0:1], mod[1:2]).astype(BF16)
    for j in range(N_COL_BLOCKS):
        pj = _dot(h, w_ref[:, j * CB:(j + 1) * CB])
        if j == COL_A:
            u = _dot(pj.astype(BF16), wf_ref[...])
            ur_ref[0] = u[:, :CB].astype(BF16)
            ui_ref[0] = u[:, CB:].astype(BF16)
        if rope and j in (COL_DQ, COL_DK):
            cos = jnp.concatenate([cos_ref[...]] * 2, axis=1)
            s1 = jnp.concatenate([s1_ref[...]] * 2, axis=1)
            s2 = jnp.concatenate([s2_ref[...]] * 2, axis=1)
            pj = pj * cos + pltpu.roll(pj, CB - 8, 1) * s1 + pltpu.roll(pj, 8, 1) * s2
            if j == COL_DQ:
                pj = pj * (DIFF_QK_DIM ** -0.5)
        p_ref[0, :, j * CB:(j + 1) * CB] = pj.astype(BF16)


def _inproj(x, g, mods, w_bf, wf, rope_tabs, mod_row, rope, tm):
    b, s, d = x.shape
    mod_idx = (lambda bi, i: (bi, 0, 0)) if mod_row is None else (lambda bi, i: (mod_row, 0, 0))
    tab_spec = pl.BlockSpec((tm, 128), lambda bi, i: (i, 0))
    seq_spec = lambda width: pl.BlockSpec((1, tm, width), lambda bi, i: (bi, i, 0))
    return pl.pallas_call(
        functools.partial(_inproj_kernel, rope=rope),
        grid=(b, s // tm),
        in_specs=[seq_spec(d),
                  pl.BlockSpec((1, d), lambda bi, i: (0, 0)),
                  pl.BlockSpec((1, N_MOD, d), mod_idx),
                  pl.BlockSpec((d, IN_DIM), lambda bi, i: (0, 0)),
                  pl.BlockSpec((CB, 2 * CB), lambda bi, i: (0, 0)),
                  tab_spec, tab_spec, tab_spec],
        out_specs=[seq_spec(IN_DIM), seq_spec(CB), seq_spec(CB)],
        out_shape=[jax.ShapeDtypeStruct((b, s, IN_DIM), BF16),
                   jax.ShapeDtypeStruct((b, s, CB), BF16),
                   jax.ShapeDtypeStruct((b, s, CB), BF16)],
        compiler_params=_params("parallel", "arbitrary"),
    )(x, g.reshape(1, d), mods, w_bf, wf, *rope_tabs)


def _channel_dft_matrix():
    c = np.arange(FNET_GROUP_DIM)
    ang = 2.0 * np.pi * ((c[:, None] * c[None, :]) % FNET_GROUP_DIM) / FNET_GROUP_DIM
    eye = np.eye(CB // FNET_GROUP_DIM)
    m = np.concatenate([np.kron(eye, np.cos(ang)), -np.kron(eye, np.sin(ang))], axis=1)
    return jnp.asarray(m, BF16)


def _rope_tables(s):
    half = DIFF_QK_DIM // 2
    t = jnp.arange(s)
    inv = 1.0 / (ROPE_BASE ** (jnp.arange(0, half, 2, dtype=F32) / half))
    ang_r = (t // GRID_W).astype(F32)[:, None] * inv
    ang_c = (t % GRID_W).astype(F32)[:, None] * inv
    zero = jnp.zeros_like(ang_r)
    cos = jnp.concatenate([jnp.cos(ang_r)] * 2 + [jnp.cos(ang_c)] * 2, axis=1)
    s1 = jnp.concatenate([-jnp.sin(ang_r), zero, -jnp.sin(ang_c), zero], axis=1)
    s2 = jnp.concatenate([zero, jnp.sin(ang_r), zero, jnp.sin(ang_c)], axis=1)
    return tuple(jnp.concatenate([a] * 4, axis=1) for a in (cos, s1, s2))


def _fft1_kernel(ur_ref, ui_ref, w_ref, ct_ref, st_ref, ar_ref, ai_ref):
    n1 = ur_ref.shape[1]
    u = jnp.concatenate([ur_ref[0], ui_ref[0]], axis=0)
    a = _dot(w_ref[...], u)
    ar, ai = a[:n1], a[n1:]
    ct, st = ct_ref[...], st_ref[...]
    ar_ref[0] = (ar * ct + ai * st).astype(BF16)
    ai_ref[0] = (ai * ct - ar * st).astype(BF16)


def _fft2_kernel(ar_ref, ai_ref, w_ref, y_ref, *, norm):
    for j in range(ar_ref.shape[1]):
        a = jnp.concatenate([ar_ref[0, j], ai_ref[0, j]], axis=0)
        y_ref[0, j] = (_dot(w_ref[...], a) * norm).astype(BF16)


def _dft_cos_sin(n):
    k = np.arange(n)
    ang = 2.0 * np.pi * ((k[:, None] * k[None, :]) % n) / n
    return np.cos(ang), np.sin(ang)


def _fourier_latent(ur, ui):
    b, s, cb = ur.shape
    n1, n2 = FFT_N1, s // FFT_N1
    c1, s1 = _dft_cos_sin(n1)
    w1 = jnp.asarray(np.block([[c1, s1], [-s1, c1]]), BF16)
    c2, s2 = _dft_cos_sin(n2)
    w2 = jnp.asarray(np.concatenate([c2, s2], axis=1), BF16)
    tw = 2.0 * np.pi * (np.arange(n1)[:, None] * np.arange(n2)[None, :]) / s
    ct = jnp.broadcast_to(jnp.asarray(np.cos(tw), F32)[:, :, None], (n1, n2, cb)).reshape(n1, n2 * cb)
    st = jnp.broadcast_to(jnp.asarray(np.sin(tw), F32)[:, :, None], (n1, n2, cb)).reshape(n1, n2 * cb)
    lanes = n2 * cb
    tn = min(lanes, 4096)
    u_spec = pl.BlockSpec((1, n1, tn), lambda j, bi: (bi, 0, j))
    t_spec = pl.BlockSpec((n1, tn), lambda j, bi: (0, j))
    ar, ai = pl.pallas_call(
        _fft1_kernel,
        grid=(lanes // tn, b),
        in_specs=[u_spec, u_spec, pl.BlockSpec((2 * n1, 2 * n1), lambda j, bi: (0, 0)), t_spec, t_spec],
        out_specs=[u_spec, u_spec],
        out_shape=[jax.ShapeDtypeStruct((b, n1, lanes), BF16)] * 2,
        compiler_params=_params("arbitrary", "arbitrary"),
    )(ur.reshape(b, n1, lanes), ui.reshape(b, n1, lanes), w1, ct, st)
    kc = 8
    a_spec = pl.BlockSpec((1, kc, n2, cb), lambda bi, j: (bi, j, 0, 0))
    y = pl.pallas_call(
        functools.partial(_fft2_kernel, norm=1.0 / math.sqrt(s * FNET_GROUP_DIM)),
        grid=(b, n1 // kc),
        in_specs=[a_spec, a_spec, pl.BlockSpec((n2, 2 * n2), lambda bi, j: (0, 0))],
        out_specs=a_spec,
        out_shape=jax.ShapeDtypeStruct((b, n1, n2, cb), BF16),
        compiler_params=_params("parallel", "arbitrary"),
    )(ar.reshape(b, n1, n2, cb), ai.reshape(b, n1, n2, cb), w2)
    return jnp.transpose(y, (0, 2, 1, 3)).reshape(b, s, cb)


def _na_kernel(q_ref, k_ref, v_ref, kc_ref, vc_ref, bias_ref, o_ref, *, rows):
    rb = pl.program_id(1)
    kb = jnp.clip(rb * NA_QROWS - NA_WIN_ROWS // 2, 0, rows - NA_KROWS)
    nk = NA_KROWS * GRID_W
    tok0 = pl.multiple_of(kb * GRID_W, 256)
    scale = NA_HEAD_DIM ** -0.5
    outs = []
    for h in range(NA_HEADS):
        sl = slice(h * NA_HEAD_DIM, (h + 1) * NA_HEAD_DIM)
        q = q_ref[0, :, sl]
        s = _dot_nt(q, k_ref[0, pl.ds(tok0, nk), sl]) * scale + bias_ref[0, h]
        sc = _dot_nt(q, kc_ref[0, :, sl]) * scale
        m = jnp.maximum(jnp.max(s, axis=-1, keepdims=True), jnp.max(sc, axis=-1, keepdims=True))
        e = jnp.exp(s - m)
        ec = jnp.exp(sc - m)
        l = jnp.sum(e, axis=-1, keepdims=True) + jnp.sum(ec, axis=-1, keepdims=True)
        o = _dot(e.astype(BF16), v_ref[0, pl.ds(tok0, nk), sl]) + _dot(ec.astype(BF16), vc_ref[0, :, sl])
        outs.append(o / l)
    o_ref[0] = jnp.concatenate(outs, axis=1).astype(BF16)


def _na_bias_tables(rel_bias, rows):
    tabs = []
    cq = np.arange(GRID_W)
    col_lo = np.clip(cq - NA_WIN_COLS // 2, 0, GRID_W - NA_WIN_COLS)
    col_ok = (cq[None, :] >= col_lo[:, None]) & (cq[None, :] < col_lo[:, None] + NA_WIN_COLS)
    dc_idx = np.clip(cq[None, :] - cq[:, None] + NA_WIN_COLS - 1, 0, 2 * NA_WIN_COLS - 2)
    for r0 in (0, NA_QROWS, rows - NA_QROWS):
        kb = int(np.clip(r0 - NA_WIN_ROWS // 2, 0, rows - NA_KROWS))
        r = r0 + np.arange(NA_QROWS)
        rk = kb + np.arange(NA_KROWS)
        start = np.clip(r - NA_WIN_ROWS // 2, 0, rows - NA_WIN_ROWS)
        row_ok = (rk[None, :] >= start[:, None]) & (rk[None, :] < start[:, None] + NA_WIN_ROWS)
        dr_idx = np.clip(rk[None, :] - r[:, None] + NA_WIN_ROWS - 1, 0, 2 * NA_WIN_ROWS - 2)
        ok = row_ok[:, None, :, None] & col_ok[None, :, None, :]
        oh_r = jnp.asarray(dr_idx[:, :, None] == np.arange(2 * NA_WIN_ROWS - 1), F32)
        oh_c = jnp.asarray(dc_idx[:, :, None] == np.arange(2 * NA_WIN_COLS - 1), F32)
        by_row = jnp.einsum('qka,hab->hqkb', oh_r, rel_bias.astype(F32), precision=lax.Precision.HIGHEST)
        vals = jnp.einsum('hqkb,cdb->hqckd', by_row, oh_c, precision=lax.Precision.HIGHEST)
        tab = jnp.where(ok[None], vals, -jnp.inf)
        tabs.append(tab.reshape(NA_HEADS, NA_QROWS * GRID_W, NA_KROWS * GRID_W))
    return jnp.stack(tabs)


def _na_latent(p, pc, bias_tabs):
    b, s, _ = p.shape
    l = pc.shape[1]
    rows = s // GRID_W
    nrb = rows // NA_QROWS
    tq = NA_QROWS * GRID_W
    nk = NA_KROWS * GRID_W
    return pl.pallas_call(
        functools.partial(_na_kernel, rows=rows),
        grid=(b, nrb),
        in_specs=[pl.BlockSpec((1, tq, CB), lambda bi, i: (bi, i, COL_BQ)),
                  pl.BlockSpec((1, s, CB), lambda bi, i: (bi, 0, COL_BK)),
                  pl.BlockSpec((1, s, CB), lambda bi, i: (bi, 0, COL_BV)),
                  pl.BlockSpec((1, l, CB), lambda bi, i: (bi, 0, COL_BK)),
                  pl.BlockSpec((1, l, CB), lambda bi, i: (bi, 0, COL_BV)),
                  pl.BlockSpec((1, NA_HEADS, tq, nk),
                               lambda bi, i: (jnp.minimum(i, 1) + (i == nrb - 1).astype(jnp.int32), 0, 0, 0))],
        out_specs=pl.BlockSpec((1, tq, CB), lambda bi, i: (bi, i, 0)),
        out_shape=jax.ShapeDtypeStruct((b, s, CB), BF16),
        compiler_params=_params("parallel", "arbitrary"),
    )(p, p, p, pc, pc, bias_tabs)


def _diff_lambda(lp, lam_init):
    return (jnp.exp(jnp.sum(lp[0:1] * lp[1:2], axis=-1, keepdims=True))
            - jnp.exp(jnp.sum(lp[2:3] * lp[3:4], axis=-1, keepdims=True)) + lam_init)


def _diff_finish(o, g, lam_init):
    y = o * lax.rsqrt(jnp.mean(o * o, axis=-1, keepdims=True) + EPS)
    return y * g * (1.0 - lam_init)


def _diff_kernel(q_ref, kt_ref, v_ref, lp_ref, g_ref, o_ref, m_sc, l_sc, acc_sc, *, tk, lam_init):
    tq = q_ref.shape[1]
    nk = kt_ref.shape[2] // tk
    n_maps = 2 * DIFF_HEADS
    m_sc[...] = jnp.full(m_sc.shape, -jnp.inf, F32)
    l_sc[...] = jnp.zeros(l_sc.shape, F32)
    acc_sc[...] = jnp.zeros(acc_sc.shape, F32)

    def body(c, carry):
        k0 = pl.multiple_of(c * tk, 128)
        for hm in range(n_maps):
            q = q_ref[0, :, hm * DIFF_QK_DIM:(hm + 1) * DIFF_QK_DIM]
            s = _dot(q, kt_ref[0, hm * DIFF_QK_DIM:(hm + 1) * DIFF_QK_DIM, pl.ds(k0, tk)])
            m_old = m_sc[hm]
            m_new = jnp.maximum(m_old, jnp.max(s, axis=-1, keepdims=True))
            alpha = jnp.exp(m_old - m_new)
            e = jnp.exp(s - m_new[:, :1])
            l_sc[hm] = alpha * l_sc[hm] + jnp.sum(e, axis=-1, keepdims=True)
            pv = _dot(e.astype(BF16), v_ref[0, hm // 2, pl.ds(k0, tk), :])
            acc_sc[hm] = alpha[:, :DIFF_V_DIM] * acc_sc[hm] + pv
            m_sc[hm] = m_new
        return carry

    lax.fori_loop(0, nk, body, 0)
    lam = _diff_lambda(lp_ref[...], lam_init)
    outs = []
    for h in range(DIFF_HEADS):
        o0 = acc_sc[2 * h] / l_sc[2 * h][:, :1]
        o1 = acc_sc[2 * h + 1] / l_sc[2 * h + 1][:, :1]
        outs.append(_diff_finish(o0 - lam * o1, g_ref[...], lam_init))
    o_ref[0] = jnp.concatenate(outs, axis=1).astype(BF16)


def _key_tile(nkeys, cap):
    return max(t for t in range(128, cap + 1, 128) if nkeys % t == 0)


def _diff_latent(p, pc, lp, sub_g, lam_init, tq, tk):
    b, s, _ = p.shape
    l = pc.shape[1]
    nkeys = s + l
    k_all = jnp.concatenate([p[:, :, COL_DK * CB:(COL_DK + 1) * CB], pc[:, :, COL_DK * CB:(COL_DK + 1) * CB]], axis=1)
    v_all = jnp.concatenate([p[:, :, COL_DV * CB:], pc[:, :, COL_DV * CB:]], axis=1)
    kt = jnp.transpose(k_all, (0, 2, 1))
    vh = jnp.transpose(v_all.reshape(b, nkeys, DIFF_HEADS, DIFF_V_DIM), (0, 2, 1, 3))
    n_maps = 2 * DIFF_HEADS
    return pl.pallas_call(
        functools.partial(_diff_kernel, tk=tk, lam_init=lam_init),
        grid=(b, s // tq),
        in_specs=[pl.BlockSpec((1, tq, CB), lambda bi, i: (bi, i, COL_DQ)),
                  pl.BlockSpec((1, CB, nkeys), lambda bi, i: (bi, 0, 0)),
                  pl.BlockSpec((1, DIFF_HEADS, nkeys, DIFF_V_DIM), lambda bi, i: (bi, 0, 0, 0)),
                  pl.BlockSpec((4, DIFF_QK_DIM), lambda bi, i: (0, 0)),
                  pl.BlockSpec((1, DIFF_V_DIM), lambda bi, i: (0, 0))],
        out_specs=pl.BlockSpec((1, tq, CB), lambda bi, i: (bi, i, 0)),
        out_shape=jax.ShapeDtypeStruct((b, s, CB), BF16),
        scratch_shapes=[pltpu.VMEM((n_maps, tq, 128), F32),
                        pltpu.VMEM((n_maps, tq, 128), F32),
                        pltpu.VMEM((n_maps, tq, DIFF_V_DIM), F32)],
        compiler_params=_params("parallel", "arbitrary"),
    )(p, kt, vh, lp, sub_g.reshape(1, DIFF_V_DIM))


def _softmax_rows(s):
    e = jnp.exp(s - jnp.max(s, axis=-1, keepdims=True))
    return e / jnp.sum(e, axis=-1, keepdims=True)


def _ctx_kernel(pc_ref, ur_ref, ui_ref, wf_ref, lp_ref, g_ref, ya_ref, yb_ref, yd_ref, *, lam_init):
    l = pc_ref.shape[1]
    col = lambda j, lo, hi: pc_ref[0, :, j * CB + lo:j * CB + hi]
    u = jnp.concatenate([ur_ref[0], ui_ref[0]], axis=0)
    ya_ref[0] = (_dot(wf_ref[...], u) * (1.0 / math.sqrt(l * FNET_GROUP_DIM))).astype(BF16)
    outs = []
    for h in range(NA_HEADS):
        lo, hi = h * NA_HEAD_DIM, (h + 1) * NA_HEAD_DIM
        pr = _softmax_rows(_dot_nt(col(COL_BQ, lo, hi), col(COL_BK, lo, hi)) * NA_HEAD_DIM ** -0.5)
        outs.append(_dot(pr.astype(BF16), col(COL_BV, lo, hi)))
    yb_ref[0] = jnp.concatenate(outs, axis=1).astype(BF16)
    lam = _diff_lambda(lp_ref[...], lam_init)
    outs = []
    for h in range(DIFF_HEADS):
        pm = []
        for m in range(2):
            lo = (2 * h + m) * DIFF_QK_DIM
            pm.append(_softmax_rows(_dot_nt(col(COL_DQ, lo, lo + DIFF_QK_DIM), col(COL_DK, lo, lo + DIFF_QK_DIM))
                                    * DIFF_QK_DIM ** -0.5))
        a = (pm[0] - lam * pm[1]).astype(BF16)
        o = _dot(a, col(COL_DV, h * DIFF_V_DIM, (h + 1) * DIFF_V_DIM))
        outs.append(_diff_finish(o, g_ref[...], lam_init))
    yd_ref[0] = jnp.concatenate(outs, axis=1).astype(BF16)


def _ctx_branches(pc, ucr, uci, lp, sub_g, lam_init):
    b, l, _ = pc.shape
    c, s = _dft_cos_sin(l)
    wf = jnp.asarray(np.concatenate([c, s], axis=1), BF16)
    y_spec = pl.BlockSpec((1, l, CB), lambda bi: (bi, 0, 0))
    return pl.pallas_call(
        functools.partial(_ctx_kernel, lam_init=lam_init),
        grid=(b,),
        in_specs=[pl.BlockSpec((1, l, IN_DIM), lambda bi: (bi, 0, 0)), y_spec, y_spec,
                  pl.BlockSpec((l, 2 * l), lambda bi: (0, 0)),
                  pl.BlockSpec((4, DIFF_QK_DIM), lambda bi: (0, 0)),
                  pl.BlockSpec((1, DIFF_V_DIM), lambda bi: (0, 0))],
        out_specs=[y_spec] * 3,
        out_shape=[jax.ShapeDtypeStruct((b, l, CB), BF16)] * 3,
        compiler_params=_params("parallel"),
    )(pc, ucr, uci, wf, lp, sub_g.reshape(1, DIFF_V_DIM))


def _merge_kernel(x_ref, mod_ref, g1_ref, g2_ref, ya_ref, yb_ref, yd_ref, pb_ref, pc_ref, px_ref,
                  cp_ref, xp_ref, cn_ref, xn_ref, cw_ref, wg_ref, wb_ref, wo_ref,
                  xo_ref, h2_ref):
    i = pl.program_id(1)
    last = pl.num_programs(1) - 1
    tm = x_ref.shape[1]
    mod = mod_ref[0]
    x = x_ref[0]
    h = _norm_mod(x, g1_ref[...], mod[0:1], mod[1:2]).astype(BF16)

    u = pc_ref[0].astype(F32) * px_ref[0].astype(F32)
    up = cp_ref[0, BF16_SUBLANES - 1:, :].astype(F32) * xp_ref[0, BF16_SUBLANES - 1:, :].astype(F32)
    un = cn_ref[0, :1, :].astype(F32) * xn_ref[0, :1, :].astype(F32)
    up = jnp.where(i == 0, 0.0, up)
    un = jnp.where(i == last, 0.0, un)
    rid = lax.broadcasted_iota(jnp.int32, u.shape, 0)
    u_prev = jnp.where(rid == 0, up, pltpu.roll(u, 1, 0))
    u_next = jnp.where(rid == tm - 1, un, pltpu.roll(u, tm - 1, 0))
    cw = cw_ref[...]
    yc = pb_ref[0].astype(F32) * (cw[0:1] * u_prev + cw[1:2] * u + cw[2:3] * u_next)

    branches = (ya_ref[0], yb_ref[0], yc.astype(BF16), yd_ref[0])
    d = x.shape[1]
    out = None
    for n in range(d // CB):
        cols = slice(n * CB, (n + 1) * CB)
        merged = None
        for j in range(N_BRANCHES):
            t = jax.nn.sigmoid(_dot(h, wg_ref[j, :, cols])) * _dot(branches[j], wb_ref[j, :, cols])
            merged = t if merged is None else merged + t
        t = _dot(merged.astype(BF16), wo_ref[cols, :])
        out = t if out is None else out + t
    xn = x + mod[2:3] * out
    xo_ref[0] = xn
    h2_ref[0] = _norm_mod(xn, g2_ref[...], mod[3:4], mod[4:5]).astype(BF16)


def _merge(x, mods, mod_row, g1, g2, ya, yb, yd, p, conv_w, wg, wb, wo, tm):
    b, s, d = x.shape
    hb = tm // BF16_SUBLANES
    n_halo = s // BF16_SUBLANES
    mod_idx = (lambda bi, i: (bi, 0, 0)) if mod_row is None else (lambda bi, i: (mod_row, 0, 0))
    seq = lambda width, col=0: pl.BlockSpec((1, tm, width), lambda bi, i: (bi, i, col))
    prev = lambda col: pl.BlockSpec((1, BF16_SUBLANES, CB), lambda bi, i: (bi, jnp.maximum(i * hb - 1, 0), col))
    nxt = lambda col: pl.BlockSpec((1, BF16_SUBLANES, CB),
                                   lambda bi, i: (bi, jnp.minimum((i + 1) * hb, n_halo - 1), col))
    const = lambda shape: pl.BlockSpec(shape, lambda bi, i: (0,) * len(shape))
    return pl.pallas_call(
        _merge_kernel,
        grid=(b, s // tm),
        in_specs=[seq(d), pl.BlockSpec((1, N_MOD, d), mod_idx), const((1, d)), const((1, d)),
                  seq(CB), seq(CB), seq(CB),
                  seq(CB, COL_CB), seq(CB, COL_CC), seq(CB, COL_CX),
                  prev(COL_CC), prev(COL_CX), nxt(COL_CC), nxt(COL_CX),
                  const((CONV_WIDTH, CB)),
                  const((N_BRANCHES, d, d)), const((N_BRANCHES, BRANCH_DIM, d)), const((d, d))],
        out_specs=[seq(d), seq(d)],
        out_shape=[jax.ShapeDtypeStruct((b, s, d), F32),
                   jax.ShapeDtypeStruct((b, s, d), BF16)],
        compiler_params=_params("parallel", "arbitrary"),
    )(x, mods, g1.reshape(1, d), g2.reshape(1, d), ya, yb, yd, p, p, p, p, p, p, p,
      conv_w, wg, wb, wo)


def _route_kernel(h2_ref, wr_ref, bias_ref, tri_ref, ones_ref, idx_ref, w_ref, rank_ref, cnt_ref,
                  score_sc, sel_sc, carry_sc):
    i = pl.program_id(0)
    tm = h2_ref.shape[0]
    ne = wr_ref.shape[0]
    gsz = ne // N_GROUPS
    n_chunks = tm // LANES

    @pl.when(i == 0)
    def _():
        carry_sc[...] = jnp.zeros(carry_sc.shape, F32)

    score_sc[...] = jax.nn.sigmoid(_dot_nt(wr_ref[...], h2_ref[...]))

    def select(cidx, carry):
        c0 = pl.multiple_of(cidx * LANES, LANES)
        scores = score_sc[:, pl.ds(c0, LANES)]
        biased = scores + bias_ref[...]
        liota = lax.broadcasted_iota(jnp.int32, (gsz, LANES), 0)
        gs = []
        for g in range(N_GROUPS):
            v = biased[g * gsz:(g + 1) * gsz]
            m1 = jnp.max(v, axis=0, keepdims=True)
            i1 = jnp.min(jnp.where(v == m1, liota, gsz), axis=0, keepdims=True)
            m2 = jnp.max(jnp.where(liota == i1, -jnp.inf, v), axis=0, keepdims=True)
            gs.append(m1 + m2)
        gsm = jnp.concatenate(gs, axis=0)
        giota = lax.broadcasted_iota(jnp.int32, gsm.shape, 0)
        keep = jnp.zeros(gsm.shape, F32)
        for _ in range(TOPK_GROUPS):
            m = jnp.max(gsm, axis=0, keepdims=True)
            gi = jnp.min(jnp.where(gsm == m, giota, N_GROUPS), axis=0, keepdims=True)
            hit = giota == gi
            keep = jnp.where(hit, 1.0, keep)
            gsm = jnp.where(hit, -jnp.inf, gsm)
        cur = jnp.concatenate(
            [jnp.where(jnp.broadcast_to(keep[g:g + 1], (gsz, LANES)) > 0.0, biased[g * gsz:(g + 1) * gsz], -jnp.inf)
             for g in range(N_GROUPS)], axis=0)
        eiota = lax.broadcasted_iota(jnp.int32, (ne, LANES), 0)
        sel = jnp.zeros((ne, LANES), F32)
        idxs, ws = [], []
        for _ in range(TOP_K):
            m = jnp.max(cur, axis=0, keepdims=True)
            ik = jnp.min(jnp.where(cur == m, eiota, ne), axis=0, keepdims=True)
            hit = eiota == ik
            cur = jnp.where(hit, -jnp.inf, cur)
            ws.append(jnp.sum(jnp.where(hit, scores, 0.0), axis=0, keepdims=True))
            idxs.append(ik)
            sel = jnp.where(hit, 1.0, sel)
        w = jnp.concatenate(ws, axis=0)
        w_ref[:, pl.ds(c0, LANES)] = w / jnp.sum(w, axis=0, keepdims=True) * ROUTED_SCALE
        idx_ref[:, pl.ds(c0, LANES)] = jnp.concatenate(idxs, axis=0)
        sel_sc[:, pl.ds(c0, LANES)] = sel.astype(BF16)
        return carry

    lax.fori_loop(0, n_chunks, select, 0)

    sel_all = sel_sc[...]
    score_sc[...] = _dot(sel_all, tri_ref[...]) + jnp.concatenate([carry_sc[...]] * n_chunks, axis=1)

    def ranks(cidx, carry):
        c0 = pl.multiple_of(cidx * LANES, LANES)
        before = score_sc[:, pl.ds(c0, LANES)]
        idx = idx_ref[:, pl.ds(c0, LANES)]
        eiota = lax.broadcasted_iota(jnp.int32, (ne, LANES), 0)
        rows = [jnp.sum(jnp.where(eiota == idx[k:k + 1], before, 0.0), axis=0, keepdims=True)
                for k in range(TOP_K)]
        rank_ref[:, pl.ds(c0, LANES)] = jnp.concatenate(rows, axis=0).astype(jnp.int32)
        return carry

    lax.fori_loop(0, n_chunks, ranks, 0)
    carry_sc[...] += _dot(sel_all, ones_ref[...])
    cnt_ref[...] = carry_sc[...]


def _route(h2_all, wr_t, bias, tm):
    n, d = h2_all.shape
    ne = wr_t.shape[0]
    tri = jnp.asarray(np.triu(np.ones((tm, tm), np.float32), 1), BF16)
    ones = jnp.ones((tm, LANES), BF16)
    bias_b = jnp.broadcast_to(bias.astype(F32)[:, None], (ne, LANES))
    const = lambda shape: pl.BlockSpec(shape, lambda i: (0,) * len(shape))
    tok = pl.BlockSpec((TOP_K, tm), lambda i: (0, i))
    return pl.pallas_call(
        _route_kernel,
        grid=(n // tm,),
        in_specs=[pl.BlockSpec((tm, d), lambda i: (i, 0)), const((ne, d)), const((ne, LANES)),
                  const((tm, tm)), const((tm, LANES))],
        out_specs=[tok, tok, tok, const((ne, LANES))],
        out_shape=[jax.ShapeDtypeStruct((TOP_K, n), jnp.int32),
                   jax.ShapeDtypeStruct((TOP_K, n), F32),
                   jax.ShapeDtypeStruct((TOP_K, n), jnp.int32),
                   jax.ShapeDtypeStruct((ne, LANES), F32)],
        scratch_shapes=[pltpu.VMEM((ne, tm), F32), pltpu.VMEM((ne, tm), BF16), pltpu.VMEM((ne, LANES), F32)],
        compiler_params=_params("arbitrary"),
    )(h2_all, wr_t, bias_b, tri, ones)


def _pos_kernel(idx_ref, rank_ref, start_ref, pos_ref):
    ne = start_ref.shape[0]
    start = start_ref[...]

    def body(cidx, carry):
        c0 = pl.multiple_of(cidx * LANES, LANES)
        idx = idx_ref[:, pl.ds(c0, LANES)]
        eiota = lax.broadcasted_iota(jnp.int32, (ne, LANES), 0)
        rows = [jnp.sum(jnp.where(eiota == idx[k:k + 1], start, 0.0), axis=0, keepdims=True)
                for k in range(TOP_K)]
        pos_ref[:, pl.ds(c0, LANES)] = jnp.concatenate(rows, axis=0).astype(jnp.int32) + rank_ref[:, pl.ds(c0, LANES)]
        return carry

    lax.fori_loop(0, idx_ref.shape[1] // LANES, body, 0)


def _positions(idx, rank, start_rows, tm):
    k, n = idx.shape
    ne = start_rows.shape[0]
    start_b = jnp.broadcast_to(start_rows.astype(F32)[:, None], (ne, LANES))
    tok = pl.BlockSpec((k, tm), lambda i: (0, i))
    return pl.pallas_call(
        _pos_kernel,
        grid=(n // tm,),
        in_specs=[tok, tok, pl.BlockSpec((ne, LANES), lambda i: (0, 0))],
        out_specs=tok,
        out_shape=jax.ShapeDtypeStruct((k, n), jnp.int32),
        compiler_params=_params("parallel"),
    )(idx, rank, start_b)


def _expert_kernel(be_ref, nu_ref, x_ref, wg_ref, wu_ref, wd_ref, y_ref):
    i = pl.program_id(0)

    @pl.when(i < nu_ref[0])
    def _():
        x = x_ref[...]
        g = _dot(x, wg_ref[0, 0].astype(BF16))
        u = _dot(x, wu_ref[0, 0].astype(BF16))
        y_ref[...] = _dot((_silu(g) * u).astype(BF16), wd_ref[0, 0].astype(BF16)).astype(y_ref.dtype)

    @pl.when(i >= nu_ref[0])
    def _():
        y_ref[...] = jnp.zeros(y_ref.shape, y_ref.dtype)


def _experts(x_sorted, block_e, n_used, layer, w_g, w_u, w_d):
    n_rows, d = x_sorted.shape
    f = w_g.shape[3]
    n_blocks = n_rows // MOE_ROWS
    grid_spec = pltpu.PrefetchScalarGridSpec(
        num_scalar_prefetch=2,
        grid=(n_blocks,),
        in_specs=[pl.BlockSpec((MOE_ROWS, d), lambda i, be, nu: (i, 0)),
                  pl.BlockSpec((1, 1, d, f), lambda i, be, nu: (layer, be[i], 0, 0)),
                  pl.BlockSpec((1, 1, d, f), lambda i, be, nu: (layer, be[i], 0, 0)),
                  pl.BlockSpec((1, 1, f, d), lambda i, be, nu: (layer, be[i], 0, 0))],
        out_specs=pl.BlockSpec((MOE_ROWS, d), lambda i, be, nu: (i, 0)))
    return pl.pallas_call(
        _expert_kernel,
        grid_spec=grid_spec,
        out_shape=jax.ShapeDtypeStruct((n_rows, d), BF16),
        compiler_params=_params("arbitrary"),
    )(block_e, n_used, x_sorted, w_g, w_u, w_d)


def _resid_kernel(x_ref, h2_ref, y_ref, w_ref, mod_ref, sg_ref, su_ref, sd_ref, gf_ref, o_ref, *, final):
    h2 = h2_ref[...]
    a = (_silu(_dot(h2, sg_ref[...])) * _dot(h2, su_ref[...])).astype(BF16)
    y = _dot(a, sd_ref[...])
    w = w_ref[...]
    for k in range(TOP_K):
        y = y + w[:, k:k + 1] * y_ref[k].astype(F32)
    xo = x_ref[...] + mod_ref[0][5:6] * y
    if final:
        xo = xo * lax.rsqrt(jnp.mean(xo * xo, axis=-1, keepdims=True) + EPS) * gf_ref[...]
    o_ref[...] = xo


def _resid(x_flat, h2_all, y_tok, w_tok, row_off, mods, mod_row, rows_per_mod, sg, su, sd, gf, final, tm):
    n, d = x_flat.shape
    f = sg.shape[1]
    off = row_off // tm
    per = rows_per_mod // tm
    mod_idx = (lambda i: (i // per, 0, 0)) if mod_row is None else (lambda i: (mod_row, 0, 0))
    const = lambda shape: pl.BlockSpec(shape, lambda i: (0,) * len(shape))
    return pl.pallas_call(
        functools.partial(_resid_kernel, final=final),
        grid=(n // tm,),
        in_specs=[pl.BlockSpec((tm, d), lambda i: (i, 0)),
                  pl.BlockSpec((tm, d), lambda i: (i + off, 0)),
                  pl.BlockSpec((TOP_K, tm, d), lambda i: (0, i + off, 0)),
                  pl.BlockSpec((tm, TOP_K), lambda i: (i + off, 0)),
                  pl.BlockSpec((1, N_MOD, d), mod_idx),
                  const((d, f)), const((d, f)), const((f, d)), const((1, d))],
        out_specs=pl.BlockSpec((tm, d), lambda i: (i, 0)),
        out_shape=jax.ShapeDtypeStruct((n, d), F32),
        compiler_params=_params("parallel"),
    )(x_flat, h2_all, y_tok, w_tok, mods, sg, su, sd, gf.reshape(1, d))


def _moe_routed(h2_all, wr_t, bias, layer, w_g, w_u, w_d):
    n, d = h2_all.shape
    idx, wts, rank, cnt = _route(h2_all, wr_t, bias, 512)
    counts = cnt[:, 0].astype(jnp.int32)
    padded = (counts + MOE_ROWS - 1) // MOE_ROWS * MOE_ROWS
    pad_end = jnp.cumsum(padded)
    n_blocks = (n * TOP_K + N_EXPERTS * (MOE_ROWS - 1)) // MOE_ROWS
    block_e = jnp.minimum(jnp.searchsorted(pad_end, jnp.arange(n_blocks) * MOE_ROWS, side='right'),
                          N_EXPERTS - 1).astype(jnp.int32)
    n_used = (pad_end[-1] // MOE_ROWS).astype(jnp.int32).reshape(1)
    pos = _positions(idx, rank, pad_end - padded, 512).reshape(TOP_K * n)
    tok = jnp.tile(jnp.arange(n, dtype=jnp.int32), TOP_K)
    n_rows = n_blocks * MOE_ROWS
    row_tok = (jnp.arange(n_rows, dtype=jnp.int32) % n).at[pos].set(tok)
    x_sorted = jnp.concatenate([h2_all, h2_all], axis=0).at[row_tok].get(mode='promise_in_bounds')
    y_sorted = _experts(x_sorted, block_e, n_used, layer, w_g, w_u, w_d)
    return y_sorted.at[pos].get(mode='promise_in_bounds').reshape(TOP_K, n, d), wts.T


def kernel(x, c, ctx, c_ctx, ada_w, ada_b, norm1_g, w_in, conv_w, na_rel_bias, diff_lambda,
           diff_subln_g, w_branch_gate, w_branch, w_out, norm2_g, router_w, router_bias,
           expert_w_gate, expert_w_up, expert_w_down, shared_w_gate, shared_w_up, shared_w_down,
           final_norm_g):
    b, s, d = x.shape
    l_ctx = ctx.shape[1]
    rows = s // GRID_W
    ctx_row = b
    cvec = jnp.zeros((8, d), F32).at[:b].set(c).at[ctx_row].set(c_ctx)
    rope_tabs = _rope_tables(s)
    wf = _channel_dft_matrix()
    tm = 512
    xc = ctx
    for layer in range(DEPTH):
        last = layer == DEPTH - 1
        lam_init = 0.8 - 0.6 * math.exp(-0.3 * layer)
        mods = _ada(cvec, ada_w, ada_b, layer).reshape(8, N_MOD, d)
        w_in_bf = w_in[layer].astype(BF16)
        wg_bf = w_branch_gate[layer].astype(BF16)
        wb_bf = w_branch[layer].astype(BF16)
        wo_bf = w_out[layer].astype(BF16)
        wr_bf = router_w[layer].T.astype(BF16)
        lp = diff_lambda[layer]
        sub_g = diff_subln_g[layer]

        p, ur, ui = _inproj(x, norm1_g[layer], mods, w_in_bf, wf, rope_tabs, None, True, tm)
        ctx_tabs = tuple(t[:l_ctx] for t in rope_tabs)
        pc, ucr, uci = _inproj(xc, norm1_g[layer], mods, w_in_bf, wf, ctx_tabs, ctx_row, False, l_ctx)

        ya = _fourier_latent(ur, ui)
        yb = _na_latent(p, pc, _na_bias_tables(na_rel_bias[layer], rows))
        yd = _diff_latent(p, pc, lp, sub_g, lam_init, 256, _key_tile(s + l_ctx, 768))
        x, h2 = _merge(x, mods, None, norm1_g[layer], norm2_g[layer], ya, yb, yd, p,
                       conv_w[layer], wg_bf, wb_bf, wo_bf, tm)
        h2_all = h2.reshape(b * s, d)
        if not last:
            yac, ybc, ydc = _ctx_branches(pc, ucr, uci, lp, sub_g, lam_init)
            xc, h2c = _merge(xc, mods, ctx_row, norm1_g[layer], norm2_g[layer], yac, ybc, ydc, pc,
                             conv_w[layer], wg_bf, wb_bf, wo_bf, l_ctx)
            h2_all = jnp.concatenate([h2_all, h2c.reshape(b * l_ctx, d)], axis=0)

        y_tok, w_tok = _moe_routed(h2_all, wr_bf, router_bias[layer], layer, expert_w_gate,
                                   expert_w_up, expert_w_down)
        sg_bf = shared_w_gate[layer].astype(BF16)
        su_bf = shared_w_up[layer].astype(BF16)
        sd_bf = shared_w_down[layer].astype(BF16)
        x = _resid(x.reshape(b * s, d), h2_all, y_tok, w_tok, 0, mods, None, s, sg_bf, su_bf, sd_bf,
                   final_norm_g, last, tm).reshape(b, s, d)
        if not last:
            xc = _resid(xc.reshape(b * l_ctx, d), h2_all, y_tok, w_tok, b * s, mods, ctx_row, l_ctx,
                        sg_bf, su_bf, sd_bf, final_norm_g, False, l_ctx).reshape(b, l_ctx, d)
    return x
```

```python
import functools
import math

import numpy as np
import jax
import jax.numpy as jnp
from jax import lax
from jax.experimental import pallas as pl
from jax.experimental.pallas import tpu as pltpu

F32 = jnp.float32
BF16 = jnp.bfloat16

DEPTH = 2
GRID_W = 64
EPS = 1e-6
N_MOD = 6

FNET_GROUP_DIM = 64
NA_HEADS = 4
NA_HEAD_DIM = 64
NA_WIN_ROWS = 8
NA_WIN_COLS = 16
CONV_WIDTH = 3
DIFF_HEADS = 4
DIFF_QK_DIM = 32
DIFF_V_DIM = 64
ROPE_BASE = 10000.0
N_BRANCHES = 4
BRANCH_DIM = 256

COL_A, COL_BQ, COL_BK, COL_BV, COL_CB, COL_CC, COL_CX, COL_DQ, COL_DK, COL_DV = range(10)
N_COL_BLOCKS = 10
CB = 256
IN_DIM = N_COL_BLOCKS * CB

N_EXPERTS = 256
TOP_K = 8
N_GROUPS = 8
TOPK_GROUPS = 4
ROUTED_SCALE = 2.5
LOG2_E = 1.4426950408889634

VMEM_LIMIT_BYTES = 56 * 1024 * 1024
LANES = 128
BF16_SUBLANES = 16
FFT_N1 = 64
NA_QROWS = 8
NA_KROWS = 16
MOE_ROWS = 256


def _params(*sem):
    return pltpu.CompilerParams(dimension_semantics=sem, vmem_limit_bytes=VMEM_LIMIT_BYTES)


def _dot(a, b):
    return jnp.dot(a, b, preferred_element_type=F32)


def _dot_nt(a, b):
    return lax.dot_general(a, b, (((1,), (1,)), ((), ())), preferred_element_type=F32)


def _norm_mod(xf, g, shift, scale):
    y = xf * lax.rsqrt(jnp.mean(xf * xf, axis=-1, keepdims=True) + EPS)
    return (y * g) * (1.0 + scale) + shift


def _silu(v):
    return v * jax.nn.sigmoid(v)


def _ada_kernel(c_ref, w_ref, b_ref, o_ref):
    s = _silu(c_ref[...])
    o_ref[...] = _dot(s.astype(BF16), w_ref[0].astype(BF16)) + b_ref[0]


def _ada(cvec, w, b, layer):
    rows, d = cvec.shape
    depth, _, n = w.shape
    tn = 1536
    return pl.pallas_call(
        _ada_kernel,
        grid=(n // tn,),
        in_specs=[pl.BlockSpec((rows, d), lambda j: (0, 0)),
                  pl.BlockSpec((1, d, tn), lambda j: (layer, 0, j)),
                  pl.BlockSpec((1, 1, tn), lambda j: (layer, 0, j))],
        out_specs=pl.BlockSpec((rows, tn), lambda j: (0, j)),
        out_shape=jax.ShapeDtypeStruct((rows, n), F32),
        compiler_params=_params("arbitrary"),
    )(cvec, w, b.reshape(depth, 1, n))


def _inproj_kernel(x_ref, g_ref, mod_ref, w_ref, wf_ref, cos_ref, s1_ref, s2_ref,
                   p_ref, ur_ref, ui_ref, *, rope):
    mod = mod_ref[0]
    h = _norm_mod(x_ref[0], g_ref[...], mod[0:1], mod[1:2]).astype(BF16)
    for j in range(N_COL_BLOCKS):
        pj = _dot(h, w_ref[:, j * CB:(j + 1) * CB])
        if j == COL_A:
            u = _dot(pj.astype(BF16), wf_ref[...])
            ur_ref[0] = u[:, :CB].astype(BF16)
            ui_ref[0] = u[:, CB:].astype(BF16)
        if rope and j in (COL_DQ, COL_DK):
            cos = jnp.concatenate([cos_ref[...]] * 2, axis=1)
            s1 = jnp.concatenate([s1_ref[...]] * 2, axis=1)
            s2 = jnp.concatenate([s2_ref[...]] * 2, axis=1)
            pj = pj * cos + pltpu.roll(pj, CB - 8, 1) * s1 + pltpu.roll(pj, 8, 1) * s2
            if j == COL_DQ:
                pj = pj * (DIFF_QK_DIM ** -0.5 * LOG2_E)
        p_ref[0, :, j * CB:(j + 1) * CB] = pj.astype(BF16)


def _inproj(x, g, mods, w_bf, wf, rope_tabs, mod_row, rope, tm):
    b, s, d = x.shape
    mod_idx = (lambda bi, i: (bi, 0, 0)) if mod_row is None else (lambda bi, i: (mod_row, 0, 0))
    tab_spec = pl.BlockSpec((tm, 128), lambda bi, i: (i, 0))
    seq_spec = lambda width: pl.BlockSpec((1, tm, width), lambda bi, i: (bi, i, 0))
    return pl.pallas_call(
        functools.partial(_inproj_kernel, rope=rope),
        grid=(b, s // tm),
        in_specs=[seq_spec(d),
                  pl.BlockSpec((1, d), lambda bi, i: (0, 0)),
                  pl.BlockSpec((1, N_MOD, d), mod_idx),
                  pl.BlockSpec((d, IN_DIM), lambda bi, i: (0, 0)),
                  pl.BlockSpec((CB, 2 * CB), lambda bi, i: (0, 0)),
                  tab_spec, tab_spec, tab_spec],
        out_specs=[seq_spec(IN_DIM), seq_spec(CB), seq_spec(CB)],
        out_shape=[jax.ShapeDtypeStruct((b, s, IN_DIM), BF16),
                   jax.ShapeDtypeStruct((b, s, CB), BF16),
                   jax.ShapeDtypeStruct((b, s, CB), BF16)],
        compiler_params=_params("parallel", "arbitrary"),
    )(x, g.reshape(1, d), mods, w_bf, wf, *rope_tabs)


def _channel_dft_matrix():
    c = np.arange(FNET_GROUP_DIM)
    ang = 2.0 * np.pi * ((c[:, None] * c[None, :]) % FNET_GROUP_DIM) / FNET_GROUP_DIM
    eye = np.eye(CB // FNET_GROUP_DIM)
    m = np.concatenate([np.kron(eye, np.cos(ang)), -np.kron(eye, np.sin(ang))], axis=1)
    return jnp.asarray(m, BF16)


def _rope_tables(s):
    half = DIFF_QK_DIM // 2
    t = jnp.arange(s)
    inv = 1.0 / (ROPE_BASE ** (jnp.arange(0, half, 2, dtype=F32) / half))
    ang_r = (t // GRID_W).astype(F32)[:, None] * inv
    ang_c = (t % GRID_W).astype(F32)[:, None] * inv
    zero = jnp.zeros_like(ang_r)
    cos = jnp.concatenate([jnp.cos(ang_r)] * 2 + [jnp.cos(ang_c)] * 2, axis=1)
    s1 = jnp.concatenate([-jnp.sin(ang_r), zero, -jnp.sin(ang_c), zero], axis=1)
    s2 = jnp.concatenate([zero, jnp.sin(ang_r), zero, jnp.sin(ang_c)], axis=1)
    return tuple(jnp.concatenate([a] * 4, axis=1) for a in (cos, s1, s2))


def _fft1_kernel(ur_ref, ui_ref, w_ref, ct_ref, st_ref, ar_ref, ai_ref):
    n1 = ur_ref.shape[1]
    u = jnp.concatenate([ur_ref[0], ui_ref[0]], axis=0)
    a = _dot(w_ref[...], u)
    ar, ai = a[:n1], a[n1:]
    ct, st = ct_ref[...], st_ref[...]
    ar_ref[0] = (ar * ct + ai * st).astype(BF16)
    ai_ref[0] = (ai * ct - ar * st).astype(BF16)


def _fft2_kernel(ar_ref, ai_ref, w_ref, y_ref, *, norm):
    for j in range(ar_ref.shape[1]):
        a = jnp.concatenate([ar_ref[0, j], ai_ref[0, j]], axis=0)
        y_ref[0, j] = (_dot(w_ref[...], a) * norm).astype(BF16)


def _dft_cos_sin(n):
    k = np.arange(n)
    ang = 2.0 * np.pi * ((k[:, None] * k[None, :]) % n) / n
    return np.cos(ang), np.sin(ang)


def _fourier_latent(ur, ui):
    b, s, cb = ur.shape
    n1, n2 = FFT_N1, s // FFT_N1
    c1, s1 = _dft_cos_sin(n1)
    w1 = jnp.asarray(np.block([[c1, s1], [-s1, c1]]), BF16)
    c2, s2 = _dft_cos_sin(n2)
    w2 = jnp.asarray(np.concatenate([c2, s2], axis=1), BF16)
    tw = 2.0 * np.pi * (np.arange(n1)[:, None] * np.arange(n2)[None, :]) / s
    ct = jnp.broadcast_to(jnp.asarray(np.cos(tw), F32)[:, :, None], (n1, n2, cb)).reshape(n1, n2 * cb)
    st = jnp.broadcast_to(jnp.asarray(np.sin(tw), F32)[:, :, None], (n1, n2, cb)).reshape(n1, n2 * cb)
    lanes = n2 * cb
    tn = min(lanes, 4096)
    u_spec = pl.BlockSpec((1, n1, tn), lambda j, bi: (bi, 0, j))
    t_spec = pl.BlockSpec((n1, tn), lambda j, bi: (0, j))
    ar, ai = pl.pallas_call(
        _fft1_kernel,
        grid=(lanes // tn, b),
        in_specs=[u_spec, u_spec, pl.BlockSpec((2 * n1, 2 * n1), lambda j, bi: (0, 0)), t_spec, t_spec],
        out_specs=[u_spec, u_spec],
        out_shape=[jax.ShapeDtypeStruct((b, n1, lanes), BF16)] * 2,
        compiler_params=_params("arbitrary", "arbitrary"),
    )(ur.reshape(b, n1, lanes), ui.reshape(b, n1, lanes), w1, ct, st)
    kc = 8
    a_spec = pl.BlockSpec((1, kc, n2, cb), lambda bi, j: (bi, j, 0, 0))
    y = pl.pallas_call(
        functools.partial(_fft2_kernel, norm=1.0 / math.sqrt(s * FNET_GROUP_DIM)),
        grid=(b, n1 // kc),
        in_specs=[a_spec, a_spec, pl.BlockSpec((n2, 2 * n2), lambda bi, j: (0, 0))],
        out_specs=a_spec,
        out_shape=jax.ShapeDtypeStruct((b, n1, n2, cb), BF16),
        compiler_params=_params("parallel", "arbitrary"),
    )(ar.reshape(b, n1, n2, cb), ai.reshape(b, n1, n2, cb), w2)
    return jnp.transpose(y, (0, 2, 1, 3)).reshape(b, s, cb)


def _na_kernel(q_ref, k_ref, v_ref, kc_ref, vc_ref, bias_ref, o_ref, *, rows):
    rb = pl.program_id(1)
    kb = jnp.clip(rb * NA_QROWS - NA_WIN_ROWS // 2, 0, rows - NA_KROWS)
    nk = NA_KROWS * GRID_W
    tok0 = pl.multiple_of(kb * GRID_W, 256)
    scale = NA_HEAD_DIM ** -0.5
    outs = []
    for h in range(NA_HEADS):
        sl = slice(h * NA_HEAD_DIM, (h + 1) * NA_HEAD_DIM)
        q = q_ref[0, :, sl]
        s = _dot_nt(q, k_ref[0, pl.ds(tok0, nk), sl]) * scale + bias_ref[0, h]
        sc = _dot_nt(q, kc_ref[0, :, sl]) * scale
        m = jnp.maximum(jnp.max(s, axis=-1, keepdims=True), jnp.max(sc, axis=-1, keepdims=True))
        e = jnp.exp(s - m)
        ec = jnp.exp(sc - m)
        l = jnp.sum(e, axis=-1, keepdims=True) + jnp.sum(ec, axis=-1, keepdims=True)
        o = _dot(e.astype(BF16), v_ref[0, pl.ds(tok0, nk), sl]) + _dot(ec.astype(BF16), vc_ref[0, :, sl])
        outs.append(o / l)
    o_ref[0] = jnp.concatenate(outs, axis=1).astype(BF16)


def _na_bias_tables(rel_bias, rows):
    tabs = []
    cq = np.arange(GRID_W)
    col_lo = np.clip(cq - NA_WIN_COLS // 2, 0, GRID_W - NA_WIN_COLS)
    col_ok = (cq[None, :] >= col_lo[:, None]) & (cq[None, :] < col_lo[:, None] + NA_WIN_COLS)
    dc_idx = np.clip(cq[None, :] - cq[:, None] + NA_WIN_COLS - 1, 0, 2 * NA_WIN_COLS - 2)
    for r0 in (0, NA_QROWS, rows - NA_QROWS):
        kb = int(np.clip(r0 - NA_WIN_ROWS // 2, 0, rows - NA_KROWS))
        r = r0 + np.arange(NA_QROWS)
        rk = kb + np.arange(NA_KROWS)
        start = np.clip(r - NA_WIN_ROWS // 2, 0, rows - NA_WIN_ROWS)
        row_ok = (rk[None, :] >= start[:, None]) & (rk[None, :] < start[:, None] + NA_WIN_ROWS)
        dr_idx = np.clip(rk[None, :] - r[:, None] + NA_WIN_ROWS - 1, 0, 2 * NA_WIN_ROWS - 2)
        ok = row_ok[:, None, :, None] & col_ok[None, :, None, :]
        oh_r = jnp.asarray(dr_idx[:, :, None] == np.arange(2 * NA_WIN_ROWS - 1), F32)
        oh_c = jnp.asarray(dc_idx[:, :, None] == np.arange(2 * NA_WIN_COLS - 1), F32)
        by_row = jnp.einsum('qka,hab->hqkb', oh_r, rel_bias.astype(F32), precision=lax.Precision.HIGHEST)
        vals = jnp.einsum('hqkb,cdb->hqckd', by_row, oh_c, precision=lax.Precision.HIGHEST)
        tab = jnp.where(ok[None], vals, -jnp.inf)
        tabs.append(tab.reshape(NA_HEADS, NA_QROWS * GRID_W, NA_KROWS * GRID_W))
    return jnp.stack(tabs)


def _na_latent(p, pc, bias_tabs):
    b, s, _ = p.shape
    l = pc.shape[1]
    rows = s // GRID_W
    nrb = rows // NA_QROWS
    tq = NA_QROWS * GRID_W
    nk = NA_KROWS * GRID_W
    return pl.pallas_call(
        functools.partial(_na_kernel, rows=rows),
        grid=(b, nrb),
        in_specs=[pl.BlockSpec((1, tq, CB), lambda bi, i: (bi, i, COL_BQ)),
                  pl.BlockSpec((1, s, CB), lambda bi, i: (bi, 0, COL_BK)),
                  pl.BlockSpec((1, s, CB), lambda bi, i: (bi, 0, COL_BV)),
                  pl.BlockSpec((1, l, CB), lambda bi, i: (bi, 0, COL_BK)),
                  pl.BlockSpec((1, l, CB), lambda bi, i: (bi, 0, COL_BV)),
                  pl.BlockSpec((1, NA_HEADS, tq, nk),
                               lambda bi, i: (jnp.minimum(i, 1) + (i == nrb - 1).astype(jnp.int32), 0, 0, 0))],
        out_specs=pl.BlockSpec((1, tq, CB), lambda bi, i: (bi, i, 0)),
        out_shape=jax.ShapeDtypeStruct((b, s, CB), BF16),
        compiler_params=_params("parallel", "arbitrary"),
    )(p, p, p, pc, pc, bias_tabs)


def _diff_lambda(lp, lam_init):
    return (jnp.exp(jnp.sum(lp[0:1] * lp[1:2], axis=-1, keepdims=True))
            - jnp.exp(jnp.sum(lp[2:3] * lp[3:4], axis=-1, keepdims=True)) + lam_init)


def _diff_finish(o, g, lam_init):
    y = o * lax.rsqrt(jnp.mean(o * o, axis=-1, keepdims=True) + EPS)
    return y * g * (1.0 - lam_init)


def _diff_kernel(q_ref, kt_ref, v_ref, lp_ref, g_ref, o_ref, m_sc, acc_sc, *, tk, lam_init):
    nk = kt_ref.shape[2] // tk
    n_maps = 2 * DIFF_HEADS
    m_sc[...] = jnp.full(m_sc.shape, -jnp.inf, F32)
    acc_sc[...] = jnp.zeros(acc_sc.shape, F32)

    def body(c, carry):
        k0 = pl.multiple_of(c * tk, LANES)
        for hm in range(n_maps):
            dims = slice(hm * DIFF_QK_DIM, (hm + 1) * DIFF_QK_DIM)
            s = _dot(q_ref[0, :, dims], kt_ref[0, dims, pl.ds(k0, tk)])
            m_old = m_sc[hm]
            m_new = jnp.maximum(m_old, jnp.max(s, axis=-1, keepdims=True))
            e = jnp.exp2(s - m_new[:, :1]).astype(BF16)
            acc_sc[hm] = jnp.exp2(m_old - m_new) * acc_sc[hm] + _dot(e, v_ref[0, hm // 2, pl.ds(k0, tk), :])
            m_sc[hm] = m_new
        return carry

    lax.fori_loop(0, nk, body, 0)
    maps = []
    for hm in range(n_maps):
        acc = acc_sc[hm]
        maps.append(acc[:, :DIFF_V_DIM] / acc[:, DIFF_V_DIM:DIFF_V_DIM + 1])
    lam = _diff_lambda(lp_ref[...], lam_init)
    outs = [_diff_finish(maps[2 * h] - lam * maps[2 * h + 1], g_ref[...], lam_init) for h in range(DIFF_HEADS)]
    o_ref[0] = jnp.concatenate(outs, axis=1).astype(BF16)


def _key_tile(nkeys, cap):
    return max(t for t in range(128, cap + 1, 128) if nkeys % t == 0)


def _diff_latent(p, pc, lp, sub_g, lam_init, tq, tk):
    b, s, _ = p.shape
    l = pc.shape[1]
    nkeys = s + l
    k_all = jnp.concatenate([p[:, :, COL_DK * CB:(COL_DK + 1) * CB], pc[:, :, COL_DK * CB:(COL_DK + 1) * CB]], axis=1)
    v_all = jnp.concatenate([p[:, :, COL_DV * CB:], pc[:, :, COL_DV * CB:]], axis=1)
    kt = jnp.transpose(k_all, (0, 2, 1))
    vh = jnp.transpose(v_all.reshape(b, nkeys, DIFF_HEADS, DIFF_V_DIM), (0, 2, 1, 3))
    vh = jnp.concatenate([vh, jnp.ones((b, DIFF_HEADS, nkeys, LANES - DIFF_V_DIM), BF16)], axis=-1)
    return pl.pallas_call(
        functools.partial(_diff_kernel, tk=tk, lam_init=lam_init),
        grid=(b, s // tq),
        in_specs=[pl.BlockSpec((1, tq, CB), lambda bi, i: (bi, i, COL_DQ)),
                  pl.BlockSpec((1, CB, nkeys), lambda bi, i: (bi, 0, 0)),
                  pl.BlockSpec((1, DIFF_HEADS, nkeys, LANES), lambda bi, i: (bi, 0, 0, 0)),
                  pl.BlockSpec((4, DIFF_QK_DIM), lambda bi, i: (0, 0)),
                  pl.BlockSpec((1, DIFF_V_DIM), lambda bi, i: (0, 0))],
        out_specs=pl.BlockSpec((1, tq, CB), lambda bi, i: (bi, i, 0)),
        out_shape=jax.ShapeDtypeStruct((b, s, CB), BF16),
        scratch_shapes=[pltpu.VMEM((2 * DIFF_HEADS, tq, LANES), F32),
                        pltpu.VMEM((2 * DIFF_HEADS, tq, LANES), F32)],
        compiler_params=_params("parallel", "arbitrary"),
    )(p, kt, vh, lp, sub_g.reshape(1, DIFF_V_DIM))


def _softmax_rows(s):
    e = jnp.exp(s - jnp.max(s, axis=-1, keepdims=True))
    return e / jnp.sum(e, axis=-1, keepdims=True)


def _ctx_kernel(pc_ref, ur_ref, ui_ref, wf_ref, lp_ref, g_ref, ya_ref, yb_ref, yd_ref, *, lam_init):
    l = pc_ref.shape[1]
    col = lambda j, lo, hi: pc_ref[0, :, j * CB + lo:j * CB + hi]
    u = jnp.concatenate([ur_ref[0], ui_ref[0]], axis=0)
    ya_ref[0] = (_dot(wf_ref[...], u) * (1.0 / math.sqrt(l * FNET_GROUP_DIM))).astype(BF16)
    outs = []
    for h in range(NA_HEADS):
        lo, hi = h * NA_HEAD_DIM, (h + 1) * NA_HEAD_DIM
        pr = _softmax_rows(_dot_nt(col(COL_BQ, lo, hi), col(COL_BK, lo, hi)) * NA_HEAD_DIM ** -0.5)
        outs.append(_dot(pr.astype(BF16), col(COL_BV, lo, hi)))
    yb_ref[0] = jnp.concatenate(outs, axis=1).astype(BF16)
    lam = _diff_lambda(lp_ref[...], lam_init)
    outs = []
    for h in range(DIFF_HEADS):
        pm = []
        for m in range(2):
            lo = (2 * h + m) * DIFF_QK_DIM
            pm.append(_softmax_rows(_dot_nt(col(COL_DQ, lo, lo + DIFF_QK_DIM), col(COL_DK, lo, lo + DIFF_QK_DIM))
                                    * DIFF_QK_DIM ** -0.5))
        a = (pm[0] - lam * pm[1]).astype(BF16)
        o = _dot(a, col(COL_DV, h * DIFF_V_DIM, (h + 1) * DIFF_V_DIM))
        outs.append(_diff_finish(o, g_ref[...], lam_init))
    yd_ref[0] = jnp.concatenate(outs, axis=1).astype(BF16)


def _ctx_branches(pc, ucr, uci, lp, sub_g, lam_init):
    b, l, _ = pc.shape
    c, s = _dft_cos_sin(l)
    wf = jnp.asarray(np.concatenate([c, s], axis=1), BF16)
    y_spec = pl.BlockSpec((1, l, CB), lambda bi: (bi, 0, 0))
    return pl.pallas_call(
        functools.partial(_ctx_kernel, lam_init=lam_init),
        grid=(b,),
        in_specs=[pl.BlockSpec((1, l, IN_DIM), lambda bi: (bi, 0, 0)), y_spec, y_spec,
                  pl.BlockSpec((l, 2 * l), lambda bi: (0, 0)),
                  pl.BlockSpec((4, DIFF_QK_DIM), lambda bi: (0, 0)),
                  pl.BlockSpec((1, DIFF_V_DIM), lambda bi: (0, 0))],
        out_specs=[y_spec] * 3,
        out_shape=[jax.ShapeDtypeStruct((b, l, CB), BF16)] * 3,
        compiler_params=_params("parallel"),
    )(pc, ucr, uci, wf, lp, sub_g.reshape(1, DIFF_V_DIM))


def _merge_kernel(x_ref, mod_ref, g1_ref, g2_ref, ya_ref, yb_ref, yd_ref, pb_ref, pc_ref, px_ref,
                  cp_ref, xp_ref, cn_ref, xn_ref, cw_ref, wg_ref, wb_ref, wo_ref,
                  xo_ref, h2_ref):
    i = pl.program_id(1)
    last = pl.num_programs(1) - 1
    tm = x_ref.shape[1]
    mod = mod_ref[0]
    x = x_ref[0]
    h = _norm_mod(x, g1_ref[...], mod[0:1], mod[1:2]).astype(BF16)

    u = pc_ref[0].astype(F32) * px_ref[0].astype(F32)
    up = cp_ref[0, BF16_SUBLANES - 1:, :].astype(F32) * xp_ref[0, BF16_SUBLANES - 1:, :].astype(F32)
    un = cn_ref[0, :1, :].astype(F32) * xn_ref[0, :1, :].astype(F32)
    up = jnp.where(i == 0, 0.0, up)
    un = jnp.where(i == last, 0.0, un)
    rid = lax.broadcasted_iota(jnp.int32, u.shape, 0)
    u_prev = jnp.where(rid == 0, up, pltpu.roll(u, 1, 0))
    u_next = jnp.where(rid == tm - 1, un, pltpu.roll(u, tm - 1, 0))
    cw = cw_ref[...]
    yc = pb_ref[0].astype(F32) * (cw[0:1] * u_prev + cw[1:2] * u + cw[2:3] * u_next)

    branches = (ya_ref[0], yb_ref[0], yc.astype(BF16), yd_ref[0])
    d = x.shape[1]
    out = None
    for n in range(d // CB):
        cols = slice(n * CB, (n + 1) * CB)
        merged = None
        for j in range(N_BRANCHES):
            t = jax.nn.sigmoid(_dot(h, wg_ref[j, :, cols])) * _dot(branches[j], wb_ref[j, :, cols])
            merged = t if merged is None else merged + t
        t = _dot(merged.astype(BF16), wo_ref[cols, :])
        out = t if out is None else out + t
    xn = x + mod[2:3] * out
    xo_ref[0] = xn
    h2_ref[0] = _norm_mod(xn, g2_ref[...], mod[3:4], mod[4:5]).astype(BF16)


def _merge(x, mods, mod_row, g1, g2, ya, yb, yd, p, conv_w, wg, wb, wo, tm):
    b, s, d = x.shape
    hb = tm // BF16_SUBLANES
    n_halo = s // BF16_SUBLANES
    mod_idx = (lambda bi, i: (bi, 0, 0)) if mod_row is None else (lambda bi, i: (mod_row, 0, 0))
    seq = lambda width, col=0: pl.BlockSpec((1, tm, width), lambda bi, i: (bi, i, col))
    prev = lambda col: pl.BlockSpec((1, BF16_SUBLANES, CB), lambda bi, i: (bi, jnp.maximum(i * hb - 1, 0), col))
    nxt = lambda col: pl.BlockSpec((1, BF16_SUBLANES, CB),
                                   lambda bi, i: (bi, jnp.minimum((i + 1) * hb, n_halo - 1), col))
    const = lambda shape: pl.BlockSpec(shape, lambda bi, i: (0,) * len(shape))
    return pl.pallas_call(
        _merge_kernel,
        grid=(b, s // tm),
        in_specs=[seq(d), pl.BlockSpec((1, N_MOD, d), mod_idx), const((1, d)), const((1, d)),
                  seq(CB), seq(CB), seq(CB),
                  seq(CB, COL_CB), seq(CB, COL_CC), seq(CB, COL_CX),
                  prev(COL_CC), prev(COL_CX), nxt(COL_CC), nxt(COL_CX),
                  const((CONV_WIDTH, CB)),
                  const((N_BRANCHES, d, d)), const((N_BRANCHES, BRANCH_DIM, d)), const((d, d))],
        out_specs=[seq(d), seq(d)],
        out_shape=[jax.ShapeDtypeStruct((b, s, d), F32),
                   jax.ShapeDtypeStruct((b, s, d), BF16)],
        compiler_params=_params("parallel", "arbitrary"),
    )(x, mods, g1.reshape(1, d), g2.reshape(1, d), ya, yb, yd, p, p, p, p, p, p, p,
      conv_w, wg, wb, wo)


def _route_kernel(h2_ref, wr_ref, bias_ref, tri_ref, ones_ref, idx_ref, w_ref, rank_ref, cnt_ref,
                  score_sc, sel_sc, carry_sc):
    i = pl.program_id(0)
    tm = h2_ref.shape[0]
    ne = wr_ref.shape[0]
    gsz = ne // N_GROUPS
    n_chunks = tm // LANES

    @pl.when(i == 0)
    def _():
        carry_sc[...] = jnp.zeros(carry_sc.shape, F32)

    score_sc[...] = jax.nn.sigmoid(_dot_nt(wr_ref[...], h2_ref[...]))

    def select(cidx, carry):
        c0 = pl.multiple_of(cidx * LANES, LANES)
        scores = score_sc[:, pl.ds(c0, LANES)]
        biased = scores + bias_ref[...]
        liota = lax.broadcasted_iota(jnp.int32, (gsz, LANES), 0)
        gs = []
        for g in range(N_GROUPS):
            v = biased[g * gsz:(g + 1) * gsz]
            m1 = jnp.max(v, axis=0, keepdims=True)
            i1 = jnp.min(jnp.where(v == m1, liota, gsz), axis=0, keepdims=True)
            m2 = jnp.max(jnp.where(liota == i1, -jnp.inf, v), axis=0, keepdims=True)
            gs.append(m1 + m2)
        gsm = jnp.concatenate(gs, axis=0)
        giota = lax.broadcasted_iota(jnp.int32, gsm.shape, 0)
        keep = jnp.zeros(gsm.shape, F32)
        for _ in range(TOPK_GROUPS):
            m = jnp.max(gsm, axis=0, keepdims=True)
            gi = jnp.min(jnp.where(gsm == m, giota, N_GROUPS), axis=0, keepdims=True)
            hit = giota == gi
            keep = jnp.where(hit, 1.0, keep)
            gsm = jnp.where(hit, -jnp.inf, gsm)
        cur = jnp.concatenate(
            [jnp.where(jnp.broadcast_to(keep[g:g + 1], (gsz, LANES)) > 0.0, biased[g * gsz:(g + 1) * gsz], -jnp.inf)
             for g in range(N_GROUPS)], axis=0)
        eiota = lax.broadcasted_iota(jnp.int32, (ne, LANES), 0)
        sel = jnp.zeros((ne, LANES), F32)
        idxs, ws = [], []
        for _ in range(TOP_K):
            m = jnp.max(cur, axis=0, keepdims=True)
            ik = jnp.min(jnp.where(cur == m, eiota, ne), axis=0, keepdims=True)
            hit = eiota == ik
            cur = jnp.where(hit, -jnp.inf, cur)
            ws.append(jnp.sum(jnp.where(hit, scores, 0.0), axis=0, keepdims=True))
            idxs.append(ik)
            sel = jnp.where(hit, 1.0, sel)
        w = jnp.concatenate(ws, axis=0)
        w_ref[:, pl.ds(c0, LANES)] = w / jnp.sum(w, axis=0, keepdims=True) * ROUTED_SCALE
        idx_ref[:, pl.ds(c0, LANES)] = jnp.concatenate(idxs, axis=0)
        sel_sc[:, pl.ds(c0, LANES)] = sel.astype(BF16)
        return carry

    lax.fori_loop(0, n_chunks, select, 0)

    sel_all = sel_sc[...]
    score_sc[...] = _dot(sel_all, tri_ref[...]) + jnp.concatenate([carry_sc[...]] * n_chunks, axis=1)

    def ranks(cidx, carry):
        c0 = pl.multiple_of(cidx * LANES, LANES)
        before = score_sc[:, pl.ds(c0, LANES)]
        idx = idx_ref[:, pl.ds(c0, LANES)]
        eiota = lax.broadcasted_iota(jnp.int32, (ne, LANES), 0)
        rows = [jnp.sum(jnp.where(eiota == idx[k:k + 1], before, 0.0), axis=0, keepdims=True)
                for k in range(TOP_K)]
        rank_ref[:, pl.ds(c0, LANES)] = jnp.concatenate(rows, axis=0).astype(jnp.int32)
        return carry

    lax.fori_loop(0, n_chunks, ranks, 0)
    carry_sc[...] += _dot(sel_all, ones_ref[...])
    cnt_ref[...] = carry_sc[...]


def _route(h2_all, wr_t, bias, tm):
    n, d = h2_all.shape
    ne = wr_t.shape[0]
    tri = jnp.asarray(np.triu(np.ones((tm, tm), np.float32), 1), BF16)
    ones = jnp.ones((tm, LANES), BF16)
    bias_b = jnp.broadcast_to(bias.astype(F32)[:, None], (ne, LANES))
    const = lambda shape: pl.BlockSpec(shape, lambda i: (0,) * len(shape))
    tok = pl.BlockSpec((TOP_K, tm), lambda i: (0, i))
    return pl.pallas_call(
        _route_kernel,
        grid=(n // tm,),
        in_specs=[pl.BlockSpec((tm, d), lambda i: (i, 0)), const((ne, d)), const((ne, LANES)),
                  const((tm, tm)), const((tm, LANES))],
        out_specs=[tok, tok, tok, const((ne, LANES))],
        out_shape=[jax.ShapeDtypeStruct((TOP_K, n), jnp.int32),
                   jax.ShapeDtypeStruct((TOP_K, n), F32),
                   jax.ShapeDtypeStruct((TOP_K, n), jnp.int32),
                   jax.ShapeDtypeStruct((ne, LANES), F32)],
        scratch_shapes=[pltpu.VMEM((ne, tm), F32), pltpu.VMEM((ne, tm), BF16), pltpu.VMEM((ne, LANES), F32)],
        compiler_params=_params("arbitrary"),
    )(h2_all, wr_t, bias_b, tri, ones)


def _pos_kernel(idx_ref, rank_ref, start_ref, pos_ref):
    ne = start_ref.shape[0]
    start = start_ref[...]

    def body(cidx, carry):
        c0 = pl.multiple_of(cidx * LANES, LANES)
        idx = idx_ref[:, pl.ds(c0, LANES)]
        eiota = lax.broadcasted_iota(jnp.int32, (ne, LANES), 0)
        rows = [jnp.sum(jnp.where(eiota == idx[k:k + 1], start, 0.0), axis=0, keepdims=True)
                for k in range(TOP_K)]
        pos_ref[:, pl.ds(c0, LANES)] = jnp.concatenate(rows, axis=0).astype(jnp.int32) + rank_ref[:, pl.ds(c0, LANES)]
        return carry

    lax.fori_loop(0, idx_ref.shape[1] // LANES, body, 0)


def _positions(idx, rank, start_rows, tm):
    k, n = idx.shape
    ne = start_rows.shape[0]
    start_b = jnp.broadcast_to(start_rows.astype(F32)[:, None], (ne, LANES))
    tok = pl.BlockSpec((k, tm), lambda i: (0, i))
    return pl.pallas_call(
        _pos_kernel,
        grid=(n // tm,),
        in_specs=[tok, tok, pl.BlockSpec((ne, LANES), lambda i: (0, 0))],
        out_specs=tok,
        out_shape=jax.ShapeDtypeStruct((k, n), jnp.int32),
        compiler_params=_params("parallel"),
    )(idx, rank, start_b)


def _expert_kernel(be_ref, nu_ref, x_ref, wg_ref, wu_ref, wd_ref, y_ref):
    i = pl.program_id(0)

    @pl.when(i < nu_ref[0])
    def _():
        x = x_ref[...]
        g = _dot(x, wg_ref[0, 0].astype(BF16))
        u = _dot(x, wu_ref[0, 0].astype(BF16))
        y_ref[...] = _dot((_silu(g) * u).astype(BF16), wd_ref[0, 0].astype(BF16)).astype(y_ref.dtype)

    @pl.when(i >= nu_ref[0])
    def _():
        y_ref[...] = jnp.zeros(y_ref.shape, y_ref.dtype)


def _experts(x_sorted, block_e, n_used, layer, w_g, w_u, w_d):
    n_rows, d = x_sorted.shape
    f = w_g.shape[3]
    n_blocks = n_rows // MOE_ROWS
    grid_spec = pltpu.PrefetchScalarGridSpec(
        num_scalar_prefetch=2,
        grid=(n_blocks,),
        in_specs=[pl.BlockSpec((MOE_ROWS, d), lambda i, be, nu: (i, 0)),
                  pl.BlockSpec((1, 1, d, f), lambda i, be, nu: (layer, be[i], 0, 0)),
                  pl.BlockSpec((1, 1, d, f), lambda i, be, nu: (layer, be[i], 0, 0)),
                  pl.BlockSpec((1, 1, f, d), lambda i, be, nu: (layer, be[i], 0, 0))],
        out_specs=pl.BlockSpec((MOE_ROWS, d), lambda i, be, nu: (i, 0)))
    return pl.pallas_call(
        _expert_kernel,
        grid_spec=grid_spec,
        out_shape=jax.ShapeDtypeStruct((n_rows, d), BF16),
        compiler_params=_params("arbitrary"),
    )(block_e, n_used, x_sorted, w_g, w_u, w_d)


def _resid_kernel(x_ref, h2_ref, y_ref, w_ref, mod_ref, sg_ref, su_ref, sd_ref, gf_ref, o_ref, *, final):
    h2 = h2_ref[...]
    a = (_silu(_dot(h2, sg_ref[...])) * _dot(h2, su_ref[...])).astype(BF16)
    y = _dot(a, sd_ref[...])
    w = w_ref[...]
    for k in range(TOP_K):
        y = y + w[:, k:k + 1] * y_ref[k].astype(F32)
    xo = x_ref[...] + mod_ref[0][5:6] * y
    if final:
        xo = xo * lax.rsqrt(jnp.mean(xo * xo, axis=-1, keepdims=True) + EPS) * gf_ref[...]
    o_ref[...] = xo


def _resid(x_flat, h2_all, y_tok, w_tok, row_off, mods, mod_row, rows_per_mod, sg, su, sd, gf, final, tm):
    n, d = x_flat.shape
    f = sg.shape[1]
    off = row_off // tm
    per = rows_per_mod // tm
    mod_idx = (lambda i: (i // per, 0, 0)) if mod_row is None else (lambda i: (mod_row, 0, 0))
    const = lambda shape: pl.BlockSpec(shape, lambda i: (0,) * len(shape))
    return pl.pallas_call(
        functools.partial(_resid_kernel, final=final),
        grid=(n // tm,),
        in_specs=[pl.BlockSpec((tm, d), lambda i: (i, 0)),
                  pl.BlockSpec((tm, d), lambda i: (i + off, 0)),
                  pl.BlockSpec((TOP_K, tm, d), lambda i: (0, i + off, 0)),
                  pl.BlockSpec((tm, TOP_K), lambda i: (i + off, 0)),
                  pl.BlockSpec((1, N_MOD, d), mod_idx),
                  const((d, f)), const((d, f)), const((f, d)), const((1, d))],
        out_specs=pl.BlockSpec((tm, d), lambda i: (i, 0)),
        out_shape=jax.ShapeDtypeStruct((n, d), F32),
        compiler_params=_params("parallel"),
    )(x_flat, h2_all, y_tok, w_tok, mods, sg, su, sd, gf.reshape(1, d))


def _moe_routed(h2_all, wr_t, bias, layer, w_g, w_u, w_d):
    n, d = h2_all.shape
    idx, wts, rank, cnt = _route(h2_all, wr_t, bias, 512)
    counts = cnt[:, 0].astype(jnp.int32)
    padded = (counts + MOE_ROWS - 1) // MOE_ROWS * MOE_ROWS
    pad_end = jnp.cumsum(padded)
    n_blocks = (n * TOP_K + N_EXPERTS * (MOE_ROWS - 1)) // MOE_ROWS
    block_e = jnp.minimum(jnp.searchsorted(pad_end, jnp.arange(n_blocks) * MOE_ROWS, side='right'),
                          N_EXPERTS - 1).astype(jnp.int32)
    n_used = (pad_end[-1] // MOE_ROWS).astype(jnp.int32).reshape(1)
    pos = _positions(idx, rank, pad_end - padded, 512).reshape(TOP_K * n)
    tok = jnp.tile(jnp.arange(n, dtype=jnp.int32), TOP_K)
    n_rows = n_blocks * MOE_ROWS
    row_tok = (jnp.arange(n_rows, dtype=jnp.int32) % n).at[pos].set(tok)
    x_sorted = jnp.concatenate([h2_all, h2_all], axis=0).at[row_tok].get(mode='promise_in_bounds')
    y_sorted = _experts(x_sorted, block_e, n_used, layer, w_g, w_u, w_d)
    return y_sorted.at[pos].get(mode='promise_in_bounds').reshape(TOP_K, n, d), wts.T


def kernel(x, c, ctx, c_ctx, ada_w, ada_b, norm1_g, w_in, conv_w, na_rel_bias, diff_lambda,
           diff_subln_g, w_branch_gate, w_branch, w_out, norm2_g, router_w, router_bias,
           expert_w_gate, expert_w_up, expert_w_down, shared_w_gate, shared_w_up, shared_w_down,
           final_norm_g):
    b, s, d = x.shape
    l_ctx = ctx.shape[1]
    rows = s // GRID_W
    ctx_row = b
    cvec = jnp.zeros((8, d), F32).at[:b].set(c).at[ctx_row].set(c_ctx)
    rope_tabs = _rope_tables(s)
    wf = _channel_dft_matrix()
    tm = 512
    xc = ctx
    for layer in range(DEPTH):
        last = layer == DEPTH - 1
        lam_init = 0.8 - 0.6 * math.exp(-0.3 * layer)
        mods = _ada(cvec, ada_w, ada_b, layer).reshape(8, N_MOD, d)
        w_in_bf = w_in[layer].astype(BF16)
        wg_bf = w_branch_gate[layer].astype(BF16)
        wb_bf = w_branch[layer].astype(BF16)
        wo_bf = w_out[layer].astype(BF16)
        wr_bf = router_w[layer].T.astype(BF16)
        lp = diff_lambda[layer]
        sub_g = diff_subln_g[layer]

        p, ur, ui = _inproj(x, norm1_g[layer], mods, w_in_bf, wf, rope_tabs, None, True, tm)
        ctx_tabs = tuple(t[:l_ctx] for t in rope_tabs)
        pc, ucr, uci = _inproj(xc, norm1_g[layer], mods, w_in_bf, wf, ctx_tabs, ctx_row, False, l_ctx)

        ya = _fourier_latent(ur, ui)
        yb = _na_latent(p, pc, _na_bias_tables(na_rel_bias[layer], rows))
        yd = _diff_latent(p, pc, lp, sub_g, lam_init, 1024, _key_tile(s + l_ctx, 768))
        x, h2 = _merge(x, mods, None, norm1_g[layer], norm2_g[layer], ya, yb, yd, p,
                       conv_w[layer], wg_bf, wb_bf, wo_bf, tm)
        h2_all = h2.reshape(b * s, d)
        if not last:
            yac, ybc, ydc = _ctx_branches(pc, ucr, uci, lp, sub_g, lam_init)
            xc, h2c = _merge(xc, mods, ctx_row, norm1_g[layer], norm2_g[layer], yac, ybc, ydc, pc,
                             conv_w[layer], wg_bf, wb_bf, wo_bf, l_ctx)
            h2_all = jnp.concatenate([h2_all, h2c.reshape(b * l_ctx, d)], axis=0)

        y_tok, w_tok = _moe_routed(h2_all, wr_bf, router_bias[layer], layer, expert_w_gate,
                                   expert_w_up, expert_w_down)
        sg_bf = shared_w_gate[layer].astype(BF16)
        su_bf = shared_w_up[layer].astype(BF16)
        sd_bf = shared_w_down[layer].astype(BF16)
        x = _resid(x.reshape(b * s, d), h2_all, y_tok, w_tok, 0, mods, None, s, sg_bf, su_bf, sd_bf,
                   final_norm_g, last, tm).reshape(b, s, d)
        if not last:
            xc = _resid(xc.reshape(b * l_ctx, d), h2_all, y_tok, w_tok, b * s, mods, ctx_row, l_ctx,
                        sg_bf, su_bf, sd_bf, final_norm_g, False, l_ctx).reshape(b, l_ctx, d)
    return x
```

```python
import functools
import math

import numpy as np
import jax
import jax.numpy as jnp
from jax import lax
from jax.experimental import pallas as pl
from jax.experimental.pallas import tpu as pltpu

F32 = jnp.float32
BF16 = jnp.bfloat16

DEPTH = 2
GRID_W = 64
EPS = 1e-6
N_MOD = 6

FNET_GROUP_DIM = 64
NA_HEADS = 4
NA_HEAD_DIM = 64
NA_WIN_ROWS = 8
NA_WIN_COLS = 16
CONV_WIDTH = 3
DIFF_HEADS = 4
DIFF_QK_DIM = 32
DIFF_V_DIM = 64
ROPE_BASE = 10000.0
N_BRANCHES = 4
BRANCH_DIM = 256

COL_A, COL_BQ, COL_BK, COL_BV, COL_CB, COL_CC, COL_CX, COL_DQ, COL_DK, COL_DV = range(10)
N_COL_BLOCKS = 10
CB = 256
IN_DIM = N_COL_BLOCKS * CB

N_EXPERTS = 256
TOP_K = 8
N_GROUPS = 8
TOPK_GROUPS = 4
ROUTED_SCALE = 2.5
LOG2_E = 1.4426950408889634

VMEM_LIMIT_BYTES = 56 * 1024 * 1024
LANES = 128
BF16_SUBLANES = 16
FFT_N1 = 64
NA_QROWS = 8
NA_KROWS = 16
MOE_ROWS = 256
MOE_ALIGN = BF16_SUBLANES


def _params(*sem):
    return pltpu.CompilerParams(dimension_semantics=sem, vmem_limit_bytes=VMEM_LIMIT_BYTES)


def _dot(a, b):
    return jnp.dot(a, b, preferred_element_type=F32)


def _dot_nt(a, b):
    return lax.dot_general(a, b, (((1,), (1,)), ((), ())), preferred_element_type=F32)


def _norm_mod(xf, g, shift, scale):
    y = xf * lax.rsqrt(jnp.mean(xf * xf, axis=-1, keepdims=True) + EPS)
    return (y * g) * (1.0 + scale) + shift


def _silu(v):
    return v * jax.nn.sigmoid(v)


def _ada_kernel(c_ref, w_ref, b_ref, o_ref):
    s = _silu(c_ref[...])
    o_ref[...] = _dot(s.astype(BF16), w_ref[0].astype(BF16)) + b_ref[0]


def _ada(cvec, w, b, layer):
    rows, d = cvec.shape
    depth, _, n = w.shape
    tn = 1536
    return pl.pallas_call(
        _ada_kernel,
        grid=(n // tn,),
        in_specs=[pl.BlockSpec((rows, d), lambda j: (0, 0)),
                  pl.BlockSpec((1, d, tn), lambda j: (layer, 0, j)),
                  pl.BlockSpec((1, 1, tn), lambda j: (layer, 0, j))],
        out_specs=pl.BlockSpec((rows, tn), lambda j: (0, j)),
        out_shape=jax.ShapeDtypeStruct((rows, n), F32),
        compiler_params=_params("arbitrary"),
    )(cvec, w, b.reshape(depth, 1, n))


def _inproj_kernel(x_ref, g_ref, mod_ref, w_ref, wf_ref, cos_ref, s1_ref, s2_ref,
                   p_ref, ur_ref, ui_ref, *, rope):
    mod = mod_ref[0]
    h = _norm_mod(x_ref[0], g_ref[...], mod[0:1], mod[1:2]).astype(BF16)
    for j in range(N_COL_BLOCKS):
        pj = _dot(h, w_ref[:, j * CB:(j + 1) * CB])
        if j == COL_A:
            u = _dot(pj.astype(BF16), wf_ref[...])
            ur_ref[0] = u[:, :CB].astype(BF16)
            ui_ref[0] = u[:, CB:].astype(BF16)
        if rope and j in (COL_DQ, COL_DK):
            cos = jnp.concatenate([cos_ref[...]] * 2, axis=1)
            s1 = jnp.concatenate([s1_ref[...]] * 2, axis=1)
            s2 = jnp.concatenate([s2_ref[...]] * 2, axis=1)
            pj = pj * cos + pltpu.roll(pj, CB - 8, 1) * s1 + pltpu.roll(pj, 8, 1) * s2
            if j == COL_DQ:
                pj = pj * (DIFF_QK_DIM ** -0.5 * LOG2_E)
        p_ref[0, :, j * CB:(j + 1) * CB] = pj.astype(BF16)


def _inproj(x, g, mods, w_bf, wf, rope_tabs, mod_row, rope, tm):
    b, s, d = x.shape
    mod_idx = (lambda bi, i: (bi, 0, 0)) if mod_row is None else (lambda bi, i: (mod_row, 0, 0))
    tab_spec = pl.BlockSpec((tm, 128), lambda bi, i: (i, 0))
    seq_spec = lambda width: pl.BlockSpec((1, tm, width), lambda bi, i: (bi, i, 0))
    return pl.pallas_call(
        functools.partial(_inproj_kernel, rope=rope),
        grid=(b, s // tm),
        in_specs=[seq_spec(d),
                  pl.BlockSpec((1, d), lambda bi, i: (0, 0)),
                  pl.BlockSpec((1, N_MOD, d), mod_idx),
                  pl.BlockSpec((d, IN_DIM), lambda bi, i: (0, 0)),
                  pl.BlockSpec((CB, 2 * CB), lambda bi, i: (0, 0)),
                  tab_spec, tab_spec, tab_spec],
        out_specs=[seq_spec(IN_DIM), seq_spec(CB), seq_spec(CB)],
        out_shape=[jax.ShapeDtypeStruct((b, s, IN_DIM), BF16),
                   jax.ShapeDtypeStruct((b, s, CB), BF16),
                   jax.ShapeDtypeStruct((b, s, CB), BF16)],
        compiler_params=_params("parallel", "arbitrary"),
    )(x, g.reshape(1, d), mods, w_bf, wf, *rope_tabs)


def _channel_dft_matrix():
    c = np.arange(FNET_GROUP_DIM)
    ang = 2.0 * np.pi * ((c[:, None] * c[None, :]) % FNET_GROUP_DIM) / FNET_GROUP_DIM
    eye = np.eye(CB // FNET_GROUP_DIM)
    m = np.concatenate([np.kron(eye, np.cos(ang)), -np.kron(eye, np.sin(ang))], axis=1)
    return jnp.asarray(m, BF16)


def _rope_tables(s):
    half = DIFF_QK_DIM // 2
    t = jnp.arange(s)
    inv = 1.0 / (ROPE_BASE ** (jnp.arange(0, half, 2, dtype=F32) / half))
    ang_r = (t // GRID_W).astype(F32)[:, None] * inv
    ang_c = (t % GRID_W).astype(F32)[:, None] * inv
    zero = jnp.zeros_like(ang_r)
    cos = jnp.concatenate([jnp.cos(ang_r)] * 2 + [jnp.cos(ang_c)] * 2, axis=1)
    s1 = jnp.concatenate([-jnp.sin(ang_r), zero, -jnp.sin(ang_c), zero], axis=1)
    s2 = jnp.concatenate([zero, jnp.sin(ang_r), zero, jnp.sin(ang_c)], axis=1)
    return tuple(jnp.concatenate([a] * 4, axis=1) for a in (cos, s1, s2))


def _fft1_kernel(ur_ref, ui_ref, w_ref, ct_ref, st_ref, ar_ref, ai_ref):
    n1 = ur_ref.shape[1]
    u = jnp.concatenate([ur_ref[0], ui_ref[0]], axis=0)
    a = _dot(w_ref[...], u)
    ar, ai = a[:n1], a[n1:]
    ct, st = ct_ref[...], st_ref[...]
    ar_ref[0] = (ar * ct + ai * st).astype(BF16)
    ai_ref[0] = (ai * ct - ar * st).astype(BF16)


def _fft2_kernel(ar_ref, ai_ref, w_ref, y_ref, *, norm):
    for j in range(ar_ref.shape[1]):
        a = jnp.concatenate([ar_ref[0, j], ai_ref[0, j]], axis=0)
        y_ref[0, j] = (_dot(w_ref[...], a) * norm).astype(BF16)


def _dft_cos_sin(n):
    k = np.arange(n)
    ang = 2.0 * np.pi * ((k[:, None] * k[None, :]) % n) / n
    return np.cos(ang), np.sin(ang)


def _fourier_latent(ur, ui):
    b, s, cb = ur.shape
    n1, n2 = FFT_N1, s // FFT_N1
    c1, s1 = _dft_cos_sin(n1)
    w1 = jnp.asarray(np.block([[c1, s1], [-s1, c1]]), BF16)
    c2, s2 = _dft_cos_sin(n2)
    w2 = jnp.asarray(np.concatenate([c2, s2], axis=1), BF16)
    tw = 2.0 * np.pi * (np.arange(n1)[:, None] * np.arange(n2)[None, :]) / s
    ct = jnp.broadcast_to(jnp.asarray(np.cos(tw), F32)[:, :, None], (n1, n2, cb)).reshape(n1, n2 * cb)
    st = jnp.broadcast_to(jnp.asarray(np.sin(tw), F32)[:, :, None], (n1, n2, cb)).reshape(n1, n2 * cb)
    lanes = n2 * cb
    tn = min(lanes, 4096)
    u_spec = pl.BlockSpec((1, n1, tn), lambda j, bi: (bi, 0, j))
    t_spec = pl.BlockSpec((n1, tn), lambda j, bi: (0, j))
    ar, ai = pl.pallas_call(
        _fft1_kernel,
        grid=(lanes // tn, b),
        in_specs=[u_spec, u_spec, pl.BlockSpec((2 * n1, 2 * n1), lambda j, bi: (0, 0)), t_spec, t_spec],
        out_specs=[u_spec, u_spec],
        out_shape=[jax.ShapeDtypeStruct((b, n1, lanes), BF16)] * 2,
        compiler_params=_params("arbitrary", "arbitrary"),
    )(ur.reshape(b, n1, lanes), ui.reshape(b, n1, lanes), w1, ct, st)
    kc = 8
    a_spec = pl.BlockSpec((1, kc, n2, cb), lambda bi, j: (bi, j, 0, 0))
    y = pl.pallas_call(
        functools.partial(_fft2_kernel, norm=1.0 / math.sqrt(s * FNET_GROUP_DIM)),
        grid=(b, n1 // kc),
        in_specs=[a_spec, a_spec, pl.BlockSpec((n2, 2 * n2), lambda bi, j: (0, 0))],
        out_specs=a_spec,
        out_shape=jax.ShapeDtypeStruct((b, n1, n2, cb), BF16),
        compiler_params=_params("parallel", "arbitrary"),
    )(ar.reshape(b, n1, n2, cb), ai.reshape(b, n1, n2, cb), w2)
    return jnp.transpose(y, (0, 2, 1, 3)).reshape(b, s, cb)


def _na_kernel(q_ref, k_ref, v_ref, kc_ref, vc_ref, bias_ref, o_ref, *, rows):
    rb = pl.program_id(1)
    kb = jnp.clip(rb * NA_QROWS - NA_WIN_ROWS // 2, 0, rows - NA_KROWS)
    nk = NA_KROWS * GRID_W
    tok0 = pl.multiple_of(kb * GRID_W, 256)
    scale = NA_HEAD_DIM ** -0.5
    outs = []
    for h in range(NA_HEADS):
        sl = slice(h * NA_HEAD_DIM, (h + 1) * NA_HEAD_DIM)
        q = q_ref[0, :, sl]
        s = _dot_nt(q, k_ref[0, pl.ds(tok0, nk), sl]) * scale + bias_ref[0, h]
        sc = _dot_nt(q, kc_ref[0, :, sl]) * scale
        m = jnp.maximum(jnp.max(s, axis=-1, keepdims=True), jnp.max(sc, axis=-1, keepdims=True))
        e = jnp.exp(s - m)
        ec = jnp.exp(sc - m)
        l = jnp.sum(e, axis=-1, keepdims=True) + jnp.sum(ec, axis=-1, keepdims=True)
        o = _dot(e.astype(BF16), v_ref[0, pl.ds(tok0, nk), sl]) + _dot(ec.astype(BF16), vc_ref[0, :, sl])
        outs.append(o / l)
    o_ref[0] = jnp.concatenate(outs, axis=1).astype(BF16)


def _na_bias_tables(rel_bias, rows):
    tabs = []
    cq = np.arange(GRID_W)
    col_lo = np.clip(cq - NA_WIN_COLS // 2, 0, GRID_W - NA_WIN_COLS)
    col_ok = (cq[None, :] >= col_lo[:, None]) & (cq[None, :] < col_lo[:, None] + NA_WIN_COLS)
    dc_idx = np.clip(cq[None, :] - cq[:, None] + NA_WIN_COLS - 1, 0, 2 * NA_WIN_COLS - 2)
    for r0 in (0, NA_QROWS, rows - NA_QROWS):
        kb = int(np.clip(r0 - NA_WIN_ROWS // 2, 0, rows - NA_KROWS))
        r = r0 + np.arange(NA_QROWS)
        rk = kb + np.arange(NA_KROWS)
        start = np.clip(r - NA_WIN_ROWS // 2, 0, rows - NA_WIN_ROWS)
        row_ok = (rk[None, :] >= start[:, None]) & (rk[None, :] < start[:, None] + NA_WIN_ROWS)
        dr_idx = np.clip(rk[None, :] - r[:, None] + NA_WIN_ROWS - 1, 0, 2 * NA_WIN_ROWS - 2)
        ok = row_ok[:, None, :, None] & col_ok[None, :, None, :]
        oh_r = jnp.asarray(dr_idx[:, :, None] == np.arange(2 * NA_WIN_ROWS - 1), F32)
        oh_c = jnp.asarray(dc_idx[:, :, None] == np.arange(2 * NA_WIN_COLS - 1), F32)
        by_row = jnp.einsum('qka,hab->hqkb', oh_r, rel_bias.astype(F32), precision=lax.Precision.HIGHEST)
        vals = jnp.einsum('hqkb,cdb->hqckd', by_row, oh_c, precision=lax.Precision.HIGHEST)
        tab = jnp.where(ok[None], vals, -jnp.inf)
        tabs.append(tab.reshape(NA_HEADS, NA_QROWS * GRID_W, NA_KROWS * GRID_W))
    return jnp.stack(tabs)


def _na_latent(p, pc, bias_tabs):
    b, s, _ = p.shape
    l = pc.shape[1]
    rows = s // GRID_W
    nrb = rows // NA_QROWS
    tq = NA_QROWS * GRID_W
    nk = NA_KROWS * GRID_W
    return pl.pallas_call(
        functools.partial(_na_kernel, rows=rows),
        grid=(b, nrb),
        in_specs=[pl.BlockSpec((1, tq, CB), lambda bi, i: (bi, i, COL_BQ)),
                  pl.BlockSpec((1, s, CB), lambda bi, i: (bi, 0, COL_BK)),
                  pl.BlockSpec((1, s, CB), lambda bi, i: (bi, 0, COL_BV)),
                  pl.BlockSpec((1, l, CB), lambda bi, i: (bi, 0, COL_BK)),
                  pl.BlockSpec((1, l, CB), lambda bi, i: (bi, 0, COL_BV)),
                  pl.BlockSpec((1, NA_HEADS, tq, nk),
                               lambda bi, i: (jnp.minimum(i, 1) + (i == nrb - 1).astype(jnp.int32), 0, 0, 0))],
        out_specs=pl.BlockSpec((1, tq, CB), lambda bi, i: (bi, i, 0)),
        out_shape=jax.ShapeDtypeStruct((b, s, CB), BF16),
        compiler_params=_params("parallel", "arbitrary"),
    )(p, p, p, pc, pc, bias_tabs)


def _diff_lambda(lp, lam_init):
    return (jnp.exp(jnp.sum(lp[0:1] * lp[1:2], axis=-1, keepdims=True))
            - jnp.exp(jnp.sum(lp[2:3] * lp[3:4], axis=-1, keepdims=True)) + lam_init)


def _diff_finish(o, g, lam_init):
    y = o * lax.rsqrt(jnp.mean(o * o, axis=-1, keepdims=True) + EPS)
    return y * g * (1.0 - lam_init)


def _diff_kernel(q_ref, kt_ref, v_ref, lp_ref, g_ref, o_ref, m_sc, acc_sc, *, tk, lam_init):
    nk = kt_ref.shape[2] // tk
    n_maps = 2 * DIFF_HEADS
    m_sc[...] = jnp.full(m_sc.shape, -jnp.inf, F32)
    acc_sc[...] = jnp.zeros(acc_sc.shape, F32)

    def body(c, carry):
        k0 = pl.multiple_of(c * tk, LANES)
        for hm in range(n_maps):
            dims = slice(hm * DIFF_QK_DIM, (hm + 1) * DIFF_QK_DIM)
            s = _dot(q_ref[0, :, dims], kt_ref[0, dims, pl.ds(k0, tk)])
            m_old = m_sc[hm]
            m_new = jnp.maximum(m_old, jnp.max(s, axis=-1, keepdims=True))
            e = jnp.exp2(s - m_new[:, :1]).astype(BF16)
            acc_sc[hm] = jnp.exp2(m_old - m_new) * acc_sc[hm] + _dot(e, v_ref[0, hm // 2, pl.ds(k0, tk), :])
            m_sc[hm] = m_new
        return carry

    lax.fori_loop(0, nk, body, 0)
    maps = []
    for hm in range(n_maps):
        acc = acc_sc[hm]
        maps.append(acc[:, :DIFF_V_DIM] / acc[:, DIFF_V_DIM:DIFF_V_DIM + 1])
    lam = _diff_lambda(lp_ref[...], lam_init)
    outs = [_diff_finish(maps[2 * h] - lam * maps[2 * h + 1], g_ref[...], lam_init) for h in range(DIFF_HEADS)]
    o_ref[0] = jnp.concatenate(outs, axis=1).astype(BF16)


def _key_tile(nkeys, cap):
    return max(t for t in range(128, cap + 1, 128) if nkeys % t == 0)


def _diff_latent(p, pc, lp, sub_g, lam_init, tq, tk):
    b, s, _ = p.shape
    l = pc.shape[1]
    nkeys = s + l
    k_all = jnp.concatenate([p[:, :, COL_DK * CB:(COL_DK + 1) * CB], pc[:, :, COL_DK * CB:(COL_DK + 1) * CB]], axis=1)
    v_all = jnp.concatenate([p[:, :, COL_DV * CB:], pc[:, :, COL_DV * CB:]], axis=1)
    kt = jnp.transpose(k_all, (0, 2, 1))
    vh = jnp.transpose(v_all.reshape(b, nkeys, DIFF_HEADS, DIFF_V_DIM), (0, 2, 1, 3))
    vh = jnp.concatenate([vh, jnp.ones((b, DIFF_HEADS, nkeys, LANES - DIFF_V_DIM), BF16)], axis=-1)
    return pl.pallas_call(
        functools.partial(_diff_kernel, tk=tk, lam_init=lam_init),
        grid=(b, s // tq),
        in_specs=[pl.BlockSpec((1, tq, CB), lambda bi, i: (bi, i, COL_DQ)),
                  pl.BlockSpec((1, CB, nkeys), lambda bi, i: (bi, 0, 0)),
                  pl.BlockSpec((1, DIFF_HEADS, nkeys, LANES), lambda bi, i: (bi, 0, 0, 0)),
                  pl.BlockSpec((4, DIFF_QK_DIM), lambda bi, i: (0, 0)),
                  pl.BlockSpec((1, DIFF_V_DIM), lambda bi, i: (0, 0))],
        out_specs=pl.BlockSpec((1, tq, CB), lambda bi, i: (bi, i, 0)),
        out_shape=jax.ShapeDtypeStruct((b, s, CB), BF16),
        scratch_shapes=[pltpu.VMEM((2 * DIFF_HEADS, tq, LANES), F32),
                        pltpu.VMEM((2 * DIFF_HEADS, tq, LANES), F32)],
        compiler_params=_params("parallel", "arbitrary"),
    )(p, kt, vh, lp, sub_g.reshape(1, DIFF_V_DIM))


def _softmax_rows(s):
    e = jnp.exp(s - jnp.max(s, axis=-1, keepdims=True))
    return e / jnp.sum(e, axis=-1, keepdims=True)


def _ctx_kernel(pc_ref, ur_ref, ui_ref, wf_ref, lp_ref, g_ref, ya_ref, yb_ref, yd_ref, *, lam_init):
    l = pc_ref.shape[1]
    col = lambda j, lo, hi: pc_ref[0, :, j * CB + lo:j * CB + hi]
    u = jnp.concatenate([ur_ref[0], ui_ref[0]], axis=0)
    ya_ref[0] = (_dot(wf_ref[...], u) * (1.0 / math.sqrt(l * FNET_GROUP_DIM))).astype(BF16)
    outs = []
    for h in range(NA_HEADS):
        lo, hi = h * NA_HEAD_DIM, (h + 1) * NA_HEAD_DIM
        pr = _softmax_rows(_dot_nt(col(COL_BQ, lo, hi), col(COL_BK, lo, hi)) * NA_HEAD_DIM ** -0.5)
        outs.append(_dot(pr.astype(BF16), col(COL_BV, lo, hi)))
    yb_ref[0] = jnp.concatenate(outs, axis=1).astype(BF16)
    lam = _diff_lambda(lp_ref[...], lam_init)
    outs = []
    for h in range(DIFF_HEADS):
        pm = []
        for m in range(2):
            lo = (2 * h + m) * DIFF_QK_DIM
            pm.append(_softmax_rows(_dot_nt(col(COL_DQ, lo, lo + DIFF_QK_DIM), col(COL_DK, lo, lo + DIFF_QK_DIM))
                                    * DIFF_QK_DIM ** -0.5))
        a = (pm[0] - lam * pm[1]).astype(BF16)
        o = _dot(a, col(COL_DV, h * DIFF_V_DIM, (h + 1) * DIFF_V_DIM))
        outs.append(_diff_finish(o, g_ref[...], lam_init))
    yd_ref[0] = jnp.concatenate(outs, axis=1).astype(BF16)


def _ctx_branches(pc, ucr, uci, lp, sub_g, lam_init):
    b, l, _ = pc.shape
    c, s = _dft_cos_sin(l)
    wf = jnp.asarray(np.concatenate([c, s], axis=1), BF16)
    y_spec = pl.BlockSpec((1, l, CB), lambda bi: (bi, 0, 0))
    return pl.pallas_call(
        functools.partial(_ctx_kernel, lam_init=lam_init),
        grid=(b,),
        in_specs=[pl.BlockSpec((1, l, IN_DIM), lambda bi: (bi, 0, 0)), y_spec, y_spec,
                  pl.BlockSpec((l, 2 * l), lambda bi: (0, 0)),
                  pl.BlockSpec((4, DIFF_QK_DIM), lambda bi: (0, 0)),
                  pl.BlockSpec((1, DIFF_V_DIM), lambda bi: (0, 0))],
        out_specs=[y_spec] * 3,
        out_shape=[jax.ShapeDtypeStruct((b, l, CB), BF16)] * 3,
        compiler_params=_params("parallel"),
    )(pc, ucr, uci, wf, lp, sub_g.reshape(1, DIFF_V_DIM))


def _merge_kernel(x_ref, mod_ref, g1_ref, g2_ref, ya_ref, yb_ref, yd_ref, pb_ref, pc_ref, px_ref,
                  cp_ref, xp_ref, cn_ref, xn_ref, cw_ref, wg_ref, wb_ref, wo_ref,
                  xo_ref, h2_ref):
    i = pl.program_id(1)
    last = pl.num_programs(1) - 1
    tm = x_ref.shape[1]
    mod = mod_ref[0]
    x = x_ref[0]
    h = _norm_mod(x, g1_ref[...], mod[0:1], mod[1:2]).astype(BF16)

    u = pc_ref[0].astype(F32) * px_ref[0].astype(F32)
    up = cp_ref[0, BF16_SUBLANES - 1:, :].astype(F32) * xp_ref[0, BF16_SUBLANES - 1:, :].astype(F32)
    un = cn_ref[0, :1, :].astype(F32) * xn_ref[0, :1, :].astype(F32)
    up = jnp.where(i == 0, 0.0, up)
    un = jnp.where(i == last, 0.0, un)
    rid = lax.broadcasted_iota(jnp.int32, u.shape, 0)
    u_prev = jnp.where(rid == 0, up, pltpu.roll(u, 1, 0))
    u_next = jnp.where(rid == tm - 1, un, pltpu.roll(u, tm - 1, 0))
    cw = cw_ref[...]
    yc = pb_ref[0].astype(F32) * (cw[0:1] * u_prev + cw[1:2] * u + cw[2:3] * u_next)

    branches = (ya_ref[0], yb_ref[0], yc.astype(BF16), yd_ref[0])
    d = x.shape[1]
    out = None
    for n in range(d // CB):
        cols = slice(n * CB, (n + 1) * CB)
        merged = None
        for j in range(N_BRANCHES):
            t = jax.nn.sigmoid(_dot(h, wg_ref[j, :, cols])) * _dot(branches[j], wb_ref[j, :, cols])
            merged = t if merged is None else merged + t
        t = _dot(merged.astype(BF16), wo_ref[cols, :])
        out = t if out is None else out + t
    xn = x + mod[2:3] * out
    xo_ref[0] = xn
    h2_ref[0] = _norm_mod(xn, g2_ref[...], mod[3:4], mod[4:5]).astype(BF16)


def _merge(x, mods, mod_row, g1, g2, ya, yb, yd, p, conv_w, wg, wb, wo, tm):
    b, s, d = x.shape
    hb = tm // BF16_SUBLANES
    n_halo = s // BF16_SUBLANES
    mod_idx = (lambda bi, i: (bi, 0, 0)) if mod_row is None else (lambda bi, i: (mod_row, 0, 0))
    seq = lambda width, col=0: pl.BlockSpec((1, tm, width), lambda bi, i: (bi, i, col))
    prev = lambda col: pl.BlockSpec((1, BF16_SUBLANES, CB), lambda bi, i: (bi, jnp.maximum(i * hb - 1, 0), col))
    nxt = lambda col: pl.BlockSpec((1, BF16_SUBLANES, CB),
                                   lambda bi, i: (bi, jnp.minimum((i + 1) * hb, n_halo - 1), col))
    const = lambda shape: pl.BlockSpec(shape, lambda bi, i: (0,) * len(shape))
    return pl.pallas_call(
        _merge_kernel,
        grid=(b, s // tm),
        in_specs=[seq(d), pl.BlockSpec((1, N_MOD, d), mod_idx), const((1, d)), const((1, d)),
                  seq(CB), seq(CB), seq(CB),
                  seq(CB, COL_CB), seq(CB, COL_CC), seq(CB, COL_CX),
                  prev(COL_CC), prev(COL_CX), nxt(COL_CC), nxt(COL_CX),
                  const((CONV_WIDTH, CB)),
                  const((N_BRANCHES, d, d)), const((N_BRANCHES, BRANCH_DIM, d)), const((d, d))],
        out_specs=[seq(d), seq(d)],
        out_shape=[jax.ShapeDtypeStruct((b, s, d), F32),
                   jax.ShapeDtypeStruct((b, s, d), BF16)],
        compiler_params=_params("parallel", "arbitrary"),
    )(x, mods, g1.reshape(1, d), g2.reshape(1, d), ya, yb, yd, p, p, p, p, p, p, p,
      conv_w, wg, wb, wo)


def _route_kernel(h2_ref, wr_ref, bias_ref, tri_ref, ones_ref, idx_ref, w_ref, rank_ref, cnt_ref,
                  score_sc, sel_sc, carry_sc):
    i = pl.program_id(0)
    tm = h2_ref.shape[0]
    ne = wr_ref.shape[0]
    gsz = ne // N_GROUPS
    n_chunks = tm // LANES

    @pl.when(i == 0)
    def _():
        carry_sc[...] = jnp.zeros(carry_sc.shape, F32)

    score_sc[...] = jax.nn.sigmoid(_dot_nt(wr_ref[...], h2_ref[...]))

    def select(cidx, carry):
        c0 = pl.multiple_of(cidx * LANES, LANES)
        scores = score_sc[:, pl.ds(c0, LANES)]
        biased = scores + bias_ref[...]
        liota = lax.broadcasted_iota(jnp.int32, (gsz, LANES), 0)
        gs = []
        for g in range(N_GROUPS):
            v = biased[g * gsz:(g + 1) * gsz]
            m1 = jnp.max(v, axis=0, keepdims=True)
            i1 = jnp.min(jnp.where(v == m1, liota, gsz), axis=0, keepdims=True)
            m2 = jnp.max(jnp.where(liota == i1, -jnp.inf, v), axis=0, keepdims=True)
            gs.append(m1 + m2)
        gsm = jnp.concatenate(gs, axis=0)
        giota = lax.broadcasted_iota(jnp.int32, gsm.shape, 0)
        keep = jnp.zeros(gsm.shape, F32)
        for _ in range(TOPK_GROUPS):
            m = jnp.max(gsm, axis=0, keepdims=True)
            gi = jnp.min(jnp.where(gsm == m, giota, N_GROUPS), axis=0, keepdims=True)
            hit = giota == gi
            keep = jnp.where(hit, 1.0, keep)
            gsm = jnp.where(hit, -jnp.inf, gsm)
        cur = jnp.concatenate(
            [jnp.where(jnp.broadcast_to(keep[g:g + 1], (gsz, LANES)) > 0.0, biased[g * gsz:(g + 1) * gsz], -jnp.inf)
             for g in range(N_GROUPS)], axis=0)
        eiota = lax.broadcasted_iota(jnp.int32, (ne, LANES), 0)
        sel = jnp.zeros((ne, LANES), F32)
        idxs, ws = [], []
        for _ in range(TOP_K):
            m = jnp.max(cur, axis=0, keepdims=True)
            ik = jnp.min(jnp.where(cur == m, eiota, ne), axis=0, keepdims=True)
            hit = eiota == ik
            cur = jnp.where(hit, -jnp.inf, cur)
            ws.append(jnp.sum(jnp.where(hit, scores, 0.0), axis=0, keepdims=True))
            idxs.append(ik)
            sel = jnp.where(hit, 1.0, sel)
        w = jnp.concatenate(ws, axis=0)
        w_ref[:, pl.ds(c0, LANES)] = w / jnp.sum(w, axis=0, keepdims=True) * ROUTED_SCALE
        idx_ref[:, pl.ds(c0, LANES)] = jnp.concatenate(idxs, axis=0)
        sel_sc[:, pl.ds(c0, LANES)] = sel.astype(BF16)
        return carry

    lax.fori_loop(0, n_chunks, select, 0)

    sel_all = sel_sc[...]
    score_sc[...] = _dot(sel_all, tri_ref[...]) + jnp.concatenate([carry_sc[...]] * n_chunks, axis=1)

    def ranks(cidx, carry):
        c0 = pl.multiple_of(cidx * LANES, LANES)
        before = score_sc[:, pl.ds(c0, LANES)]
        idx = idx_ref[:, pl.ds(c0, LANES)]
        eiota = lax.broadcasted_iota(jnp.int32, (ne, LANES), 0)
        rows = [jnp.sum(jnp.where(eiota == idx[k:k + 1], before, 0.0), axis=0, keepdims=True)
                for k in range(TOP_K)]
        rank_ref[:, pl.ds(c0, LANES)] = jnp.concatenate(rows, axis=0).astype(jnp.int32)
        return carry

    lax.fori_loop(0, n_chunks, ranks, 0)
    carry_sc[...] += _dot(sel_all, ones_ref[...])
    cnt_ref[...] = carry_sc[...]


def _route(h2_all, wr_t, bias, tm):
    n, d = h2_all.shape
    ne = wr_t.shape[0]
    tri = jnp.asarray(np.triu(np.ones((tm, tm), np.float32), 1), BF16)
    ones = jnp.ones((tm, LANES), BF16)
    bias_b = jnp.broadcast_to(bias.astype(F32)[:, None], (ne, LANES))
    const = lambda shape: pl.BlockSpec(shape, lambda i: (0,) * len(shape))
    tok = pl.BlockSpec((TOP_K, tm), lambda i: (0, i))
    return pl.pallas_call(
        _route_kernel,
        grid=(n // tm,),
        in_specs=[pl.BlockSpec((tm, d), lambda i: (i, 0)), const((ne, d)), const((ne, LANES)),
                  const((tm, tm)), const((tm, LANES))],
        out_specs=[tok, tok, tok, const((ne, LANES))],
        out_shape=[jax.ShapeDtypeStruct((TOP_K, n), jnp.int32),
                   jax.ShapeDtypeStruct((TOP_K, n), F32),
                   jax.ShapeDtypeStruct((TOP_K, n), jnp.int32),
                   jax.ShapeDtypeStruct((ne, LANES), F32)],
        scratch_shapes=[pltpu.VMEM((ne, tm), F32), pltpu.VMEM((ne, tm), BF16), pltpu.VMEM((ne, LANES), F32)],
        compiler_params=_params("arbitrary"),
    )(h2_all, wr_t, bias_b, tri, ones)


def _pos_kernel(idx_ref, rank_ref, start_ref, pos_ref):
    ne = start_ref.shape[0]
    start = start_ref[...]

    def body(cidx, carry):
        c0 = pl.multiple_of(cidx * LANES, LANES)
        idx = idx_ref[:, pl.ds(c0, LANES)]
        eiota = lax.broadcasted_iota(jnp.int32, (ne, LANES), 0)
        rows = [jnp.sum(jnp.where(eiota == idx[k:k + 1], start, 0.0), axis=0, keepdims=True)
                for k in range(TOP_K)]
        pos_ref[:, pl.ds(c0, LANES)] = jnp.concatenate(rows, axis=0).astype(jnp.int32) + rank_ref[:, pl.ds(c0, LANES)]
        return carry

    lax.fori_loop(0, idx_ref.shape[1] // LANES, body, 0)


def _positions(idx, rank, start_rows, tm):
    k, n = idx.shape
    ne = start_rows.shape[0]
    start_b = jnp.broadcast_to(start_rows.astype(F32)[:, None], (ne, LANES))
    tok = pl.BlockSpec((k, tm), lambda i: (0, i))
    return pl.pallas_call(
        _pos_kernel,
        grid=(n // tm,),
        in_specs=[tok, tok, pl.BlockSpec((ne, LANES), lambda i: (0, 0))],
        out_specs=tok,
        out_shape=jax.ShapeDtypeStruct((k, n), jnp.int32),
        compiler_params=_params("parallel"),
    )(idx, rank, start_b)


def _expert_kernel(start_ref, cnt_ref, tail_ref, x_hbm, wg_hbm, wu_hbm, wd_hbm, y_hbm,
                   wg_buf, wu_buf, wd_buf, wg_bf, wu_bf, wd_bf, x_buf, y_buf, zero_buf,
                   w_sem, x_sem, y_sem, z_sem, nout_ref, *, layer):
    e = pl.program_id(0)
    ne = pl.num_programs(0)
    slot = e % 2
    n_rows = x_hbm.shape[0]

    def w_copies(expert, s):
        return (pltpu.make_async_copy(wg_hbm.at[layer, expert], wg_buf.at[s], w_sem.at[s, 0]),
                pltpu.make_async_copy(wu_hbm.at[layer, expert], wu_buf.at[s], w_sem.at[s, 1]),
                pltpu.make_async_copy(wd_hbm.at[layer, expert], wd_buf.at[s], w_sem.at[s, 2]))

    def x_copy(row0, s):
        return pltpu.make_async_copy(x_hbm.at[pl.ds(row0, MOE_ROWS)], x_buf.at[s], x_sem.at[s])

    def y_copy(row0, s):
        return pltpu.make_async_copy(y_buf.at[s], y_hbm.at[pl.ds(row0, MOE_ROWS)], y_sem.at[s])

    def zero_copy(row0):
        return pltpu.make_async_copy(zero_buf, y_hbm.at[pl.ds(row0, MOE_ALIGN)], z_sem.at[0])

    @pl.when(e == 0)
    def _():
        nout_ref[0] = 0
        for c in w_copies(0, 0):
            c.start()

    @pl.when(e + 1 < ne)
    def _():
        for c in w_copies(e + 1, 1 - slot):
            c.start()

    start = pl.multiple_of(start_ref[e], MOE_ALIGN)
    n_chunks = (cnt_ref[e] + MOE_ROWS - 1) // MOE_ROWS

    @pl.when(n_chunks > 0)
    def _():
        x_copy(start, 0).start()

    for c in w_copies(e, slot):
        c.wait()

    @pl.when(n_chunks > 0)
    def _():
        wg_bf[...] = wg_buf[slot].astype(BF16)
        wu_bf[...] = wu_buf[slot].astype(BF16)
        wd_bf[...] = wd_buf[slot].astype(BF16)

    def chunk(j, carry):
        xs = j % 2
        row0 = pl.multiple_of(start + j * MOE_ROWS, MOE_ALIGN)
        x_copy(row0, xs).wait()

        @pl.when(j + 1 < n_chunks)
        def _():
            x_copy(row0 + MOE_ROWS, 1 - xs).start()

        x = x_buf[xs]
        a = (_silu(_dot(x, wg_bf[...])) * _dot(x, wu_bf[...])).astype(BF16)
        y = _dot(a, wd_bf[...]).astype(BF16)
        done = nout_ref[0]
        ys = done % 2

        @pl.when(done > 0)
        def _():
            y_copy(row0, 1 - ys).wait()

        y_buf[ys] = y
        y_copy(row0, ys).start()
        nout_ref[0] = done + 1
        return carry

    lax.fori_loop(0, n_chunks, chunk, 0)

    @pl.when(e == ne - 1)
    def _():
        done = nout_ref[0]

        @pl.when(done > 0)
        def _():
            y_copy(0, 1 - done % 2).wait()

        zero_buf[...] = jnp.zeros(zero_buf.shape, zero_buf.dtype)
        n_tail = (n_rows - tail_ref[0]) // MOE_ALIGN

        def fill(t, carry):
            zero_copy(pl.multiple_of(tail_ref[0] + t * MOE_ALIGN, MOE_ALIGN)).start()
            return carry

        def drain(t, carry):
            zero_copy(0).wait()
            return carry

        lax.fori_loop(0, n_tail, fill, 0)
        lax.fori_loop(0, n_tail, drain, 0)


def _experts(x_sorted, seg_start, counts, tail_start, layer, w_g, w_u, w_d):
    n_rows, d = x_sorted.shape
    ne, f = w_g.shape[1], w_g.shape[3]
    any_spec = pl.BlockSpec(memory_space=pl.ANY)
    grid_spec = pltpu.PrefetchScalarGridSpec(
        num_scalar_prefetch=3,
        grid=(ne,),
        in_specs=[any_spec] * 4,
        out_specs=any_spec,
        scratch_shapes=[pltpu.VMEM((2, d, f), F32), pltpu.VMEM((2, d, f), F32), pltpu.VMEM((2, f, d), F32),
                        pltpu.VMEM((d, f), BF16), pltpu.VMEM((d, f), BF16), pltpu.VMEM((f, d), BF16),
                        pltpu.VMEM((2, MOE_ROWS, d), BF16), pltpu.VMEM((2, MOE_ROWS, d), BF16),
                        pltpu.VMEM((MOE_ALIGN, d), BF16),
                        pltpu.SemaphoreType.DMA((2, 3)), pltpu.SemaphoreType.DMA((2,)),
                        pltpu.SemaphoreType.DMA((2,)), pltpu.SemaphoreType.DMA((1,)),
                        pltpu.SMEM((1,), jnp.int32)])
    return pl.pallas_call(
        functools.partial(_expert_kernel, layer=layer),
        grid_spec=grid_spec,
        out_shape=jax.ShapeDtypeStruct((n_rows, d), BF16),
        compiler_params=_params("arbitrary"),
    )(seg_start, counts, tail_start, x_sorted, w_g, w_u, w_d)


def _resid_kernel(x_ref, h2_ref, y_ref, w_ref, mod_ref, sg_ref, su_ref, sd_ref, gf_ref, o_ref, *, final):
    h2 = h2_ref[...]
    a = (_silu(_dot(h2, sg_ref[...])) * _dot(h2, su_ref[...])).astype(BF16)
    y = _dot(a, sd_ref[...])
    w = w_ref[...]
    for k in range(TOP_K):
        y = y + w[:, k:k + 1] * y_ref[k].astype(F32)
    xo = x_ref[...] + mod_ref[0][5:6] * y
    if final:
        xo = xo * lax.rsqrt(jnp.mean(xo * xo, axis=-1, keepdims=True) + EPS) * gf_ref[...]
    o_ref[...] = xo


def _resid(x_flat, h2_all, y_tok, w_tok, row_off, mods, mod_row, rows_per_mod, sg, su, sd, gf, final, tm):
    n, d = x_flat.shape
    f = sg.shape[1]
    off = row_off // tm
    per = rows_per_mod // tm
    mod_idx = (lambda i: (i // per, 0, 0)) if mod_row is None else (lambda i: (mod_row, 0, 0))
    const = lambda shape: pl.BlockSpec(shape, lambda i: (0,) * len(shape))
    return pl.pallas_call(
        functools.partial(_resid_kernel, final=final),
        grid=(n // tm,),
        in_specs=[pl.BlockSpec((tm, d), lambda i: (i, 0)),
                  pl.BlockSpec((tm, d), lambda i: (i + off, 0)),
                  pl.BlockSpec((TOP_K, tm, d), lambda i: (0, i + off, 0)),
                  pl.BlockSpec((tm, TOP_K), lambda i: (i + off, 0)),
                  pl.BlockSpec((1, N_MOD, d), mod_idx),
                  const((d, f)), const((d, f)), const((f, d)), const((1, d))],
        out_specs=pl.BlockSpec((tm, d), lambda i: (i, 0)),
        out_shape=jax.ShapeDtypeStruct((n, d), F32),
        compiler_params=_params("parallel"),
    )(x_flat, h2_all, y_tok, w_tok, mods, sg, su, sd, gf.reshape(1, d))


def _moe_routed(h2_all, wr_t, bias, layer, w_g, w_u, w_d):
    n, d = h2_all.shape
    idx, wts, rank, cnt = _route(h2_all, wr_t, bias, 512)
    counts = cnt[:, 0].astype(jnp.int32)
    padded = (counts + MOE_ALIGN - 1) // MOE_ALIGN * MOE_ALIGN
    seg_start = jnp.cumsum(padded) - padded
    chunk_end = jnp.where(counts > 0, seg_start + (counts + MOE_ROWS - 1) // MOE_ROWS * MOE_ROWS, 0)
    tail_start = jnp.max(chunk_end).reshape(1)
    n_rows = -(-(n * TOP_K + N_EXPERTS * (MOE_ALIGN - 1)) // MOE_ROWS) * MOE_ROWS + MOE_ROWS
    pos = _positions(idx, rank, seg_start, 512).reshape(TOP_K * n)
    tok = jnp.tile(jnp.arange(n, dtype=jnp.int32), TOP_K)
    row_tok = (jnp.arange(n_rows, dtype=jnp.int32) % n).at[pos].set(
        tok, unique_indices=True, mode='promise_in_bounds')
    x_sorted = jnp.concatenate([h2_all, h2_all], axis=0).at[row_tok].get(mode='promise_in_bounds')
    y_sorted = _experts(x_sorted, seg_start, counts, tail_start, layer, w_g, w_u, w_d)
    return y_sorted.at[pos].get(mode='promise_in_bounds').reshape(TOP_K, n, d), wts.T


def kernel(x, c, ctx, c_ctx, ada_w, ada_b, norm1_g, w_in, conv_w, na_rel_bias, diff_lambda,
           diff_subln_g, w_branch_gate, w_branch, w_out, norm2_g, router_w, router_bias,
           expert_w_gate, expert_w_up, expert_w_down, shared_w_gate, shared_w_up, shared_w_down,
           final_norm_g):
    b, s, d = x.shape
    l_ctx = ctx.shape[1]
    rows = s // GRID_W
    ctx_row = b
    cvec = jnp.zeros((8, d), F32).at[:b].set(c).at[ctx_row].set(c_ctx)
    rope_tabs = _rope_tables(s)
    wf = _channel_dft_matrix()
    tm = 512
    xc = ctx
    for layer in range(DEPTH):
        last = layer == DEPTH - 1
        lam_init = 0.8 - 0.6 * math.exp(-0.3 * layer)
        mods = _ada(cvec, ada_w, ada_b, layer).reshape(8, N_MOD, d)
        w_in_bf = w_in[layer].astype(BF16)
        wg_bf = w_branch_gate[layer].astype(BF16)
        wb_bf = w_branch[layer].astype(BF16)
        wo_bf = w_out[layer].astype(BF16)
        wr_bf = router_w[layer].T.astype(BF16)
        lp = diff_lambda[layer]
        sub_g = diff_subln_g[layer]

        p, ur, ui = _inproj(x, norm1_g[layer], mods, w_in_bf, wf, rope_tabs, None, True, tm)
        ctx_tabs = tuple(t[:l_ctx] for t in rope_tabs)
        pc, ucr, uci = _inproj(xc, norm1_g[layer], mods, w_in_bf, wf, ctx_tabs, ctx_row, False, l_ctx)

        ya = _fourier_latent(ur, ui)
        yb = _na_latent(p, pc, _na_bias_tables(na_rel_bias[layer], rows))
        yd = _diff_latent(p, pc, lp, sub_g, lam_init, 1024, _key_tile(s + l_ctx, 768))
        x, h2 = _merge(x, mods, None, norm1_g[layer], norm2_g[layer], ya, yb, yd, p,
                       conv_w[layer], wg_bf, wb_bf, wo_bf, tm)
        h2_all = h2.reshape(b * s, d)
        if not last:
            yac, ybc, ydc = _ctx_branches(pc, ucr, uci, lp, sub_g, lam_init)
            xc, h2c = _merge(xc, mods, ctx_row, norm1_g[layer], norm2_g[layer], yac, ybc, ydc, pc,
                             conv_w[layer], wg_bf, wb_bf, wo_bf, l_ctx)
            h2_all = jnp.concatenate([h2_all, h2c.reshape(b * l_ctx, d)], axis=0)

        y_tok, w_tok = _moe_routed(h2_all, wr_bf, router_bias[layer], layer, expert_w_gate,
                                   expert_w_up, expert_w_down)
        sg_bf = shared_w_gate[layer].astype(BF16)
        su_bf = shared_w_up[layer].astype(BF16)
        sd_bf = shared_w_down[layer].astype(BF16)
        x = _resid(x.reshape(b * s, d), h2_all, y_tok, w_tok, 0, mods, None, s, sg_bf, su_bf, sd_bf,
                   final_norm_g, last, tm).reshape(b, s, d)
        if not last:
            xc = _resid(xc.reshape(b * l_ctx, d), h2_all, y_tok, w_tok, b * s, mods, ctx_row, l_ctx,
                        sg_bf, su_bf, sd_bf, final_norm_g, False, l_ctx).reshape(b, l_ctx, d)
    return x
```

```python
import functools
import math

import numpy as np
import jax
import jax.numpy as jnp
from jax import lax
from jax.experimental import pallas as pl
from jax.experimental.pallas import tpu as pltpu

F32 = jnp.float32
BF16 = jnp.bfloat16

DEPTH = 2
GRID_W = 64
EPS = 1e-6
N_MOD = 6

FNET_GROUP_DIM = 64
NA_HEADS = 4
NA_HEAD_DIM = 64
NA_WIN_ROWS = 8
NA_WIN_COLS = 16
CONV_WIDTH = 3
DIFF_HEADS = 4
DIFF_QK_DIM = 32
DIFF_V_DIM = 64
ROPE_BASE = 10000.0
N_BRANCHES = 4
BRANCH_DIM = 256

COL_A, COL_BQ, COL_BK, COL_BV, COL_CB, COL_CC, COL_CX, COL_DQ, COL_DK, COL_DV = range(10)
N_COL_BLOCKS = 10
CB = 256
IN_DIM = N_COL_BLOCKS * CB

N_EXPERTS = 256
TOP_K = 8
N_GROUPS = 8
TOPK_GROUPS = 4
ROUTED_SCALE = 2.5
LOG2_E = 1.4426950408889634

VMEM_LIMIT_BYTES = 56 * 1024 * 1024
LANES = 128
BF16_SUBLANES = 16
FFT_N1 = 64
NA_QROWS = 8
NA_KROWS = 16
MOE_ROWS = 256
MOE_ALIGN = BF16_SUBLANES
MOE_X_SLOTS = 4
MOE_W_SLOTS = 3


def _params(*sem):
    return pltpu.CompilerParams(dimension_semantics=sem, vmem_limit_bytes=VMEM_LIMIT_BYTES)


def _dot(a, b):
    return jnp.dot(a, b, preferred_element_type=F32)


def _dot_nt(a, b):
    return lax.dot_general(a, b, (((1,), (1,)), ((), ())), preferred_element_type=F32)


def _norm_mod(xf, g, shift, scale):
    y = xf * lax.rsqrt(jnp.mean(xf * xf, axis=-1, keepdims=True) + EPS)
    return (y * g) * (1.0 + scale) + shift


def _silu(v):
    return v * jax.nn.sigmoid(v)


def _ada_kernel(c_ref, w_ref, b_ref, o_ref):
    s = _silu(c_ref[...])
    o_ref[...] = _dot(s.astype(BF16), w_ref[0].astype(BF16)) + b_ref[0]


def _ada(cvec, w, b, layer):
    rows, d = cvec.shape
    depth, _, n = w.shape
    tn = 1536
    return pl.pallas_call(
        _ada_kernel,
        grid=(n // tn,),
        in_specs=[pl.BlockSpec((rows, d), lambda j: (0, 0)),
                  pl.BlockSpec((1, d, tn), lambda j: (layer, 0, j)),
                  pl.BlockSpec((1, 1, tn), lambda j: (layer, 0, j))],
        out_specs=pl.BlockSpec((rows, tn), lambda j: (0, j)),
        out_shape=jax.ShapeDtypeStruct((rows, n), F32),
        compiler_params=_params("arbitrary"),
    )(cvec, w, b.reshape(depth, 1, n))


def _inproj_kernel(x_ref, g_ref, mod_ref, w_ref, wf_ref, cos_ref, s1_ref, s2_ref,
                   p_ref, ur_ref, ui_ref, *, rope):
    mod = mod_ref[0]
    h = _norm_mod(x_ref[0], g_ref[...], mod[0:1], mod[1:2]).astype(BF16)
    for j in range(N_COL_BLOCKS):
        pj = _dot(h, w_ref[:, j * CB:(j + 1) * CB])
        if j == COL_A:
            u = _dot(pj.astype(BF16), wf_ref[...])
            ur_ref[0] = u[:, :CB].astype(BF16)
            ui_ref[0] = u[:, CB:].astype(BF16)
        if rope and j in (COL_DQ, COL_DK):
            cos = jnp.concatenate([cos_ref[...]] * 2, axis=1)
            s1 = jnp.concatenate([s1_ref[...]] * 2, axis=1)
            s2 = jnp.concatenate([s2_ref[...]] * 2, axis=1)
            pj = pj * cos + pltpu.roll(pj, CB - 8, 1) * s1 + pltpu.roll(pj, 8, 1) * s2
            if j == COL_DQ:
                pj = pj * (DIFF_QK_DIM ** -0.5 * LOG2_E)
        p_ref[0, :, j * CB:(j + 1) * CB] = pj.astype(BF16)


def _inproj(x, g, mods, w_bf, wf, rope_tabs, mod_row, rope, tm):
    b, s, d = x.shape
    mod_idx = (lambda bi, i: (bi, 0, 0)) if mod_row is None else (lambda bi, i: (mod_row, 0, 0))
    tab_spec = pl.BlockSpec((tm, 128), lambda bi, i: (i, 0))
    seq_spec = lambda width: pl.BlockSpec((1, tm, width), lambda bi, i: (bi, i, 0))
    return pl.pallas_call(
        functools.partial(_inproj_kernel, rope=rope),
        grid=(b, s // tm),
        in_specs=[seq_spec(d),
                  pl.BlockSpec((1, d), lambda bi, i: (0, 0)),
                  pl.BlockSpec((1, N_MOD, d), mod_idx),
                  pl.BlockSpec((d, IN_DIM), lambda bi, i: (0, 0)),
                  pl.BlockSpec((CB, 2 * CB), lambda bi, i: (0, 0)),
                  tab_spec, tab_spec, tab_spec],
        out_specs=[seq_spec(IN_DIM), seq_spec(CB), seq_spec(CB)],
        out_shape=[jax.ShapeDtypeStruct((b, s, IN_DIM), BF16),
                   jax.ShapeDtypeStruct((b, s, CB), BF16),
                   jax.ShapeDtypeStruct((b, s, CB), BF16)],
        compiler_params=_params("parallel", "arbitrary"),
    )(x, g.reshape(1, d), mods, w_bf, wf, *rope_tabs)


def _channel_dft_matrix():
    c = np.arange(FNET_GROUP_DIM)
    ang = 2.0 * np.pi * ((c[:, None] * c[None, :]) % FNET_GROUP_DIM) / FNET_GROUP_DIM
    eye = np.eye(CB // FNET_GROUP_DIM)
    m = np.concatenate([np.kron(eye, np.cos(ang)), -np.kron(eye, np.sin(ang))], axis=1)
    return jnp.asarray(m, BF16)


def _rope_tables(s):
    half = DIFF_QK_DIM // 2
    t = jnp.arange(s)
    inv = 1.0 / (ROPE_BASE ** (jnp.arange(0, half, 2, dtype=F32) / half))
    ang_r = (t // GRID_W).astype(F32)[:, None] * inv
    ang_c = (t % GRID_W).astype(F32)[:, None] * inv
    zero = jnp.zeros_like(ang_r)
    cos = jnp.concatenate([jnp.cos(ang_r)] * 2 + [jnp.cos(ang_c)] * 2, axis=1)
    s1 = jnp.concatenate([-jnp.sin(ang_r), zero, -jnp.sin(ang_c), zero], axis=1)
    s2 = jnp.concatenate([zero, jnp.sin(ang_r), zero, jnp.sin(ang_c)], axis=1)
    return tuple(jnp.concatenate([a] * 4, axis=1) for a in (cos, s1, s2))


def _fft1_kernel(ur_ref, ui_ref, w_ref, ct_ref, st_ref, ar_ref, ai_ref):
    n1 = ur_ref.shape[1]
    u = jnp.concatenate([ur_ref[0], ui_ref[0]], axis=0)
    a = _dot(w_ref[...], u)
    ar, ai = a[:n1], a[n1:]
    ct, st = ct_ref[...], st_ref[...]
    ar_ref[0] = (ar * ct + ai * st).astype(BF16)
    ai_ref[0] = (ai * ct - ar * st).astype(BF16)


def _fft2_kernel(ar_ref, ai_ref, w_ref, y_ref, *, norm):
    for j in range(ar_ref.shape[1]):
        a = jnp.concatenate([ar_ref[0, j], ai_ref[0, j]], axis=0)
        y_ref[0, j] = (_dot(w_ref[...], a) * norm).astype(BF16)


def _dft_cos_sin(n):
    k = np.arange(n)
    ang = 2.0 * np.pi * ((k[:, None] * k[None, :]) % n) / n
    return np.cos(ang), np.sin(ang)


def _fourier_latent(ur, ui):
    b, s, cb = ur.shape
    n1, n2 = FFT_N1, s // FFT_N1
    c1, s1 = _dft_cos_sin(n1)
    w1 = jnp.asarray(np.block([[c1, s1], [-s1, c1]]), BF16)
    c2, s2 = _dft_cos_sin(n2)
    w2 = jnp.asarray(np.concatenate([c2, s2], axis=1), BF16)
    tw = 2.0 * np.pi * (np.arange(n1)[:, None] * np.arange(n2)[None, :]) / s
    ct = jnp.broadcast_to(jnp.asarray(np.cos(tw), F32)[:, :, None], (n1, n2, cb)).reshape(n1, n2 * cb)
    st = jnp.broadcast_to(jnp.asarray(np.sin(tw), F32)[:, :, None], (n1, n2, cb)).reshape(n1, n2 * cb)
    lanes = n2 * cb
    tn = min(lanes, 4096)
    u_spec = pl.BlockSpec((1, n1, tn), lambda j, bi: (bi, 0, j))
    t_spec = pl.BlockSpec((n1, tn), lambda j, bi: (0, j))
    ar, ai = pl.pallas_call(
        _fft1_kernel,
        grid=(lanes // tn, b),
        in_specs=[u_spec, u_spec, pl.BlockSpec((2 * n1, 2 * n1), lambda j, bi: (0, 0)), t_spec, t_spec],
        out_specs=[u_spec, u_spec],
        out_shape=[jax.ShapeDtypeStruct((b, n1, lanes), BF16)] * 2,
        compiler_params=_params("arbitrary", "arbitrary"),
    )(ur.reshape(b, n1, lanes), ui.reshape(b, n1, lanes), w1, ct, st)
    kc = 8
    a_spec = pl.BlockSpec((1, kc, n2, cb), lambda bi, j: (bi, j, 0, 0))
    y = pl.pallas_call(
        functools.partial(_fft2_kernel, norm=1.0 / math.sqrt(s * FNET_GROUP_DIM)),
        grid=(b, n1 // kc),
        in_specs=[a_spec, a_spec, pl.BlockSpec((n2, 2 * n2), lambda bi, j: (0, 0))],
        out_specs=a_spec,
        out_shape=jax.ShapeDtypeStruct((b, n1, n2, cb), BF16),
        compiler_params=_params("parallel", "arbitrary"),
    )(ar.reshape(b, n1, n2, cb), ai.reshape(b, n1, n2, cb), w2)
    return jnp.transpose(y, (0, 2, 1, 3)).reshape(b, s, cb)


def _na_kernel(q_ref, k_ref, v_ref, kc_ref, vc_ref, bias_ref, o_ref, *, rows):
    rb = pl.program_id(1)
    kb = jnp.clip(rb * NA_QROWS - NA_WIN_ROWS // 2, 0, rows - NA_KROWS)
    nk = NA_KROWS * GRID_W
    tok0 = pl.multiple_of(kb * GRID_W, 256)
    scale = NA_HEAD_DIM ** -0.5
    outs = []
    for h in range(NA_HEADS):
        sl = slice(h * NA_HEAD_DIM, (h + 1) * NA_HEAD_DIM)
        q = q_ref[0, :, sl]
        s = _dot_nt(q, k_ref[0, pl.ds(tok0, nk), sl]) * scale + bias_ref[0, h]
        sc = _dot_nt(q, kc_ref[0, :, sl]) * scale
        m = jnp.maximum(jnp.max(s, axis=-1, keepdims=True), jnp.max(sc, axis=-1, keepdims=True))
        e = jnp.exp(s - m)
        ec = jnp.exp(sc - m)
        l = jnp.sum(e, axis=-1, keepdims=True) + jnp.sum(ec, axis=-1, keepdims=True)
        o = _dot(e.astype(BF16), v_ref[0, pl.ds(tok0, nk), sl]) + _dot(ec.astype(BF16), vc_ref[0, :, sl])
        outs.append(o / l)
    o_ref[0] = jnp.concatenate(outs, axis=1).astype(BF16)


def _na_bias_tables(rel_bias, rows):
    tabs = []
    cq = np.arange(GRID_W)
    col_lo = np.clip(cq - NA_WIN_COLS // 2, 0, GRID_W - NA_WIN_COLS)
    col_ok = (cq[None, :] >= col_lo[:, None]) & (cq[None, :] < col_lo[:, None] + NA_WIN_COLS)
    dc_idx = np.clip(cq[None, :] - cq[:, None] + NA_WIN_COLS - 1, 0, 2 * NA_WIN_COLS - 2)
    for r0 in (0, NA_QROWS, rows - NA_QROWS):
        kb = int(np.clip(r0 - NA_WIN_ROWS // 2, 0, rows - NA_KROWS))
        r = r0 + np.arange(NA_QROWS)
        rk = kb + np.arange(NA_KROWS)
        start = np.clip(r - NA_WIN_ROWS // 2, 0, rows - NA_WIN_ROWS)
        row_ok = (rk[None, :] >= start[:, None]) & (rk[None, :] < start[:, None] + NA_WIN_ROWS)
        dr_idx = np.clip(rk[None, :] - r[:, None] + NA_WIN_ROWS - 1, 0, 2 * NA_WIN_ROWS - 2)
        ok = row_ok[:, None, :, None] & col_ok[None, :, None, :]
        oh_r = jnp.asarray(dr_idx[:, :, None] == np.arange(2 * NA_WIN_ROWS - 1), F32)
        oh_c = jnp.asarray(dc_idx[:, :, None] == np.arange(2 * NA_WIN_COLS - 1), F32)
        by_row = jnp.einsum('qka,hab->hqkb', oh_r, rel_bias.astype(F32), precision=lax.Precision.HIGHEST)
        vals = jnp.einsum('hqkb,cdb->hqckd', by_row, oh_c, precision=lax.Precision.HIGHEST)
        tab = jnp.where(ok[None], vals, -jnp.inf)
        tabs.append(tab.reshape(NA_HEADS, NA_QROWS * GRID_W, NA_KROWS * GRID_W))
    return jnp.stack(tabs)


def _na_latent(p, pc, bias_tabs):
    b, s, _ = p.shape
    l = pc.shape[1]
    rows = s // GRID_W
    nrb = rows // NA_QROWS
    tq = NA_QROWS * GRID_W
    nk = NA_KROWS * GRID_W
    return pl.pallas_call(
        functools.partial(_na_kernel, rows=rows),
        grid=(b, nrb),
        in_specs=[pl.BlockSpec((1, tq, CB), lambda bi, i: (bi, i, COL_BQ)),
                  pl.BlockSpec((1, s, CB), lambda bi, i: (bi, 0, COL_BK)),
                  pl.BlockSpec((1, s, CB), lambda bi, i: (bi, 0, COL_BV)),
                  pl.BlockSpec((1, l, CB), lambda bi, i: (bi, 0, COL_BK)),
                  pl.BlockSpec((1, l, CB), lambda bi, i: (bi, 0, COL_BV)),
                  pl.BlockSpec((1, NA_HEADS, tq, nk),
                               lambda bi, i: (jnp.minimum(i, 1) + (i == nrb - 1).astype(jnp.int32), 0, 0, 0))],
        out_specs=pl.BlockSpec((1, tq, CB), lambda bi, i: (bi, i, 0)),
        out_shape=jax.ShapeDtypeStruct((b, s, CB), BF16),
        compiler_params=_params("parallel", "arbitrary"),
    )(p, p, p, pc, pc, bias_tabs)


def _diff_lambda(lp, lam_init):
    return (jnp.exp(jnp.sum(lp[0:1] * lp[1:2], axis=-1, keepdims=True))
            - jnp.exp(jnp.sum(lp[2:3] * lp[3:4], axis=-1, keepdims=True)) + lam_init)


def _diff_finish(o, g, lam_init):
    y = o * lax.rsqrt(jnp.mean(o * o, axis=-1, keepdims=True) + EPS)
    return y * g * (1.0 - lam_init)


def _diff_kernel(q_ref, kt_ref, v_ref, lp_ref, g_ref, o_ref, m_sc, acc_sc, *, tk, lam_init):
    nk = kt_ref.shape[2] // tk
    n_maps = 2 * DIFF_HEADS
    m_sc[...] = jnp.full(m_sc.shape, -jnp.inf, F32)
    acc_sc[...] = jnp.zeros(acc_sc.shape, F32)

    def body(c, carry):
        k0 = pl.multiple_of(c * tk, LANES)
        for hm in range(n_maps):
            dims = slice(hm * DIFF_QK_DIM, (hm + 1) * DIFF_QK_DIM)
            s = _dot(q_ref[0, :, dims], kt_ref[0, dims, pl.ds(k0, tk)])
            m_old = m_sc[hm]
            m_new = jnp.maximum(m_old, jnp.max(s, axis=-1, keepdims=True))
            e = jnp.exp2(s - m_new[:, :1]).astype(BF16)
            acc_sc[hm] = jnp.exp2(m_old - m_new) * acc_sc[hm] + _dot(e, v_ref[0, hm // 2, pl.ds(k0, tk), :])
            m_sc[hm] = m_new
        return carry

    lax.fori_loop(0, nk, body, 0)
    maps = []
    for hm in range(n_maps):
        acc = acc_sc[hm]
        maps.append(acc[:, :DIFF_V_DIM] / acc[:, DIFF_V_DIM:DIFF_V_DIM + 1])
    lam = _diff_lambda(lp_ref[...], lam_init)
    outs = [_diff_finish(maps[2 * h] - lam * maps[2 * h + 1], g_ref[...], lam_init) for h in range(DIFF_HEADS)]
    o_ref[0] = jnp.concatenate(outs, axis=1).astype(BF16)


def _key_tile(nkeys, cap):
    return max(t for t in range(128, cap + 1, 128) if nkeys % t == 0)


def _diff_latent(p, pc, lp, sub_g, lam_init, tq, tk):
    b, s, _ = p.shape
    l = pc.shape[1]
    nkeys = s + l
    k_all = jnp.concatenate([p[:, :, COL_DK * CB:(COL_DK + 1) * CB], pc[:, :, COL_DK * CB:(COL_DK + 1) * CB]], axis=1)
    v_all = jnp.concatenate([p[:, :, COL_DV * CB:], pc[:, :, COL_DV * CB:]], axis=1)
    kt = jnp.transpose(k_all, (0, 2, 1))
    vh = jnp.transpose(v_all.reshape(b, nkeys, DIFF_HEADS, DIFF_V_DIM), (0, 2, 1, 3))
    vh = jnp.concatenate([vh, jnp.ones((b, DIFF_HEADS, nkeys, LANES - DIFF_V_DIM), BF16)], axis=-1)
    return pl.pallas_call(
        functools.partial(_diff_kernel, tk=tk, lam_init=lam_init),
        grid=(b, s // tq),
        in_specs=[pl.BlockSpec((1, tq, CB), lambda bi, i: (bi, i, COL_DQ)),
                  pl.BlockSpec((1, CB, nkeys), lambda bi, i: (bi, 0, 0)),
                  pl.BlockSpec((1, DIFF_HEADS, nkeys, LANES), lambda bi, i: (bi, 0, 0, 0)),
                  pl.BlockSpec((4, DIFF_QK_DIM), lambda bi, i: (0, 0)),
                  pl.BlockSpec((1, DIFF_V_DIM), lambda bi, i: (0, 0))],
        out_specs=pl.BlockSpec((1, tq, CB), lambda bi, i: (bi, i, 0)),
        out_shape=jax.ShapeDtypeStruct((b, s, CB), BF16),
        scratch_shapes=[pltpu.VMEM((2 * DIFF_HEADS, tq, LANES), F32),
                        pltpu.VMEM((2 * DIFF_HEADS, tq, LANES), F32)],
        compiler_params=_params("parallel", "arbitrary"),
    )(p, kt, vh, lp, sub_g.reshape(1, DIFF_V_DIM))


def _softmax_rows(s):
    e = jnp.exp(s - jnp.max(s, axis=-1, keepdims=True))
    return e / jnp.sum(e, axis=-1, keepdims=True)


def _ctx_kernel(pc_ref, ur_ref, ui_ref, wf_ref, lp_ref, g_ref, ya_ref, yb_ref, yd_ref, *, lam_init):
    l = pc_ref.shape[1]
    col = lambda j, lo, hi: pc_ref[0, :, j * CB + lo:j * CB + hi]
    u = jnp.concatenate([ur_ref[0], ui_ref[0]], axis=0)
    ya_ref[0] = (_dot(wf_ref[...], u) * (1.0 / math.sqrt(l * FNET_GROUP_DIM))).astype(BF16)
    outs = []
    for h in range(NA_HEADS):
        lo, hi = h * NA_HEAD_DIM, (h + 1) * NA_HEAD_DIM
        pr = _softmax_rows(_dot_nt(col(COL_BQ, lo, hi), col(COL_BK, lo, hi)) * NA_HEAD_DIM ** -0.5)
        outs.append(_dot(pr.astype(BF16), col(COL_BV, lo, hi)))
    yb_ref[0] = jnp.concatenate(outs, axis=1).astype(BF16)
    lam = _diff_lambda(lp_ref[...], lam_init)
    outs = []
    for h in range(DIFF_HEADS):
        pm = []
        for m in range(2):
            lo = (2 * h + m) * DIFF_QK_DIM
            pm.append(_softmax_rows(_dot_nt(col(COL_DQ, lo, lo + DIFF_QK_DIM), col(COL_DK, lo, lo + DIFF_QK_DIM))
                                    * DIFF_QK_DIM ** -0.5))
        a = (pm[0] - lam * pm[1]).astype(BF16)
        o = _dot(a, col(COL_DV, h * DIFF_V_DIM, (h + 1) * DIFF_V_DIM))
        outs.append(_diff_finish(o, g_ref[...], lam_init))
    yd_ref[0] = jnp.concatenate(outs, axis=1).astype(BF16)


def _ctx_branches(pc, ucr, uci, lp, sub_g, lam_init):
    b, l, _ = pc.shape
    c, s = _dft_cos_sin(l)
    wf = jnp.asarray(np.concatenate([c, s], axis=1), BF16)
    y_spec = pl.BlockSpec((1, l, CB), lambda bi: (bi, 0, 0))
    return pl.pallas_call(
        functools.partial(_ctx_kernel, lam_init=lam_init),
        grid=(b,),
        in_specs=[pl.BlockSpec((1, l, IN_DIM), lambda bi: (bi, 0, 0)), y_spec, y_spec,
                  pl.BlockSpec((l, 2 * l), lambda bi: (0, 0)),
                  pl.BlockSpec((4, DIFF_QK_DIM), lambda bi: (0, 0)),
                  pl.BlockSpec((1, DIFF_V_DIM), lambda bi: (0, 0))],
        out_specs=[y_spec] * 3,
        out_shape=[jax.ShapeDtypeStruct((b, l, CB), BF16)] * 3,
        compiler_params=_params("parallel"),
    )(pc, ucr, uci, wf, lp, sub_g.reshape(1, DIFF_V_DIM))


def _merge_kernel(x_ref, mod_ref, g1_ref, g2_ref, ya_ref, yb_ref, yd_ref, pb_ref, pc_ref, px_ref,
                  cp_ref, xp_ref, cn_ref, xn_ref, cw_ref, wg_ref, wb_ref, wo_ref,
                  xo_ref, h2_ref):
    i = pl.program_id(1)
    last = pl.num_programs(1) - 1
    tm = x_ref.shape[1]
    mod = mod_ref[0]
    x = x_ref[0]
    h = _norm_mod(x, g1_ref[...], mod[0:1], mod[1:2]).astype(BF16)

    u = pc_ref[0].astype(F32) * px_ref[0].astype(F32)
    up = cp_ref[0, BF16_SUBLANES - 1:, :].astype(F32) * xp_ref[0, BF16_SUBLANES - 1:, :].astype(F32)
    un = cn_ref[0, :1, :].astype(F32) * xn_ref[0, :1, :].astype(F32)
    up = jnp.where(i == 0, 0.0, up)
    un = jnp.where(i == last, 0.0, un)
    rid = lax.broadcasted_iota(jnp.int32, u.shape, 0)
    u_prev = jnp.where(rid == 0, up, pltpu.roll(u, 1, 0))
    u_next = jnp.where(rid == tm - 1, un, pltpu.roll(u, tm - 1, 0))
    cw = cw_ref[...]
    yc = pb_ref[0].astype(F32) * (cw[0:1] * u_prev + cw[1:2] * u + cw[2:3] * u_next)

    branches = (ya_ref[0], yb_ref[0], yc.astype(BF16), yd_ref[0])
    d = x.shape[1]
    out = None
    for n in range(d // CB):
        cols = slice(n * CB, (n + 1) * CB)
        merged = None
        for j in range(N_BRANCHES):
            t = jax.nn.sigmoid(_dot(h, wg_ref[j, :, cols])) * _dot(branches[j], wb_ref[j, :, cols])
            merged = t if merged is None else merged + t
        t = _dot(merged.astype(BF16), wo_ref[cols, :])
        out = t if out is None else out + t
    xn = x + mod[2:3] * out
    xo_ref[0] = xn
    h2_ref[0] = _norm_mod(xn, g2_ref[...], mod[3:4], mod[4:5]).astype(BF16)


def _merge(x, mods, mod_row, g1, g2, ya, yb, yd, p, conv_w, wg, wb, wo, tm):
    b, s, d = x.shape
    hb = tm // BF16_SUBLANES
    n_halo = s // BF16_SUBLANES
    mod_idx = (lambda bi, i: (bi, 0, 0)) if mod_row is None else (lambda bi, i: (mod_row, 0, 0))
    seq = lambda width, col=0: pl.BlockSpec((1, tm, width), lambda bi, i: (bi, i, col))
    prev = lambda col: pl.BlockSpec((1, BF16_SUBLANES, CB), lambda bi, i: (bi, jnp.maximum(i * hb - 1, 0), col))
    nxt = lambda col: pl.BlockSpec((1, BF16_SUBLANES, CB),
                                   lambda bi, i: (bi, jnp.minimum((i + 1) * hb, n_halo - 1), col))
    const = lambda shape: pl.BlockSpec(shape, lambda bi, i: (0,) * len(shape))
    return pl.pallas_call(
        _merge_kernel,
        grid=(b, s // tm),
        in_specs=[seq(d), pl.BlockSpec((1, N_MOD, d), mod_idx), const((1, d)), const((1, d)),
                  seq(CB), seq(CB), seq(CB),
                  seq(CB, COL_CB), seq(CB, COL_CC), seq(CB, COL_CX),
                  prev(COL_CC), prev(COL_CX), nxt(COL_CC), nxt(COL_CX),
                  const((CONV_WIDTH, CB)),
                  const((N_BRANCHES, d, d)), const((N_BRANCHES, BRANCH_DIM, d)), const((d, d))],
        out_specs=[seq(d), seq(d)],
        out_shape=[jax.ShapeDtypeStruct((b, s, d), F32),
                   jax.ShapeDtypeStruct((b, s, d), BF16)],
        compiler_params=_params("parallel", "arbitrary"),
    )(x, mods, g1.reshape(1, d), g2.reshape(1, d), ya, yb, yd, p, p, p, p, p, p, p,
      conv_w, wg, wb, wo)


def _route_kernel(h2_ref, wr_ref, bias_ref, tri_ref, ones_ref, idx_ref, w_ref, rank_ref, cnt_ref,
                  score_sc, sel_sc, carry_sc):
    i = pl.program_id(0)
    tm = h2_ref.shape[0]
    ne = wr_ref.shape[0]
    gsz = ne // N_GROUPS
    n_chunks = tm // LANES

    @pl.when(i == 0)
    def _():
        carry_sc[...] = jnp.zeros(carry_sc.shape, F32)

    score_sc[...] = jax.nn.sigmoid(_dot_nt(wr_ref[...], h2_ref[...]))

    def select(cidx, carry):
        c0 = pl.multiple_of(cidx * LANES, LANES)
        scores = score_sc[:, pl.ds(c0, LANES)]
        biased = scores + bias_ref[...]
        liota = lax.broadcasted_iota(jnp.int32, (gsz, LANES), 0)
        gs = []
        for g in range(N_GROUPS):
            v = biased[g * gsz:(g + 1) * gsz]
            m1 = jnp.max(v, axis=0, keepdims=True)
            i1 = jnp.min(jnp.where(v == m1, liota, gsz), axis=0, keepdims=True)
            m2 = jnp.max(jnp.where(liota == i1, -jnp.inf, v), axis=0, keepdims=True)
            gs.append(m1 + m2)
        gsm = jnp.concatenate(gs, axis=0)
        giota = lax.broadcasted_iota(jnp.int32, gsm.shape, 0)
        keep = jnp.zeros(gsm.shape, F32)
        for _ in range(TOPK_GROUPS):
            m = jnp.max(gsm, axis=0, keepdims=True)
            gi = jnp.min(jnp.where(gsm == m, giota, N_GROUPS), axis=0, keepdims=True)
            hit = giota == gi
            keep = jnp.where(hit, 1.0, keep)
            gsm = jnp.where(hit, -jnp.inf, gsm)
        cur = jnp.concatenate(
            [jnp.where(jnp.broadcast_to(keep[g:g + 1], (gsz, LANES)) > 0.0, biased[g * gsz:(g + 1) * gsz], -jnp.inf)
             for g in range(N_GROUPS)], axis=0)
        eiota = lax.broadcasted_iota(jnp.int32, (ne, LANES), 0)
        sel = jnp.zeros((ne, LANES), F32)
        idxs, ws = [], []
        for _ in range(TOP_K):
            m = jnp.max(cur, axis=0, keepdims=True)
            ik = jnp.min(jnp.where(cur == m, eiota, ne), axis=0, keepdims=True)
            hit = eiota == ik
            cur = jnp.where(hit, -jnp.inf, cur)
            ws.append(jnp.sum(jnp.where(hit, scores, 0.0), axis=0, keepdims=True))
            idxs.append(ik)
            sel = jnp.where(hit, 1.0, sel)
        w = jnp.concatenate(ws, axis=0)
        w_ref[:, pl.ds(c0, LANES)] = w / jnp.sum(w, axis=0, keepdims=True) * ROUTED_SCALE
        idx_ref[:, pl.ds(c0, LANES)] = jnp.concatenate(idxs, axis=0)
        sel_sc[:, pl.ds(c0, LANES)] = sel.astype(BF16)
        return carry

    lax.fori_loop(0, n_chunks, select, 0)

    sel_all = sel_sc[...]
    score_sc[...] = _dot(sel_all, tri_ref[...]) + jnp.concatenate([carry_sc[...]] * n_chunks, axis=1)

    def ranks(cidx, carry):
        c0 = pl.multiple_of(cidx * LANES, LANES)
        before = score_sc[:, pl.ds(c0, LANES)]
        idx = idx_ref[:, pl.ds(c0, LANES)]
        eiota = lax.broadcasted_iota(jnp.int32, (ne, LANES), 0)
        rows = [jnp.sum(jnp.where(eiota == idx[k:k + 1], before, 0.0), axis=0, keepdims=True)
                for k in range(TOP_K)]
        rank_ref[:, pl.ds(c0, LANES)] = jnp.concatenate(rows, axis=0).astype(jnp.int32)
        return carry

    lax.fori_loop(0, n_chunks, ranks, 0)
    carry_sc[...] += _dot(sel_all, ones_ref[...])
    cnt_ref[...] = carry_sc[...]


def _route(h2_all, wr_t, bias, tm):
    n, d = h2_all.shape
    ne = wr_t.shape[0]
    tri = jnp.asarray(np.triu(np.ones((tm, tm), np.float32), 1), BF16)
    ones = jnp.ones((tm, LANES), BF16)
    bias_b = jnp.broadcast_to(bias.astype(F32)[:, None], (ne, LANES))
    const = lambda shape: pl.BlockSpec(shape, lambda i: (0,) * len(shape))
    tok = pl.BlockSpec((TOP_K, tm), lambda i: (0, i))
    return pl.pallas_call(
        _route_kernel,
        grid=(n // tm,),
        in_specs=[pl.BlockSpec((tm, d), lambda i: (i, 0)), const((ne, d)), const((ne, LANES)),
                  const((tm, tm)), const((tm, LANES))],
        out_specs=[tok, tok, tok, const((ne, LANES))],
        out_shape=[jax.ShapeDtypeStruct((TOP_K, n), jnp.int32),
                   jax.ShapeDtypeStruct((TOP_K, n), F32),
                   jax.ShapeDtypeStruct((TOP_K, n), jnp.int32),
                   jax.ShapeDtypeStruct((ne, LANES), F32)],
        scratch_shapes=[pltpu.VMEM((ne, tm), F32), pltpu.VMEM((ne, tm), BF16), pltpu.VMEM((ne, LANES), F32)],
        compiler_params=_params("arbitrary"),
    )(h2_all, wr_t, bias_b, tri, ones)


def _pos_kernel(idx_ref, rank_ref, start_ref, pos_ref):
    ne = start_ref.shape[0]
    start = start_ref[...]

    def body(cidx, carry):
        c0 = pl.multiple_of(cidx * LANES, LANES)
        idx = idx_ref[:, pl.ds(c0, LANES)]
        eiota = lax.broadcasted_iota(jnp.int32, (ne, LANES), 0)
        rows = [jnp.sum(jnp.where(eiota == idx[k:k + 1], start, 0.0), axis=0, keepdims=True)
                for k in range(TOP_K)]
        pos_ref[:, pl.ds(c0, LANES)] = jnp.concatenate(rows, axis=0).astype(jnp.int32) + rank_ref[:, pl.ds(c0, LANES)]
        return carry

    lax.fori_loop(0, idx_ref.shape[1] // LANES, body, 0)


def _positions(idx, rank, start_rows, tm):
    k, n = idx.shape
    ne = start_rows.shape[0]
    start_b = jnp.broadcast_to(start_rows.astype(F32)[:, None], (ne, LANES))
    tok = pl.BlockSpec((k, tm), lambda i: (0, i))
    return pl.pallas_call(
        _pos_kernel,
        grid=(n // tm,),
        in_specs=[tok, tok, pl.BlockSpec((ne, LANES), lambda i: (0, 0))],
        out_specs=tok,
        out_shape=jax.ShapeDtypeStruct((k, n), jnp.int32),
        compiler_params=_params("parallel"),
    )(idx, rank, start_b)


def _expert_kernel(crow_ref, crun_ref, rexp_ref, meta_ref, x_hbm, wg_hbm, wu_hbm, wd_hbm, y_hbm,
                   wg_buf, wu_buf, wd_buf, wg_bf, wu_bf, wd_bf, x_buf, y_buf, zero_buf,
                   w_sem, x_sem, y_sem, z_sem, *, layer):
    n_chunks, n_runs, tail_start = meta_ref[0], meta_ref[1], meta_ref[2]
    n_rows = x_hbm.shape[0]

    def w_copies(run):
        expert, s = rexp_ref[run], run % MOE_W_SLOTS
        return (pltpu.make_async_copy(wg_hbm.at[layer, expert], wg_buf.at[s], w_sem.at[s, 0]),
                pltpu.make_async_copy(wu_hbm.at[layer, expert], wu_buf.at[s], w_sem.at[s, 1]),
                pltpu.make_async_copy(wd_hbm.at[layer, expert], wd_buf.at[s], w_sem.at[s, 2]))

    def x_copy(c):
        rows = pl.ds(pl.multiple_of(crow_ref[c], MOE_ALIGN), MOE_ROWS)
        return pltpu.make_async_copy(x_hbm.at[rows], x_buf.at[c % MOE_X_SLOTS], x_sem.at[c % MOE_X_SLOTS])

    def y_copy(c):
        rows = pl.ds(pl.multiple_of(crow_ref[c], MOE_ALIGN), MOE_ROWS)
        return pltpu.make_async_copy(y_buf.at[c % 2], y_hbm.at[rows], y_sem.at[c % 2])

    def zero_copy(row0):
        return pltpu.make_async_copy(zero_buf, y_hbm.at[pl.ds(row0, MOE_ALIGN)], z_sem.at[0])

    for r in range(MOE_W_SLOTS - 1):
        @pl.when(r < n_runs)
        def _(r=r):
            for cp in w_copies(r):
                cp.start()

    for c in range(MOE_X_SLOTS - 1):
        @pl.when(c < n_chunks)
        def _(c=c):
            x_copy(c).start()

    def chunk(c, carry):
        run = crun_ref[c]
        first = jnp.logical_or(c == 0, crun_ref[jnp.maximum(c - 1, 0)] != run)

        @pl.when(c + MOE_X_SLOTS - 1 < n_chunks)
        def _():
            x_copy(c + MOE_X_SLOTS - 1).start()

        @pl.when(first)
        def _():
            @pl.when(run + MOE_W_SLOTS - 1 < n_runs)
            def _():
                for cp in w_copies(run + MOE_W_SLOTS - 1):
                    cp.start(priority=1)

            for cp in w_copies(run):
                cp.wait()
            s = run % MOE_W_SLOTS
            wg_bf[...] = wg_buf[s].astype(BF16)
            wu_bf[...] = wu_buf[s].astype(BF16)
            wd_bf[...] = wd_buf[s].astype(BF16)

        x_copy(c).wait()
        x = x_buf[c % MOE_X_SLOTS]
        a = (_silu(_dot(x, wg_bf[...])) * _dot(x, wu_bf[...])).astype(BF16)
        y = _dot(a, wd_bf[...]).astype(BF16)

        @pl.when(c > 0)
        def _():
            y_copy(c - 1).wait()

        y_buf[c % 2] = y
        y_copy(c).start()
        return carry

    lax.fori_loop(0, n_chunks, chunk, 0)

    @pl.when(n_chunks > 0)
    def _():
        y_copy(n_chunks - 1).wait()

    zero_buf[...] = jnp.zeros(zero_buf.shape, zero_buf.dtype)
    n_tail = (n_rows - tail_start) // MOE_ALIGN

    def fill(t, carry):
        zero_copy(pl.multiple_of(tail_start + t * MOE_ALIGN, MOE_ALIGN)).start()
        return carry

    def drain(t, carry):
        zero_copy(0).wait()
        return carry

    lax.fori_loop(0, n_tail, fill, 0)
    lax.fori_loop(0, n_tail, drain, 0)


def _experts(x_sorted, seg_start, counts, layer, w_g, w_u, w_d):
    n_rows, d = x_sorted.shape
    ne, f = w_g.shape[1], w_g.shape[3]
    nch = (counts + MOE_ROWS - 1) // MOE_ROWS
    c_end = jnp.cumsum(nch)
    max_chunks = (n_rows - MOE_ROWS) // MOE_ROWS + ne
    g = jnp.arange(max_chunks, dtype=jnp.int32)
    c_exp = jnp.minimum(jnp.sum((c_end[None, :] <= g[:, None]).astype(jnp.int32), axis=1), ne - 1)
    c_row = jnp.where(g < c_end[-1], seg_start[c_exp] + (g - (c_end - nch)[c_exp]) * MOE_ROWS, 0)
    has_rows = (nch > 0).astype(jnp.int32)
    run_end = jnp.cumsum(has_rows)
    c_run = (run_end - 1)[c_exp]
    r = jnp.arange(ne, dtype=jnp.int32)
    r_exp = jnp.minimum(jnp.sum((run_end[None, :] <= r[:, None]).astype(jnp.int32), axis=1), ne - 1)
    tail_start = jnp.max(jnp.where(nch > 0, seg_start + nch * MOE_ROWS, 0))
    meta = jnp.stack([c_end[-1], run_end[-1], tail_start]).astype(jnp.int32)
    any_spec = pl.BlockSpec(memory_space=pl.ANY)
    grid_spec = pltpu.PrefetchScalarGridSpec(
        num_scalar_prefetch=4,
        grid=(1,),
        in_specs=[any_spec] * 4,
        out_specs=any_spec,
        scratch_shapes=[pltpu.VMEM((MOE_W_SLOTS, d, f), F32), pltpu.VMEM((MOE_W_SLOTS, d, f), F32),
                        pltpu.VMEM((MOE_W_SLOTS, f, d), F32),
                        pltpu.VMEM((d, f), BF16), pltpu.VMEM((d, f), BF16), pltpu.VMEM((f, d), BF16),
                        pltpu.VMEM((MOE_X_SLOTS, MOE_ROWS, d), BF16), pltpu.VMEM((2, MOE_ROWS, d), BF16),
                        pltpu.VMEM((MOE_ALIGN, d), BF16),
                        pltpu.SemaphoreType.DMA((MOE_W_SLOTS, 3)), pltpu.SemaphoreType.DMA((MOE_X_SLOTS,)),
                        pltpu.SemaphoreType.DMA((2,)), pltpu.SemaphoreType.DMA((1,))])
    return pl.pallas_call(
        functools.partial(_expert_kernel, layer=layer),
        grid_spec=grid_spec,
        out_shape=jax.ShapeDtypeStruct((n_rows, d), BF16),
        compiler_params=_params("arbitrary"),
    )(c_row.astype(jnp.int32), c_run.astype(jnp.int32), r_exp, meta, x_sorted, w_g, w_u, w_d)


def _resid_kernel(x_ref, h2_ref, y_ref, w_ref, mod_ref, sg_ref, su_ref, sd_ref, gf_ref, o_ref, *, final):
    h2 = h2_ref[...]
    a = (_silu(_dot(h2, sg_ref[...])) * _dot(h2, su_ref[...])).astype(BF16)
    y = _dot(a, sd_ref[...])
    w = w_ref[...]
    for k in range(TOP_K):
        y = y + w[:, k:k + 1] * y_ref[k].astype(F32)
    xo = x_ref[...] + mod_ref[0][5:6] * y
    if final:
        xo = xo * lax.rsqrt(jnp.mean(xo * xo, axis=-1, keepdims=True) + EPS) * gf_ref[...]
    o_ref[...] = xo


def _resid(x_flat, h2_all, y_tok, w_tok, row_off, mods, mod_row, rows_per_mod, sg, su, sd, gf, final, tm):
    n, d = x_flat.shape
    f = sg.shape[1]
    off = row_off // tm
    per = rows_per_mod // tm
    mod_idx = (lambda i: (i // per, 0, 0)) if mod_row is None else (lambda i: (mod_row, 0, 0))
    const = lambda shape: pl.BlockSpec(shape, lambda i: (0,) * len(shape))
    return pl.pallas_call(
        functools.partial(_resid_kernel, final=final),
        grid=(n // tm,),
        in_specs=[pl.BlockSpec((tm, d), lambda i: (i, 0)),
                  pl.BlockSpec((tm, d), lambda i: (i + off, 0)),
                  pl.BlockSpec((TOP_K, tm, d), lambda i: (0, i + off, 0)),
                  pl.BlockSpec((tm, TOP_K), lambda i: (i + off, 0)),
                  pl.BlockSpec((1, N_MOD, d), mod_idx),
                  const((d, f)), const((d, f)), const((f, d)), const((1, d))],
        out_specs=pl.BlockSpec((tm, d), lambda i: (i, 0)),
        out_shape=jax.ShapeDtypeStruct((n, d), F32),
        compiler_params=_params("parallel"),
    )(x_flat, h2_all, y_tok, w_tok, mods, sg, su, sd, gf.reshape(1, d))


def _moe_routed(h2_all, wr_t, bias, layer, w_g, w_u, w_d):
    n, d = h2_all.shape
    idx, wts, rank, cnt = _route(h2_all, wr_t, bias, 512)
    counts = cnt[:, 0].astype(jnp.int32)
    padded = (counts + MOE_ALIGN - 1) // MOE_ALIGN * MOE_ALIGN
    seg_start = jnp.cumsum(padded) - padded
    n_rows = -(-(n * TOP_K + N_EXPERTS * (MOE_ALIGN - 1)) // MOE_ROWS) * MOE_ROWS + MOE_ROWS
    pos = _positions(idx, rank, seg_start, 512).reshape(TOP_K * n)
    tok = jnp.tile(jnp.arange(n, dtype=jnp.int32), TOP_K)
    row_tok = (jnp.arange(n_rows, dtype=jnp.int32) % n).at[pos].set(
        tok, unique_indices=True, mode='promise_in_bounds')
    x_sorted = jnp.concatenate([h2_all, h2_all], axis=0).at[row_tok].get(mode='promise_in_bounds')
    y_sorted = _experts(x_sorted, seg_start, counts, layer, w_g, w_u, w_d)
    return y_sorted.at[pos].get(mode='promise_in_bounds').reshape(TOP_K, n, d), wts.T


def kernel(x, c, ctx, c_ctx, ada_w, ada_b, norm1_g, w_in, conv_w, na_rel_bias, diff_lambda,
           diff_subln_g, w_branch_gate, w_branch, w_out, norm2_g, router_w, router_bias,
           expert_w_gate, expert_w_up, expert_w_down, shared_w_gate, shared_w_up, shared_w_down,
           final_norm_g):
    b, s, d = x.shape
    l_ctx = ctx.shape[1]
    rows = s // GRID_W
    ctx_row = b
    cvec = jnp.zeros((8, d), F32).at[:b].set(c).at[ctx_row].set(c_ctx)
    rope_tabs = _rope_tables(s)
    wf = _channel_dft_matrix()
    tm = 512
    xc = ctx
    for layer in range(DEPTH):
        last = layer == DEPTH - 1
        lam_init = 0.8 - 0.6 * math.exp(-0.3 * layer)
        mods = _ada(cvec, ada_w, ada_b, layer).reshape(8, N_MOD, d)
        w_in_bf = w_in[layer].astype(BF16)
        wg_bf = w_branch_gate[layer].astype(BF16)
        wb_bf = w_branch[layer].astype(BF16)
        wo_bf = w_out[layer].astype(BF16)
        wr_bf = router_w[layer].T.astype(BF16)
        lp = diff_lambda[layer]
        sub_g = diff_subln_g[layer]

        p, ur, ui = _inproj(x, norm1_g[layer], mods, w_in_bf, wf, rope_tabs, None, True, tm)
        ctx_tabs = tuple(t[:l_ctx] for t in rope_tabs)
        pc, ucr, uci = _inproj(xc, norm1_g[layer], mods, w_in_bf, wf, ctx_tabs, ctx_row, False, l_ctx)

        ya = _fourier_latent(ur, ui)
        yb = _na_latent(p, pc, _na_bias_tables(na_rel_bias[layer], rows))
        yd = _diff_latent(p, pc, lp, sub_g, lam_init, 1024, _key_tile(s + l_ctx, 768))
        x, h2 = _merge(x, mods, None, norm1_g[layer], norm2_g[layer], ya, yb, yd, p,
                       conv_w[layer], wg_bf, wb_bf, wo_bf, tm)
        h2_all = h2.reshape(b * s, d)
        if not last:
            yac, ybc, ydc = _ctx_branches(pc, ucr, uci, lp, sub_g, lam_init)
            xc, h2c = _merge(xc, mods, ctx_row, norm1_g[layer], norm2_g[layer], yac, ybc, ydc, pc,
                             conv_w[layer], wg_bf, wb_bf, wo_bf, l_ctx)
            h2_all = jnp.concatenate([h2_all, h2c.reshape(b * l_ctx, d)], axis=0)

        y_tok, w_tok = _moe_routed(h2_all, wr_bf, router_bias[layer], layer, expert_w_gate,
                                   expert_w_up, expert_w_down)
        sg_bf = shared_w_gate[layer].astype(BF16)
        su_bf = shared_w_up[layer].astype(BF16)
        sd_bf = shared_w_down[layer].astype(BF16)
        x = _resid(x.reshape(b * s, d), h2_all, y_tok, w_tok, 0, mods, None, s, sg_bf, su_bf, sd_bf,
                   final_norm_g, last, tm).reshape(b, s, d)
        if not last:
            xc = _resid(xc.reshape(b * l_ctx, d), h2_all, y_tok, w_tok, b * s, mods, ctx_row, l_ctx,
                        sg_bf, su_bf, sd_bf, final_norm_g, False, l_ctx).reshape(b, l_ctx, d)
    return x
```

```python
import functools
import math

import numpy as np
import jax
import jax.numpy as jnp
from jax import lax
from jax.experimental import pallas as pl
from jax.experimental.pallas import tpu as pltpu

F32 = jnp.float32
BF16 = jnp.bfloat16

DEPTH = 2
GRID_W = 64
EPS = 1e-6
N_MOD = 6

FNET_GROUP_DIM = 64
NA_HEADS = 4
NA_HEAD_DIM = 64
NA_WIN_ROWS = 8
NA_WIN_COLS = 16
CONV_WIDTH = 3
DIFF_HEADS = 4
DIFF_QK_DIM = 32
DIFF_V_DIM = 64
ROPE_BASE = 10000.0
N_BRANCHES = 4
BRANCH_DIM = 256

COL_A, COL_BQ, COL_BK, COL_BV, COL_CB, COL_CC, COL_CX, COL_DQ, COL_DK, COL_DV = range(10)
N_COL_BLOCKS = 10
CB = 256
IN_DIM = N_COL_BLOCKS * CB

N_EXPERTS = 256
TOP_K = 8
N_GROUPS = 8
TOPK_GROUPS = 4
ROUTED_SCALE = 2.5
LOG2_E = 1.4426950408889634

VMEM_LIMIT_BYTES = 56 * 1024 * 1024
LANES = 128
BF16_SUBLANES = 16
FFT_N1 = 64
NA_QROWS = 8
NA_KROWS = 16
MOE_ROWS = 256
MOE_ALIGN = BF16_SUBLANES
MOE_X_SLOTS = 4
MOE_W_SLOTS = 3


def _params(*sem):
    return pltpu.CompilerParams(dimension_semantics=sem, vmem_limit_bytes=VMEM_LIMIT_BYTES)


def _dot(a, b):
    return jnp.dot(a, b, preferred_element_type=F32)


def _dot_nt(a, b):
    return lax.dot_general(a, b, (((1,), (1,)), ((), ())), preferred_element_type=F32)


def _norm_mod(xf, g, shift, scale):
    y = xf * lax.rsqrt(jnp.mean(xf * xf, axis=-1, keepdims=True) + EPS)
    return (y * g) * (1.0 + scale) + shift


def _silu(v):
    return v * jax.nn.sigmoid(v)


def _ada_kernel(c_ref, w_ref, b_ref, o_ref):
    s = _silu(c_ref[...])
    o_ref[...] = _dot(s.astype(BF16), w_ref[0].astype(BF16)) + b_ref[0]


def _ada(cvec, w, b, layer):
    rows, d = cvec.shape
    depth, _, n = w.shape
    tn = 1536
    return pl.pallas_call(
        _ada_kernel,
        grid=(n // tn,),
        in_specs=[pl.BlockSpec((rows, d), lambda j: (0, 0)),
                  pl.BlockSpec((1, d, tn), lambda j: (layer, 0, j)),
                  pl.BlockSpec((1, 1, tn), lambda j: (layer, 0, j))],
        out_specs=pl.BlockSpec((rows, tn), lambda j: (0, j)),
        out_shape=jax.ShapeDtypeStruct((rows, n), F32),
        compiler_params=_params("arbitrary"),
    )(cvec, w, b.reshape(depth, 1, n))


def _inproj_kernel(x_ref, g_ref, mod_ref, w_ref, wf_ref, cos_ref, s1_ref, s2_ref,
                   p_ref, ur_ref, ui_ref, *, rope):
    mod = mod_ref[0]
    h = _norm_mod(x_ref[0], g_ref[...], mod[0:1], mod[1:2]).astype(BF16)
    for j in range(N_COL_BLOCKS):
        pj = _dot(h, w_ref[:, j * CB:(j + 1) * CB])
        if j == COL_A:
            u = _dot(pj.astype(BF16), wf_ref[...])
            ur_ref[0] = u[:, :CB].astype(BF16)
            ui_ref[0] = u[:, CB:].astype(BF16)
        if rope and j in (COL_DQ, COL_DK):
            cos = jnp.concatenate([cos_ref[...]] * 2, axis=1)
            s1 = jnp.concatenate([s1_ref[...]] * 2, axis=1)
            s2 = jnp.concatenate([s2_ref[...]] * 2, axis=1)
            pj = pj * cos + pltpu.roll(pj, CB - 8, 1) * s1 + pltpu.roll(pj, 8, 1) * s2
            if j == COL_DQ:
                pj = pj * (DIFF_QK_DIM ** -0.5 * LOG2_E)
        p_ref[0, :, j * CB:(j + 1) * CB] = pj.astype(BF16)


def _inproj(x, g, mods, w_bf, wf, rope_tabs, mod_row, rope, tm):
    b, s, d = x.shape
    mod_idx = (lambda bi, i: (bi, 0, 0)) if mod_row is None else (lambda bi, i: (mod_row, 0, 0))
    tab_spec = pl.BlockSpec((tm, 128), lambda bi, i: (i, 0))
    seq_spec = lambda width: pl.BlockSpec((1, tm, width), lambda bi, i: (bi, i, 0))
    return pl.pallas_call(
        functools.partial(_inproj_kernel, rope=rope),
        grid=(b, s // tm),
        in_specs=[seq_spec(d),
                  pl.BlockSpec((1, d), lambda bi, i: (0, 0)),
                  pl.BlockSpec((1, N_MOD, d), mod_idx),
                  pl.BlockSpec((d, IN_DIM), lambda bi, i: (0, 0)),
                  pl.BlockSpec((CB, 2 * CB), lambda bi, i: (0, 0)),
                  tab_spec, tab_spec, tab_spec],
        out_specs=[seq_spec(IN_DIM), seq_spec(CB), seq_spec(CB)],
        out_shape=[jax.ShapeDtypeStruct((b, s, IN_DIM), BF16),
                   jax.ShapeDtypeStruct((b, s, CB), BF16),
                   jax.ShapeDtypeStruct((b, s, CB), BF16)],
        compiler_params=_params("parallel", "arbitrary"),
    )(x, g.reshape(1, d), mods, w_bf, wf, *rope_tabs)


def _channel_dft_matrix():
    c = np.arange(FNET_GROUP_DIM)
    ang = 2.0 * np.pi * ((c[:, None] * c[None, :]) % FNET_GROUP_DIM) / FNET_GROUP_DIM
    eye = np.eye(CB // FNET_GROUP_DIM)
    m = np.concatenate([np.kron(eye, np.cos(ang)), -np.kron(eye, np.sin(ang))], axis=1)
    return jnp.asarray(m, BF16)


def _rope_tables(s):
    half = DIFF_QK_DIM // 2
    t = jnp.arange(s)
    inv = 1.0 / (ROPE_BASE ** (jnp.arange(0, half, 2, dtype=F32) / half))
    ang_r = (t // GRID_W).astype(F32)[:, None] * inv
    ang_c = (t % GRID_W).astype(F32)[:, None] * inv
    zero = jnp.zeros_like(ang_r)
    cos = jnp.concatenate([jnp.cos(ang_r)] * 2 + [jnp.cos(ang_c)] * 2, axis=1)
    s1 = jnp.concatenate([-jnp.sin(ang_r), zero, -jnp.sin(ang_c), zero], axis=1)
    s2 = jnp.concatenate([zero, jnp.sin(ang_r), zero, jnp.sin(ang_c)], axis=1)
    return tuple(jnp.concatenate([a] * 4, axis=1) for a in (cos, s1, s2))


def _fft1_kernel(ur_ref, ui_ref, w_ref, ct_ref, st_ref, ar_ref, ai_ref):
    n1 = ur_ref.shape[1]
    u = jnp.concatenate([ur_ref[0], ui_ref[0]], axis=0)
    a = _dot(w_ref[...], u)
    ar, ai = a[:n1], a[n1:]
    ct, st = ct_ref[...], st_ref[...]
    ar_ref[0] = (ar * ct + ai * st).astype(BF16)
    ai_ref[0] = (ai * ct - ar * st).astype(BF16)


def _fft2_kernel(ar_ref, ai_ref, w_ref, y_ref, *, norm):
    for j in range(ar_ref.shape[1]):
        a = jnp.concatenate([ar_ref[0, j], ai_ref[0, j]], axis=0)
        y_ref[0, j] = (_dot(w_ref[...], a) * norm).astype(BF16)


def _dft_cos_sin(n):
    k = np.arange(n)
    ang = 2.0 * np.pi * ((k[:, None] * k[None, :]) % n) / n
    return np.cos(ang), np.sin(ang)


def _fourier_latent(ur, ui):
    b, s, cb = ur.shape
    n1, n2 = FFT_N1, s // FFT_N1
    c1, s1 = _dft_cos_sin(n1)
    w1 = jnp.asarray(np.block([[c1, s1], [-s1, c1]]), BF16)
    c2, s2 = _dft_cos_sin(n2)
    w2 = jnp.asarray(np.concatenate([c2, s2], axis=1), BF16)
    tw = 2.0 * np.pi * (np.arange(n1)[:, None] * np.arange(n2)[None, :]) / s
    ct = jnp.broadcast_to(jnp.asarray(np.cos(tw), F32)[:, :, None], (n1, n2, cb)).reshape(n1, n2 * cb)
    st = jnp.broadcast_to(jnp.asarray(np.sin(tw), F32)[:, :, None], (n1, n2, cb)).reshape(n1, n2 * cb)
    lanes = n2 * cb
    tn = min(lanes, 4096)
    u_spec = pl.BlockSpec((1, n1, tn), lambda j, bi: (bi, 0, j))
    t_spec = pl.BlockSpec((n1, tn), lambda j, bi: (0, j))
    ar, ai = pl.pallas_call(
        _fft1_kernel,
        grid=(lanes // tn, b),
        in_specs=[u_spec, u_spec, pl.BlockSpec((2 * n1, 2 * n1), lambda j, bi: (0, 0)), t_spec, t_spec],
        out_specs=[u_spec, u_spec],
        out_shape=[jax.ShapeDtypeStruct((b, n1, lanes), BF16)] * 2,
        compiler_params=_params("arbitrary", "arbitrary"),
    )(ur.reshape(b, n1, lanes), ui.reshape(b, n1, lanes), w1, ct, st)
    kc = 8
    a_spec = pl.BlockSpec((1, kc, n2, cb), lambda bi, j: (bi, j, 0, 0))
    y = pl.pallas_call(
        functools.partial(_fft2_kernel, norm=1.0 / math.sqrt(s * FNET_GROUP_DIM)),
        grid=(b, n1 // kc),
        in_specs=[a_spec, a_spec, pl.BlockSpec((n2, 2 * n2), lambda bi, j: (0, 0))],
        out_specs=a_spec,
        out_shape=jax.ShapeDtypeStruct((b, n1, n2, cb), BF16),
        compiler_params=_params("parallel", "arbitrary"),
    )(ar.reshape(b, n1, n2, cb), ai.reshape(b, n1, n2, cb), w2)
    return jnp.transpose(y, (0, 2, 1, 3)).reshape(b, s, cb)


def _na_kernel(q_ref, k_ref, v_ref, kc_ref, vc_ref, bias_ref, o_ref, *, rows):
    rb = pl.program_id(1)
    kb = jnp.clip(rb * NA_QROWS - NA_WIN_ROWS // 2, 0, rows - NA_KROWS)
    nk = NA_KROWS * GRID_W
    tok0 = pl.multiple_of(kb * GRID_W, 256)
    scale = NA_HEAD_DIM ** -0.5
    outs = []
    for h in range(NA_HEADS):
        sl = slice(h * NA_HEAD_DIM, (h + 1) * NA_HEAD_DIM)
        q = q_ref[0, :, sl]
        s = _dot_nt(q, k_ref[0, pl.ds(tok0, nk), sl]) * scale + bias_ref[0, h]
        sc = _dot_nt(q, kc_ref[0, :, sl]) * scale
        m = jnp.maximum(jnp.max(s, axis=-1, keepdims=True), jnp.max(sc, axis=-1, keepdims=True))
        e = jnp.exp(s - m)
        ec = jnp.exp(sc - m)
        l = jnp.sum(e, axis=-1, keepdims=True) + jnp.sum(ec, axis=-1, keepdims=True)
        o = _dot(e.astype(BF16), v_ref[0, pl.ds(tok0, nk), sl]) + _dot(ec.astype(BF16), vc_ref[0, :, sl])
        outs.append(o / l)
    o_ref[0] = jnp.concatenate(outs, axis=1).astype(BF16)


def _na_bias_tables(rel_bias, rows):
    tabs = []
    cq = np.arange(GRID_W)
    col_lo = np.clip(cq - NA_WIN_COLS // 2, 0, GRID_W - NA_WIN_COLS)
    col_ok = (cq[None, :] >= col_lo[:, None]) & (cq[None, :] < col_lo[:, None] + NA_WIN_COLS)
    dc_idx = np.clip(cq[None, :] - cq[:, None] + NA_WIN_COLS - 1, 0, 2 * NA_WIN_COLS - 2)
    for r0 in (0, NA_QROWS, rows - NA_QROWS):
        kb = int(np.clip(r0 - NA_WIN_ROWS // 2, 0, rows - NA_KROWS))
        r = r0 + np.arange(NA_QROWS)
        rk = kb + np.arange(NA_KROWS)
        start = np.clip(r - NA_WIN_ROWS // 2, 0, rows - NA_WIN_ROWS)
        row_ok = (rk[None, :] >= start[:, None]) & (rk[None, :] < start[:, None] + NA_WIN_ROWS)
        dr_idx = np.clip(rk[None, :] - r[:, None] + NA_WIN_ROWS - 1, 0, 2 * NA_WIN_ROWS - 2)
        ok = row_ok[:, None, :, None] & col_ok[None, :, None, :]
        oh_r = jnp.asarray(dr_idx[:, :, None] == np.arange(2 * NA_WIN_ROWS - 1), F32)
        oh_c = jnp.asarray(dc_idx[:, :, None] == np.arange(2 * NA_WIN_COLS - 1), F32)
        by_row = jnp.einsum('qka,hab->hqkb', oh_r, rel_bias.astype(F32), precision=lax.Precision.HIGHEST)
        vals = jnp.einsum('hqkb,cdb->hqckd', by_row, oh_c, precision=lax.Precision.HIGHEST)
        tab = jnp.where(ok[None], vals, -jnp.inf)
        tabs.append(tab.reshape(NA_HEADS, NA_QROWS * GRID_W, NA_KROWS * GRID_W))
    return jnp.stack(tabs)


def _na_latent(p, pc, bias_tabs):
    b, s, _ = p.shape
    l = pc.shape[1]
    rows = s // GRID_W
    nrb = rows // NA_QROWS
    tq = NA_QROWS * GRID_W
    nk = NA_KROWS * GRID_W
    return pl.pallas_call(
        functools.partial(_na_kernel, rows=rows),
        grid=(b, nrb),
        in_specs=[pl.BlockSpec((1, tq, CB), lambda bi, i: (bi, i, COL_BQ)),
                  pl.BlockSpec((1, s, CB), lambda bi, i: (bi, 0, COL_BK)),
                  pl.BlockSpec((1, s, CB), lambda bi, i: (bi, 0, COL_BV)),
                  pl.BlockSpec((1, l, CB), lambda bi, i: (bi, 0, COL_BK)),
                  pl.BlockSpec((1, l, CB), lambda bi, i: (bi, 0, COL_BV)),
                  pl.BlockSpec((1, NA_HEADS, tq, nk),
                               lambda bi, i: (jnp.minimum(i, 1) + (i == nrb - 1).astype(jnp.int32), 0, 0, 0))],
        out_specs=pl.BlockSpec((1, tq, CB), lambda bi, i: (bi, i, 0)),
        out_shape=jax.ShapeDtypeStruct((b, s, CB), BF16),
        compiler_params=_params("parallel", "arbitrary"),
    )(p, p, p, pc, pc, bias_tabs)


def _diff_lambda(lp, lam_init):
    return (jnp.exp(jnp.sum(lp[0:1] * lp[1:2], axis=-1, keepdims=True))
            - jnp.exp(jnp.sum(lp[2:3] * lp[3:4], axis=-1, keepdims=True)) + lam_init)


def _diff_finish(o, g, lam_init):
    y = o * lax.rsqrt(jnp.mean(o * o, axis=-1, keepdims=True) + EPS)
    return y * g * (1.0 - lam_init)


def _diff_kernel(q_ref, kt_ref, v_ref, lp_ref, g_ref, o_ref, m_sc, acc_sc, *, tk, lam_init):
    nk = kt_ref.shape[2] // tk
    n_maps = 2 * DIFF_HEADS
    m_sc[...] = jnp.full(m_sc.shape, -jnp.inf, F32)
    acc_sc[...] = jnp.zeros(acc_sc.shape, F32)

    def body(c, carry):
        k0 = pl.multiple_of(c * tk, LANES)
        for hm in range(n_maps):
            dims = slice(hm * DIFF_QK_DIM, (hm + 1) * DIFF_QK_DIM)
            s = _dot(q_ref[0, :, dims], kt_ref[0, dims, pl.ds(k0, tk)])
            m_old = m_sc[hm]
            m_new = jnp.maximum(m_old, jnp.max(s, axis=-1, keepdims=True))
            e = jnp.exp2(s - m_new[:, :1]).astype(BF16)
            acc_sc[hm] = jnp.exp2(m_old - m_new) * acc_sc[hm] + _dot(e, v_ref[0, hm // 2, pl.ds(k0, tk), :])
            m_sc[hm] = m_new
        return carry

    lax.fori_loop(0, nk, body, 0)
    maps = []
    for hm in range(n_maps):
        acc = acc_sc[hm]
        maps.append(acc[:, :DIFF_V_DIM] / acc[:, DIFF_V_DIM:DIFF_V_DIM + 1])
    lam = _diff_lambda(lp_ref[...], lam_init)
    outs = [_diff_finish(maps[2 * h] - lam * maps[2 * h + 1], g_ref[...], lam_init) for h in range(DIFF_HEADS)]
    o_ref[0] = jnp.concatenate(outs, axis=1).astype(BF16)


def _key_tile(nkeys, cap):
    return max(t for t in range(128, cap + 1, 128) if nkeys % t == 0)


def _diff_latent(p, pc, lp, sub_g, lam_init, tq, tk):
    b, s, _ = p.shape
    l = pc.shape[1]
    nkeys = s + l
    k_all = jnp.concatenate([p[:, :, COL_DK * CB:(COL_DK + 1) * CB], pc[:, :, COL_DK * CB:(COL_DK + 1) * CB]], axis=1)
    v_all = jnp.concatenate([p[:, :, COL_DV * CB:], pc[:, :, COL_DV * CB:]], axis=1)
    kt = jnp.transpose(k_all, (0, 2, 1))
    vh = jnp.transpose(v_all.reshape(b, nkeys, DIFF_HEADS, DIFF_V_DIM), (0, 2, 1, 3))
    vh = jnp.concatenate([vh, jnp.ones((b, DIFF_HEADS, nkeys, LANES - DIFF_V_DIM), BF16)], axis=-1)
    return pl.pallas_call(
        functools.partial(_diff_kernel, tk=tk, lam_init=lam_init),
        grid=(b, s // tq),
        in_specs=[pl.BlockSpec((1, tq, CB), lambda bi, i: (bi, i, COL_DQ)),
                  pl.BlockSpec((1, CB, nkeys), lambda bi, i: (bi, 0, 0)),
                  pl.BlockSpec((1, DIFF_HEADS, nkeys, LANES), lambda bi, i: (bi, 0, 0, 0)),
                  pl.BlockSpec((4, DIFF_QK_DIM), lambda bi, i: (0, 0)),
                  pl.BlockSpec((1, DIFF_V_DIM), lambda bi, i: (0, 0))],
        out_specs=pl.BlockSpec((1, tq, CB), lambda bi, i: (bi, i, 0)),
        out_shape=jax.ShapeDtypeStruct((b, s, CB), BF16),
        scratch_shapes=[pltpu.VMEM((2 * DIFF_HEADS, tq, LANES), F32),
                        pltpu.VMEM((2 * DIFF_HEADS, tq, LANES), F32)],
        compiler_params=_params("parallel", "arbitrary"),
    )(p, kt, vh, lp, sub_g.reshape(1, DIFF_V_DIM))


def _softmax_rows(s):
    e = jnp.exp(s - jnp.max(s, axis=-1, keepdims=True))
    return e / jnp.sum(e, axis=-1, keepdims=True)


def _ctx_kernel(pc_ref, ur_ref, ui_ref, wf_ref, lp_ref, g_ref, ya_ref, yb_ref, yd_ref, *, lam_init):
    l = pc_ref.shape[1]
    col = lambda j, lo, hi: pc_ref[0, :, j * CB + lo:j * CB + hi]
    u = jnp.concatenate([ur_ref[0], ui_ref[0]], axis=0)
    ya_ref[0] = (_dot(wf_ref[...], u) * (1.0 / math.sqrt(l * FNET_GROUP_DIM))).astype(BF16)
    outs = []
    for h in range(NA_HEADS):
        lo, hi = h * NA_HEAD_DIM, (h + 1) * NA_HEAD_DIM
        pr = _softmax_rows(_dot_nt(col(COL_BQ, lo, hi), col(COL_BK, lo, hi)) * NA_HEAD_DIM ** -0.5)
        outs.append(_dot(pr.astype(BF16), col(COL_BV, lo, hi)))
    yb_ref[0] = jnp.concatenate(outs, axis=1).astype(BF16)
    lam = _diff_lambda(lp_ref[...], lam_init)
    outs = []
    for h in range(DIFF_HEADS):
        pm = []
        for m in range(2):
            lo = (2 * h + m) * DIFF_QK_DIM
            pm.append(_softmax_rows(_dot_nt(col(COL_DQ, lo, lo + DIFF_QK_DIM), col(COL_DK, lo, lo + DIFF_QK_DIM))
                                    * DIFF_QK_DIM ** -0.5))
        a = (pm[0] - lam * pm[1]).astype(BF16)
        o = _dot(a, col(COL_DV, h * DIFF_V_DIM, (h + 1) * DIFF_V_DIM))
        outs.append(_diff_finish(o, g_ref[...], lam_init))
    yd_ref[0] = jnp.concatenate(outs, axis=1).astype(BF16)


def _ctx_branches(pc, ucr, uci, lp, sub_g, lam_init):
    b, l, _ = pc.shape
    c, s = _dft_cos_sin(l)
    wf = jnp.asarray(np.concatenate([c, s], axis=1), BF16)
    y_spec = pl.BlockSpec((1, l, CB), lambda bi: (bi, 0, 0))
    return pl.pallas_call(
        functools.partial(_ctx_kernel, lam_init=lam_init),
        grid=(b,),
        in_specs=[pl.BlockSpec((1, l, IN_DIM), lambda bi: (bi, 0, 0)), y_spec, y_spec,
                  pl.BlockSpec((l, 2 * l), lambda bi: (0, 0)),
                  pl.BlockSpec((4, DIFF_QK_DIM), lambda bi: (0, 0)),
                  pl.BlockSpec((1, DIFF_V_DIM), lambda bi: (0, 0))],
        out_specs=[y_spec] * 3,
        out_shape=[jax.ShapeDtypeStruct((b, l, CB), BF16)] * 3,
        compiler_params=_params("parallel"),
    )(pc, ucr, uci, wf, lp, sub_g.reshape(1, DIFF_V_DIM))


def _merge_kernel(x_ref, mod_ref, g1_ref, g2_ref, ya_ref, yb_ref, yd_ref, pb_ref, pc_ref, px_ref,
                  cp_ref, xp_ref, cn_ref, xn_ref, cw_ref, wg_ref, wb_ref, wo_ref,
                  xo_ref, h2_ref):
    i = pl.program_id(1)
    last = pl.num_programs(1) - 1
    tm = x_ref.shape[1]
    mod = mod_ref[0]
    x = x_ref[0]
    h = _norm_mod(x, g1_ref[...], mod[0:1], mod[1:2]).astype(BF16)

    u = pc_ref[0].astype(F32) * px_ref[0].astype(F32)
    up = cp_ref[0, BF16_SUBLANES - 1:, :].astype(F32) * xp_ref[0, BF16_SUBLANES - 1:, :].astype(F32)
    un = cn_ref[0, :1, :].astype(F32) * xn_ref[0, :1, :].astype(F32)
    up = jnp.where(i == 0, 0.0, up)
    un = jnp.where(i == last, 0.0, un)
    rid = lax.broadcasted_iota(jnp.int32, u.shape, 0)
    u_prev = jnp.where(rid == 0, up, pltpu.roll(u, 1, 0))
    u_next = jnp.where(rid == tm - 1, un, pltpu.roll(u, tm - 1, 0))
    cw = cw_ref[...]
    yc = pb_ref[0].astype(F32) * (cw[0:1] * u_prev + cw[1:2] * u + cw[2:3] * u_next)

    branches = (ya_ref[0], yb_ref[0], yc.astype(BF16), yd_ref[0])
    d = x.shape[1]
    out = None
    for n in range(d // CB):
        cols = slice(n * CB, (n + 1) * CB)
        merged = None
        for j in range(N_BRANCHES):
            t = jax.nn.sigmoid(_dot(h, wg_ref[j, :, cols])) * _dot(branches[j], wb_ref[j, :, cols])
            merged = t if merged is None else merged + t
        t = _dot(merged.astype(BF16), wo_ref[cols, :])
        out = t if out is None else out + t
    xn = x + mod[2:3] * out
    xo_ref[0] = xn
    h2_ref[0] = _norm_mod(xn, g2_ref[...], mod[3:4], mod[4:5]).astype(BF16)


def _merge(x, mods, mod_row, g1, g2, ya, yb, yd, p, conv_w, wg, wb, wo, tm):
    b, s, d = x.shape
    hb = tm // BF16_SUBLANES
    n_halo = s // BF16_SUBLANES
    mod_idx = (lambda bi, i: (bi, 0, 0)) if mod_row is None else (lambda bi, i: (mod_row, 0, 0))
    seq = lambda width, col=0: pl.BlockSpec((1, tm, width), lambda bi, i: (bi, i, col))
    prev = lambda col: pl.BlockSpec((1, BF16_SUBLANES, CB), lambda bi, i: (bi, jnp.maximum(i * hb - 1, 0), col))
    nxt = lambda col: pl.BlockSpec((1, BF16_SUBLANES, CB),
                                   lambda bi, i: (bi, jnp.minimum((i + 1) * hb, n_halo - 1), col))
    const = lambda shape: pl.BlockSpec(shape, lambda bi, i: (0,) * len(shape))
    return pl.pallas_call(
        _merge_kernel,
        grid=(b, s // tm),
        in_specs=[seq(d), pl.BlockSpec((1, N_MOD, d), mod_idx), const((1, d)), const((1, d)),
                  seq(CB), seq(CB), seq(CB),
                  seq(CB, COL_CB), seq(CB, COL_CC), seq(CB, COL_CX),
                  prev(COL_CC), prev(COL_CX), nxt(COL_CC), nxt(COL_CX),
                  const((CONV_WIDTH, CB)),
                  const((N_BRANCHES, d, d)), const((N_BRANCHES, BRANCH_DIM, d)), const((d, d))],
        out_specs=[seq(d), seq(d)],
        out_shape=[jax.ShapeDtypeStruct((b, s, d), F32),
                   jax.ShapeDtypeStruct((b, s, d), BF16)],
        compiler_params=_params("parallel", "arbitrary"),
    )(x, mods, g1.reshape(1, d), g2.reshape(1, d), ya, yb, yd, p, p, p, p, p, p, p,
      conv_w, wg, wb, wo)


def _route_kernel(h2_ref, wr_ref, bias_ref, tri_ref, ones_ref, idx_ref, w_ref, rank_ref, cnt_ref,
                  score_sc, sel_sc, carry_sc):
    i = pl.program_id(0)
    tm = h2_ref.shape[0]
    ne = wr_ref.shape[0]
    gsz = ne // N_GROUPS
    n_chunks = tm // LANES

    @pl.when(i == 0)
    def _():
        carry_sc[...] = jnp.zeros(carry_sc.shape, F32)

    score_sc[...] = jax.nn.sigmoid(_dot_nt(wr_ref[...], h2_ref[...]))

    def select(cidx, carry):
        c0 = pl.multiple_of(cidx * LANES, LANES)
        scores = score_sc[:, pl.ds(c0, LANES)]
        biased = scores + bias_ref[...]
        liota = lax.broadcasted_iota(jnp.int32, (gsz, LANES), 0)
        gs = []
        for g in range(N_GROUPS):
            v = biased[g * gsz:(g + 1) * gsz]
            m1 = jnp.max(v, axis=0, keepdims=True)
            i1 = jnp.min(jnp.where(v == m1, liota, gsz), axis=0, keepdims=True)
            m2 = jnp.max(jnp.where(liota == i1, -jnp.inf, v), axis=0, keepdims=True)
            gs.append(m1 + m2)
        gsm = jnp.concatenate(gs, axis=0)
        giota = lax.broadcasted_iota(jnp.int32, gsm.shape, 0)
        keep = jnp.zeros(gsm.shape, F32)
        for _ in range(TOPK_GROUPS):
            m = jnp.max(gsm, axis=0, keepdims=True)
            gi = jnp.min(jnp.where(gsm == m, giota, N_GROUPS), axis=0, keepdims=True)
            hit = giota == gi
            keep = jnp.where(hit, 1.0, keep)
            gsm = jnp.where(hit, -jnp.inf, gsm)
        cur = jnp.concatenate(
            [jnp.where(jnp.broadcast_to(keep[g:g + 1], (gsz, LANES)) > 0.0, biased[g * gsz:(g + 1) * gsz], -jnp.inf)
             for g in range(N_GROUPS)], axis=0)
        eiota = lax.broadcasted_iota(jnp.int32, (ne, LANES), 0)
        sel = jnp.zeros((ne, LANES), F32)
        idxs, ws = [], []
        for _ in range(TOP_K):
            m = jnp.max(cur, axis=0, keepdims=True)
            ik = jnp.min(jnp.where(cur == m, eiota, ne), axis=0, keepdims=True)
            hit = eiota == ik
            cur = jnp.where(hit, -jnp.inf, cur)
            ws.append(jnp.sum(jnp.where(hit, scores, 0.0), axis=0, keepdims=True))
            idxs.append(ik)
            sel = jnp.where(hit, 1.0, sel)
        w = jnp.concatenate(ws, axis=0)
        w_ref[:, pl.ds(c0, LANES)] = w / jnp.sum(w, axis=0, keepdims=True) * ROUTED_SCALE
        idx_ref[:, pl.ds(c0, LANES)] = jnp.concatenate(idxs, axis=0)
        sel_sc[:, pl.ds(c0, LANES)] = sel.astype(BF16)
        return carry

    lax.fori_loop(0, n_chunks, select, 0)

    sel_all = sel_sc[...]
    score_sc[...] = _dot(sel_all, tri_ref[...]) + jnp.concatenate([carry_sc[...]] * n_chunks, axis=1)

    def ranks(cidx, carry):
        c0 = pl.multiple_of(cidx * LANES, LANES)
        before = score_sc[:, pl.ds(c0, LANES)]
        idx = idx_ref[:, pl.ds(c0, LANES)]
        eiota = lax.broadcasted_iota(jnp.int32, (ne, LANES), 0)
        rows = [jnp.sum(jnp.where(eiota == idx[k:k + 1], before, 0.0), axis=0, keepdims=True)
                for k in range(TOP_K)]
        rank_ref[:, pl.ds(c0, LANES)] = jnp.concatenate(rows, axis=0).astype(jnp.int32)
        return carry

    lax.fori_loop(0, n_chunks, ranks, 0)
    carry_sc[...] += _dot(sel_all, ones_ref[...])
    cnt_ref[...] = carry_sc[...]


def _route(h2_all, wr_t, bias, tm):
    n, d = h2_all.shape
    ne = wr_t.shape[0]
    tri = jnp.asarray(np.triu(np.ones((tm, tm), np.float32), 1), BF16)
    ones = jnp.ones((tm, LANES), BF16)
    bias_b = jnp.broadcast_to(bias.astype(F32)[:, None], (ne, LANES))
    const = lambda shape: pl.BlockSpec(shape, lambda i: (0,) * len(shape))
    tok = pl.BlockSpec((TOP_K, tm), lambda i: (0, i))
    return pl.pallas_call(
        _route_kernel,
        grid=(n // tm,),
        in_specs=[pl.BlockSpec((tm, d), lambda i: (i, 0)), const((ne, d)), const((ne, LANES)),
                  const((tm, tm)), const((tm, LANES))],
        out_specs=[tok, tok, tok, const((ne, LANES))],
        out_shape=[jax.ShapeDtypeStruct((TOP_K, n), jnp.int32),
                   jax.ShapeDtypeStruct((TOP_K, n), F32),
                   jax.ShapeDtypeStruct((TOP_K, n), jnp.int32),
                   jax.ShapeDtypeStruct((ne, LANES), F32)],
        scratch_shapes=[pltpu.VMEM((ne, tm), F32), pltpu.VMEM((ne, tm), BF16), pltpu.VMEM((ne, LANES), F32)],
        compiler_params=_params("arbitrary"),
    )(h2_all, wr_t, bias_b, tri, ones)


def _pos_kernel(idx_ref, rank_ref, start_ref, pos_ref):
    ne = start_ref.shape[0]
    start = start_ref[...]

    def body(cidx, carry):
        c0 = pl.multiple_of(cidx * LANES, LANES)
        idx = idx_ref[:, pl.ds(c0, LANES)]
        eiota = lax.broadcasted_iota(jnp.int32, (ne, LANES), 0)
        rows = [jnp.sum(jnp.where(eiota == idx[k:k + 1], start, 0.0), axis=0, keepdims=True)
                for k in range(TOP_K)]
        pos_ref[:, pl.ds(c0, LANES)] = jnp.concatenate(rows, axis=0).astype(jnp.int32) + rank_ref[:, pl.ds(c0, LANES)]
        return carry

    lax.fori_loop(0, idx_ref.shape[1] // LANES, body, 0)


def _positions(idx, rank, start_rows, tm):
    k, n = idx.shape
    ne = start_rows.shape[0]
    start_b = jnp.broadcast_to(start_rows.astype(F32)[:, None], (ne, LANES))
    tok = pl.BlockSpec((k, tm), lambda i: (0, i))
    return pl.pallas_call(
        _pos_kernel,
        grid=(n // tm,),
        in_specs=[tok, tok, pl.BlockSpec((ne, LANES), lambda i: (0, 0))],
        out_specs=tok,
        out_shape=jax.ShapeDtypeStruct((k, n), jnp.int32),
        compiler_params=_params("parallel"),
    )(idx, rank, start_b)


def _rowtok_kernel(start_ref, cnt_ref, pos_ref, out_ref, *, n_tokens):
    i = pl.program_id(0)
    tn = pos_ref.shape[1]
    n_rows = out_ref.shape[0]
    ne = start_ref.shape[0]

    @pl.when(i == 0)
    def _():
        def gaps(e, carry):
            lo = start_ref[e] + cnt_ref[e]
            hi = jnp.where(e + 1 < ne, start_ref[jnp.minimum(e + 1, ne - 1)], n_rows)

            def one(r, c):
                out_ref[r] = lax.rem(r, n_tokens)
                return c

            return lax.fori_loop(lo, hi, one, carry)

        lax.fori_loop(0, ne, gaps, 0)

    def body(t, carry):
        for k in range(TOP_K):
            out_ref[pos_ref[k, t]] = i * tn + t
        return carry

    lax.fori_loop(0, tn, body, 0)


def _row_tokens(pos, seg_start, counts, n_rows):
    k, n = pos.shape
    tn = max(t for t in range(LANES, 2048 + 1, LANES) if n % t == 0)
    grid_spec = pltpu.PrefetchScalarGridSpec(
        num_scalar_prefetch=2,
        grid=(n // tn,),
        in_specs=[pl.BlockSpec((k, tn), lambda i, st, ct: (0, i), memory_space=pltpu.SMEM)],
        out_specs=pl.BlockSpec((n_rows,), lambda i, st, ct: (0,), memory_space=pltpu.SMEM))
    return pl.pallas_call(
        functools.partial(_rowtok_kernel, n_tokens=n),
        grid_spec=grid_spec,
        out_shape=jax.ShapeDtypeStruct((n_rows,), jnp.int32),
        compiler_params=_params("arbitrary"),
    )(seg_start, counts, pos)


def _expert_kernel(crow_ref, crun_ref, rexp_ref, meta_ref, x_hbm, wg_hbm, wu_hbm, wd_hbm, y_hbm,
                   wg_buf, wu_buf, wd_buf, wg_bf, wu_bf, wd_bf, x_buf, y_buf, zero_buf,
                   w_sem, x_sem, y_sem, z_sem, *, layer):
    n_chunks, n_runs, tail_start = meta_ref[0], meta_ref[1], meta_ref[2]
    n_rows = x_hbm.shape[0]

    def w_copies(run):
        expert, s = rexp_ref[run], run % MOE_W_SLOTS
        return (pltpu.make_async_copy(wg_hbm.at[layer, expert], wg_buf.at[s], w_sem.at[s, 0]),
                pltpu.make_async_copy(wu_hbm.at[layer, expert], wu_buf.at[s], w_sem.at[s, 1]),
                pltpu.make_async_copy(wd_hbm.at[layer, expert], wd_buf.at[s], w_sem.at[s, 2]))

    def x_copy(c):
        rows = pl.ds(pl.multiple_of(crow_ref[c], MOE_ALIGN), MOE_ROWS)
        return pltpu.make_async_copy(x_hbm.at[rows], x_buf.at[c % MOE_X_SLOTS], x_sem.at[c % MOE_X_SLOTS])

    def y_copy(c):
        rows = pl.ds(pl.multiple_of(crow_ref[c], MOE_ALIGN), MOE_ROWS)
        return pltpu.make_async_copy(y_buf.at[c % 2], y_hbm.at[rows], y_sem.at[c % 2])

    def zero_copy(row0):
        return pltpu.make_async_copy(zero_buf, y_hbm.at[pl.ds(row0, MOE_ALIGN)], z_sem.at[0])

    for r in range(MOE_W_SLOTS - 1):
        @pl.when(r < n_runs)
        def _(r=r):
            for cp in w_copies(r):
                cp.start()

    for c in range(MOE_X_SLOTS - 1):
        @pl.when(c < n_chunks)
        def _(c=c):
            x_copy(c).start()

    def chunk(c, carry):
        run = crun_ref[c]
        first = jnp.logical_or(c == 0, crun_ref[jnp.maximum(c - 1, 0)] != run)

        @pl.when(c + MOE_X_SLOTS - 1 < n_chunks)
        def _():
            x_copy(c + MOE_X_SLOTS - 1).start()

        @pl.when(first)
        def _():
            @pl.when(run + MOE_W_SLOTS - 1 < n_runs)
            def _():
                for cp in w_copies(run + MOE_W_SLOTS - 1):
                    cp.start(priority=1)

            for cp in w_copies(run):
                cp.wait()
            s = run % MOE_W_SLOTS
            wg_bf[...] = wg_buf[s].astype(BF16)
            wu_bf[...] = wu_buf[s].astype(BF16)
            wd_bf[...] = wd_buf[s].astype(BF16)

        x_copy(c).wait()
        x = x_buf[c % MOE_X_SLOTS]
        a = (_silu(_dot(x, wg_bf[...])) * _dot(x, wu_bf[...])).astype(BF16)
        y = _dot(a, wd_bf[...]).astype(BF16)

        @pl.when(c > 0)
        def _():
            y_copy(c - 1).wait()

        y_buf[c % 2] = y
        y_copy(c).start()
        return carry

    lax.fori_loop(0, n_chunks, chunk, 0)

    @pl.when(n_chunks > 0)
    def _():
        y_copy(n_chunks - 1).wait()

    zero_buf[...] = jnp.zeros(zero_buf.shape, zero_buf.dtype)
    n_tail = (n_rows - tail_start) // MOE_ALIGN

    def fill(t, carry):
        zero_copy(pl.multiple_of(tail_start + t * MOE_ALIGN, MOE_ALIGN)).start()
        return carry

    def drain(t, carry):
        zero_copy(0).wait()
        return carry

    lax.fori_loop(0, n_tail, fill, 0)
    lax.fori_loop(0, n_tail, drain, 0)


def _experts(x_sorted, seg_start, counts, layer, w_g, w_u, w_d):
    n_rows, d = x_sorted.shape
    ne, f = w_g.shape[1], w_g.shape[3]
    nch = (counts + MOE_ROWS - 1) // MOE_ROWS
    c_end = jnp.cumsum(nch)
    max_chunks = (n_rows - MOE_ROWS) // MOE_ROWS + ne
    g = jnp.arange(max_chunks, dtype=jnp.int32)
    c_exp = jnp.minimum(jnp.sum((c_end[None, :] <= g[:, None]).astype(jnp.int32), axis=1), ne - 1)
    c_row = jnp.where(g < c_end[-1], seg_start[c_exp] + (g - (c_end - nch)[c_exp]) * MOE_ROWS, 0)
    has_rows = (nch > 0).astype(jnp.int32)
    run_end = jnp.cumsum(has_rows)
    c_run = (run_end - 1)[c_exp]
    r = jnp.arange(ne, dtype=jnp.int32)
    r_exp = jnp.minimum(jnp.sum((run_end[None, :] <= r[:, None]).astype(jnp.int32), axis=1), ne - 1)
    tail_start = jnp.max(jnp.where(nch > 0, seg_start + nch * MOE_ROWS, 0))
    meta = jnp.stack([c_end[-1], run_end[-1], tail_start]).astype(jnp.int32)
    any_spec = pl.BlockSpec(memory_space=pl.ANY)
    grid_spec = pltpu.PrefetchScalarGridSpec(
        num_scalar_prefetch=4,
        grid=(1,),
        in_specs=[any_spec] * 4,
        out_specs=any_spec,
        scratch_shapes=[pltpu.VMEM((MOE_W_SLOTS, d, f), F32), pltpu.VMEM((MOE_W_SLOTS, d, f), F32),
                        pltpu.VMEM((MOE_W_SLOTS, f, d), F32),
                        pltpu.VMEM((d, f), BF16), pltpu.VMEM((d, f), BF16), pltpu.VMEM((f, d), BF16),
                        pltpu.VMEM((MOE_X_SLOTS, MOE_ROWS, d), BF16), pltpu.VMEM((2, MOE_ROWS, d), BF16),
                        pltpu.VMEM((MOE_ALIGN, d), BF16),
                        pltpu.SemaphoreType.DMA((MOE_W_SLOTS, 3)), pltpu.SemaphoreType.DMA((MOE_X_SLOTS,)),
                        pltpu.SemaphoreType.DMA((2,)), pltpu.SemaphoreType.DMA((1,))])
    return pl.pallas_call(
        functools.partial(_expert_kernel, layer=layer),
        grid_spec=grid_spec,
        out_shape=jax.ShapeDtypeStruct((n_rows, d), BF16),
        compiler_params=_params("arbitrary"),
    )(c_row.astype(jnp.int32), c_run.astype(jnp.int32), r_exp, meta, x_sorted, w_g, w_u, w_d)


def _resid_kernel(x_ref, h2_ref, y_ref, w_ref, mod_ref, sg_ref, su_ref, sd_ref, gf_ref, o_ref, *, final):
    h2 = h2_ref[...]
    a = (_silu(_dot(h2, sg_ref[...])) * _dot(h2, su_ref[...])).astype(BF16)
    y = _dot(a, sd_ref[...])
    w = w_ref[...]
    for k in range(TOP_K):
        y = y + w[:, k:k + 1] * y_ref[k].astype(F32)
    xo = x_ref[...] + mod_ref[0][5:6] * y
    if final:
        xo = xo * lax.rsqrt(jnp.mean(xo * xo, axis=-1, keepdims=True) + EPS) * gf_ref[...]
    o_ref[...] = xo


def _resid(x_flat, h2_all, y_tok, w_tok, row_off, mods, mod_row, rows_per_mod, sg, su, sd, gf, final, tm):
    n, d = x_flat.shape
    f = sg.shape[1]
    off = row_off // tm
    per = rows_per_mod // tm
    mod_idx = (lambda i: (i // per, 0, 0)) if mod_row is None else (lambda i: (mod_row, 0, 0))
    const = lambda shape: pl.BlockSpec(shape, lambda i: (0,) * len(shape))
    return pl.pallas_call(
        functools.partial(_resid_kernel, final=final),
        grid=(n // tm,),
        in_specs=[pl.BlockSpec((tm, d), lambda i: (i, 0)),
                  pl.BlockSpec((tm, d), lambda i: (i + off, 0)),
                  pl.BlockSpec((TOP_K, tm, d), lambda i: (0, i + off, 0)),
                  pl.BlockSpec((tm, TOP_K), lambda i: (i + off, 0)),
                  pl.BlockSpec((1, N_MOD, d), mod_idx),
                  const((d, f)), const((d, f)), const((f, d)), const((1, d))],
        out_specs=pl.BlockSpec((tm, d), lambda i: (i, 0)),
        out_shape=jax.ShapeDtypeStruct((n, d), F32),
        compiler_params=_params("parallel"),
    )(x_flat, h2_all, y_tok, w_tok, mods, sg, su, sd, gf.reshape(1, d))


def _moe_routed(h2_all, wr_t, bias, layer, w_g, w_u, w_d):
    n, d = h2_all.shape
    idx, wts, rank, cnt = _route(h2_all, wr_t, bias, 512)
    counts = cnt[:, 0].astype(jnp.int32)
    padded = (counts + MOE_ALIGN - 1) // MOE_ALIGN * MOE_ALIGN
    seg_start = jnp.cumsum(padded) - padded
    n_rows = -(-(n * TOP_K + N_EXPERTS * (MOE_ALIGN - 1)) // MOE_ROWS) * MOE_ROWS + MOE_ROWS
    pos2 = _positions(idx, rank, seg_start, 512)
    pos = pos2.reshape(TOP_K * n)
    row_tok = _row_tokens(pos2, seg_start, counts, n_rows)
    x_sorted = jnp.concatenate([h2_all, h2_all], axis=0).at[row_tok].get(mode='promise_in_bounds')
    y_sorted = _experts(x_sorted, seg_start, counts, layer, w_g, w_u, w_d)
    return y_sorted.at[pos].get(mode='promise_in_bounds').reshape(TOP_K, n, d), wts.T


def kernel(x, c, ctx, c_ctx, ada_w, ada_b, norm1_g, w_in, conv_w, na_rel_bias, diff_lambda,
           diff_subln_g, w_branch_gate, w_branch, w_out, norm2_g, router_w, router_bias,
           expert_w_gate, expert_w_up, expert_w_down, shared_w_gate, shared_w_up, shared_w_down,
           final_norm_g):
    b, s, d = x.shape
    l_ctx = ctx.shape[1]
    rows = s // GRID_W
    ctx_row = b
    cvec = jnp.zeros((8, d), F32).at[:b].set(c).at[ctx_row].set(c_ctx)
    rope_tabs = _rope_tables(s)
    wf = _channel_dft_matrix()
    tm = 512
    xc = ctx
    for layer in range(DEPTH):
        last = layer == DEPTH - 1
        lam_init = 0.8 - 0.6 * math.exp(-0.3 * layer)
        mods = _ada(cvec, ada_w, ada_b, layer).reshape(8, N_MOD, d)
        w_in_bf = w_in[layer].astype(BF16)
        wg_bf = w_branch_gate[layer].astype(BF16)
        wb_bf = w_branch[layer].astype(BF16)
        wo_bf = w_out[layer].astype(BF16)
        wr_bf = router_w[layer].T.astype(BF16)
        lp = diff_lambda[layer]
        sub_g = diff_subln_g[layer]

        p, ur, ui = _inproj(x, norm1_g[layer], mods, w_in_bf, wf, rope_tabs, None, True, tm)
        ctx_tabs = tuple(t[:l_ctx] for t in rope_tabs)
        pc, ucr, uci = _inproj(xc, norm1_g[layer], mods, w_in_bf, wf, ctx_tabs, ctx_row, False, l_ctx)

        ya = _fourier_latent(ur, ui)
        yb = _na_latent(p, pc, _na_bias_tables(na_rel_bias[layer], rows))
        yd = _diff_latent(p, pc, lp, sub_g, lam_init, 1024, _key_tile(s + l_ctx, 768))
        x, h2 = _merge(x, mods, None, norm1_g[layer], norm2_g[layer], ya, yb, yd, p,
                       conv_w[layer], wg_bf, wb_bf, wo_bf, tm)
        h2_all = h2.reshape(b * s, d)
        if not last:
            yac, ybc, ydc = _ctx_branches(pc, ucr, uci, lp, sub_g, lam_init)
            xc, h2c = _merge(xc, mods, ctx_row, norm1_g[layer], norm2_g[layer], yac, ybc, ydc, pc,
                             conv_w[layer], wg_bf, wb_bf, wo_bf, l_ctx)
            h2_all = jnp.concatenate([h2_all, h2c.reshape(b * l_ctx, d)], axis=0)

        y_tok, w_tok = _moe_routed(h2_all, wr_bf, router_bias[layer], layer, expert_w_gate,
                                   expert_w_up, expert_w_down)
        sg_bf = shared_w_gate[layer].astype(BF16)
        su_bf = shared_w_up[layer].astype(BF16)
        sd_bf = shared_w_down[layer].astype(BF16)
        x = _resid(x.reshape(b * s, d), h2_all, y_tok, w_tok, 0, mods, None, s, sg_bf, su_bf, sd_bf,
                   final_norm_g, last, tm).reshape(b, s, d)
        if not last:
            xc = _resid(xc.reshape(b * l_ctx, d), h2_all, y_tok, w_tok, b * s, mods, ctx_row, l_ctx,
                        sg_bf, su_bf, sd_bf, final_norm_g, False, l_ctx).reshape(b, l_ctx, d)
    return x
```

```python
import functools
import math

import numpy as np
import jax
import jax.numpy as jnp
from jax import lax
from jax.experimental import pallas as pl
from jax.experimental.pallas import tpu as pltpu

F32 = jnp.float32
BF16 = jnp.bfloat16

DEPTH = 2
GRID_W = 64
EPS = 1e-6
N_MOD = 6

FNET_GROUP_DIM = 64
NA_HEADS = 4
NA_HEAD_DIM = 64
NA_WIN_ROWS = 8
NA_WIN_COLS = 16
CONV_WIDTH = 3
DIFF_HEADS = 4
DIFF_QK_DIM = 32
DIFF_V_DIM = 64
ROPE_BASE = 10000.0
N_BRANCHES = 4
BRANCH_DIM = 256

COL_A, COL_BQ, COL_BK, COL_BV, COL_CB, COL_CC, COL_CX, COL_DQ, COL_DK, COL_DV = range(10)
N_COL_BLOCKS = 10
CB = 256
IN_DIM = N_COL_BLOCKS * CB

N_EXPERTS = 256
TOP_K = 8
N_GROUPS = 8
TOPK_GROUPS = 4
ROUTED_SCALE = 2.5
LOG2_E = 1.4426950408889634

VMEM_LIMIT_BYTES = 56 * 1024 * 1024
LANES = 128
BF16_SUBLANES = 16
FFT_N1 = 64
NA_QROWS = 8
NA_KROWS = 16
MOE_ROWS = 256
MOE_ALIGN = BF16_SUBLANES
MOE_X_SLOTS = 4
MOE_W_SLOTS = 3


def _params(*sem):
    return pltpu.CompilerParams(dimension_semantics=sem, vmem_limit_bytes=VMEM_LIMIT_BYTES)


def _dot(a, b):
    return jnp.dot(a, b, preferred_element_type=F32)


def _dot_nt(a, b):
    return lax.dot_general(a, b, (((1,), (1,)), ((), ())), preferred_element_type=F32)


def _norm_mod(xf, g, shift, scale):
    y = xf * lax.rsqrt(jnp.mean(xf * xf, axis=-1, keepdims=True) + EPS)
    return (y * g) * (1.0 + scale) + shift


def _silu(v):
    return v * jax.nn.sigmoid(v)


def _ada_kernel(c_ref, w_ref, b_ref, o_ref):
    s = _silu(c_ref[...])
    o_ref[...] = _dot(s.astype(BF16), w_ref[0].astype(BF16)) + b_ref[0]


def _ada(cvec, w, b, layer):
    rows, d = cvec.shape
    depth, _, n = w.shape
    tn = 1536
    return pl.pallas_call(
        _ada_kernel,
        grid=(n // tn,),
        in_specs=[pl.BlockSpec((rows, d), lambda j: (0, 0)),
                  pl.BlockSpec((1, d, tn), lambda j: (layer, 0, j)),
                  pl.BlockSpec((1, 1, tn), lambda j: (layer, 0, j))],
        out_specs=pl.BlockSpec((rows, tn), lambda j: (0, j)),
        out_shape=jax.ShapeDtypeStruct((rows, n), F32),
        compiler_params=_params("arbitrary"),
    )(cvec, w, b.reshape(depth, 1, n))


def _inproj_kernel(x_ref, g_ref, mod_ref, w_ref, wf_ref, cos_ref, s1_ref, s2_ref,
                   p_ref, ur_ref, ui_ref, *, rope):
    mod = mod_ref[0]
    h = _norm_mod(x_ref[0], g_ref[...], mod[0:1], mod[1:2]).astype(BF16)
    for j in range(N_COL_BLOCKS):
        pj = _dot(h, w_ref[:, j * CB:(j + 1) * CB])
        if j == COL_A:
            u = _dot(pj.astype(BF16), wf_ref[...])
            ur_ref[0] = u[:, :CB].astype(BF16)
            ui_ref[0] = u[:, CB:].astype(BF16)
        if rope and j in (COL_DQ, COL_DK):
            cos = jnp.concatenate([cos_ref[...]] * 2, axis=1)
            s1 = jnp.concatenate([s1_ref[...]] * 2, axis=1)
            s2 = jnp.concatenate([s2_ref[...]] * 2, axis=1)
            pj = pj * cos + pltpu.roll(pj, CB - 8, 1) * s1 + pltpu.roll(pj, 8, 1) * s2
            if j == COL_DQ:
                pj = pj * (DIFF_QK_DIM ** -0.5 * LOG2_E)
        p_ref[0, :, j * CB:(j + 1) * CB] = pj.astype(BF16)


def _inproj(x, g, mods, w_bf, wf, rope_tabs, mod_row, rope, tm):
    b, s, d = x.shape
    mod_idx = (lambda bi, i: (bi, 0, 0)) if mod_row is None else (lambda bi, i: (mod_row, 0, 0))
    tab_spec = pl.BlockSpec((tm, 128), lambda bi, i: (i, 0))
    seq_spec = lambda width: pl.BlockSpec((1, tm, width), lambda bi, i: (bi, i, 0))
    return pl.pallas_call(
        functools.partial(_inproj_kernel, rope=rope),
        grid=(b, s // tm),
        in_specs=[seq_spec(d),
                  pl.BlockSpec((1, d), lambda bi, i: (0, 0)),
                  pl.BlockSpec((1, N_MOD, d), mod_idx),
                  pl.BlockSpec((d, IN_DIM), lambda bi, i: (0, 0)),
                  pl.BlockSpec((CB, 2 * CB), lambda bi, i: (0, 0)),
                  tab_spec, tab_spec, tab_spec],
        out_specs=[seq_spec(IN_DIM), seq_spec(CB), seq_spec(CB)],
        out_shape=[jax.ShapeDtypeStruct((b, s, IN_DIM), BF16),
                   jax.ShapeDtypeStruct((b, s, CB), BF16),
                   jax.ShapeDtypeStruct((b, s, CB), BF16)],
        compiler_params=_params("parallel", "arbitrary"),
    )(x, g.reshape(1, d), mods, w_bf, wf, *rope_tabs)


def _channel_dft_matrix():
    c = np.arange(FNET_GROUP_DIM)
    ang = 2.0 * np.pi * ((c[:, None] * c[None, :]) % FNET_GROUP_DIM) / FNET_GROUP_DIM
    eye = np.eye(CB // FNET_GROUP_DIM)
    m = np.concatenate([np.kron(eye, np.cos(ang)), -np.kron(eye, np.sin(ang))], axis=1)
    return jnp.asarray(m, BF16)


def _rope_tables(s):
    half = DIFF_QK_DIM // 2
    t = jnp.arange(s)
    inv = 1.0 / (ROPE_BASE ** (jnp.arange(0, half, 2, dtype=F32) / half))
    ang_r = (t // GRID_W).astype(F32)[:, None] * inv
    ang_c = (t % GRID_W).astype(F32)[:, None] * inv
    zero = jnp.zeros_like(ang_r)
    cos = jnp.concatenate([jnp.cos(ang_r)] * 2 + [jnp.cos(ang_c)] * 2, axis=1)
    s1 = jnp.concatenate([-jnp.sin(ang_r), zero, -jnp.sin(ang_c), zero], axis=1)
    s2 = jnp.concatenate([zero, jnp.sin(ang_r), zero, jnp.sin(ang_c)], axis=1)
    return tuple(jnp.concatenate([a] * 4, axis=1) for a in (cos, s1, s2))


def _fft1_kernel(ur_ref, ui_ref, w_ref, ct_ref, st_ref, ar_ref, ai_ref):
    n1 = ur_ref.shape[1]
    u = jnp.concatenate([ur_ref[0], ui_ref[0]], axis=0)
    a = _dot(w_ref[...], u)
    ar, ai = a[:n1], a[n1:]
    ct, st = ct_ref[...], st_ref[...]
    ar_ref[0] = (ar * ct + ai * st).astype(BF16)
    ai_ref[0] = (ai * ct - ar * st).astype(BF16)


def _fft2_kernel(ar_ref, ai_ref, w_ref, y_ref, *, norm):
    for j in range(ar_ref.shape[1]):
        a = jnp.concatenate([ar_ref[0, j], ai_ref[0, j]], axis=0)
        y_ref[0, j] = (_dot(w_ref[...], a) * norm).astype(BF16)


def _dft_cos_sin(n):
    k = np.arange(n)
    ang = 2.0 * np.pi * ((k[:, None] * k[None, :]) % n) / n
    return np.cos(ang), np.sin(ang)


def _fourier_latent(ur, ui):
    b, s, cb = ur.shape
    n1, n2 = FFT_N1, s // FFT_N1
    c1, s1 = _dft_cos_sin(n1)
    w1 = jnp.asarray(np.block([[c1, s1], [-s1, c1]]), BF16)
    c2, s2 = _dft_cos_sin(n2)
    w2 = jnp.asarray(np.concatenate([c2, s2], axis=1), BF16)
    tw = 2.0 * np.pi * (np.arange(n1)[:, None] * np.arange(n2)[None, :]) / s
    ct = jnp.broadcast_to(jnp.asarray(np.cos(tw), F32)[:, :, None], (n1, n2, cb)).reshape(n1, n2 * cb)
    st = jnp.broadcast_to(jnp.asarray(np.sin(tw), F32)[:, :, None], (n1, n2, cb)).reshape(n1, n2 * cb)
    lanes = n2 * cb
    tn = min(lanes, 4096)
    u_spec = pl.BlockSpec((1, n1, tn), lambda j, bi: (bi, 0, j))
    t_spec = pl.BlockSpec((n1, tn), lambda j, bi: (0, j))
    ar, ai = pl.pallas_call(
        _fft1_kernel,
        grid=(lanes // tn, b),
        in_specs=[u_spec, u_spec, pl.BlockSpec((2 * n1, 2 * n1), lambda j, bi: (0, 0)), t_spec, t_spec],
        out_specs=[u_spec, u_spec],
        out_shape=[jax.ShapeDtypeStruct((b, n1, lanes), BF16)] * 2,
        compiler_params=_params("arbitrary", "arbitrary"),
    )(ur.reshape(b, n1, lanes), ui.reshape(b, n1, lanes), w1, ct, st)
    kc = 8
    a_spec = pl.BlockSpec((1, kc, n2, cb), lambda bi, j: (bi, j, 0, 0))
    y = pl.pallas_call(
        functools.partial(_fft2_kernel, norm=1.0 / math.sqrt(s * FNET_GROUP_DIM)),
        grid=(b, n1 // kc),
        in_specs=[a_spec, a_spec, pl.BlockSpec((n2, 2 * n2), lambda bi, j: (0, 0))],
        out_specs=a_spec,
        out_shape=jax.ShapeDtypeStruct((b, n1, n2, cb), BF16),
        compiler_params=_params("parallel", "arbitrary"),
    )(ar.reshape(b, n1, n2, cb), ai.reshape(b, n1, n2, cb), w2)
    return jnp.transpose(y, (0, 2, 1, 3)).reshape(b, s, cb)


def _na_kernel(q_ref, k_ref, v_ref, kc_ref, vc_ref, bias_ref, o_ref, *, rows):
    rb = pl.program_id(1)
    kb = jnp.clip(rb * NA_QROWS - NA_WIN_ROWS // 2, 0, rows - NA_KROWS)
    nk = NA_KROWS * GRID_W
    tok0 = pl.multiple_of(kb * GRID_W, 256)
    scale = NA_HEAD_DIM ** -0.5
    outs = []
    for h in range(NA_HEADS):
        sl = slice(h * NA_HEAD_DIM, (h + 1) * NA_HEAD_DIM)
        q = q_ref[0, :, sl]
        s = _dot_nt(q, k_ref[0, pl.ds(tok0, nk), sl]) * scale + bias_ref[0, h]
        sc = _dot_nt(q, kc_ref[0, :, sl]) * scale
        m = jnp.maximum(jnp.max(s, axis=-1, keepdims=True), jnp.max(sc, axis=-1, keepdims=True))
        e = jnp.exp(s - m)
        ec = jnp.exp(sc - m)
        l = jnp.sum(e, axis=-1, keepdims=True) + jnp.sum(ec, axis=-1, keepdims=True)
        o = _dot(e.astype(BF16), v_ref[0, pl.ds(tok0, nk), sl]) + _dot(ec.astype(BF16), vc_ref[0, :, sl])
        outs.append(o / l)
    o_ref[0] = jnp.concatenate(outs, axis=1).astype(BF16)


def _na_bias_tables(rel_bias, rows):
    tabs = []
    cq = np.arange(GRID_W)
    col_lo = np.clip(cq - NA_WIN_COLS // 2, 0, GRID_W - NA_WIN_COLS)
    col_ok = (cq[None, :] >= col_lo[:, None]) & (cq[None, :] < col_lo[:, None] + NA_WIN_COLS)
    dc_idx = np.clip(cq[None, :] - cq[:, None] + NA_WIN_COLS - 1, 0, 2 * NA_WIN_COLS - 2)
    for r0 in (0, NA_QROWS, rows - NA_QROWS):
        kb = int(np.clip(r0 - NA_WIN_ROWS // 2, 0, rows - NA_KROWS))
        r = r0 + np.arange(NA_QROWS)
        rk = kb + np.arange(NA_KROWS)
        start = np.clip(r - NA_WIN_ROWS // 2, 0, rows - NA_WIN_ROWS)
        row_ok = (rk[None, :] >= start[:, None]) & (rk[None, :] < start[:, None] + NA_WIN_ROWS)
        dr_idx = np.clip(rk[None, :] - r[:, None] + NA_WIN_ROWS - 1, 0, 2 * NA_WIN_ROWS - 2)
        ok = row_ok[:, None, :, None] & col_ok[None, :, None, :]
        oh_r = jnp.asarray(dr_idx[:, :, None] == np.arange(2 * NA_WIN_ROWS - 1), F32)
        oh_c = jnp.asarray(dc_idx[:, :, None] == np.arange(2 * NA_WIN_COLS - 1), F32)
        by_row = jnp.einsum('qka,hab->hqkb', oh_r, rel_bias.astype(F32), precision=lax.Precision.HIGHEST)
        vals = jnp.einsum('hqkb,cdb->hqckd', by_row, oh_c, precision=lax.Precision.HIGHEST)
        tab = jnp.where(ok[None], vals, -jnp.inf)
        tabs.append(tab.reshape(NA_HEADS, NA_QROWS * GRID_W, NA_KROWS * GRID_W))
    return jnp.stack(tabs)


def _na_latent(p, pc, bias_tabs):
    b, s, _ = p.shape
    l = pc.shape[1]
    rows = s // GRID_W
    nrb = rows // NA_QROWS
    tq = NA_QROWS * GRID_W
    nk = NA_KROWS * GRID_W
    return pl.pallas_call(
        functools.partial(_na_kernel, rows=rows),
        grid=(b, nrb),
        in_specs=[pl.BlockSpec((1, tq, CB), lambda bi, i: (bi, i, COL_BQ)),
                  pl.BlockSpec((1, s, CB), lambda bi, i: (bi, 0, COL_BK)),
                  pl.BlockSpec((1, s, CB), lambda bi, i: (bi, 0, COL_BV)),
                  pl.BlockSpec((1, l, CB), lambda bi, i: (bi, 0, COL_BK)),
                  pl.BlockSpec((1, l, CB), lambda bi, i: (bi, 0, COL_BV)),
                  pl.BlockSpec((1, NA_HEADS, tq, nk),
                               lambda bi, i: (jnp.minimum(i, 1) + (i == nrb - 1).astype(jnp.int32), 0, 0, 0))],
        out_specs=pl.BlockSpec((1, tq, CB), lambda bi, i: (bi, i, 0)),
        out_shape=jax.ShapeDtypeStruct((b, s, CB), BF16),
        compiler_params=_params("parallel", "arbitrary"),
    )(p, p, p, pc, pc, bias_tabs)


def _diff_lambda(lp, lam_init):
    return (jnp.exp(jnp.sum(lp[0:1] * lp[1:2], axis=-1, keepdims=True))
            - jnp.exp(jnp.sum(lp[2:3] * lp[3:4], axis=-1, keepdims=True)) + lam_init)


def _diff_finish(o, g, lam_init):
    y = o * lax.rsqrt(jnp.mean(o * o, axis=-1, keepdims=True) + EPS)
    return y * g * (1.0 - lam_init)


def _diff_kernel(q_ref, kt_ref, v_ref, lp_ref, g_ref, o_ref, m_sc, acc_sc, e_sc, *, tk, lam_init):
    tq = q_ref.shape[1]
    nk = kt_ref.shape[2] // tk
    m_sc[...] = jnp.full(m_sc.shape, -jnp.inf, F32)
    acc_sc[...] = jnp.zeros(acc_sc.shape, F32)

    def body(c, carry):
        k0 = pl.multiple_of(c * tk, LANES)
        for h in range(DIFF_HEADS):
            for m in range(2):
                rows = slice(m * tq, (m + 1) * tq)
                dims = slice((2 * h + m) * DIFF_QK_DIM, (2 * h + m + 1) * DIFF_QK_DIM)
                s = _dot(q_ref[0, :, dims], kt_ref[0, dims, pl.ds(k0, tk)])
                m_old = m_sc[h, rows]
                m_new = jnp.maximum(m_old, jnp.max(s, axis=-1, keepdims=True))
                e_sc[h, rows] = jnp.exp2(s - m_new[:, :1]).astype(BF16)
                acc_sc[h, rows] = jnp.exp2(m_old - m_new) * acc_sc[h, rows]
                m_sc[h, rows] = m_new
            acc_sc[h] += _dot(e_sc[h], v_ref[0, h, pl.ds(k0, tk), :])
        return carry

    lax.fori_loop(0, nk, body, 0)
    lam = _diff_lambda(lp_ref[...], lam_init)
    outs = []
    for h in range(DIFF_HEADS):
        acc = acc_sc[h]
        o = acc[:, :DIFF_V_DIM] / acc[:, DIFF_V_DIM:DIFF_V_DIM + 1]
        outs.append(_diff_finish(o[:tq] - lam * o[tq:], g_ref[...], lam_init))
    o_ref[0] = jnp.concatenate(outs, axis=1).astype(BF16)


def _key_tile(nkeys, cap):
    return max(t for t in range(128, cap + 1, 128) if nkeys % t == 0)


def _diff_latent(p, pc, lp, sub_g, lam_init, tq, tk):
    b, s, _ = p.shape
    l = pc.shape[1]
    nkeys = s + l
    k_all = jnp.concatenate([p[:, :, COL_DK * CB:(COL_DK + 1) * CB], pc[:, :, COL_DK * CB:(COL_DK + 1) * CB]], axis=1)
    v_all = jnp.concatenate([p[:, :, COL_DV * CB:], pc[:, :, COL_DV * CB:]], axis=1)
    kt = jnp.transpose(k_all, (0, 2, 1))
    vh = jnp.transpose(v_all.reshape(b, nkeys, DIFF_HEADS, DIFF_V_DIM), (0, 2, 1, 3))
    vh = jnp.concatenate([vh, jnp.ones((b, DIFF_HEADS, nkeys, LANES - DIFF_V_DIM), BF16)], axis=-1)
    return pl.pallas_call(
        functools.partial(_diff_kernel, tk=tk, lam_init=lam_init),
        grid=(b, s // tq),
        in_specs=[pl.BlockSpec((1, tq, CB), lambda bi, i: (bi, i, COL_DQ)),
                  pl.BlockSpec((1, CB, nkeys), lambda bi, i: (bi, 0, 0), pipeline_mode=pl.Buffered(1)),
                  pl.BlockSpec((1, DIFF_HEADS, nkeys, LANES), lambda bi, i: (bi, 0, 0, 0),
                               pipeline_mode=pl.Buffered(1)),
                  pl.BlockSpec((4, DIFF_QK_DIM), lambda bi, i: (0, 0)),
                  pl.BlockSpec((1, DIFF_V_DIM), lambda bi, i: (0, 0))],
        out_specs=pl.BlockSpec((1, tq, CB), lambda bi, i: (bi, i, 0)),
        out_shape=jax.ShapeDtypeStruct((b, s, CB), BF16),
        scratch_shapes=[pltpu.VMEM((DIFF_HEADS, 2 * tq, LANES), F32),
                        pltpu.VMEM((DIFF_HEADS, 2 * tq, LANES), F32),
                        pltpu.VMEM((DIFF_HEADS, 2 * tq, tk), BF16)],
        compiler_params=_params("parallel", "arbitrary"),
    )(p, kt, vh, lp, sub_g.reshape(1, DIFF_V_DIM))


def _softmax_rows(s):
    e = jnp.exp(s - jnp.max(s, axis=-1, keepdims=True))
    return e / jnp.sum(e, axis=-1, keepdims=True)


def _ctx_kernel(pc_ref, ur_ref, ui_ref, wf_ref, lp_ref, g_ref, ya_ref, yb_ref, yd_ref, *, lam_init):
    l = pc_ref.shape[1]
    col = lambda j, lo, hi: pc_ref[0, :, j * CB + lo:j * CB + hi]
    u = jnp.concatenate([ur_ref[0], ui_ref[0]], axis=0)
    ya_ref[0] = (_dot(wf_ref[...], u) * (1.0 / math.sqrt(l * FNET_GROUP_DIM))).astype(BF16)
    outs = []
    for h in range(NA_HEADS):
        lo, hi = h * NA_HEAD_DIM, (h + 1) * NA_HEAD_DIM
        pr = _softmax_rows(_dot_nt(col(COL_BQ, lo, hi), col(COL_BK, lo, hi)) * NA_HEAD_DIM ** -0.5)
        outs.append(_dot(pr.astype(BF16), col(COL_BV, lo, hi)))
    yb_ref[0] = jnp.concatenate(outs, axis=1).astype(BF16)
    lam = _diff_lambda(lp_ref[...], lam_init)
    outs = []
    for h in range(DIFF_HEADS):
        pm = []
        for m in range(2):
            lo = (2 * h + m) * DIFF_QK_DIM
            pm.append(_softmax_rows(_dot_nt(col(COL_DQ, lo, lo + DIFF_QK_DIM), col(COL_DK, lo, lo + DIFF_QK_DIM))
                                    * DIFF_QK_DIM ** -0.5))
        a = (pm[0] - lam * pm[1]).astype(BF16)
        o = _dot(a, col(COL_DV, h * DIFF_V_DIM, (h + 1) * DIFF_V_DIM))
        outs.append(_diff_finish(o, g_ref[...], lam_init))
    yd_ref[0] = jnp.concatenate(outs, axis=1).astype(BF16)


def _ctx_branches(pc, ucr, uci, lp, sub_g, lam_init):
    b, l, _ = pc.shape
    c, s = _dft_cos_sin(l)
    wf = jnp.asarray(np.concatenate([c, s], axis=1), BF16)
    y_spec = pl.BlockSpec((1, l, CB), lambda bi: (bi, 0, 0))
    return pl.pallas_call(
        functools.partial(_ctx_kernel, lam_init=lam_init),
        grid=(b,),
        in_specs=[pl.BlockSpec((1, l, IN_DIM), lambda bi: (bi, 0, 0)), y_spec, y_spec,
                  pl.BlockSpec((l, 2 * l), lambda bi: (0, 0)),
                  pl.BlockSpec((4, DIFF_QK_DIM), lambda bi: (0, 0)),
                  pl.BlockSpec((1, DIFF_V_DIM), lambda bi: (0, 0))],
        out_specs=[y_spec] * 3,
        out_shape=[jax.ShapeDtypeStruct((b, l, CB), BF16)] * 3,
        compiler_params=_params("parallel"),
    )(pc, ucr, uci, wf, lp, sub_g.reshape(1, DIFF_V_DIM))


def _merge_kernel(x_ref, mod_ref, g1_ref, g2_ref, ya_ref, yb_ref, yd_ref, pb_ref, pc_ref, px_ref,
                  cp_ref, xp_ref, cn_ref, xn_ref, cw_ref, wg_ref, wb_ref, wo_ref,
                  xo_ref, h2_ref):
    i = pl.program_id(1)
    last = pl.num_programs(1) - 1
    tm = x_ref.shape[1]
    mod = mod_ref[0]
    x = x_ref[0]
    h = _norm_mod(x, g1_ref[...], mod[0:1], mod[1:2]).astype(BF16)

    u = pc_ref[0].astype(F32) * px_ref[0].astype(F32)
    up = cp_ref[0, BF16_SUBLANES - 1:, :].astype(F32) * xp_ref[0, BF16_SUBLANES - 1:, :].astype(F32)
    un = cn_ref[0, :1, :].astype(F32) * xn_ref[0, :1, :].astype(F32)
    up = jnp.where(i == 0, 0.0, up)
    un = jnp.where(i == last, 0.0, un)
    rid = lax.broadcasted_iota(jnp.int32, u.shape, 0)
    u_prev = jnp.where(rid == 0, up, pltpu.roll(u, 1, 0))
    u_next = jnp.where(rid == tm - 1, un, pltpu.roll(u, tm - 1, 0))
    cw = cw_ref[...]
    yc = pb_ref[0].astype(F32) * (cw[0:1] * u_prev + cw[1:2] * u + cw[2:3] * u_next)

    branches = (ya_ref[0], yb_ref[0], yc.astype(BF16), yd_ref[0])
    d = x.shape[1]
    out = None
    for n in range(d // CB):
        cols = slice(n * CB, (n + 1) * CB)
        merged = None
        for j in range(N_BRANCHES):
            t = jax.nn.sigmoid(_dot(h, wg_ref[j, :, cols])) * _dot(branches[j], wb_ref[j, :, cols])
            merged = t if merged is None else merged + t
        t = _dot(merged.astype(BF16), wo_ref[cols, :])
        out = t if out is None else out + t
    xn = x + mod[2:3] * out
    xo_ref[0] = xn
    h2_ref[0] = _norm_mod(xn, g2_ref[...], mod[3:4], mod[4:5]).astype(BF16)


def _merge(x, mods, mod_row, g1, g2, ya, yb, yd, p, conv_w, wg, wb, wo, tm):
    b, s, d = x.shape
    hb = tm // BF16_SUBLANES
    n_halo = s // BF16_SUBLANES
    mod_idx = (lambda bi, i: (bi, 0, 0)) if mod_row is None else (lambda bi, i: (mod_row, 0, 0))
    seq = lambda width, col=0: pl.BlockSpec((1, tm, width), lambda bi, i: (bi, i, col))
    prev = lambda col: pl.BlockSpec((1, BF16_SUBLANES, CB), lambda bi, i: (bi, jnp.maximum(i * hb - 1, 0), col))
    nxt = lambda col: pl.BlockSpec((1, BF16_SUBLANES, CB),
                                   lambda bi, i: (bi, jnp.minimum((i + 1) * hb, n_halo - 1), col))
    const = lambda shape: pl.BlockSpec(shape, lambda bi, i: (0,) * len(shape))
    return pl.pallas_call(
        _merge_kernel,
        grid=(b, s // tm),
        in_specs=[seq(d), pl.BlockSpec((1, N_MOD, d), mod_idx), const((1, d)), const((1, d)),
                  seq(CB), seq(CB), seq(CB),
                  seq(CB, COL_CB), seq(CB, COL_CC), seq(CB, COL_CX),
                  prev(COL_CC), prev(COL_CX), nxt(COL_CC), nxt(COL_CX),
                  const((CONV_WIDTH, CB)),
                  const((N_BRANCHES, d, d)), const((N_BRANCHES, BRANCH_DIM, d)), const((d, d))],
        out_specs=[seq(d), seq(d)],
        out_shape=[jax.ShapeDtypeStruct((b, s, d), F32),
                   jax.ShapeDtypeStruct((b, s, d), BF16)],
        compiler_params=_params("parallel", "arbitrary"),
    )(x, mods, g1.reshape(1, d), g2.reshape(1, d), ya, yb, yd, p, p, p, p, p, p, p,
      conv_w, wg, wb, wo)


def _route_kernel(h2_ref, wr_ref, bias_ref, tri_ref, ones_ref, idx_ref, w_ref, rank_ref, cnt_ref,
                  score_sc, sel_sc, carry_sc):
    i = pl.program_id(0)
    tm = h2_ref.shape[0]
    ne = wr_ref.shape[0]
    gsz = ne // N_GROUPS
    n_chunks = tm // LANES

    @pl.when(i == 0)
    def _():
        carry_sc[...] = jnp.zeros(carry_sc.shape, F32)

    score_sc[...] = jax.nn.sigmoid(_dot_nt(wr_ref[...], h2_ref[...]))

    def select(cidx, carry):
        c0 = pl.multiple_of(cidx * LANES, LANES)
        scores = score_sc[:, pl.ds(c0, LANES)]
        biased = scores + bias_ref[...]
        liota = lax.broadcasted_iota(jnp.int32, (gsz, LANES), 0)
        gs = []
        for g in range(N_GROUPS):
            v = biased[g * gsz:(g + 1) * gsz]
            m1 = jnp.max(v, axis=0, keepdims=True)
            i1 = jnp.min(jnp.where(v == m1, liota, gsz), axis=0, keepdims=True)
            m2 = jnp.max(jnp.where(liota == i1, -jnp.inf, v), axis=0, keepdims=True)
            gs.append(m1 + m2)
        gsm = jnp.concatenate(gs, axis=0)
        giota = lax.broadcasted_iota(jnp.int32, gsm.shape, 0)
        keep = jnp.zeros(gsm.shape, F32)
        for _ in range(TOPK_GROUPS):
            m = jnp.max(gsm, axis=0, keepdims=True)
            gi = jnp.min(jnp.where(gsm == m, giota, N_GROUPS), axis=0, keepdims=True)
            hit = giota == gi
            keep = jnp.where(hit, 1.0, keep)
            gsm = jnp.where(hit, -jnp.inf, gsm)
        cur = jnp.concatenate(
            [jnp.where(jnp.broadcast_to(keep[g:g + 1], (gsz, LANES)) > 0.0, biased[g * gsz:(g + 1) * gsz], -jnp.inf)
             for g in range(N_GROUPS)], axis=0)
        eiota = lax.broadcasted_iota(jnp.int32, (ne, LANES), 0)
        sel = jnp.zeros((ne, LANES), F32)
        idxs, ws = [], []
        for _ in range(TOP_K):
            m = jnp.max(cur, axis=0, keepdims=True)
            ik = jnp.min(jnp.where(cur == m, eiota, ne), axis=0, keepdims=True)
            hit = eiota == ik
            cur = jnp.where(hit, -jnp.inf, cur)
            ws.append(jnp.sum(jnp.where(hit, scores, 0.0), axis=0, keepdims=True))
            idxs.append(ik)
            sel = jnp.where(hit, 1.0, sel)
        w = jnp.concatenate(ws, axis=0)
        w_ref[:, pl.ds(c0, LANES)] = w / jnp.sum(w, axis=0, keepdims=True) * ROUTED_SCALE
        idx_ref[:, pl.ds(c0, LANES)] = jnp.concatenate(idxs, axis=0)
        sel_sc[:, pl.ds(c0, LANES)] = sel.astype(BF16)
        return carry

    lax.fori_loop(0, n_chunks, select, 0)

    sel_all = sel_sc[...]
    score_sc[...] = _dot(sel_all, tri_ref[...]) + jnp.concatenate([carry_sc[...]] * n_chunks, axis=1)

    def ranks(cidx, carry):
        c0 = pl.multiple_of(cidx * LANES, LANES)
        before = score_sc[:, pl.ds(c0, LANES)]
        idx = idx_ref[:, pl.ds(c0, LANES)]
        eiota = lax.broadcasted_iota(jnp.int32, (ne, LANES), 0)
        rows = [jnp.sum(jnp.where(eiota == idx[k:k + 1], before, 0.0), axis=0, keepdims=True)
                for k in range(TOP_K)]
        rank_ref[:, pl.ds(c0, LANES)] = jnp.concatenate(rows, axis=0).astype(jnp.int32)
        return carry

    lax.fori_loop(0, n_chunks, ranks, 0)
    carry_sc[...] += _dot(sel_all, ones_ref[...])
    cnt_ref[...] = carry_sc[...]


def _route(h2_all, wr_t, bias, tm):
    n, d = h2_all.shape
    ne = wr_t.shape[0]
    tri = jnp.asarray(np.triu(np.ones((tm, tm), np.float32), 1), BF16)
    ones = jnp.ones((tm, LANES), BF16)
    bias_b = jnp.broadcast_to(bias.astype(F32)[:, None], (ne, LANES))
    const = lambda shape: pl.BlockSpec(shape, lambda i: (0,) * len(shape))
    tok = pl.BlockSpec((TOP_K, tm), lambda i: (0, i))
    return pl.pallas_call(
        _route_kernel,
        grid=(n // tm,),
        in_specs=[pl.BlockSpec((tm, d), lambda i: (i, 0)), const((ne, d)), const((ne, LANES)),
                  const((tm, tm)), const((tm, LANES))],
        out_specs=[tok, tok, tok, const((ne, LANES))],
        out_shape=[jax.ShapeDtypeStruct((TOP_K, n), jnp.int32),
                   jax.ShapeDtypeStruct((TOP_K, n), F32),
                   jax.ShapeDtypeStruct((TOP_K, n), jnp.int32),
                   jax.ShapeDtypeStruct((ne, LANES), F32)],
        scratch_shapes=[pltpu.VMEM((ne, tm), F32), pltpu.VMEM((ne, tm), BF16), pltpu.VMEM((ne, LANES), F32)],
        compiler_params=_params("arbitrary"),
    )(h2_all, wr_t, bias_b, tri, ones)


def _pos_kernel(idx_ref, rank_ref, start_ref, pos_ref):
    ne = start_ref.shape[0]
    start = start_ref[...]

    def body(cidx, carry):
        c0 = pl.multiple_of(cidx * LANES, LANES)
        idx = idx_ref[:, pl.ds(c0, LANES)]
        eiota = lax.broadcasted_iota(jnp.int32, (ne, LANES), 0)
        rows = [jnp.sum(jnp.where(eiota == idx[k:k + 1], start, 0.0), axis=0, keepdims=True)
                for k in range(TOP_K)]
        pos_ref[:, pl.ds(c0, LANES)] = jnp.concatenate(rows, axis=0).astype(jnp.int32) + rank_ref[:, pl.ds(c0, LANES)]
        return carry

    lax.fori_loop(0, idx_ref.shape[1] // LANES, body, 0)


def _positions(idx, rank, start_rows, tm):
    k, n = idx.shape
    ne = start_rows.shape[0]
    start_b = jnp.broadcast_to(start_rows.astype(F32)[:, None], (ne, LANES))
    tok = pl.BlockSpec((k, tm), lambda i: (0, i))
    return pl.pallas_call(
        _pos_kernel,
        grid=(n // tm,),
        in_specs=[tok, tok, pl.BlockSpec((ne, LANES), lambda i: (0, 0))],
        out_specs=tok,
        out_shape=jax.ShapeDtypeStruct((k, n), jnp.int32),
        compiler_params=_params("parallel"),
    )(idx, rank, start_b)


def _rowtok_kernel(start_ref, cnt_ref, pos_ref, out_ref, *, n_tokens):
    i = pl.program_id(0)
    tn = pos_ref.shape[1]
    n_rows = out_ref.shape[0]
    ne = start_ref.shape[0]

    @pl.when(i == 0)
    def _():
        def gaps(e, carry):
            lo = start_ref[e] + cnt_ref[e]
            hi = jnp.where(e + 1 < ne, start_ref[jnp.minimum(e + 1, ne - 1)], n_rows)

            def one(r, c):
                out_ref[r] = lax.rem(r, n_tokens)
                return c

            return lax.fori_loop(lo, hi, one, carry)

        lax.fori_loop(0, ne, gaps, 0)

    def body(t, carry):
        for k in range(TOP_K):
            out_ref[pos_ref[k, t]] = i * tn + t
        return carry

    lax.fori_loop(0, tn, body, 0)


def _row_tokens(pos, seg_start, counts, n_rows):
    k, n = pos.shape
    tn = max(t for t in range(LANES, 2048 + 1, LANES) if n % t == 0)
    grid_spec = pltpu.PrefetchScalarGridSpec(
        num_scalar_prefetch=2,
        grid=(n // tn,),
        in_specs=[pl.BlockSpec((k, tn), lambda i, st, ct: (0, i), memory_space=pltpu.SMEM)],
        out_specs=pl.BlockSpec((n_rows,), lambda i, st, ct: (0,), memory_space=pltpu.SMEM))
    return pl.pallas_call(
        functools.partial(_rowtok_kernel, n_tokens=n),
        grid_spec=grid_spec,
        out_shape=jax.ShapeDtypeStruct((n_rows,), jnp.int32),
        compiler_params=_params("arbitrary"),
    )(seg_start, counts, pos)


def _expert_kernel(crow_ref, crun_ref, rexp_ref, meta_ref, x_hbm, wg_hbm, wu_hbm, wd_hbm, y_hbm,
                   wg_buf, wu_buf, wd_buf, wg_bf, wu_bf, wd_bf, x_buf, y_buf, zero_buf,
                   w_sem, x_sem, y_sem, z_sem, *, layer):
    n_chunks, n_runs, tail_start = meta_ref[0], meta_ref[1], meta_ref[2]
    n_rows = x_hbm.shape[0]

    def w_copies(run):
        expert, s = rexp_ref[run], run % MOE_W_SLOTS
        return (pltpu.make_async_copy(wg_hbm.at[layer, expert], wg_buf.at[s], w_sem.at[s, 0]),
                pltpu.make_async_copy(wu_hbm.at[layer, expert], wu_buf.at[s], w_sem.at[s, 1]),
                pltpu.make_async_copy(wd_hbm.at[layer, expert], wd_buf.at[s], w_sem.at[s, 2]))

    def x_copy(c):
        rows = pl.ds(pl.multiple_of(crow_ref[c], MOE_ALIGN), MOE_ROWS)
        return pltpu.make_async_copy(x_hbm.at[rows], x_buf.at[c % MOE_X_SLOTS], x_sem.at[c % MOE_X_SLOTS])

    def y_copy(c):
        rows = pl.ds(pl.multiple_of(crow_ref[c], MOE_ALIGN), MOE_ROWS)
        return pltpu.make_async_copy(y_buf.at[c % 2], y_hbm.at[rows], y_sem.at[c % 2])

    def zero_copy(row0):
        return pltpu.make_async_copy(zero_buf, y_hbm.at[pl.ds(row0, MOE_ALIGN)], z_sem.at[0])

    for r in range(MOE_W_SLOTS - 1):
        @pl.when(r < n_runs)
        def _(r=r):
            for cp in w_copies(r):
                cp.start()

    for c in range(MOE_X_SLOTS - 1):
        @pl.when(c < n_chunks)
        def _(c=c):
            x_copy(c).start()

    def chunk(c, carry):
        run = crun_ref[c]
        first = jnp.logical_or(c == 0, crun_ref[jnp.maximum(c - 1, 0)] != run)

        @pl.when(c + MOE_X_SLOTS - 1 < n_chunks)
        def _():
            x_copy(c + MOE_X_SLOTS - 1).start()

        @pl.when(first)
        def _():
            @pl.when(run + MOE_W_SLOTS - 1 < n_runs)
            def _():
                for cp in w_copies(run + MOE_W_SLOTS - 1):
                    cp.start(priority=1)

            for cp in w_copies(run):
                cp.wait()
            s = run % MOE_W_SLOTS
            wg_bf[...] = wg_buf[s].astype(BF16)
            wu_bf[...] = wu_buf[s].astype(BF16)
            wd_bf[...] = wd_buf[s].astype(BF16)

        x_copy(c).wait()
        x = x_buf[c % MOE_X_SLOTS]
        a = (_silu(_dot(x, wg_bf[...])) * _dot(x, wu_bf[...])).astype(BF16)
        y = _dot(a, wd_bf[...]).astype(BF16)

        @pl.when(c > 0)
        def _():
            y_copy(c - 1).wait()

        y_buf[c % 2] = y
        y_copy(c).start()
        return carry

    lax.fori_loop(0, n_chunks, chunk, 0)

    @pl.when(n_chunks > 0)
    def _():
        y_copy(n_chunks - 1).wait()

    zero_buf[...] = jnp.zeros(zero_buf.shape, zero_buf.dtype)
    n_tail = (n_rows - tail_start) // MOE_ALIGN

    def fill(t, carry):
        zero_copy(pl.multiple_of(tail_start + t * MOE_ALIGN, MOE_ALIGN)).start()
        return carry

    def drain(t, carry):
        zero_copy(0).wait()
        return carry

    lax.fori_loop(0, n_tail, fill, 0)
    lax.fori_loop(0, n_tail, drain, 0)


def _experts(x_sorted, seg_start, counts, layer, w_g, w_u, w_d):
    n_rows, d = x_sorted.shape
    ne, f = w_g.shape[1], w_g.shape[3]
    nch = (counts + MOE_ROWS - 1) // MOE_ROWS
    c_end = jnp.cumsum(nch)
    max_chunks = (n_rows - MOE_ROWS) // MOE_ROWS + ne
    g = jnp.arange(max_chunks, dtype=jnp.int32)
    c_exp = jnp.minimum(jnp.sum((c_end[None, :] <= g[:, None]).astype(jnp.int32), axis=1), ne - 1)
    c_row = jnp.where(g < c_end[-1], seg_start[c_exp] + (g - (c_end - nch)[c_exp]) * MOE_ROWS, 0)
    has_rows = (nch > 0).astype(jnp.int32)
    run_end = jnp.cumsum(has_rows)
    c_run = (run_end - 1)[c_exp]
    r = jnp.arange(ne, dtype=jnp.int32)
    r_exp = jnp.minimum(jnp.sum((run_end[None, :] <= r[:, None]).astype(jnp.int32), axis=1), ne - 1)
    tail_start = jnp.max(jnp.where(nch > 0, seg_start + nch * MOE_ROWS, 0))
    meta = jnp.stack([c_end[-1], run_end[-1], tail_start]).astype(jnp.int32)
    any_spec = pl.BlockSpec(memory_space=pl.ANY)
    grid_spec = pltpu.PrefetchScalarGridSpec(
        num_scalar_prefetch=4,
        grid=(1,),
        in_specs=[any_spec] * 4,
        out_specs=any_spec,
        scratch_shapes=[pltpu.VMEM((MOE_W_SLOTS, d, f), F32), pltpu.VMEM((MOE_W_SLOTS, d, f), F32),
                        pltpu.VMEM((MOE_W_SLOTS, f, d), F32),
                        pltpu.VMEM((d, f), BF16), pltpu.VMEM((d, f), BF16), pltpu.VMEM((f, d), BF16),
                        pltpu.VMEM((MOE_X_SLOTS, MOE_ROWS, d), BF16), pltpu.VMEM((2, MOE_ROWS, d), BF16),
                        pltpu.VMEM((MOE_ALIGN, d), BF16),
                        pltpu.SemaphoreType.DMA((MOE_W_SLOTS, 3)), pltpu.SemaphoreType.DMA((MOE_X_SLOTS,)),
                        pltpu.SemaphoreType.DMA((2,)), pltpu.SemaphoreType.DMA((1,))])
    return pl.pallas_call(
        functools.partial(_expert_kernel, layer=layer),
        grid_spec=grid_spec,
        out_shape=jax.ShapeDtypeStruct((n_rows, d), BF16),
        compiler_params=_params("arbitrary"),
    )(c_row.astype(jnp.int32), c_run.astype(jnp.int32), r_exp, meta, x_sorted, w_g, w_u, w_d)


def _resid_kernel(x_ref, h2_ref, y_ref, w_ref, mod_ref, sg_ref, su_ref, sd_ref, gf_ref, o_ref, *, final):
    h2 = h2_ref[...]
    a = (_silu(_dot(h2, sg_ref[...])) * _dot(h2, su_ref[...])).astype(BF16)
    y = _dot(a, sd_ref[...])
    w = w_ref[...]
    for k in range(TOP_K):
        y = y + w[:, k:k + 1] * y_ref[k].astype(F32)
    xo = x_ref[...] + mod_ref[0][5:6] * y
    if final:
        xo = xo * lax.rsqrt(jnp.mean(xo * xo, axis=-1, keepdims=True) + EPS) * gf_ref[...]
    o_ref[...] = xo


def _resid(x_flat, h2_all, y_tok, w_tok, row_off, mods, mod_row, rows_per_mod, sg, su, sd, gf, final, tm):
    n, d = x_flat.shape
    f = sg.shape[1]
    off = row_off // tm
    per = rows_per_mod // tm
    mod_idx = (lambda i: (i // per, 0, 0)) if mod_row is None else (lambda i: (mod_row, 0, 0))
    const = lambda shape: pl.BlockSpec(shape, lambda i: (0,) * len(shape))
    return pl.pallas_call(
        functools.partial(_resid_kernel, final=final),
        grid=(n // tm,),
        in_specs=[pl.BlockSpec((tm, d), lambda i: (i, 0)),
                  pl.BlockSpec((tm, d), lambda i: (i + off, 0)),
                  pl.BlockSpec((TOP_K, tm, d), lambda i: (0, i + off, 0)),
                  pl.BlockSpec((tm, TOP_K), lambda i: (i + off, 0)),
                  pl.BlockSpec((1, N_MOD, d), mod_idx),
                  const((d, f)), const((d, f)), const((f, d)), const((1, d))],
        out_specs=pl.BlockSpec((tm, d), lambda i: (i, 0)),
        out_shape=jax.ShapeDtypeStruct((n, d), F32),
        compiler_params=_params("parallel"),
    )(x_flat, h2_all, y_tok, w_tok, mods, sg, su, sd, gf.reshape(1, d))


def _moe_routed(h2_all, wr_t, bias, layer, w_g, w_u, w_d):
    n, d = h2_all.shape
    idx, wts, rank, cnt = _route(h2_all, wr_t, bias, 512)
    counts = cnt[:, 0].astype(jnp.int32)
    padded = (counts + MOE_ALIGN - 1) // MOE_ALIGN * MOE_ALIGN
    seg_start = jnp.cumsum(padded) - padded
    n_rows = -(-(n * TOP_K + N_EXPERTS * (MOE_ALIGN - 1)) // MOE_ROWS) * MOE_ROWS + MOE_ROWS
    pos2 = _positions(idx, rank, seg_start, 512)
    pos = pos2.reshape(TOP_K * n)
    row_tok = _row_tokens(pos2, seg_start, counts, n_rows)
    x_sorted = jnp.concatenate([h2_all, h2_all], axis=0).at[row_tok].get(mode='promise_in_bounds')
    y_sorted = _experts(x_sorted, seg_start, counts, layer, w_g, w_u, w_d)
    return y_sorted.at[pos].get(mode='promise_in_bounds').reshape(TOP_K, n, d), wts.T


def kernel(x, c, ctx, c_ctx, ada_w, ada_b, norm1_g, w_in, conv_w, na_rel_bias, diff_lambda,
           diff_subln_g, w_branch_gate, w_branch, w_out, norm2_g, router_w, router_bias,
           expert_w_gate, expert_w_up, expert_w_down, shared_w_gate, shared_w_up, shared_w_down,
           final_norm_g):
    b, s, d = x.shape
    l_ctx = ctx.shape[1]
    rows = s // GRID_W
    ctx_row = b
    cvec = jnp.zeros((8, d), F32).at[:b].set(c).at[ctx_row].set(c_ctx)
    rope_tabs = _rope_tables(s)
    wf = _channel_dft_matrix()
    tm = 512
    xc = ctx
    for layer in range(DEPTH):
        last = layer == DEPTH - 1
        lam_init = 0.8 - 0.6 * math.exp(-0.3 * layer)
        mods = _ada(cvec, ada_w, ada_b, layer).reshape(8, N_MOD, d)
        w_in_bf = w_in[layer].astype(BF16)
        wg_bf = w_branch_gate[layer].astype(BF16)
        wb_bf = w_branch[layer].astype(BF16)
        wo_bf = w_out[layer].astype(BF16)
        wr_bf = router_w[layer].T.astype(BF16)
        lp = diff_lambda[layer]
        sub_g = diff_subln_g[layer]

        p, ur, ui = _inproj(x, norm1_g[layer], mods, w_in_bf, wf, rope_tabs, None, True, tm)
        ctx_tabs = tuple(t[:l_ctx] for t in rope_tabs)
        pc, ucr, uci = _inproj(xc, norm1_g[layer], mods, w_in_bf, wf, ctx_tabs, ctx_row, False, l_ctx)

        ya = _fourier_latent(ur, ui)
        yb = _na_latent(p, pc, _na_bias_tables(na_rel_bias[layer], rows))
        yd = _diff_latent(p, pc, lp, sub_g, lam_init, 1024, _key_tile(s + l_ctx, 768))
        x, h2 = _merge(x, mods, None, norm1_g[layer], norm2_g[layer], ya, yb, yd, p,
                       conv_w[layer], wg_bf, wb_bf, wo_bf, tm)
        h2_all = h2.reshape(b * s, d)
        if not last:
            yac, ybc, ydc = _ctx_branches(pc, ucr, uci, lp, sub_g, lam_init)
            xc, h2c = _merge(xc, mods, ctx_row, norm1_g[layer], norm2_g[layer], yac, ybc, ydc, pc,
                             conv_w[layer], wg_bf, wb_bf, wo_bf, l_ctx)
            h2_all = jnp.concatenate([h2_all, h2c.reshape(b * l_ctx, d)], axis=0)

        y_tok, w_tok = _moe_routed(h2_all, wr_bf, router_bias[layer], layer, expert_w_gate,
                                   expert_w_up, expert_w_down)
        sg_bf = shared_w_gate[layer].astype(BF16)
        su_bf = shared_w_up[layer].astype(BF16)
        sd_bf = shared_w_down[layer].astype(BF16)
        x = _resid(x.reshape(b * s, d), h2_all, y_tok, w_tok, 0, mods, None, s, sg_bf, su_bf, sd_bf,
                   final_norm_g, last, tm).reshape(b, s, d)
        if not last:
            xc = _resid(xc.reshape(b * l_ctx, d), h2_all, y_tok, w_tok, b * s, mods, ctx_row, l_ctx,
                        sg_bf, su_bf, sd_bf, final_norm_g, False, l_ctx).reshape(b, l_ctx, d)
    return x
```

```python
import functools
import math

import numpy as np
import jax
import jax.numpy as jnp
from jax import lax
from jax.experimental import pallas as pl
from jax.experimental.pallas import tpu as pltpu

F32 = jnp.float32
BF16 = jnp.bfloat16

DEPTH = 2
GRID_W = 64
EPS = 1e-6
N_MOD = 6

FNET_GROUP_DIM = 64
NA_HEADS = 4
NA_HEAD_DIM = 64
NA_WIN_ROWS = 8
NA_WIN_COLS = 16
CONV_WIDTH = 3
DIFF_HEADS = 4
DIFF_QK_DIM = 32
DIFF_V_DIM = 64
ROPE_BASE = 10000.0
N_BRANCHES = 4
BRANCH_DIM = 256

COL_A, COL_BQ, COL_BK, COL_BV, COL_CB, COL_CC, COL_CX, COL_DQ, COL_DK, COL_DV = range(10)
N_COL_BLOCKS = 10
CB = 256
IN_DIM = N_COL_BLOCKS * CB

N_EXPERTS = 256
TOP_K = 8
N_GROUPS = 8
TOPK_GROUPS = 4
ROUTED_SCALE = 2.5
LOG2_E = 1.4426950408889634

VMEM_LIMIT_BYTES = 56 * 1024 * 1024
LANES = 128
BF16_SUBLANES = 16
FFT_N1 = 64
NA_QROWS = 8
NA_KROWS = 16
MOE_ROWS = 256
MOE_ALIGN = BF16_SUBLANES
MOE_X_SLOTS = 4
MOE_W_SLOTS = 3


def _params(*sem):
    return pltpu.CompilerParams(dimension_semantics=sem, vmem_limit_bytes=VMEM_LIMIT_BYTES)


def _dot(a, b):
    return jnp.dot(a, b, preferred_element_type=F32)


def _dot_nt(a, b):
    return lax.dot_general(a, b, (((1,), (1,)), ((), ())), preferred_element_type=F32)


def _norm_mod(xf, g, shift, scale):
    y = xf * lax.rsqrt(jnp.mean(xf * xf, axis=-1, keepdims=True) + EPS)
    return (y * g) * (1.0 + scale) + shift


def _silu(v):
    return v * jax.nn.sigmoid(v)


def _ada_kernel(c_ref, w_ref, b_ref, o_ref):
    s = _silu(c_ref[...])
    o_ref[...] = _dot(s.astype(BF16), w_ref[0].astype(BF16)) + b_ref[0]


def _ada(cvec, w, b, layer):
    rows, d = cvec.shape
    depth, _, n = w.shape
    tn = 1536
    return pl.pallas_call(
        _ada_kernel,
        grid=(n // tn,),
        in_specs=[pl.BlockSpec((rows, d), lambda j: (0, 0)),
                  pl.BlockSpec((1, d, tn), lambda j: (layer, 0, j)),
                  pl.BlockSpec((1, 1, tn), lambda j: (layer, 0, j))],
        out_specs=pl.BlockSpec((rows, tn), lambda j: (0, j)),
        out_shape=jax.ShapeDtypeStruct((rows, n), F32),
        compiler_params=_params("arbitrary"),
    )(cvec, w, b.reshape(depth, 1, n))


def _inproj_kernel(x_ref, g_ref, mod_ref, w_ref, wf_ref, cos_ref, s1_ref, s2_ref, kt_prev, vh_prev,
                   p_ref, ur_ref, ui_ref, kt_ref, vh_ref, *, rope):
    del kt_prev, vh_prev
    mod = mod_ref[0]
    h = _norm_mod(x_ref[0], g_ref[...], mod[0:1], mod[1:2]).astype(BF16)
    for j in range(N_COL_BLOCKS):
        pj = _dot(h, w_ref[:, j * CB:(j + 1) * CB])
        if j == COL_A:
            u = _dot(pj.astype(BF16), wf_ref[...])
            ur_ref[0] = u[:, :CB].astype(BF16)
            ui_ref[0] = u[:, CB:].astype(BF16)
        if rope and j in (COL_DQ, COL_DK):
            cos = jnp.concatenate([cos_ref[...]] * 2, axis=1)
            s1 = jnp.concatenate([s1_ref[...]] * 2, axis=1)
            s2 = jnp.concatenate([s2_ref[...]] * 2, axis=1)
            pj = pj * cos + pltpu.roll(pj, CB - 8, 1) * s1 + pltpu.roll(pj, 8, 1) * s2
            if j == COL_DQ:
                pj = pj * (DIFF_QK_DIM ** -0.5 * LOG2_E)
        p_ref[0, :, j * CB:(j + 1) * CB] = pj.astype(BF16)
        if j == COL_DK:
            kt_ref[0] = pj.T.astype(BF16)
        if j == COL_DV:
            ones = jnp.ones((pj.shape[0], LANES - DIFF_V_DIM), F32)
            for hd in range(DIFF_HEADS):
                v = pj[:, hd * DIFF_V_DIM:(hd + 1) * DIFF_V_DIM]
                vh_ref[0, hd] = jnp.concatenate([v, ones], axis=1).astype(BF16)


def _inproj(x, g, mods, w_bf, wf, rope_tabs, mod_row, rope, tm, key_off, kv_prev):
    b, s, d = x.shape
    n_keys = kv_prev[0].shape[2]
    off = key_off // tm
    mod_idx = (lambda bi, i: (bi, 0, 0)) if mod_row is None else (lambda bi, i: (mod_row, 0, 0))
    tab_spec = pl.BlockSpec((tm, 128), lambda bi, i: (i, 0))
    seq_spec = lambda width: pl.BlockSpec((1, tm, width), lambda bi, i: (bi, i, 0))
    in_specs = [seq_spec(d),
                pl.BlockSpec((1, d), lambda bi, i: (0, 0)),
                pl.BlockSpec((1, N_MOD, d), mod_idx),
                pl.BlockSpec((d, IN_DIM), lambda bi, i: (0, 0)),
                pl.BlockSpec((CB, 2 * CB), lambda bi, i: (0, 0)),
                tab_spec, tab_spec, tab_spec]
    args = [x, g.reshape(1, d), mods, w_bf, wf, *rope_tabs]
    aliases = {len(args): 3, len(args) + 1: 4}
    in_specs += [pl.BlockSpec(memory_space=pl.ANY)] * 2
    args += list(kv_prev)
    return pl.pallas_call(
        functools.partial(_inproj_kernel, rope=rope),
        grid=(b, s // tm),
        in_specs=in_specs,
        out_specs=[seq_spec(IN_DIM), seq_spec(CB), seq_spec(CB),
                   pl.BlockSpec((1, CB, tm), lambda bi, i: (bi, 0, i + off)),
                   pl.BlockSpec((1, DIFF_HEADS, tm, LANES), lambda bi, i: (bi, 0, i + off, 0))],
        out_shape=[jax.ShapeDtypeStruct((b, s, IN_DIM), BF16),
                   jax.ShapeDtypeStruct((b, s, CB), BF16),
                   jax.ShapeDtypeStruct((b, s, CB), BF16),
                   jax.ShapeDtypeStruct((b, CB, n_keys), BF16),
                   jax.ShapeDtypeStruct((b, DIFF_HEADS, n_keys, LANES), BF16)],
        input_output_aliases=aliases,
        compiler_params=_params("parallel", "arbitrary"),
    )(*args)


def _channel_dft_matrix():
    c = np.arange(FNET_GROUP_DIM)
    ang = 2.0 * np.pi * ((c[:, None] * c[None, :]) % FNET_GROUP_DIM) / FNET_GROUP_DIM
    eye = np.eye(CB // FNET_GROUP_DIM)
    m = np.concatenate([np.kron(eye, np.cos(ang)), -np.kron(eye, np.sin(ang))], axis=1)
    return jnp.asarray(m, BF16)


def _rope_tables(s):
    half = DIFF_QK_DIM // 2
    t = jnp.arange(s)
    inv = 1.0 / (ROPE_BASE ** (jnp.arange(0, half, 2, dtype=F32) / half))
    ang_r = (t // GRID_W).astype(F32)[:, None] * inv
    ang_c = (t % GRID_W).astype(F32)[:, None] * inv
    zero = jnp.zeros_like(ang_r)
    cos = jnp.concatenate([jnp.cos(ang_r)] * 2 + [jnp.cos(ang_c)] * 2, axis=1)
    s1 = jnp.concatenate([-jnp.sin(ang_r), zero, -jnp.sin(ang_c), zero], axis=1)
    s2 = jnp.concatenate([zero, jnp.sin(ang_r), zero, jnp.sin(ang_c)], axis=1)
    return tuple(jnp.concatenate([a] * 4, axis=1) for a in (cos, s1, s2))


def _fft1_kernel(ur_ref, ui_ref, w_ref, ct_ref, st_ref, ar_ref, ai_ref):
    n1 = ur_ref.shape[1]
    u = jnp.concatenate([ur_ref[0], ui_ref[0]], axis=0)
    a = _dot(w_ref[...], u)
    ar, ai = a[:n1], a[n1:]
    ct, st = ct_ref[...], st_ref[...]
    ar_ref[0] = (ar * ct + ai * st).astype(BF16)
    ai_ref[0] = (ai * ct - ar * st).astype(BF16)


def _fft2_kernel(ar_ref, ai_ref, w_ref, y_ref, *, norm):
    for j in range(ar_ref.shape[1]):
        a = jnp.concatenate([ar_ref[0, j], ai_ref[0, j]], axis=0)
        y_ref[0, j] = (_dot(w_ref[...], a) * norm).astype(BF16)


def _dft_cos_sin(n):
    k = np.arange(n)
    ang = 2.0 * np.pi * ((k[:, None] * k[None, :]) % n) / n
    return np.cos(ang), np.sin(ang)


def _fourier_latent(ur, ui):
    b, s, cb = ur.shape
    n1, n2 = FFT_N1, s // FFT_N1
    c1, s1 = _dft_cos_sin(n1)
    w1 = jnp.asarray(np.block([[c1, s1], [-s1, c1]]), BF16)
    c2, s2 = _dft_cos_sin(n2)
    w2 = jnp.asarray(np.concatenate([c2, s2], axis=1), BF16)
    tw = 2.0 * np.pi * (np.arange(n1)[:, None] * np.arange(n2)[None, :]) / s
    ct = jnp.broadcast_to(jnp.asarray(np.cos(tw), F32)[:, :, None], (n1, n2, cb)).reshape(n1, n2 * cb)
    st = jnp.broadcast_to(jnp.asarray(np.sin(tw), F32)[:, :, None], (n1, n2, cb)).reshape(n1, n2 * cb)
    lanes = n2 * cb
    tn = min(lanes, 4096)
    u_spec = pl.BlockSpec((1, n1, tn), lambda j, bi: (bi, 0, j))
    t_spec = pl.BlockSpec((n1, tn), lambda j, bi: (0, j))
    ar, ai = pl.pallas_call(
        _fft1_kernel,
        grid=(lanes // tn, b),
        in_specs=[u_spec, u_spec, pl.BlockSpec((2 * n1, 2 * n1), lambda j, bi: (0, 0)), t_spec, t_spec],
        out_specs=[u_spec, u_spec],
        out_shape=[jax.ShapeDtypeStruct((b, n1, lanes), BF16)] * 2,
        compiler_params=_params("arbitrary", "arbitrary"),
    )(ur.reshape(b, n1, lanes), ui.reshape(b, n1, lanes), w1, ct, st)
    kc = 8
    a_spec = pl.BlockSpec((1, kc, n2, cb), lambda bi, j: (bi, j, 0, 0))
    y = pl.pallas_call(
        functools.partial(_fft2_kernel, norm=1.0 / math.sqrt(s * FNET_GROUP_DIM)),
        grid=(b, n1 // kc),
        in_specs=[a_spec, a_spec, pl.BlockSpec((n2, 2 * n2), lambda bi, j: (0, 0))],
        out_specs=a_spec,
        out_shape=jax.ShapeDtypeStruct((b, n1, n2, cb), BF16),
        compiler_params=_params("parallel", "arbitrary"),
    )(ar.reshape(b, n1, n2, cb), ai.reshape(b, n1, n2, cb), w2)
    return jnp.transpose(y, (0, 2, 1, 3)).reshape(b, s, cb)


def _na_kernel(q_ref, k_ref, v_ref, kc_ref, vc_ref, bias_ref, o_ref, *, rows):
    rb = pl.program_id(1)
    kb = jnp.clip(rb * NA_QROWS - NA_WIN_ROWS // 2, 0, rows - NA_KROWS)
    nk = NA_KROWS * GRID_W
    tok0 = pl.multiple_of(kb * GRID_W, 256)
    scale = NA_HEAD_DIM ** -0.5
    outs = []
    for h in range(NA_HEADS):
        sl = slice(h * NA_HEAD_DIM, (h + 1) * NA_HEAD_DIM)
        q = q_ref[0, :, sl]
        s = _dot_nt(q, k_ref[0, pl.ds(tok0, nk), sl]) * scale + bias_ref[0, h]
        sc = _dot_nt(q, kc_ref[0, :, sl]) * scale
        m = jnp.maximum(jnp.max(s, axis=-1, keepdims=True), jnp.max(sc, axis=-1, keepdims=True))
        e = jnp.exp(s - m)
        ec = jnp.exp(sc - m)
        l = jnp.sum(e, axis=-1, keepdims=True) + jnp.sum(ec, axis=-1, keepdims=True)
        o = _dot(e.astype(BF16), v_ref[0, pl.ds(tok0, nk), sl]) + _dot(ec.astype(BF16), vc_ref[0, :, sl])
        outs.append(o / l)
    o_ref[0] = jnp.concatenate(outs, axis=1).astype(BF16)


def _na_bias_tables(rel_bias, rows):
    tabs = []
    cq = np.arange(GRID_W)
    col_lo = np.clip(cq - NA_WIN_COLS // 2, 0, GRID_W - NA_WIN_COLS)
    col_ok = (cq[None, :] >= col_lo[:, None]) & (cq[None, :] < col_lo[:, None] + NA_WIN_COLS)
    dc_idx = np.clip(cq[None, :] - cq[:, None] + NA_WIN_COLS - 1, 0, 2 * NA_WIN_COLS - 2)
    for r0 in (0, NA_QROWS, rows - NA_QROWS):
        kb = int(np.clip(r0 - NA_WIN_ROWS // 2, 0, rows - NA_KROWS))
        r = r0 + np.arange(NA_QROWS)
        rk = kb + np.arange(NA_KROWS)
        start = np.clip(r - NA_WIN_ROWS // 2, 0, rows - NA_WIN_ROWS)
        row_ok = (rk[None, :] >= start[:, None]) & (rk[None, :] < start[:, None] + NA_WIN_ROWS)
        dr_idx = np.clip(rk[None, :] - r[:, None] + NA_WIN_ROWS - 1, 0, 2 * NA_WIN_ROWS - 2)
        ok = row_ok[:, None, :, None] & col_ok[None, :, None, :]
        oh_r = jnp.asarray(dr_idx[:, :, None] == np.arange(2 * NA_WIN_ROWS - 1), F32)
        oh_c = jnp.asarray(dc_idx[:, :, None] == np.arange(2 * NA_WIN_COLS - 1), F32)
        by_row = jnp.einsum('qka,hab->hqkb', oh_r, rel_bias.astype(F32), precision=lax.Precision.HIGHEST)
        vals = jnp.einsum('hqkb,cdb->hqckd', by_row, oh_c, precision=lax.Precision.HIGHEST)
        tab = jnp.where(ok[None], vals, -jnp.inf)
        tabs.append(tab.reshape(NA_HEADS, NA_QROWS * GRID_W, NA_KROWS * GRID_W))
    return jnp.stack(tabs)


def _na_latent(p, pc, bias_tabs):
    b, s, _ = p.shape
    l = pc.shape[1]
    rows = s // GRID_W
    nrb = rows // NA_QROWS
    tq = NA_QROWS * GRID_W
    nk = NA_KROWS * GRID_W
    return pl.pallas_call(
        functools.partial(_na_kernel, rows=rows),
        grid=(b, nrb),
        in_specs=[pl.BlockSpec((1, tq, CB), lambda bi, i: (bi, i, COL_BQ)),
                  pl.BlockSpec((1, s, CB), lambda bi, i: (bi, 0, COL_BK)),
                  pl.BlockSpec((1, s, CB), lambda bi, i: (bi, 0, COL_BV)),
                  pl.BlockSpec((1, l, CB), lambda bi, i: (bi, 0, COL_BK)),
                  pl.BlockSpec((1, l, CB), lambda bi, i: (bi, 0, COL_BV)),
                  pl.BlockSpec((1, NA_HEADS, tq, nk),
                               lambda bi, i: (jnp.minimum(i, 1) + (i == nrb - 1).astype(jnp.int32), 0, 0, 0))],
        out_specs=pl.BlockSpec((1, tq, CB), lambda bi, i: (bi, i, 0)),
        out_shape=jax.ShapeDtypeStruct((b, s, CB), BF16),
        compiler_params=_params("parallel", "arbitrary"),
    )(p, p, p, pc, pc, bias_tabs)


def _diff_lambda(lp, lam_init):
    return (jnp.exp(jnp.sum(lp[0:1] * lp[1:2], axis=-1, keepdims=True))
            - jnp.exp(jnp.sum(lp[2:3] * lp[3:4], axis=-1, keepdims=True)) + lam_init)


def _diff_finish(o, g, lam_init):
    y = o * lax.rsqrt(jnp.mean(o * o, axis=-1, keepdims=True) + EPS)
    return y * g * (1.0 - lam_init)


def _diff_kernel(q_ref, kt_ref, v_ref, lp_ref, g_ref, o_ref, m_sc, acc_sc, e_sc, *, tk, lam_init):
    tq = q_ref.shape[1]
    nk = kt_ref.shape[2] // tk
    m_sc[...] = jnp.full(m_sc.shape, -jnp.inf, F32)
    acc_sc[...] = jnp.zeros(acc_sc.shape, F32)

    def body(c, carry):
        k0 = pl.multiple_of(c * tk, LANES)
        for h in range(DIFF_HEADS):
            for m in range(2):
                rows = slice(m * tq, (m + 1) * tq)
                dims = slice((2 * h + m) * DIFF_QK_DIM, (2 * h + m + 1) * DIFF_QK_DIM)
                s = _dot(q_ref[0, :, dims], kt_ref[0, dims, pl.ds(k0, tk)])
                m_old = m_sc[h, rows]
                m_new = jnp.maximum(m_old, jnp.max(s, axis=-1, keepdims=True))
                e_sc[h, rows] = jnp.exp2(s - m_new[:, :1]).astype(BF16)
                acc_sc[h, rows] = jnp.exp2(m_old - m_new) * acc_sc[h, rows]
                m_sc[h, rows] = m_new
            acc_sc[h] += _dot(e_sc[h], v_ref[0, h, pl.ds(k0, tk), :])
        return carry

    lax.fori_loop(0, nk, body, 0)
    lam = _diff_lambda(lp_ref[...], lam_init)
    outs = []
    for h in range(DIFF_HEADS):
        acc = acc_sc[h]
        o = acc[:, :DIFF_V_DIM] / acc[:, DIFF_V_DIM:DIFF_V_DIM + 1]
        outs.append(_diff_finish(o[:tq] - lam * o[tq:], g_ref[...], lam_init))
    o_ref[0] = jnp.concatenate(outs, axis=1).astype(BF16)


def _key_tile(nkeys, cap):
    return max(t for t in range(128, cap + 1, 128) if nkeys % t == 0)


def _diff_latent(p, kt, vh, lp, sub_g, lam_init, tq, tk):
    b, s, _ = p.shape
    nkeys = kt.shape[2]
    return pl.pallas_call(
        functools.partial(_diff_kernel, tk=tk, lam_init=lam_init),
        grid=(b, s // tq),
        in_specs=[pl.BlockSpec((1, tq, CB), lambda bi, i: (bi, i, COL_DQ)),
                  pl.BlockSpec((1, CB, nkeys), lambda bi, i: (bi, 0, 0), pipeline_mode=pl.Buffered(1)),
                  pl.BlockSpec((1, DIFF_HEADS, nkeys, LANES), lambda bi, i: (bi, 0, 0, 0),
                               pipeline_mode=pl.Buffered(1)),
                  pl.BlockSpec((4, DIFF_QK_DIM), lambda bi, i: (0, 0)),
                  pl.BlockSpec((1, DIFF_V_DIM), lambda bi, i: (0, 0))],
        out_specs=pl.BlockSpec((1, tq, CB), lambda bi, i: (bi, i, 0)),
        out_shape=jax.ShapeDtypeStruct((b, s, CB), BF16),
        scratch_shapes=[pltpu.VMEM((DIFF_HEADS, 2 * tq, LANES), F32),
                        pltpu.VMEM((DIFF_HEADS, 2 * tq, LANES), F32),
                        pltpu.VMEM((DIFF_HEADS, 2 * tq, tk), BF16)],
        compiler_params=_params("parallel", "arbitrary"),
    )(p, kt, vh, lp, sub_g.reshape(1, DIFF_V_DIM))


def _softmax_rows(s):
    e = jnp.exp(s - jnp.max(s, axis=-1, keepdims=True))
    return e / jnp.sum(e, axis=-1, keepdims=True)


def _ctx_kernel(pc_ref, ur_ref, ui_ref, wf_ref, lp_ref, g_ref, ya_ref, yb_ref, yd_ref, *, lam_init):
    l = pc_ref.shape[1]
    col = lambda j, lo, hi: pc_ref[0, :, j * CB + lo:j * CB + hi]
    u = jnp.concatenate([ur_ref[0], ui_ref[0]], axis=0)
    ya_ref[0] = (_dot(wf_ref[...], u) * (1.0 / math.sqrt(l * FNET_GROUP_DIM))).astype(BF16)
    outs = []
    for h in range(NA_HEADS):
        lo, hi = h * NA_HEAD_DIM, (h + 1) * NA_HEAD_DIM
        pr = _softmax_rows(_dot_nt(col(COL_BQ, lo, hi), col(COL_BK, lo, hi)) * NA_HEAD_DIM ** -0.5)
        outs.append(_dot(pr.astype(BF16), col(COL_BV, lo, hi)))
    yb_ref[0] = jnp.concatenate(outs, axis=1).astype(BF16)
    lam = _diff_lambda(lp_ref[...], lam_init)
    outs = []
    for h in range(DIFF_HEADS):
        pm = []
        for m in range(2):
            lo = (2 * h + m) * DIFF_QK_DIM
            pm.append(_softmax_rows(_dot_nt(col(COL_DQ, lo, lo + DIFF_QK_DIM), col(COL_DK, lo, lo + DIFF_QK_DIM))
                                    * DIFF_QK_DIM ** -0.5))
        a = (pm[0] - lam * pm[1]).astype(BF16)
        o = _dot(a, col(COL_DV, h * DIFF_V_DIM, (h + 1) * DIFF_V_DIM))
        outs.append(_diff_finish(o, g_ref[...], lam_init))
    yd_ref[0] = jnp.concatenate(outs, axis=1).astype(BF16)


def _ctx_branches(pc, ucr, uci, lp, sub_g, lam_init):
    b, l, _ = pc.shape
    c, s = _dft_cos_sin(l)
    wf = jnp.asarray(np.concatenate([c, s], axis=1), BF16)
    y_spec = pl.BlockSpec((1, l, CB), lambda bi: (bi, 0, 0))
    return pl.pallas_call(
        functools.partial(_ctx_kernel, lam_init=lam_init),
        grid=(b,),
        in_specs=[pl.BlockSpec((1, l, IN_DIM), lambda bi: (bi, 0, 0)), y_spec, y_spec,
                  pl.BlockSpec((l, 2 * l), lambda bi: (0, 0)),
                  pl.BlockSpec((4, DIFF_QK_DIM), lambda bi: (0, 0)),
                  pl.BlockSpec((1, DIFF_V_DIM), lambda bi: (0, 0))],
        out_specs=[y_spec] * 3,
        out_shape=[jax.ShapeDtypeStruct((b, l, CB), BF16)] * 3,
        compiler_params=_params("parallel"),
    )(pc, ucr, uci, wf, lp, sub_g.reshape(1, DIFF_V_DIM))


def _merge_kernel(x_ref, mod_ref, g1_ref, g2_ref, ya_ref, yb_ref, yd_ref, pb_ref, pc_ref, px_ref,
                  cp_ref, xp_ref, cn_ref, xn_ref, cw_ref, wg_ref, wb_ref, wo_ref,
                  xo_ref, h2_ref):
    i = pl.program_id(1)
    last = pl.num_programs(1) - 1
    tm = x_ref.shape[1]
    mod = mod_ref[0]
    x = x_ref[0]
    h = _norm_mod(x, g1_ref[...], mod[0:1], mod[1:2]).astype(BF16)

    u = pc_ref[0].astype(F32) * px_ref[0].astype(F32)
    up = cp_ref[0, BF16_SUBLANES - 1:, :].astype(F32) * xp_ref[0, BF16_SUBLANES - 1:, :].astype(F32)
    un = cn_ref[0, :1, :].astype(F32) * xn_ref[0, :1, :].astype(F32)
    up = jnp.where(i == 0, 0.0, up)
    un = jnp.where(i == last, 0.0, un)
    rid = lax.broadcasted_iota(jnp.int32, u.shape, 0)
    u_prev = jnp.where(rid == 0, up, pltpu.roll(u, 1, 0))
    u_next = jnp.where(rid == tm - 1, un, pltpu.roll(u, tm - 1, 0))
    cw = cw_ref[...]
    yc = pb_ref[0].astype(F32) * (cw[0:1] * u_prev + cw[1:2] * u + cw[2:3] * u_next)

    branches = (ya_ref[0], yb_ref[0], yc.astype(BF16), yd_ref[0])
    d = x.shape[1]
    out = None
    for n in range(d // CB):
        cols = slice(n * CB, (n + 1) * CB)
        merged = None
        for j in range(N_BRANCHES):
            t = jax.nn.sigmoid(_dot(h, wg_ref[j, :, cols])) * _dot(branches[j], wb_ref[j, :, cols])
            merged = t if merged is None else merged + t
        t = _dot(merged.astype(BF16), wo_ref[cols, :])
        out = t if out is None else out + t
    xn = x + mod[2:3] * out
    xo_ref[0] = xn
    h2_ref[0] = _norm_mod(xn, g2_ref[...], mod[3:4], mod[4:5]).astype(BF16)


def _merge(x, mods, mod_row, g1, g2, ya, yb, yd, p, conv_w, wg, wb, wo, tm):
    b, s, d = x.shape
    hb = tm // BF16_SUBLANES
    n_halo = s // BF16_SUBLANES
    mod_idx = (lambda bi, i: (bi, 0, 0)) if mod_row is None else (lambda bi, i: (mod_row, 0, 0))
    seq = lambda width, col=0: pl.BlockSpec((1, tm, width), lambda bi, i: (bi, i, col))
    prev = lambda col: pl.BlockSpec((1, BF16_SUBLANES, CB), lambda bi, i: (bi, jnp.maximum(i * hb - 1, 0), col))
    nxt = lambda col: pl.BlockSpec((1, BF16_SUBLANES, CB),
                                   lambda bi, i: (bi, jnp.minimum((i + 1) * hb, n_halo - 1), col))
    const = lambda shape: pl.BlockSpec(shape, lambda bi, i: (0,) * len(shape))
    return pl.pallas_call(
        _merge_kernel,
        grid=(b, s // tm),
        in_specs=[seq(d), pl.BlockSpec((1, N_MOD, d), mod_idx), const((1, d)), const((1, d)),
                  seq(CB), seq(CB), seq(CB),
                  seq(CB, COL_CB), seq(CB, COL_CC), seq(CB, COL_CX),
                  prev(COL_CC), prev(COL_CX), nxt(COL_CC), nxt(COL_CX),
                  const((CONV_WIDTH, CB)),
                  const((N_BRANCHES, d, d)), const((N_BRANCHES, BRANCH_DIM, d)), const((d, d))],
        out_specs=[seq(d), seq(d)],
        out_shape=[jax.ShapeDtypeStruct((b, s, d), F32),
                   jax.ShapeDtypeStruct((b, s, d), BF16)],
        compiler_params=_params("parallel", "arbitrary"),
    )(x, mods, g1.reshape(1, d), g2.reshape(1, d), ya, yb, yd, p, p, p, p, p, p, p,
      conv_w, wg, wb, wo)


def _route_kernel(h2_ref, wr_ref, bias_ref, tri_ref, ones_ref, idx_ref, w_ref, rank_ref, cnt_ref,
                  score_sc, sel_sc, carry_sc):
    i = pl.program_id(0)
    tm = h2_ref.shape[0]
    ne = wr_ref.shape[0]
    gsz = ne // N_GROUPS
    n_chunks = tm // LANES

    @pl.when(i == 0)
    def _():
        carry_sc[...] = jnp.zeros(carry_sc.shape, F32)

    score_sc[...] = jax.nn.sigmoid(_dot_nt(wr_ref[...], h2_ref[...]))

    def select(cidx, carry):
        c0 = pl.multiple_of(cidx * LANES, LANES)
        scores = score_sc[:, pl.ds(c0, LANES)]
        biased = scores + bias_ref[...]
        liota = lax.broadcasted_iota(jnp.int32, (gsz, LANES), 0)
        gs = []
        for g in range(N_GROUPS):
            v = biased[g * gsz:(g + 1) * gsz]
            m1 = jnp.max(v, axis=0, keepdims=True)
            i1 = jnp.min(jnp.where(v == m1, liota, gsz), axis=0, keepdims=True)
            m2 = jnp.max(jnp.where(liota == i1, -jnp.inf, v), axis=0, keepdims=True)
            gs.append(m1 + m2)
        gsm = jnp.concatenate(gs, axis=0)
        giota = lax.broadcasted_iota(jnp.int32, gsm.shape, 0)
        keep = jnp.zeros(gsm.shape, F32)
        for _ in range(TOPK_GROUPS):
            m = jnp.max(gsm, axis=0, keepdims=True)
            gi = jnp.min(jnp.where(gsm == m, giota, N_GROUPS), axis=0, keepdims=True)
            hit = giota == gi
            keep = jnp.where(hit, 1.0, keep)
            gsm = jnp.where(hit, -jnp.inf, gsm)
        cur = jnp.concatenate(
            [jnp.where(jnp.broadcast_to(keep[g:g + 1], (gsz, LANES)) > 0.0, biased[g * gsz:(g + 1) * gsz], -jnp.inf)
             for g in range(N_GROUPS)], axis=0)
        eiota = lax.broadcasted_iota(jnp.int32, (ne, LANES), 0)
        sel = jnp.zeros((ne, LANES), F32)
        idxs, ws = [], []
        for _ in range(TOP_K):
            m = jnp.max(cur, axis=0, keepdims=True)
            ik = jnp.min(jnp.where(cur == m, eiota, ne), axis=0, keepdims=True)
            hit = eiota == ik
            cur = jnp.where(hit, -jnp.inf, cur)
            ws.append(jnp.sum(jnp.where(hit, scores, 0.0), axis=0, keepdims=True))
            idxs.append(ik)
            sel = jnp.where(hit, 1.0, sel)
        w = jnp.concatenate(ws, axis=0)
        w_ref[:, pl.ds(c0, LANES)] = w / jnp.sum(w, axis=0, keepdims=True) * ROUTED_SCALE
        idx_ref[:, pl.ds(c0, LANES)] = jnp.concatenate(idxs, axis=0)
        sel_sc[:, pl.ds(c0, LANES)] = sel.astype(BF16)
        return carry

    lax.fori_loop(0, n_chunks, select, 0)

    sel_all = sel_sc[...]
    score_sc[...] = _dot(sel_all, tri_ref[...]) + jnp.concatenate([carry_sc[...]] * n_chunks, axis=1)

    def ranks(cidx, carry):
        c0 = pl.multiple_of(cidx * LANES, LANES)
        before = score_sc[:, pl.ds(c0, LANES)]
        idx = idx_ref[:, pl.ds(c0, LANES)]
        eiota = lax.broadcasted_iota(jnp.int32, (ne, LANES), 0)
        rows = [jnp.sum(jnp.where(eiota == idx[k:k + 1], before, 0.0), axis=0, keepdims=True)
                for k in range(TOP_K)]
        rank_ref[:, pl.ds(c0, LANES)] = jnp.concatenate(rows, axis=0).astype(jnp.int32)
        return carry

    lax.fori_loop(0, n_chunks, ranks, 0)
    carry_sc[...] += _dot(sel_all, ones_ref[...])
    cnt_ref[...] = carry_sc[...]


def _route(h2_all, wr_t, bias, tm):
    n, d = h2_all.shape
    ne = wr_t.shape[0]
    tri = jnp.asarray(np.triu(np.ones((tm, tm), np.float32), 1), BF16)
    ones = jnp.ones((tm, LANES), BF16)
    bias_b = jnp.broadcast_to(bias.astype(F32)[:, None], (ne, LANES))
    const = lambda shape: pl.BlockSpec(shape, lambda i: (0,) * len(shape))
    tok = pl.BlockSpec((TOP_K, tm), lambda i: (0, i))
    return pl.pallas_call(
        _route_kernel,
        grid=(n // tm,),
        in_specs=[pl.BlockSpec((tm, d), lambda i: (i, 0)), const((ne, d)), const((ne, LANES)),
                  const((tm, tm)), const((tm, LANES))],
        out_specs=[tok, tok, tok, const((ne, LANES))],
        out_shape=[jax.ShapeDtypeStruct((TOP_K, n), jnp.int32),
                   jax.ShapeDtypeStruct((TOP_K, n), F32),
                   jax.ShapeDtypeStruct((TOP_K, n), jnp.int32),
                   jax.ShapeDtypeStruct((ne, LANES), F32)],
        scratch_shapes=[pltpu.VMEM((ne, tm), F32), pltpu.VMEM((ne, tm), BF16), pltpu.VMEM((ne, LANES), F32)],
        compiler_params=_params("arbitrary"),
    )(h2_all, wr_t, bias_b, tri, ones)


def _pos_kernel(idx_ref, rank_ref, start_ref, pos_ref):
    ne = start_ref.shape[0]
    start = start_ref[...]

    def body(cidx, carry):
        c0 = pl.multiple_of(cidx * LANES, LANES)
        idx = idx_ref[:, pl.ds(c0, LANES)]
        eiota = lax.broadcasted_iota(jnp.int32, (ne, LANES), 0)
        rows = [jnp.sum(jnp.where(eiota == idx[k:k + 1], start, 0.0), axis=0, keepdims=True)
                for k in range(TOP_K)]
        pos_ref[:, pl.ds(c0, LANES)] = jnp.concatenate(rows, axis=0).astype(jnp.int32) + rank_ref[:, pl.ds(c0, LANES)]
        return carry

    lax.fori_loop(0, idx_ref.shape[1] // LANES, body, 0)


def _positions(idx, rank, start_rows, tm):
    k, n = idx.shape
    ne = start_rows.shape[0]
    start_b = jnp.broadcast_to(start_rows.astype(F32)[:, None], (ne, LANES))
    tok = pl.BlockSpec((k, tm), lambda i: (0, i))
    return pl.pallas_call(
        _pos_kernel,
        grid=(n // tm,),
        in_specs=[tok, tok, pl.BlockSpec((ne, LANES), lambda i: (0, 0))],
        out_specs=tok,
        out_shape=jax.ShapeDtypeStruct((k, n), jnp.int32),
        compiler_params=_params("parallel"),
    )(idx, rank, start_b)


def _rowtok_kernel(start_ref, cnt_ref, pos_ref, out_ref, *, n_tokens):
    i = pl.program_id(0)
    tn = pos_ref.shape[1]
    n_rows = out_ref.shape[0]
    ne = start_ref.shape[0]

    @pl.when(i == 0)
    def _():
        def gaps(e, carry):
            lo = start_ref[e] + cnt_ref[e]
            hi = jnp.where(e + 1 < ne, start_ref[jnp.minimum(e + 1, ne - 1)], n_rows)

            def one(r, c):
                out_ref[r] = lax.rem(r, n_tokens)
                return c

            return lax.fori_loop(lo, hi, one, carry)

        lax.fori_loop(0, ne, gaps, 0)

    def body(t, carry):
        for k in range(TOP_K):
            out_ref[pos_ref[k, t]] = i * tn + t
        return carry

    lax.fori_loop(0, tn, body, 0)


def _row_tokens(pos, seg_start, counts, n_rows):
    k, n = pos.shape
    tn = max(t for t in range(LANES, 2048 + 1, LANES) if n % t == 0)
    grid_spec = pltpu.PrefetchScalarGridSpec(
        num_scalar_prefetch=2,
        grid=(n // tn,),
        in_specs=[pl.BlockSpec((k, tn), lambda i, st, ct: (0, i), memory_space=pltpu.SMEM)],
        out_specs=pl.BlockSpec((n_rows,), lambda i, st, ct: (0,), memory_space=pltpu.SMEM))
    return pl.pallas_call(
        functools.partial(_rowtok_kernel, n_tokens=n),
        grid_spec=grid_spec,
        out_shape=jax.ShapeDtypeStruct((n_rows,), jnp.int32),
        compiler_params=_params("arbitrary"),
    )(seg_start, counts, pos)


def _expert_kernel(crow_ref, crun_ref, rexp_ref, meta_ref, x_hbm, wg_hbm, wu_hbm, wd_hbm, y_hbm,
                   wg_buf, wu_buf, wd_buf, wg_bf, wu_bf, wd_bf, x_buf, y_buf, zero_buf,
                   w_sem, x_sem, y_sem, z_sem, *, layer):
    n_chunks, n_runs, tail_start = meta_ref[0], meta_ref[1], meta_ref[2]
    n_rows = x_hbm.shape[0]

    def w_copies(run):
        expert, s = rexp_ref[run], run % MOE_W_SLOTS
        return (pltpu.make_async_copy(wg_hbm.at[layer, expert], wg_buf.at[s], w_sem.at[s, 0]),
                pltpu.make_async_copy(wu_hbm.at[layer, expert], wu_buf.at[s], w_sem.at[s, 1]),
                pltpu.make_async_copy(wd_hbm.at[layer, expert], wd_buf.at[s], w_sem.at[s, 2]))

    def x_copy(c):
        rows = pl.ds(pl.multiple_of(crow_ref[c], MOE_ALIGN), MOE_ROWS)
        return pltpu.make_async_copy(x_hbm.at[rows], x_buf.at[c % MOE_X_SLOTS], x_sem.at[c % MOE_X_SLOTS])

    def y_copy(c):
        rows = pl.ds(pl.multiple_of(crow_ref[c], MOE_ALIGN), MOE_ROWS)
        return pltpu.make_async_copy(y_buf.at[c % 2], y_hbm.at[rows], y_sem.at[c % 2])

    def zero_copy(row0):
        return pltpu.make_async_copy(zero_buf, y_hbm.at[pl.ds(row0, MOE_ALIGN)], z_sem.at[0])

    for r in range(MOE_W_SLOTS - 1):
        @pl.when(r < n_runs)
        def _(r=r):
            for cp in w_copies(r):
                cp.start()

    for c in range(MOE_X_SLOTS - 1):
        @pl.when(c < n_chunks)
        def _(c=c):
            x_copy(c).start()

    def chunk(c, carry):
        run = crun_ref[c]
        first = jnp.logical_or(c == 0, crun_ref[jnp.maximum(c - 1, 0)] != run)

        @pl.when(c + MOE_X_SLOTS - 1 < n_chunks)
        def _():
            x_copy(c + MOE_X_SLOTS - 1).start()

        @pl.when(first)
        def _():
            @pl.when(run + MOE_W_SLOTS - 1 < n_runs)
            def _():
                for cp in w_copies(run + MOE_W_SLOTS - 1):
                    cp.start(priority=1)

            for cp in w_copies(run):
                cp.wait()
            s = run % MOE_W_SLOTS
            wg_bf[...] = wg_buf[s].astype(BF16)
            wu_bf[...] = wu_buf[s].astype(BF16)
            wd_bf[...] = wd_buf[s].astype(BF16)

        x_copy(c).wait()
        x = x_buf[c % MOE_X_SLOTS]
        a = (_silu(_dot(x, wg_bf[...])) * _dot(x, wu_bf[...])).astype(BF16)
        y = _dot(a, wd_bf[...]).astype(BF16)

        @pl.when(c > 0)
        def _():
            y_copy(c - 1).wait()

        y_buf[c % 2] = y
        y_copy(c).start()
        return carry

    lax.fori_loop(0, n_chunks, chunk, 0)

    @pl.when(n_chunks > 0)
    def _():
        y_copy(n_chunks - 1).wait()

    zero_buf[...] = jnp.zeros(zero_buf.shape, zero_buf.dtype)
    n_tail = (n_rows - tail_start) // MOE_ALIGN

    def fill(t, carry):
        zero_copy(pl.multiple_of(tail_start + t * MOE_ALIGN, MOE_ALIGN)).start()
        return carry

    def drain(t, carry):
        zero_copy(0).wait()
        return carry

    lax.fori_loop(0, n_tail, fill, 0)
    lax.fori_loop(0, n_tail, drain, 0)


def _experts(x_sorted, seg_start, counts, layer, w_g, w_u, w_d):
    n_rows, d = x_sorted.shape
    ne, f = w_g.shape[1], w_g.shape[3]
    nch = (counts + MOE_ROWS - 1) // MOE_ROWS
    c_end = jnp.cumsum(nch)
    max_chunks = (n_rows - MOE_ROWS) // MOE_ROWS + ne
    g = jnp.arange(max_chunks, dtype=jnp.int32)
    c_exp = jnp.minimum(jnp.sum((c_end[None, :] <= g[:, None]).astype(jnp.int32), axis=1), ne - 1)
    c_row = jnp.where(g < c_end[-1], seg_start[c_exp] + (g - (c_end - nch)[c_exp]) * MOE_ROWS, 0)
    has_rows = (nch > 0).astype(jnp.int32)
    run_end = jnp.cumsum(has_rows)
    c_run = (run_end - 1)[c_exp]
    r = jnp.arange(ne, dtype=jnp.int32)
    r_exp = jnp.minimum(jnp.sum((run_end[None, :] <= r[:, None]).astype(jnp.int32), axis=1), ne - 1)
    tail_start = jnp.max(jnp.where(nch > 0, seg_start + nch * MOE_ROWS, 0))
    meta = jnp.stack([c_end[-1], run_end[-1], tail_start]).astype(jnp.int32)
    any_spec = pl.BlockSpec(memory_space=pl.ANY)
    grid_spec = pltpu.PrefetchScalarGridSpec(
        num_scalar_prefetch=4,
        grid=(1,),
        in_specs=[any_spec] * 4,
        out_specs=any_spec,
        scratch_shapes=[pltpu.VMEM((MOE_W_SLOTS, d, f), F32), pltpu.VMEM((MOE_W_SLOTS, d, f), F32),
                        pltpu.VMEM((MOE_W_SLOTS, f, d), F32),
                        pltpu.VMEM((d, f), BF16), pltpu.VMEM((d, f), BF16), pltpu.VMEM((f, d), BF16),
                        pltpu.VMEM((MOE_X_SLOTS, MOE_ROWS, d), BF16), pltpu.VMEM((2, MOE_ROWS, d), BF16),
                        pltpu.VMEM((MOE_ALIGN, d), BF16),
                        pltpu.SemaphoreType.DMA((MOE_W_SLOTS, 3)), pltpu.SemaphoreType.DMA((MOE_X_SLOTS,)),
                        pltpu.SemaphoreType.DMA((2,)), pltpu.SemaphoreType.DMA((1,))])
    return pl.pallas_call(
        functools.partial(_expert_kernel, layer=layer),
        grid_spec=grid_spec,
        out_shape=jax.ShapeDtypeStruct((n_rows, d), BF16),
        compiler_params=_params("arbitrary"),
    )(c_row.astype(jnp.int32), c_run.astype(jnp.int32), r_exp, meta, x_sorted, w_g, w_u, w_d)


def _resid_kernel(x_ref, h2_ref, y_ref, w_ref, mod_ref, sg_ref, su_ref, sd_ref, gf_ref, o_ref, *, final):
    h2 = h2_ref[...]
    a = (_silu(_dot(h2, sg_ref[...])) * _dot(h2, su_ref[...])).astype(BF16)
    y = _dot(a, sd_ref[...])
    w = w_ref[...]
    for k in range(TOP_K):
        y = y + w[:, k:k + 1] * y_ref[k].astype(F32)
    xo = x_ref[...] + mod_ref[0][5:6] * y
    if final:
        xo = xo * lax.rsqrt(jnp.mean(xo * xo, axis=-1, keepdims=True) + EPS) * gf_ref[...]
    o_ref[...] = xo


def _resid(x_flat, h2_all, y_tok, w_tok, row_off, mods, mod_row, rows_per_mod, sg, su, sd, gf, final, tm):
    n, d = x_flat.shape
    f = sg.shape[1]
    off = row_off // tm
    per = rows_per_mod // tm
    mod_idx = (lambda i: (i // per, 0, 0)) if mod_row is None else (lambda i: (mod_row, 0, 0))
    const = lambda shape: pl.BlockSpec(shape, lambda i: (0,) * len(shape))
    return pl.pallas_call(
        functools.partial(_resid_kernel, final=final),
        grid=(n // tm,),
        in_specs=[pl.BlockSpec((tm, d), lambda i: (i, 0)),
                  pl.BlockSpec((tm, d), lambda i: (i + off, 0)),
                  pl.BlockSpec((TOP_K, tm, d), lambda i: (0, i + off, 0)),
                  pl.BlockSpec((tm, TOP_K), lambda i: (i + off, 0)),
                  pl.BlockSpec((1, N_MOD, d), mod_idx),
                  const((d, f)), const((d, f)), const((f, d)), const((1, d))],
        out_specs=pl.BlockSpec((tm, d), lambda i: (i, 0)),
        out_shape=jax.ShapeDtypeStruct((n, d), F32),
        compiler_params=_params("parallel"),
    )(x_flat, h2_all, y_tok, w_tok, mods, sg, su, sd, gf.reshape(1, d))


def _moe_routed(h2_all, wr_t, bias, layer, w_g, w_u, w_d):
    n, d = h2_all.shape
    idx, wts, rank, cnt = _route(h2_all, wr_t, bias, 512)
    counts = cnt[:, 0].astype(jnp.int32)
    padded = (counts + MOE_ALIGN - 1) // MOE_ALIGN * MOE_ALIGN
    seg_start = jnp.cumsum(padded) - padded
    n_rows = -(-(n * TOP_K + N_EXPERTS * (MOE_ALIGN - 1)) // MOE_ROWS) * MOE_ROWS + MOE_ROWS
    pos2 = _positions(idx, rank, seg_start, 512)
    pos = pos2.reshape(TOP_K * n)
    row_tok = _row_tokens(pos2, seg_start, counts, n_rows)
    x_sorted = jnp.concatenate([h2_all, h2_all], axis=0).at[row_tok].get(mode='promise_in_bounds')
    y_sorted = _experts(x_sorted, seg_start, counts, layer, w_g, w_u, w_d)
    return y_sorted.at[pos].get(mode='promise_in_bounds').reshape(TOP_K, n, d), wts.T


def kernel(x, c, ctx, c_ctx, ada_w, ada_b, norm1_g, w_in, conv_w, na_rel_bias, diff_lambda,
           diff_subln_g, w_branch_gate, w_branch, w_out, norm2_g, router_w, router_bias,
           expert_w_gate, expert_w_up, expert_w_down, shared_w_gate, shared_w_up, shared_w_down,
           final_norm_g):
    b, s, d = x.shape
    l_ctx = ctx.shape[1]
    rows = s // GRID_W
    ctx_row = b
    cvec = jnp.zeros((8, d), F32).at[:b].set(c).at[ctx_row].set(c_ctx)
    rope_tabs = _rope_tables(s)
    wf = _channel_dft_matrix()
    tm = 512
    xc = ctx
    for layer in range(DEPTH):
        last = layer == DEPTH - 1
        lam_init = 0.8 - 0.6 * math.exp(-0.3 * layer)
        mods = _ada(cvec, ada_w, ada_b, layer).reshape(8, N_MOD, d)
        w_in_bf = w_in[layer].astype(BF16)
        wg_bf = w_branch_gate[layer].astype(BF16)
        wb_bf = w_branch[layer].astype(BF16)
        wo_bf = w_out[layer].astype(BF16)
        wr_bf = router_w[layer].T.astype(BF16)
        lp = diff_lambda[layer]
        sub_g = diff_subln_g[layer]

        kv0 = (jnp.zeros((b, CB, s + l_ctx), BF16), jnp.zeros((b, DIFF_HEADS, s + l_ctx, LANES), BF16))
        p, ur, ui, kt, vh = _inproj(x, norm1_g[layer], mods, w_in_bf, wf, rope_tabs, None, True, tm,
                                    0, kv0)
        ctx_tabs = tuple(t[:l_ctx] for t in rope_tabs)
        pc, ucr, uci, kt, vh = _inproj(xc, norm1_g[layer], mods, w_in_bf, wf, ctx_tabs, ctx_row, False,
                                       l_ctx, s, (kt, vh))

        ya = _fourier_latent(ur, ui)
        yb = _na_latent(p, pc, _na_bias_tables(na_rel_bias[layer], rows))
        yd = _diff_latent(p, kt, vh, lp, sub_g, lam_init, 1024, _key_tile(s + l_ctx, 768))
        x, h2 = _merge(x, mods, None, norm1_g[layer], norm2_g[layer], ya, yb, yd, p,
                       conv_w[layer], wg_bf, wb_bf, wo_bf, tm)
        h2_all = h2.reshape(b * s, d)
        if not last:
            yac, ybc, ydc = _ctx_branches(pc, ucr, uci, lp, sub_g, lam_init)
            xc, h2c = _merge(xc, mods, ctx_row, norm1_g[layer], norm2_g[layer], yac, ybc, ydc, pc,
                             conv_w[layer], wg_bf, wb_bf, wo_bf, l_ctx)
            h2_all = jnp.concatenate([h2_all, h2c.reshape(b * l_ctx, d)], axis=0)

        y_tok, w_tok = _moe_routed(h2_all, wr_bf, router_bias[layer], layer, expert_w_gate,
                                   expert_w_up, expert_w_down)
        sg_bf = shared_w_gate[layer].astype(BF16)
        su_bf = shared_w_up[layer].astype(BF16)
        sd_bf = shared_w_down[layer].astype(BF16)
        x = _resid(x.reshape(b * s, d), h2_all, y_tok, w_tok, 0, mods, None, s, sg_bf, su_bf, sd_bf,
                   final_norm_g, last, tm).reshape(b, s, d)
        if not last:
            xc = _resid(xc.reshape(b * l_ctx, d), h2_all, y_tok, w_tok, b * s, mods, ctx_row, l_ctx,
                        sg_bf, su_bf, sd_bf, final_norm_g, False, l_ctx).reshape(b, l_ctx, d)
    return x
```

```python
import functools
import math

import numpy as np
import jax
import jax.numpy as jnp
from jax import lax
from jax.experimental import pallas as pl
from jax.experimental.pallas import tpu as pltpu

F32 = jnp.float32
BF16 = jnp.bfloat16

DEPTH = 2
GRID_W = 64
EPS = 1e-6
N_MOD = 6

FNET_GROUP_DIM = 64
NA_HEADS = 4
NA_HEAD_DIM = 64
NA_WIN_ROWS = 8
NA_WIN_COLS = 16
CONV_WIDTH = 3
DIFF_HEADS = 4
DIFF_QK_DIM = 32
DIFF_V_DIM = 64
ROPE_BASE = 10000.0
N_BRANCHES = 4
BRANCH_DIM = 256

COL_A, COL_BQ, COL_BK, COL_BV, COL_CB, COL_CC, COL_CX, COL_DQ, COL_DK, COL_DV = range(10)
N_COL_BLOCKS = 10
CB = 256
IN_DIM = N_COL_BLOCKS * CB

N_EXPERTS = 256
TOP_K = 8
N_GROUPS = 8
TOPK_GROUPS = 4
ROUTED_SCALE = 2.5
LOG2_E = 1.4426950408889634

VMEM_LIMIT_BYTES = 56 * 1024 * 1024
LANES = 128
BF16_SUBLANES = 16
FFT_N1 = 64
NA_QROWS = 8
NA_KROWS = 16
MOE_ROWS = 256
MOE_ALIGN = BF16_SUBLANES
MOE_X_SLOTS = 4
MOE_W_SLOTS = 3


def _params(*sem):
    return pltpu.CompilerParams(dimension_semantics=sem, vmem_limit_bytes=VMEM_LIMIT_BYTES)


def _dot(a, b):
    return jnp.dot(a, b, preferred_element_type=F32)


def _dot_nt(a, b):
    return lax.dot_general(a, b, (((1,), (1,)), ((), ())), preferred_element_type=F32)


def _norm_mod(xf, g, shift, scale):
    y = xf * lax.rsqrt(jnp.mean(xf * xf, axis=-1, keepdims=True) + EPS)
    return (y * g) * (1.0 + scale) + shift


def _silu(v):
    return v * jax.nn.sigmoid(v)


def _ada_kernel(c_ref, w_ref, b_ref, o_ref):
    s = _silu(c_ref[...])
    o_ref[...] = _dot(s.astype(BF16), w_ref[0].astype(BF16)) + b_ref[0]


def _ada(cvec, w, b, layer):
    rows, d = cvec.shape
    depth, _, n = w.shape
    tn = 1536
    return pl.pallas_call(
        _ada_kernel,
        grid=(n // tn,),
        in_specs=[pl.BlockSpec((rows, d), lambda j: (0, 0)),
                  pl.BlockSpec((1, d, tn), lambda j: (layer, 0, j)),
                  pl.BlockSpec((1, 1, tn), lambda j: (layer, 0, j))],
        out_specs=pl.BlockSpec((rows, tn), lambda j: (0, j)),
        out_shape=jax.ShapeDtypeStruct((rows, n), F32),
        compiler_params=_params("arbitrary"),
    )(cvec, w, b.reshape(depth, 1, n))


def _inproj_kernel(x_ref, g_ref, mod_ref, w_ref, wf_ref, cos_ref, s1_ref, s2_ref, kt_prev, vh_prev,
                   p_ref, ur_ref, ui_ref, kt_ref, vh_ref, *, rope):
    del kt_prev, vh_prev
    mod = mod_ref[0]
    h = _norm_mod(x_ref[0], g_ref[...], mod[0:1], mod[1:2]).astype(BF16)
    for j in range(N_COL_BLOCKS):
        pj = _dot(h, w_ref[:, j * CB:(j + 1) * CB])
        if j == COL_A:
            u = _dot(pj.astype(BF16), wf_ref[...])
            ur_ref[0] = u[:, :CB].astype(BF16)
            ui_ref[0] = u[:, CB:].astype(BF16)
        if rope and j in (COL_DQ, COL_DK):
            cos = jnp.concatenate([cos_ref[...]] * 2, axis=1)
            s1 = jnp.concatenate([s1_ref[...]] * 2, axis=1)
            s2 = jnp.concatenate([s2_ref[...]] * 2, axis=1)
            pj = pj * cos + pltpu.roll(pj, CB - 8, 1) * s1 + pltpu.roll(pj, 8, 1) * s2
            if j == COL_DQ:
                pj = pj * (DIFF_QK_DIM ** -0.5 * LOG2_E)
        p_ref[0, :, j * CB:(j + 1) * CB] = pj.astype(BF16)
        if j == COL_DK:
            kt_ref[0] = pj.T.astype(BF16)
        if j == COL_DV:
            ones = jnp.ones((pj.shape[0], LANES - DIFF_V_DIM), F32)
            for hd in range(DIFF_HEADS):
                v = pj[:, hd * DIFF_V_DIM:(hd + 1) * DIFF_V_DIM]
                vh_ref[0, hd] = jnp.concatenate([v, ones], axis=1).astype(BF16)


def _inproj(x, g, mods, w_bf, wf, rope_tabs, mod_row, rope, tm, key_off, kv_prev):
    b, s, d = x.shape
    n_keys = kv_prev[0].shape[2]
    off = key_off // tm
    mod_idx = (lambda bi, i: (bi, 0, 0)) if mod_row is None else (lambda bi, i: (mod_row, 0, 0))
    tab_spec = pl.BlockSpec((tm, 128), lambda bi, i: (i, 0))
    seq_spec = lambda width: pl.BlockSpec((1, tm, width), lambda bi, i: (bi, i, 0))
    in_specs = [seq_spec(d),
                pl.BlockSpec((1, d), lambda bi, i: (0, 0)),
                pl.BlockSpec((1, N_MOD, d), mod_idx),
                pl.BlockSpec((d, IN_DIM), lambda bi, i: (0, 0)),
                pl.BlockSpec((CB, 2 * CB), lambda bi, i: (0, 0)),
                tab_spec, tab_spec, tab_spec]
    args = [x, g.reshape(1, d), mods, w_bf, wf, *rope_tabs]
    aliases = {len(args): 3, len(args) + 1: 4}
    in_specs += [pl.BlockSpec(memory_space=pl.ANY)] * 2
    args += list(kv_prev)
    return pl.pallas_call(
        functools.partial(_inproj_kernel, rope=rope),
        grid=(b, s // tm),
        in_specs=in_specs,
        out_specs=[seq_spec(IN_DIM), seq_spec(CB), seq_spec(CB),
                   pl.BlockSpec((1, CB, tm), lambda bi, i: (bi, 0, i + off)),
                   pl.BlockSpec((1, DIFF_HEADS, tm, LANES), lambda bi, i: (bi, 0, i + off, 0))],
        out_shape=[jax.ShapeDtypeStruct((b, s, IN_DIM), BF16),
                   jax.ShapeDtypeStruct((b, s, CB), BF16),
                   jax.ShapeDtypeStruct((b, s, CB), BF16),
                   jax.ShapeDtypeStruct((b, CB, n_keys), BF16),
                   jax.ShapeDtypeStruct((b, DIFF_HEADS, n_keys, LANES), BF16)],
        input_output_aliases=aliases,
        compiler_params=_params("parallel", "arbitrary"),
    )(*args)


def _channel_dft_matrix():
    c = np.arange(FNET_GROUP_DIM)
    ang = 2.0 * np.pi * ((c[:, None] * c[None, :]) % FNET_GROUP_DIM) / FNET_GROUP_DIM
    eye = np.eye(CB // FNET_GROUP_DIM)
    m = np.concatenate([np.kron(eye, np.cos(ang)), -np.kron(eye, np.sin(ang))], axis=1)
    return jnp.asarray(m, BF16)


def _rope_tables(s):
    half = DIFF_QK_DIM // 2
    t = jnp.arange(s)
    inv = 1.0 / (ROPE_BASE ** (jnp.arange(0, half, 2, dtype=F32) / half))
    ang_r = (t // GRID_W).astype(F32)[:, None] * inv
    ang_c = (t % GRID_W).astype(F32)[:, None] * inv
    zero = jnp.zeros_like(ang_r)
    cos = jnp.concatenate([jnp.cos(ang_r)] * 2 + [jnp.cos(ang_c)] * 2, axis=1)
    s1 = jnp.concatenate([-jnp.sin(ang_r), zero, -jnp.sin(ang_c), zero], axis=1)
    s2 = jnp.concatenate([zero, jnp.sin(ang_r), zero, jnp.sin(ang_c)], axis=1)
    return tuple(jnp.concatenate([a] * 4, axis=1) for a in (cos, s1, s2))


def _fft1_kernel(ur_ref, ui_ref, w_ref, ct_ref, st_ref, ar_ref, ai_ref):
    n1 = ur_ref.shape[1]
    u = jnp.concatenate([ur_ref[0], ui_ref[0]], axis=0)
    a = _dot(w_ref[...], u)
    ar, ai = a[:n1], a[n1:]
    ct, st = ct_ref[...], st_ref[...]
    ar_ref[0] = (ar * ct + ai * st).astype(BF16)
    ai_ref[0] = (ai * ct - ar * st).astype(BF16)


def _fft2_kernel(ar_ref, ai_ref, w_ref, y_ref, *, norm):
    for j in range(ar_ref.shape[1]):
        a = jnp.concatenate([ar_ref[0, j], ai_ref[0, j]], axis=0)
        y_ref[0, j] = (_dot(w_ref[...], a) * norm).astype(BF16)


def _dft_cos_sin(n):
    k = np.arange(n)
    ang = 2.0 * np.pi * ((k[:, None] * k[None, :]) % n) / n
    return np.cos(ang), np.sin(ang)


def _fourier_latent(ur, ui):
    b, s, cb = ur.shape
    n1, n2 = FFT_N1, s // FFT_N1
    c1, s1 = _dft_cos_sin(n1)
    w1 = jnp.asarray(np.block([[c1, s1], [-s1, c1]]), BF16)
    c2, s2 = _dft_cos_sin(n2)
    w2 = jnp.asarray(np.concatenate([c2, s2], axis=1), BF16)
    tw = 2.0 * np.pi * (np.arange(n1)[:, None] * np.arange(n2)[None, :]) / s
    ct = jnp.broadcast_to(jnp.asarray(np.cos(tw), F32)[:, :, None], (n1, n2, cb)).reshape(n1, n2 * cb)
    st = jnp.broadcast_to(jnp.asarray(np.sin(tw), F32)[:, :, None], (n1, n2, cb)).reshape(n1, n2 * cb)
    lanes = n2 * cb
    tn = min(lanes, 4096)
    u_spec = pl.BlockSpec((1, n1, tn), lambda j, bi: (bi, 0, j))
    t_spec = pl.BlockSpec((n1, tn), lambda j, bi: (0, j))
    ar, ai = pl.pallas_call(
        _fft1_kernel,
        grid=(lanes // tn, b),
        in_specs=[u_spec, u_spec, pl.BlockSpec((2 * n1, 2 * n1), lambda j, bi: (0, 0)), t_spec, t_spec],
        out_specs=[u_spec, u_spec],
        out_shape=[jax.ShapeDtypeStruct((b, n1, lanes), BF16)] * 2,
        compiler_params=_params("arbitrary", "arbitrary"),
    )(ur.reshape(b, n1, lanes), ui.reshape(b, n1, lanes), w1, ct, st)
    kc = 8
    a_spec = pl.BlockSpec((1, kc, n2, cb), lambda bi, j: (bi, j, 0, 0))
    y = pl.pallas_call(
        functools.partial(_fft2_kernel, norm=1.0 / math.sqrt(s * FNET_GROUP_DIM)),
        grid=(b, n1 // kc),
        in_specs=[a_spec, a_spec, pl.BlockSpec((n2, 2 * n2), lambda bi, j: (0, 0))],
        out_specs=a_spec,
        out_shape=jax.ShapeDtypeStruct((b, n1, n2, cb), BF16),
        compiler_params=_params("parallel", "arbitrary"),
    )(ar.reshape(b, n1, n2, cb), ai.reshape(b, n1, n2, cb), w2)
    return jnp.transpose(y, (0, 2, 1, 3)).reshape(b, s, cb)


def _na_kernel(q_ref, k_ref, v_ref, kc_ref, vc_ref, bias_ref, o_ref, *, rows):
    rb = pl.program_id(1)
    kb = jnp.clip(rb * NA_QROWS - NA_WIN_ROWS // 2, 0, rows - NA_KROWS)
    nk = NA_KROWS * GRID_W
    tok0 = pl.multiple_of(kb * GRID_W, 256)
    scale = NA_HEAD_DIM ** -0.5
    outs = []
    for h in range(NA_HEADS):
        sl = slice(h * NA_HEAD_DIM, (h + 1) * NA_HEAD_DIM)
        q = q_ref[0, :, sl]
        s = _dot_nt(q, k_ref[0, pl.ds(tok0, nk), sl]) * scale + bias_ref[0, h]
        sc = _dot_nt(q, kc_ref[0, :, sl]) * scale
        m = jnp.maximum(jnp.max(s, axis=-1, keepdims=True), jnp.max(sc, axis=-1, keepdims=True))
        e = jnp.exp(s - m)
        ec = jnp.exp(sc - m)
        l = jnp.sum(e, axis=-1, keepdims=True) + jnp.sum(ec, axis=-1, keepdims=True)
        o = _dot(e.astype(BF16), v_ref[0, pl.ds(tok0, nk), sl]) + _dot(ec.astype(BF16), vc_ref[0, :, sl])
        outs.append(o / l)
    o_ref[0] = jnp.concatenate(outs, axis=1).astype(BF16)


def _na_bias_tables(rel_bias, rows):
    tabs = []
    cq = np.arange(GRID_W)
    col_lo = np.clip(cq - NA_WIN_COLS // 2, 0, GRID_W - NA_WIN_COLS)
    col_ok = (cq[None, :] >= col_lo[:, None]) & (cq[None, :] < col_lo[:, None] + NA_WIN_COLS)
    dc_idx = np.clip(cq[None, :] - cq[:, None] + NA_WIN_COLS - 1, 0, 2 * NA_WIN_COLS - 2)
    for r0 in (0, NA_QROWS, rows - NA_QROWS):
        kb = int(np.clip(r0 - NA_WIN_ROWS // 2, 0, rows - NA_KROWS))
        r = r0 + np.arange(NA_QROWS)
        rk = kb + np.arange(NA_KROWS)
        start = np.clip(r - NA_WIN_ROWS // 2, 0, rows - NA_WIN_ROWS)
        row_ok = (rk[None, :] >= start[:, None]) & (rk[None, :] < start[:, None] + NA_WIN_ROWS)
        dr_idx = np.clip(rk[None, :] - r[:, None] + NA_WIN_ROWS - 1, 0, 2 * NA_WIN_ROWS - 2)
        ok = row_ok[:, None, :, None] & col_ok[None, :, None, :]
        oh_r = jnp.asarray(dr_idx[:, :, None] == np.arange(2 * NA_WIN_ROWS - 1), F32)
        oh_c = jnp.asarray(dc_idx[:, :, None] == np.arange(2 * NA_WIN_COLS - 1), F32)
        by_row = jnp.einsum('qka,hab->hqkb', oh_r, rel_bias.astype(F32), precision=lax.Precision.HIGHEST)
        vals = jnp.einsum('hqkb,cdb->hqckd', by_row, oh_c, precision=lax.Precision.HIGHEST)
        tab = jnp.where(ok[None], vals, -jnp.inf)
        tabs.append(tab.reshape(NA_HEADS, NA_QROWS * GRID_W, NA_KROWS * GRID_W))
    return jnp.stack(tabs)


def _na_latent(p, pc, bias_tabs):
    b, s, _ = p.shape
    l = pc.shape[1]
    rows = s // GRID_W
    nrb = rows // NA_QROWS
    tq = NA_QROWS * GRID_W
    nk = NA_KROWS * GRID_W
    return pl.pallas_call(
        functools.partial(_na_kernel, rows=rows),
        grid=(b, nrb),
        in_specs=[pl.BlockSpec((1, tq, CB), lambda bi, i: (bi, i, COL_BQ)),
                  pl.BlockSpec((1, s, CB), lambda bi, i: (bi, 0, COL_BK)),
                  pl.BlockSpec((1, s, CB), lambda bi, i: (bi, 0, COL_BV)),
                  pl.BlockSpec((1, l, CB), lambda bi, i: (bi, 0, COL_BK)),
                  pl.BlockSpec((1, l, CB), lambda bi, i: (bi, 0, COL_BV)),
                  pl.BlockSpec((1, NA_HEADS, tq, nk),
                               lambda bi, i: (jnp.minimum(i, 1) + (i == nrb - 1).astype(jnp.int32), 0, 0, 0))],
        out_specs=pl.BlockSpec((1, tq, CB), lambda bi, i: (bi, i, 0)),
        out_shape=jax.ShapeDtypeStruct((b, s, CB), BF16),
        compiler_params=_params("parallel", "arbitrary"),
    )(p, p, p, pc, pc, bias_tabs)


def _diff_lambda(lp, lam_init):
    return (jnp.exp(jnp.sum(lp[0:1] * lp[1:2], axis=-1, keepdims=True))
            - jnp.exp(jnp.sum(lp[2:3] * lp[3:4], axis=-1, keepdims=True)) + lam_init)


def _diff_finish(o, g, lam_init):
    y = o * lax.rsqrt(jnp.mean(o * o, axis=-1, keepdims=True) + EPS)
    return y * g * (1.0 - lam_init)


def _diff_kernel(q_ref, kt_ref, v_ref, lp_ref, g_ref, o_ref, m_sc, acc_sc, e_sc, *, tk, lam_init):
    tq = q_ref.shape[1]
    nk = kt_ref.shape[2] // tk
    m_sc[...] = jnp.full(m_sc.shape, -jnp.inf, F32)
    acc_sc[...] = jnp.zeros(acc_sc.shape, F32)

    def body(c, carry):
        k0 = pl.multiple_of(c * tk, LANES)
        for h in range(DIFF_HEADS):
            for m in range(2):
                rows = slice(m * tq, (m + 1) * tq)
                dims = slice((2 * h + m) * DIFF_QK_DIM, (2 * h + m + 1) * DIFF_QK_DIM)
                s = _dot(q_ref[0, :, dims], kt_ref[0, dims, pl.ds(k0, tk)])
                m_old = m_sc[h, rows]
                m_new = jnp.maximum(m_old, jnp.max(s, axis=-1, keepdims=True))
                e_sc[h, rows] = jnp.exp2(s - m_new[:, :1]).astype(BF16)
                acc_sc[h, rows] = jnp.exp2(m_old - m_new) * acc_sc[h, rows]
                m_sc[h, rows] = m_new
            acc_sc[h] += _dot(e_sc[h], v_ref[0, h, pl.ds(k0, tk), :])
        return carry

    lax.fori_loop(0, nk, body, 0)
    lam = _diff_lambda(lp_ref[...], lam_init)
    outs = []
    for h in range(DIFF_HEADS):
        acc = acc_sc[h]
        o = acc[:, :DIFF_V_DIM] / acc[:, DIFF_V_DIM:DIFF_V_DIM + 1]
        outs.append(_diff_finish(o[:tq] - lam * o[tq:], g_ref[...], lam_init))
    o_ref[0] = jnp.concatenate(outs, axis=1).astype(BF16)


def _key_tile(nkeys, cap):
    return max(t for t in range(128, cap + 1, 128) if nkeys % t == 0)


def _diff_latent(p, kt, vh, lp, sub_g, lam_init, tq, tk):
    b, s, _ = p.shape
    nkeys = kt.shape[2]
    return pl.pallas_call(
        functools.partial(_diff_kernel, tk=tk, lam_init=lam_init),
        grid=(b, s // tq),
        in_specs=[pl.BlockSpec((1, tq, CB), lambda bi, i: (bi, i, COL_DQ)),
                  pl.BlockSpec((1, CB, nkeys), lambda bi, i: (bi, 0, 0), pipeline_mode=pl.Buffered(1)),
                  pl.BlockSpec((1, DIFF_HEADS, nkeys, LANES), lambda bi, i: (bi, 0, 0, 0),
                               pipeline_mode=pl.Buffered(1)),
                  pl.BlockSpec((4, DIFF_QK_DIM), lambda bi, i: (0, 0)),
                  pl.BlockSpec((1, DIFF_V_DIM), lambda bi, i: (0, 0))],
        out_specs=pl.BlockSpec((1, tq, CB), lambda bi, i: (bi, i, 0)),
        out_shape=jax.ShapeDtypeStruct((b, s, CB), BF16),
        scratch_shapes=[pltpu.VMEM((DIFF_HEADS, 2 * tq, LANES), F32),
                        pltpu.VMEM((DIFF_HEADS, 2 * tq, LANES), F32),
                        pltpu.VMEM((DIFF_HEADS, 2 * tq, tk), BF16)],
        compiler_params=_params("parallel", "arbitrary"),
    )(p, kt, vh, lp, sub_g.reshape(1, DIFF_V_DIM))


def _softmax_rows(s):
    e = jnp.exp(s - jnp.max(s, axis=-1, keepdims=True))
    return e / jnp.sum(e, axis=-1, keepdims=True)


def _ctx_kernel(pc_ref, ur_ref, ui_ref, wf_ref, lp_ref, g_ref, ya_ref, yb_ref, yd_ref, *, lam_init):
    l = pc_ref.shape[1]
    col = lambda j, lo, hi: pc_ref[0, :, j * CB + lo:j * CB + hi]
    u = jnp.concatenate([ur_ref[0], ui_ref[0]], axis=0)
    ya_ref[0] = (_dot(wf_ref[...], u) * (1.0 / math.sqrt(l * FNET_GROUP_DIM))).astype(BF16)
    outs = []
    for h in range(NA_HEADS):
        lo, hi = h * NA_HEAD_DIM, (h + 1) * NA_HEAD_DIM
        pr = _softmax_rows(_dot_nt(col(COL_BQ, lo, hi), col(COL_BK, lo, hi)) * NA_HEAD_DIM ** -0.5)
        outs.append(_dot(pr.astype(BF16), col(COL_BV, lo, hi)))
    yb_ref[0] = jnp.concatenate(outs, axis=1).astype(BF16)
    lam = _diff_lambda(lp_ref[...], lam_init)
    outs = []
    for h in range(DIFF_HEADS):
        pm = []
        for m in range(2):
            lo = (2 * h + m) * DIFF_QK_DIM
            pm.append(_softmax_rows(_dot_nt(col(COL_DQ, lo, lo + DIFF_QK_DIM), col(COL_DK, lo, lo + DIFF_QK_DIM))
                                    * DIFF_QK_DIM ** -0.5))
        a = (pm[0] - lam * pm[1]).astype(BF16)
        o = _dot(a, col(COL_DV, h * DIFF_V_DIM, (h + 1) * DIFF_V_DIM))
        outs.append(_diff_finish(o, g_ref[...], lam_init))
    yd_ref[0] = jnp.concatenate(outs, axis=1).astype(BF16)


def _ctx_branches(pc, ucr, uci, lp, sub_g, lam_init):
    b, l, _ = pc.shape
    c, s = _dft_cos_sin(l)
    wf = jnp.asarray(np.concatenate([c, s], axis=1), BF16)
    y_spec = pl.BlockSpec((1, l, CB), lambda bi: (bi, 0, 0))
    return pl.pallas_call(
        functools.partial(_ctx_kernel, lam_init=lam_init),
        grid=(b,),
        in_specs=[pl.BlockSpec((1, l, IN_DIM), lambda bi: (bi, 0, 0)), y_spec, y_spec,
                  pl.BlockSpec((l, 2 * l), lambda bi: (0, 0)),
                  pl.BlockSpec((4, DIFF_QK_DIM), lambda bi: (0, 0)),
                  pl.BlockSpec((1, DIFF_V_DIM), lambda bi: (0, 0))],
        out_specs=[y_spec] * 3,
        out_shape=[jax.ShapeDtypeStruct((b, l, CB), BF16)] * 3,
        compiler_params=_params("parallel"),
    )(pc, ucr, uci, wf, lp, sub_g.reshape(1, DIFF_V_DIM))


def _merge_kernel(x_ref, mod_ref, g1_ref, g2_ref, ya_ref, yb_ref, yd_ref, pb_ref, pc_ref, px_ref,
                  cp_ref, xp_ref, cn_ref, xn_ref, cw_ref, wg_ref, wb_ref, wo_ref, h2_prev,
                  xo_ref, h2_ref):
    del h2_prev
    i = pl.program_id(1)
    last = pl.num_programs(1) - 1
    tm = x_ref.shape[1]
    mod = mod_ref[0]
    x = x_ref[0]
    h = _norm_mod(x, g1_ref[...], mod[0:1], mod[1:2]).astype(BF16)

    u = pc_ref[0].astype(F32) * px_ref[0].astype(F32)
    up = cp_ref[0, BF16_SUBLANES - 1:, :].astype(F32) * xp_ref[0, BF16_SUBLANES - 1:, :].astype(F32)
    un = cn_ref[0, :1, :].astype(F32) * xn_ref[0, :1, :].astype(F32)
    up = jnp.where(i == 0, 0.0, up)
    un = jnp.where(i == last, 0.0, un)
    rid = lax.broadcasted_iota(jnp.int32, u.shape, 0)
    u_prev = jnp.where(rid == 0, up, pltpu.roll(u, 1, 0))
    u_next = jnp.where(rid == tm - 1, un, pltpu.roll(u, tm - 1, 0))
    cw = cw_ref[...]
    yc = pb_ref[0].astype(F32) * (cw[0:1] * u_prev + cw[1:2] * u + cw[2:3] * u_next)

    branches = (ya_ref[0], yb_ref[0], yc.astype(BF16), yd_ref[0])
    d = x.shape[1]
    out = None
    for n in range(d // CB):
        cols = slice(n * CB, (n + 1) * CB)
        merged = None
        for j in range(N_BRANCHES):
            t = jax.nn.sigmoid(_dot(h, wg_ref[j, :, cols])) * _dot(branches[j], wb_ref[j, :, cols])
            merged = t if merged is None else merged + t
        t = _dot(merged.astype(BF16), wo_ref[cols, :])
        out = t if out is None else out + t
    xn = x + mod[2:3] * out
    xo_ref[0] = xn
    h2 = _norm_mod(xn, g2_ref[...], mod[3:4], mod[4:5]).astype(BF16)
    h2_ref[0] = h2
    h2_ref[1] = h2


def _merge(x, mods, mod_row, g1, g2, ya, yb, yd, p, conv_w, wg, wb, wo, tm, h2_buf, tok_off):
    b, s, d = x.shape
    off = tok_off // tm
    per = s // tm
    hb = tm // BF16_SUBLANES
    n_halo = s // BF16_SUBLANES
    mod_idx = (lambda bi, i: (bi, 0, 0)) if mod_row is None else (lambda bi, i: (mod_row, 0, 0))
    seq = lambda width, col=0: pl.BlockSpec((1, tm, width), lambda bi, i: (bi, i, col))
    prev = lambda col: pl.BlockSpec((1, BF16_SUBLANES, CB), lambda bi, i: (bi, jnp.maximum(i * hb - 1, 0), col))
    nxt = lambda col: pl.BlockSpec((1, BF16_SUBLANES, CB),
                                   lambda bi, i: (bi, jnp.minimum((i + 1) * hb, n_halo - 1), col))
    const = lambda shape: pl.BlockSpec(shape, lambda bi, i: (0,) * len(shape))
    return pl.pallas_call(
        _merge_kernel,
        grid=(b, s // tm),
        in_specs=[seq(d), pl.BlockSpec((1, N_MOD, d), mod_idx), const((1, d)), const((1, d)),
                  seq(CB), seq(CB), seq(CB),
                  seq(CB, COL_CB), seq(CB, COL_CC), seq(CB, COL_CX),
                  prev(COL_CC), prev(COL_CX), nxt(COL_CC), nxt(COL_CX),
                  const((CONV_WIDTH, CB)),
                  const((N_BRANCHES, d, d)), const((N_BRANCHES, BRANCH_DIM, d)), const((d, d)),
                  pl.BlockSpec(memory_space=pl.ANY)],
        out_specs=[seq(d), pl.BlockSpec((2, tm, d), lambda bi, i: (0, off + bi * per + i, 0))],
        out_shape=[jax.ShapeDtypeStruct((b, s, d), F32),
                   jax.ShapeDtypeStruct(h2_buf.shape, BF16)],
        input_output_aliases={18: 1},
        compiler_params=_params("parallel", "arbitrary"),
    )(x, mods, g1.reshape(1, d), g2.reshape(1, d), ya, yb, yd, p, p, p, p, p, p, p,
      conv_w, wg, wb, wo, h2_buf)


def _route_kernel(h2_ref, wr_ref, bias_ref, tri_ref, ones_ref, idx_ref, w_ref, rank_ref, cnt_ref,
                  score_sc, sel_sc, carry_sc):
    i = pl.program_id(0)
    tm = h2_ref.shape[0]
    ne = wr_ref.shape[0]
    gsz = ne // N_GROUPS
    n_chunks = tm // LANES

    @pl.when(i == 0)
    def _():
        carry_sc[...] = jnp.zeros(carry_sc.shape, F32)

    score_sc[...] = jax.nn.sigmoid(_dot_nt(wr_ref[...], h2_ref[...]))

    def select(cidx, carry):
        c0 = pl.multiple_of(cidx * LANES, LANES)
        scores = score_sc[:, pl.ds(c0, LANES)]
        biased = scores + bias_ref[...]
        liota = lax.broadcasted_iota(jnp.int32, (gsz, LANES), 0)
        gs = []
        for g in range(N_GROUPS):
            v = biased[g * gsz:(g + 1) * gsz]
            m1 = jnp.max(v, axis=0, keepdims=True)
            i1 = jnp.min(jnp.where(v == m1, liota, gsz), axis=0, keepdims=True)
            m2 = jnp.max(jnp.where(liota == i1, -jnp.inf, v), axis=0, keepdims=True)
            gs.append(m1 + m2)
        gsm = jnp.concatenate(gs, axis=0)
        giota = lax.broadcasted_iota(jnp.int32, gsm.shape, 0)
        keep = jnp.zeros(gsm.shape, F32)
        for _ in range(TOPK_GROUPS):
            m = jnp.max(gsm, axis=0, keepdims=True)
            gi = jnp.min(jnp.where(gsm == m, giota, N_GROUPS), axis=0, keepdims=True)
            hit = giota == gi
            keep = jnp.where(hit, 1.0, keep)
            gsm = jnp.where(hit, -jnp.inf, gsm)
        cur = jnp.concatenate(
            [jnp.where(jnp.broadcast_to(keep[g:g + 1], (gsz, LANES)) > 0.0, biased[g * gsz:(g + 1) * gsz], -jnp.inf)
             for g in range(N_GROUPS)], axis=0)
        eiota = lax.broadcasted_iota(jnp.int32, (ne, LANES), 0)
        sel = jnp.zeros((ne, LANES), F32)
        idxs, ws = [], []
        for _ in range(TOP_K):
            m = jnp.max(cur, axis=0, keepdims=True)
            ik = jnp.min(jnp.where(cur == m, eiota, ne), axis=0, keepdims=True)
            hit = eiota == ik
            cur = jnp.where(hit, -jnp.inf, cur)
            ws.append(jnp.sum(jnp.where(hit, scores, 0.0), axis=0, keepdims=True))
            idxs.append(ik)
            sel = jnp.where(hit, 1.0, sel)
        w = jnp.concatenate(ws, axis=0)
        w_ref[:, pl.ds(c0, LANES)] = w / jnp.sum(w, axis=0, keepdims=True) * ROUTED_SCALE
        idx_ref[:, pl.ds(c0, LANES)] = jnp.concatenate(idxs, axis=0)
        sel_sc[:, pl.ds(c0, LANES)] = sel.astype(BF16)
        return carry

    lax.fori_loop(0, n_chunks, select, 0)

    sel_all = sel_sc[...]
    score_sc[...] = _dot(sel_all, tri_ref[...]) + jnp.concatenate([carry_sc[...]] * n_chunks, axis=1)

    def ranks(cidx, carry):
        c0 = pl.multiple_of(cidx * LANES, LANES)
        before = score_sc[:, pl.ds(c0, LANES)]
        idx = idx_ref[:, pl.ds(c0, LANES)]
        eiota = lax.broadcasted_iota(jnp.int32, (ne, LANES), 0)
        rows = [jnp.sum(jnp.where(eiota == idx[k:k + 1], before, 0.0), axis=0, keepdims=True)
                for k in range(TOP_K)]
        rank_ref[:, pl.ds(c0, LANES)] = jnp.concatenate(rows, axis=0).astype(jnp.int32)
        return carry

    lax.fori_loop(0, n_chunks, ranks, 0)
    carry_sc[...] += _dot(sel_all, ones_ref[...])
    cnt_ref[...] = carry_sc[...]


def _route(h2_all, n, wr_t, bias, tm):
    d = h2_all.shape[1]
    ne = wr_t.shape[0]
    tri = jnp.asarray(np.triu(np.ones((tm, tm), np.float32), 1), BF16)
    ones = jnp.ones((tm, LANES), BF16)
    bias_b = jnp.broadcast_to(bias.astype(F32)[:, None], (ne, LANES))
    const = lambda shape: pl.BlockSpec(shape, lambda i: (0,) * len(shape))
    tok = pl.BlockSpec((TOP_K, tm), lambda i: (0, i))
    return pl.pallas_call(
        _route_kernel,
        grid=(n // tm,),
        in_specs=[pl.BlockSpec((tm, d), lambda i: (i, 0)), const((ne, d)), const((ne, LANES)),
                  const((tm, tm)), const((tm, LANES))],
        out_specs=[tok, tok, tok, const((ne, LANES))],
        out_shape=[jax.ShapeDtypeStruct((TOP_K, n), jnp.int32),
                   jax.ShapeDtypeStruct((TOP_K, n), F32),
                   jax.ShapeDtypeStruct((TOP_K, n), jnp.int32),
                   jax.ShapeDtypeStruct((ne, LANES), F32)],
        scratch_shapes=[pltpu.VMEM((ne, tm), F32), pltpu.VMEM((ne, tm), BF16), pltpu.VMEM((ne, LANES), F32)],
        compiler_params=_params("arbitrary"),
    )(h2_all, wr_t, bias_b, tri, ones)


def _pos_kernel(idx_ref, rank_ref, start_ref, pos_ref):
    ne = start_ref.shape[0]
    start = start_ref[...]

    def body(cidx, carry):
        c0 = pl.multiple_of(cidx * LANES, LANES)
        idx = idx_ref[:, pl.ds(c0, LANES)]
        eiota = lax.broadcasted_iota(jnp.int32, (ne, LANES), 0)
        rows = [jnp.sum(jnp.where(eiota == idx[k:k + 1], start, 0.0), axis=0, keepdims=True)
                for k in range(TOP_K)]
        pos_ref[:, pl.ds(c0, LANES)] = jnp.concatenate(rows, axis=0).astype(jnp.int32) + rank_ref[:, pl.ds(c0, LANES)]
        return carry

    lax.fori_loop(0, idx_ref.shape[1] // LANES, body, 0)


def _positions(idx, rank, start_rows, tm):
    k, n = idx.shape
    ne = start_rows.shape[0]
    start_b = jnp.broadcast_to(start_rows.astype(F32)[:, None], (ne, LANES))
    tok = pl.BlockSpec((k, tm), lambda i: (0, i))
    return pl.pallas_call(
        _pos_kernel,
        grid=(n // tm,),
        in_specs=[tok, tok, pl.BlockSpec((ne, LANES), lambda i: (0, 0))],
        out_specs=tok,
        out_shape=jax.ShapeDtypeStruct((k, n), jnp.int32),
        compiler_params=_params("parallel"),
    )(idx, rank, start_b)


def _rowtok_kernel(start_ref, cnt_ref, pos_ref, out_ref, *, n_tokens):
    i = pl.program_id(0)
    tn = pos_ref.shape[1]
    n_rows = out_ref.shape[0]
    ne = start_ref.shape[0]

    @pl.when(i == 0)
    def _():
        def gaps(e, carry):
            lo = start_ref[e] + cnt_ref[e]
            hi = jnp.where(e + 1 < ne, start_ref[jnp.minimum(e + 1, ne - 1)], n_rows)

            def one(r, c):
                out_ref[r] = lax.rem(r, n_tokens)
                return c

            return lax.fori_loop(lo, hi, one, carry)

        lax.fori_loop(0, ne, gaps, 0)

    def body(t, carry):
        for k in range(TOP_K):
            out_ref[pos_ref[k, t]] = i * tn + t
        return carry

    lax.fori_loop(0, tn, body, 0)


def _row_tokens(pos, seg_start, counts, n_rows):
    k, n = pos.shape
    tn = max(t for t in range(LANES, 2048 + 1, LANES) if n % t == 0)
    grid_spec = pltpu.PrefetchScalarGridSpec(
        num_scalar_prefetch=2,
        grid=(n // tn,),
        in_specs=[pl.BlockSpec((k, tn), lambda i, st, ct: (0, i), memory_space=pltpu.SMEM)],
        out_specs=pl.BlockSpec((n_rows,), lambda i, st, ct: (0,), memory_space=pltpu.SMEM))
    return pl.pallas_call(
        functools.partial(_rowtok_kernel, n_tokens=n),
        grid_spec=grid_spec,
        out_shape=jax.ShapeDtypeStruct((n_rows,), jnp.int32),
        compiler_params=_params("arbitrary"),
    )(seg_start, counts, pos)


def _expert_kernel(crow_ref, crun_ref, rexp_ref, meta_ref, x_hbm, wg_hbm, wu_hbm, wd_hbm, y_hbm,
                   wg_buf, wu_buf, wd_buf, wg_bf, wu_bf, wd_bf, x_buf, y_buf, zero_buf,
                   w_sem, x_sem, y_sem, z_sem, *, layer):
    n_chunks, n_runs, tail_start = meta_ref[0], meta_ref[1], meta_ref[2]
    n_rows = x_hbm.shape[0]

    def w_copies(run):
        expert, s = rexp_ref[run], run % MOE_W_SLOTS
        return (pltpu.make_async_copy(wg_hbm.at[layer, expert], wg_buf.at[s], w_sem.at[s, 0]),
                pltpu.make_async_copy(wu_hbm.at[layer, expert], wu_buf.at[s], w_sem.at[s, 1]),
                pltpu.make_async_copy(wd_hbm.at[layer, expert], wd_buf.at[s], w_sem.at[s, 2]))

    def x_copy(c):
        rows = pl.ds(pl.multiple_of(crow_ref[c], MOE_ALIGN), MOE_ROWS)
        return pltpu.make_async_copy(x_hbm.at[rows], x_buf.at[c % MOE_X_SLOTS], x_sem.at[c % MOE_X_SLOTS])

    def y_copy(c):
        rows = pl.ds(pl.multiple_of(crow_ref[c], MOE_ALIGN), MOE_ROWS)
        return pltpu.make_async_copy(y_buf.at[c % 2], y_hbm.at[rows], y_sem.at[c % 2])

    def zero_copy(row0):
        return pltpu.make_async_copy(zero_buf, y_hbm.at[pl.ds(row0, MOE_ALIGN)], z_sem.at[0])

    for r in range(MOE_W_SLOTS - 1):
        @pl.when(r < n_runs)
        def _(r=r):
            for cp in w_copies(r):
                cp.start()

    for c in range(MOE_X_SLOTS - 1):
        @pl.when(c < n_chunks)
        def _(c=c):
            x_copy(c).start()

    def chunk(c, carry):
        run = crun_ref[c]
        first = jnp.logical_or(c == 0, crun_ref[jnp.maximum(c - 1, 0)] != run)

        @pl.when(c + MOE_X_SLOTS - 1 < n_chunks)
        def _():
            x_copy(c + MOE_X_SLOTS - 1).start()

        @pl.when(first)
        def _():
            @pl.when(run + MOE_W_SLOTS - 1 < n_runs)
            def _():
                for cp in w_copies(run + MOE_W_SLOTS - 1):
                    cp.start(priority=1)

            for cp in w_copies(run):
                cp.wait()
            s = run % MOE_W_SLOTS
            wg_bf[...] = wg_buf[s].astype(BF16)
            wu_bf[...] = wu_buf[s].astype(BF16)
            wd_bf[...] = wd_buf[s].astype(BF16)

        x_copy(c).wait()
        x = x_buf[c % MOE_X_SLOTS]
        a = (_silu(_dot(x, wg_bf[...])) * _dot(x, wu_bf[...])).astype(BF16)
        y = _dot(a, wd_bf[...]).astype(BF16)

        @pl.when(c > 0)
        def _():
            y_copy(c - 1).wait()

        y_buf[c % 2] = y
        y_copy(c).start()
        return carry

    lax.fori_loop(0, n_chunks, chunk, 0)

    @pl.when(n_chunks > 0)
    def _():
        y_copy(n_chunks - 1).wait()

    zero_buf[...] = jnp.zeros(zero_buf.shape, zero_buf.dtype)
    n_tail = (n_rows - tail_start) // MOE_ALIGN

    def fill(t, carry):
        zero_copy(pl.multiple_of(tail_start + t * MOE_ALIGN, MOE_ALIGN)).start()
        return carry

    def drain(t, carry):
        zero_copy(0).wait()
        return carry

    lax.fori_loop(0, n_tail, fill, 0)
    lax.fori_loop(0, n_tail, drain, 0)


def _experts(x_sorted, seg_start, counts, layer, w_g, w_u, w_d):
    n_rows, d = x_sorted.shape
    ne, f = w_g.shape[1], w_g.shape[3]
    nch = (counts + MOE_ROWS - 1) // MOE_ROWS
    c_end = jnp.cumsum(nch)
    max_chunks = (n_rows - MOE_ROWS) // MOE_ROWS + ne
    g = jnp.arange(max_chunks, dtype=jnp.int32)
    c_exp = jnp.minimum(jnp.sum((c_end[None, :] <= g[:, None]).astype(jnp.int32), axis=1), ne - 1)
    c_row = jnp.where(g < c_end[-1], seg_start[c_exp] + (g - (c_end - nch)[c_exp]) * MOE_ROWS, 0)
    has_rows = (nch > 0).astype(jnp.int32)
    run_end = jnp.cumsum(has_rows)
    c_run = (run_end - 1)[c_exp]
    r = jnp.arange(ne, dtype=jnp.int32)
    r_exp = jnp.minimum(jnp.sum((run_end[None, :] <= r[:, None]).astype(jnp.int32), axis=1), ne - 1)
    tail_start = jnp.max(jnp.where(nch > 0, seg_start + nch * MOE_ROWS, 0))
    meta = jnp.stack([c_end[-1], run_end[-1], tail_start]).astype(jnp.int32)
    any_spec = pl.BlockSpec(memory_space=pl.ANY)
    grid_spec = pltpu.PrefetchScalarGridSpec(
        num_scalar_prefetch=4,
        grid=(1,),
        in_specs=[any_spec] * 4,
        out_specs=any_spec,
        scratch_shapes=[pltpu.VMEM((MOE_W_SLOTS, d, f), F32), pltpu.VMEM((MOE_W_SLOTS, d, f), F32),
                        pltpu.VMEM((MOE_W_SLOTS, f, d), F32),
                        pltpu.VMEM((d, f), BF16), pltpu.VMEM((d, f), BF16), pltpu.VMEM((f, d), BF16),
                        pltpu.VMEM((MOE_X_SLOTS, MOE_ROWS, d), BF16), pltpu.VMEM((2, MOE_ROWS, d), BF16),
                        pltpu.VMEM((MOE_ALIGN, d), BF16),
                        pltpu.SemaphoreType.DMA((MOE_W_SLOTS, 3)), pltpu.SemaphoreType.DMA((MOE_X_SLOTS,)),
                        pltpu.SemaphoreType.DMA((2,)), pltpu.SemaphoreType.DMA((1,))])
    return pl.pallas_call(
        functools.partial(_expert_kernel, layer=layer),
        grid_spec=grid_spec,
        out_shape=jax.ShapeDtypeStruct((n_rows, d), BF16),
        compiler_params=_params("arbitrary"),
    )(c_row.astype(jnp.int32), c_run.astype(jnp.int32), r_exp, meta, x_sorted, w_g, w_u, w_d)


def _resid_kernel(x_ref, h2_ref, y_ref, w_ref, mod_ref, sg_ref, su_ref, sd_ref, gf_ref, o_ref, *, final):
    h2 = h2_ref[...]
    a = (_silu(_dot(h2, sg_ref[...])) * _dot(h2, su_ref[...])).astype(BF16)
    y = _dot(a, sd_ref[...])
    w = w_ref[...]
    for k in range(TOP_K):
        y = y + w[:, k:k + 1] * y_ref[k].astype(F32)
    xo = x_ref[...] + mod_ref[0][5:6] * y
    if final:
        xo = xo * lax.rsqrt(jnp.mean(xo * xo, axis=-1, keepdims=True) + EPS) * gf_ref[...]
    o_ref[...] = xo


def _resid(x_flat, h2_all, y_tok, w_tok, row_off, mods, mod_row, rows_per_mod, sg, su, sd, gf, final, tm):
    n, d = x_flat.shape
    f = sg.shape[1]
    off = row_off // tm
    per = rows_per_mod // tm
    mod_idx = (lambda i: (i // per, 0, 0)) if mod_row is None else (lambda i: (mod_row, 0, 0))
    const = lambda shape: pl.BlockSpec(shape, lambda i: (0,) * len(shape))
    return pl.pallas_call(
        functools.partial(_resid_kernel, final=final),
        grid=(n // tm,),
        in_specs=[pl.BlockSpec((tm, d), lambda i: (i, 0)),
                  pl.BlockSpec((tm, d), lambda i: (i + off, 0)),
                  pl.BlockSpec((TOP_K, tm, d), lambda i: (0, i + off, 0)),
                  pl.BlockSpec((tm, TOP_K), lambda i: (i + off, 0)),
                  pl.BlockSpec((1, N_MOD, d), mod_idx),
                  const((d, f)), const((d, f)), const((f, d)), const((1, d))],
        out_specs=pl.BlockSpec((tm, d), lambda i: (i, 0)),
        out_shape=jax.ShapeDtypeStruct((n, d), F32),
        compiler_params=_params("parallel"),
    )(x_flat, h2_all, y_tok, w_tok, mods, sg, su, sd, gf.reshape(1, d))


def _moe_routed(h2_dbl, wr_t, bias, layer, w_g, w_u, w_d):
    n, d = h2_dbl.shape[0] // 2, h2_dbl.shape[1]
    idx, wts, rank, cnt = _route(h2_dbl, n, wr_t, bias, 512)
    counts = cnt[:, 0].astype(jnp.int32)
    padded = (counts + MOE_ALIGN - 1) // MOE_ALIGN * MOE_ALIGN
    seg_start = jnp.cumsum(padded) - padded
    n_rows = -(-(n * TOP_K + N_EXPERTS * (MOE_ALIGN - 1)) // MOE_ROWS) * MOE_ROWS + MOE_ROWS
    pos2 = _positions(idx, rank, seg_start, 512)
    pos = pos2.reshape(TOP_K * n)
    row_tok = _row_tokens(pos2, seg_start, counts, n_rows)
    x_sorted = h2_dbl.at[row_tok].get(mode='promise_in_bounds')
    y_sorted = _experts(x_sorted, seg_start, counts, layer, w_g, w_u, w_d)
    return y_sorted.at[pos].get(mode='promise_in_bounds').reshape(TOP_K, n, d), wts.T


def kernel(x, c, ctx, c_ctx, ada_w, ada_b, norm1_g, w_in, conv_w, na_rel_bias, diff_lambda,
           diff_subln_g, w_branch_gate, w_branch, w_out, norm2_g, router_w, router_bias,
           expert_w_gate, expert_w_up, expert_w_down, shared_w_gate, shared_w_up, shared_w_down,
           final_norm_g):
    b, s, d = x.shape
    l_ctx = ctx.shape[1]
    rows = s // GRID_W
    ctx_row = b
    cvec = jnp.zeros((8, d), F32).at[:b].set(c).at[ctx_row].set(c_ctx)
    rope_tabs = _rope_tables(s)
    wf = _channel_dft_matrix()
    tm = 512
    xc = ctx
    for layer in range(DEPTH):
        last = layer == DEPTH - 1
        lam_init = 0.8 - 0.6 * math.exp(-0.3 * layer)
        mods = _ada(cvec, ada_w, ada_b, layer).reshape(8, N_MOD, d)
        w_in_bf = w_in[layer].astype(BF16)
        wg_bf = w_branch_gate[layer].astype(BF16)
        wb_bf = w_branch[layer].astype(BF16)
        wo_bf = w_out[layer].astype(BF16)
        wr_bf = router_w[layer].T.astype(BF16)
        lp = diff_lambda[layer]
        sub_g = diff_subln_g[layer]

        kv0 = (jnp.zeros((b, CB, s + l_ctx), BF16), jnp.zeros((b, DIFF_HEADS, s + l_ctx, LANES), BF16))
        p, ur, ui, kt, vh = _inproj(x, norm1_g[layer], mods, w_in_bf, wf, rope_tabs, None, True, tm,
                                    0, kv0)
        ctx_tabs = tuple(t[:l_ctx] for t in rope_tabs)
        pc, ucr, uci, kt, vh = _inproj(xc, norm1_g[layer], mods, w_in_bf, wf, ctx_tabs, ctx_row, False,
                                       l_ctx, s, (kt, vh))

        ya = _fourier_latent(ur, ui)
        yb = _na_latent(p, pc, _na_bias_tables(na_rel_bias[layer], rows))
        yd = _diff_latent(p, kt, vh, lp, sub_g, lam_init, 1024, _key_tile(s + l_ctx, 768))
        n_tok = b * s if last else b * (s + l_ctx)
        h2_buf = jnp.zeros((2, n_tok, d), BF16)
        x, h2_buf = _merge(x, mods, None, norm1_g[layer], norm2_g[layer], ya, yb, yd, p,
                           conv_w[layer], wg_bf, wb_bf, wo_bf, tm, h2_buf, 0)
        if not last:
            yac, ybc, ydc = _ctx_branches(pc, ucr, uci, lp, sub_g, lam_init)
            xc, h2_buf = _merge(xc, mods, ctx_row, norm1_g[layer], norm2_g[layer], yac, ybc, ydc, pc,
                                conv_w[layer], wg_bf, wb_bf, wo_bf, l_ctx, h2_buf, b * s)
        h2_all = h2_buf.reshape(2 * n_tok, d)

        y_tok, w_tok = _moe_routed(h2_all, wr_bf, router_bias[layer], layer, expert_w_gate,
                                   expert_w_up, expert_w_down)
        sg_bf = shared_w_gate[layer].astype(BF16)
        su_bf = shared_w_up[layer].astype(BF16)
        sd_bf = shared_w_down[layer].astype(BF16)
        x = _resid(x.reshape(b * s, d), h2_all, y_tok, w_tok, 0, mods, None, s, sg_bf, su_bf, sd_bf,
                   final_norm_g, last, tm).reshape(b, s, d)
        if not last:
            xc = _resid(xc.reshape(b * l_ctx, d), h2_all, y_tok, w_tok, b * s, mods, ctx_row, l_ctx,
                        sg_bf, su_bf, sd_bf, final_norm_g, False, l_ctx).reshape(b, l_ctx, d)
    return x
```

```python
import functools
import math

import numpy as np
import jax
import jax.numpy as jnp
from jax import lax
from jax.experimental import pallas as pl
from jax.experimental.pallas import tpu as pltpu

F32 = jnp.float32
BF16 = jnp.bfloat16

DEPTH = 2
GRID_W = 64
EPS = 1e-6
N_MOD = 6

FNET_GROUP_DIM = 64
NA_HEADS = 4
NA_HEAD_DIM = 64
NA_WIN_ROWS = 8
NA_WIN_COLS = 16
CONV_WIDTH = 3
DIFF_HEADS = 4
DIFF_QK_DIM = 32
DIFF_V_DIM = 64
ROPE_BASE = 10000.0
N_BRANCHES = 4
BRANCH_DIM = 256

COL_A, COL_BQ, COL_BK, COL_BV, COL_CB, COL_CC, COL_CX, COL_DQ, COL_DK, COL_DV = range(10)
N_COL_BLOCKS = 10
CB = 256
IN_DIM = N_COL_BLOCKS * CB

N_EXPERTS = 256
TOP_K = 8
N_GROUPS = 8
TOPK_GROUPS = 4
ROUTED_SCALE = 2.5
LOG2_E = 1.4426950408889634

VMEM_LIMIT_BYTES = 56 * 1024 * 1024
LANES = 128
BF16_SUBLANES = 16
FFT_N1 = 64
NA_MASKED = -1e30
NA_QROWS = 8
NA_KROWS = 16
MOE_ROWS = 256
MOE_ALIGN = BF16_SUBLANES
MOE_X_SLOTS = 4
MOE_W_SLOTS = 3


def _params(*sem):
    return pltpu.CompilerParams(dimension_semantics=sem, vmem_limit_bytes=VMEM_LIMIT_BYTES)


def _dot(a, b):
    return jnp.dot(a, b, preferred_element_type=F32)


def _dot_nt(a, b):
    return lax.dot_general(a, b, (((1,), (1,)), ((), ())), preferred_element_type=F32)


def _norm_mod(xf, g, shift, scale):
    y = xf * lax.rsqrt(jnp.mean(xf * xf, axis=-1, keepdims=True) + EPS)
    return (y * g) * (1.0 + scale) + shift


def _silu(v):
    return v * jax.nn.sigmoid(v)


def _ada_kernel(c_ref, w_ref, b_ref, o_ref):
    s = _silu(c_ref[...])
    o_ref[...] = _dot(s.astype(BF16), w_ref[0].astype(BF16)) + b_ref[0]


def _ada(cvec, w, b, layer):
    rows, d = cvec.shape
    depth, _, n = w.shape
    tn = 1536
    return pl.pallas_call(
        _ada_kernel,
        grid=(n // tn,),
        in_specs=[pl.BlockSpec((rows, d), lambda j: (0, 0)),
                  pl.BlockSpec((1, d, tn), lambda j: (layer, 0, j)),
                  pl.BlockSpec((1, 1, tn), lambda j: (layer, 0, j))],
        out_specs=pl.BlockSpec((rows, tn), lambda j: (0, j)),
        out_shape=jax.ShapeDtypeStruct((rows, n), F32),
        compiler_params=_params("arbitrary"),
    )(cvec, w, b.reshape(depth, 1, n))


def _inproj_kernel(x_ref, g_ref, mod_ref, w_ref, wf_ref, cos_ref, s1_ref, s2_ref, kt_prev, vh_prev,
                   p_ref, ur_ref, ui_ref, kt_ref, vh_ref, *, rope):
    del kt_prev, vh_prev
    mod = mod_ref[0]
    h = _norm_mod(x_ref[0], g_ref[...], mod[0:1], mod[1:2]).astype(BF16)
    for j in range(N_COL_BLOCKS):
        pj = _dot(h, w_ref[:, j * CB:(j + 1) * CB])
        if j == COL_A:
            u = _dot(pj.astype(BF16), wf_ref[...])
            ur_ref[0] = u[:, :CB].astype(BF16)
            ui_ref[0] = u[:, CB:].astype(BF16)
        if rope and j in (COL_DQ, COL_DK):
            cos = jnp.concatenate([cos_ref[...]] * 2, axis=1)
            s1 = jnp.concatenate([s1_ref[...]] * 2, axis=1)
            s2 = jnp.concatenate([s2_ref[...]] * 2, axis=1)
            pj = pj * cos + pltpu.roll(pj, CB - 8, 1) * s1 + pltpu.roll(pj, 8, 1) * s2
            if j == COL_DQ:
                pj = pj * (DIFF_QK_DIM ** -0.5 * LOG2_E)
        p_ref[0, :, j * CB:(j + 1) * CB] = pj.astype(BF16)
        if j == COL_DK:
            kt_ref[0] = pj.T.astype(BF16)
        if j == COL_DV:
            ones = jnp.ones((pj.shape[0], LANES - DIFF_V_DIM), F32)
            for hd in range(DIFF_HEADS):
                v = pj[:, hd * DIFF_V_DIM:(hd + 1) * DIFF_V_DIM]
                vh_ref[0, hd] = jnp.concatenate([v, ones], axis=1).astype(BF16)


def _inproj(x, g, mods, w_bf, wf, rope_tabs, mod_row, rope, tm, key_off, kv_prev):
    b, s, d = x.shape
    n_keys = kv_prev[0].shape[2]
    off = key_off // tm
    mod_idx = (lambda bi, i: (bi, 0, 0)) if mod_row is None else (lambda bi, i: (mod_row, 0, 0))
    tab_spec = pl.BlockSpec((tm, 128), lambda bi, i: (i, 0))
    seq_spec = lambda width: pl.BlockSpec((1, tm, width), lambda bi, i: (bi, i, 0))
    in_specs = [seq_spec(d),
                pl.BlockSpec((1, d), lambda bi, i: (0, 0)),
                pl.BlockSpec((1, N_MOD, d), mod_idx),
                pl.BlockSpec((d, IN_DIM), lambda bi, i: (0, 0)),
                pl.BlockSpec((CB, 2 * CB), lambda bi, i: (0, 0)),
                tab_spec, tab_spec, tab_spec]
    args = [x, g.reshape(1, d), mods, w_bf, wf, *rope_tabs]
    aliases = {len(args): 3, len(args) + 1: 4}
    in_specs += [pl.BlockSpec(memory_space=pl.ANY)] * 2
    args += list(kv_prev)
    return pl.pallas_call(
        functools.partial(_inproj_kernel, rope=rope),
        grid=(b, s // tm),
        in_specs=in_specs,
        out_specs=[seq_spec(IN_DIM), seq_spec(CB), seq_spec(CB),
                   pl.BlockSpec((1, CB, tm), lambda bi, i: (bi, 0, i + off)),
                   pl.BlockSpec((1, DIFF_HEADS, tm, LANES), lambda bi, i: (bi, 0, i + off, 0))],
        out_shape=[jax.ShapeDtypeStruct((b, s, IN_DIM), BF16),
                   jax.ShapeDtypeStruct((b, s, CB), BF16),
                   jax.ShapeDtypeStruct((b, s, CB), BF16),
                   jax.ShapeDtypeStruct((b, CB, n_keys), BF16),
                   jax.ShapeDtypeStruct((b, DIFF_HEADS, n_keys, LANES), BF16)],
        input_output_aliases=aliases,
        compiler_params=_params("parallel", "arbitrary"),
    )(*args)


def _channel_dft_matrix():
    c = np.arange(FNET_GROUP_DIM)
    ang = 2.0 * np.pi * ((c[:, None] * c[None, :]) % FNET_GROUP_DIM) / FNET_GROUP_DIM
    eye = np.eye(CB // FNET_GROUP_DIM)
    m = np.concatenate([np.kron(eye, np.cos(ang)), -np.kron(eye, np.sin(ang))], axis=1)
    return jnp.asarray(m, BF16)


def _rope_tables(s):
    half = DIFF_QK_DIM // 2
    inv = 1.0 / (ROPE_BASE ** (jnp.arange(0, half, 2, dtype=F32) / half))
    lane = np.arange(LANES) % DIFF_QK_DIM
    is_col = (lane // half == 1)[None, :]
    is_x2 = (lane % half // (half // 2) == 1)[None, :]
    t = jnp.arange(s)[:, None]
    pos = jnp.where(is_col, t % GRID_W, t // GRID_W).astype(F32)
    ang = pos * inv[lane % (half // 2)][None, :]
    sin = jnp.sin(ang)
    s1 = jnp.where(is_x2, 0.0, -sin)
    s2 = jnp.where(is_x2, sin, 0.0)
    return jnp.cos(ang), s1, s2


def _fft1_kernel(ur_ref, ui_ref, w_ref, ct_ref, st_ref, ar_ref, ai_ref):
    n1 = ur_ref.shape[1]
    u = jnp.concatenate([ur_ref[0], ui_ref[0]], axis=0)
    a = _dot(w_ref[...], u)
    ar, ai = a[:n1], a[n1:]
    ct, st = ct_ref[...], st_ref[...]
    ar_ref[0] = (ar * ct + ai * st).astype(BF16)
    ai_ref[0] = (ai * ct - ar * st).astype(BF16)


def _fft2_kernel(ar_ref, ai_ref, w_ref, y_ref, *, norm):
    for j in range(ar_ref.shape[1]):
        a = jnp.concatenate([ar_ref[0, j], ai_ref[0, j]], axis=0)
        y_ref[0, j] = (_dot(w_ref[...], a) * norm).astype(BF16)


def _dft_cos_sin(n):
    k = np.arange(n)
    ang = 2.0 * np.pi * ((k[:, None] * k[None, :]) % n) / n
    return np.cos(ang), np.sin(ang)


def _fourier_latent(ur, ui):
    b, s, cb = ur.shape
    n1, n2 = FFT_N1, s // FFT_N1
    c1, s1 = _dft_cos_sin(n1)
    w1 = jnp.asarray(np.block([[c1, s1], [-s1, c1]]), BF16)
    c2, s2 = _dft_cos_sin(n2)
    w2 = jnp.asarray(np.concatenate([c2, s2], axis=1), BF16)
    tw = 2.0 * np.pi * (np.arange(n1)[:, None] * np.arange(n2)[None, :]) / s
    ct = jnp.asarray(np.repeat(np.cos(tw), cb, axis=1), F32)
    st = jnp.asarray(np.repeat(np.sin(tw), cb, axis=1), F32)
    lanes = n2 * cb
    tn = min(lanes, 4096)
    u_spec = pl.BlockSpec((1, n1, tn), lambda j, bi: (bi, 0, j))
    t_spec = pl.BlockSpec((n1, tn), lambda j, bi: (0, j))
    ar, ai = pl.pallas_call(
        _fft1_kernel,
        grid=(lanes // tn, b),
        in_specs=[u_spec, u_spec, pl.BlockSpec((2 * n1, 2 * n1), lambda j, bi: (0, 0)), t_spec, t_spec],
        out_specs=[u_spec, u_spec],
        out_shape=[jax.ShapeDtypeStruct((b, n1, lanes), BF16)] * 2,
        compiler_params=_params("arbitrary", "arbitrary"),
    )(ur.reshape(b, n1, lanes), ui.reshape(b, n1, lanes), w1, ct, st)
    kc = 8
    a_spec = pl.BlockSpec((1, kc, n2, cb), lambda bi, j: (bi, j, 0, 0))
    y = pl.pallas_call(
        functools.partial(_fft2_kernel, norm=1.0 / math.sqrt(s * FNET_GROUP_DIM)),
        grid=(b, n1 // kc),
        in_specs=[a_spec, a_spec, pl.BlockSpec((n2, 2 * n2), lambda bi, j: (0, 0))],
        out_specs=a_spec,
        out_shape=jax.ShapeDtypeStruct((b, n1, n2, cb), BF16),
        compiler_params=_params("parallel", "arbitrary"),
    )(ar.reshape(b, n1, n2, cb), ai.reshape(b, n1, n2, cb), w2)
    return jnp.transpose(y, (0, 2, 1, 3)).reshape(b, s, cb)


def _na_kernel(q_ref, k_ref, v_ref, kc_ref, vc_ref, bias_ref, o_ref, *, rows):
    rb = pl.program_id(1)
    kb = jnp.clip(rb * NA_QROWS - NA_WIN_ROWS // 2, 0, rows - NA_KROWS)
    nk = NA_KROWS * GRID_W
    tok0 = pl.multiple_of(kb * GRID_W, 256)
    scale = NA_HEAD_DIM ** -0.5
    outs = []
    for h in range(NA_HEADS):
        sl = slice(h * NA_HEAD_DIM, (h + 1) * NA_HEAD_DIM)
        q = q_ref[0, :, sl]
        s = _dot_nt(q, k_ref[0, pl.ds(tok0, nk), sl]) * scale + bias_ref[0, h]
        sc = _dot_nt(q, kc_ref[0, :, sl]) * scale
        m = jnp.maximum(jnp.max(s, axis=-1, keepdims=True), jnp.max(sc, axis=-1, keepdims=True))
        e = jnp.exp(s - m)
        ec = jnp.exp(sc - m)
        l = jnp.sum(e, axis=-1, keepdims=True) + jnp.sum(ec, axis=-1, keepdims=True)
        o = _dot(e.astype(BF16), v_ref[0, pl.ds(tok0, nk), sl]) + _dot(ec.astype(BF16), vc_ref[0, :, sl])
        outs.append(o / l)
    o_ref[0] = jnp.concatenate(outs, axis=1).astype(BF16)


def _na_bias_tables(rel_bias, rows):
    n_dr, n_dc = 2 * NA_WIN_ROWS - 1, 2 * NA_WIN_COLS - 1
    cq = np.arange(GRID_W)
    col_lo = np.clip(cq - NA_WIN_COLS // 2, 0, GRID_W - NA_WIN_COLS)
    col_ok = (cq[None, :] >= col_lo[:, None]) & (cq[None, :] < col_lo[:, None] + NA_WIN_COLS)
    dc_idx = np.where(col_ok, np.clip(cq[None, :] - cq[:, None] + NA_WIN_COLS - 1, 0, n_dc - 1), n_dc)
    dr_idx = []
    for r0 in (0, NA_QROWS, rows - NA_QROWS):
        kb = int(np.clip(r0 - NA_WIN_ROWS // 2, 0, rows - NA_KROWS))
        r = r0 + np.arange(NA_QROWS)
        rk = kb + np.arange(NA_KROWS)
        start = np.clip(r - NA_WIN_ROWS // 2, 0, rows - NA_WIN_ROWS)
        row_ok = (rk[None, :] >= start[:, None]) & (rk[None, :] < start[:, None] + NA_WIN_ROWS)
        dr_idx.append(np.where(row_ok, np.clip(rk[None, :] - r[:, None] + NA_WIN_ROWS - 1, 0, n_dr - 1), n_dr))
    oh_r = jnp.asarray(np.stack(dr_idx)[..., None] == np.arange(n_dr + 1), F32)
    oh_c = jnp.asarray(dc_idx[..., None] == np.arange(n_dc + 1), F32)
    bias = jnp.pad(rel_bias.astype(F32), ((0, 0), (0, 1), (0, 1)), constant_values=NA_MASKED)
    by_row = jnp.einsum('tqka,hab->thqkb', oh_r, bias, precision=lax.Precision.HIGHEST)
    tabs = jnp.einsum('thqkb,cdb->thqckd', by_row, oh_c, precision=lax.Precision.HIGHEST)
    return tabs.reshape(3, NA_HEADS, NA_QROWS * GRID_W, NA_KROWS * GRID_W)


def _na_latent(p, pc, bias_tabs):
    b, s, _ = p.shape
    l = pc.shape[1]
    rows = s // GRID_W
    nrb = rows // NA_QROWS
    tq = NA_QROWS * GRID_W
    nk = NA_KROWS * GRID_W
    return pl.pallas_call(
        functools.partial(_na_kernel, rows=rows),
        grid=(b, nrb),
        in_specs=[pl.BlockSpec((1, tq, CB), lambda bi, i: (bi, i, COL_BQ)),
                  pl.BlockSpec((1, s, CB), lambda bi, i: (bi, 0, COL_BK)),
                  pl.BlockSpec((1, s, CB), lambda bi, i: (bi, 0, COL_BV)),
                  pl.BlockSpec((1, l, CB), lambda bi, i: (bi, 0, COL_BK)),
                  pl.BlockSpec((1, l, CB), lambda bi, i: (bi, 0, COL_BV)),
                  pl.BlockSpec((1, NA_HEADS, tq, nk),
                               lambda bi, i: (jnp.minimum(i, 1) + (i == nrb - 1).astype(jnp.int32), 0, 0, 0))],
        out_specs=pl.BlockSpec((1, tq, CB), lambda bi, i: (bi, i, 0)),
        out_shape=jax.ShapeDtypeStruct((b, s, CB), BF16),
        compiler_params=_params("parallel", "arbitrary"),
    )(p, p, p, pc, pc, bias_tabs)


def _diff_lambda(lp, lam_init):
    return (jnp.exp(jnp.sum(lp[0:1] * lp[1:2], axis=-1, keepdims=True))
            - jnp.exp(jnp.sum(lp[2:3] * lp[3:4], axis=-1, keepdims=True)) + lam_init)


def _diff_finish(o, g, lam_init):
    y = o * lax.rsqrt(jnp.mean(o * o, axis=-1, keepdims=True) + EPS)
    return y * g * (1.0 - lam_init)


def _diff_kernel(q_ref, kt_ref, v_ref, lp_ref, g_ref, o_ref, m_sc, acc_sc, e_sc, *, tk, lam_init):
    tq = q_ref.shape[1]
    nk = kt_ref.shape[2] // tk
    m_sc[...] = jnp.full(m_sc.shape, -jnp.inf, F32)
    acc_sc[...] = jnp.zeros(acc_sc.shape, F32)

    def body(c, carry):
        k0 = pl.multiple_of(c * tk, LANES)
        for h in range(DIFF_HEADS):
            for m in range(2):
                rows = slice(m * tq, (m + 1) * tq)
                dims = slice((2 * h + m) * DIFF_QK_DIM, (2 * h + m + 1) * DIFF_QK_DIM)
                s = _dot(q_ref[0, :, dims], kt_ref[0, dims, pl.ds(k0, tk)])
                m_old = m_sc[h, rows]
                m_new = jnp.maximum(m_old, jnp.max(s, axis=-1, keepdims=True))
                e_sc[h, rows] = jnp.exp2(s - m_new[:, :1]).astype(BF16)
                acc_sc[h, rows] = jnp.exp2(m_old - m_new) * acc_sc[h, rows]
                m_sc[h, rows] = m_new
            acc_sc[h] += _dot(e_sc[h], v_ref[0, h, pl.ds(k0, tk), :])
        return carry

    lax.fori_loop(0, nk, body, 0)
    lam = _diff_lambda(lp_ref[...], lam_init)
    outs = []
    for h in range(DIFF_HEADS):
        acc = acc_sc[h]
        o = acc[:, :DIFF_V_DIM] / acc[:, DIFF_V_DIM:DIFF_V_DIM + 1]
        outs.append(_diff_finish(o[:tq] - lam * o[tq:], g_ref[...], lam_init))
    o_ref[0] = jnp.concatenate(outs, axis=1).astype(BF16)


def _key_tile(nkeys, cap):
    return max(t for t in range(128, cap + 1, 128) if nkeys % t == 0)


def _diff_latent(p, kt, vh, lp, sub_g, lam_init, tq, tk):
    b, s, _ = p.shape
    nkeys = kt.shape[2]
    return pl.pallas_call(
        functools.partial(_diff_kernel, tk=tk, lam_init=lam_init),
        grid=(b, s // tq),
        in_specs=[pl.BlockSpec((1, tq, CB), lambda bi, i: (bi, i, COL_DQ)),
                  pl.BlockSpec((1, CB, nkeys), lambda bi, i: (bi, 0, 0), pipeline_mode=pl.Buffered(1)),
                  pl.BlockSpec((1, DIFF_HEADS, nkeys, LANES), lambda bi, i: (bi, 0, 0, 0),
                               pipeline_mode=pl.Buffered(1)),
                  pl.BlockSpec((4, DIFF_QK_DIM), lambda bi, i: (0, 0)),
                  pl.BlockSpec((1, DIFF_V_DIM), lambda bi, i: (0, 0))],
        out_specs=pl.BlockSpec((1, tq, CB), lambda bi, i: (bi, i, 0)),
        out_shape=jax.ShapeDtypeStruct((b, s, CB), BF16),
        scratch_shapes=[pltpu.VMEM((DIFF_HEADS, 2 * tq, LANES), F32),
                        pltpu.VMEM((DIFF_HEADS, 2 * tq, LANES), F32),
                        pltpu.VMEM((DIFF_HEADS, 2 * tq, tk), BF16)],
        compiler_params=_params("parallel", "arbitrary"),
    )(p, kt, vh, lp, sub_g.reshape(1, DIFF_V_DIM))


def _softmax_rows(s):
    e = jnp.exp(s - jnp.max(s, axis=-1, keepdims=True))
    return e / jnp.sum(e, axis=-1, keepdims=True)


def _ctx_kernel(pc_ref, ur_ref, ui_ref, wf_ref, lp_ref, g_ref, ya_ref, yb_ref, yd_ref, *, lam_init):
    l = pc_ref.shape[1]
    col = lambda j, lo, hi: pc_ref[0, :, j * CB + lo:j * CB + hi]
    u = jnp.concatenate([ur_ref[0], ui_ref[0]], axis=0)
    ya_ref[0] = (_dot(wf_ref[...], u) * (1.0 / math.sqrt(l * FNET_GROUP_DIM))).astype(BF16)
    outs = []
    for h in range(NA_HEADS):
        lo, hi = h * NA_HEAD_DIM, (h + 1) * NA_HEAD_DIM
        pr = _softmax_rows(_dot_nt(col(COL_BQ, lo, hi), col(COL_BK, lo, hi)) * NA_HEAD_DIM ** -0.5)
        outs.append(_dot(pr.astype(BF16), col(COL_BV, lo, hi)))
    yb_ref[0] = jnp.concatenate(outs, axis=1).astype(BF16)
    lam = _diff_lambda(lp_ref[...], lam_init)
    outs = []
    for h in range(DIFF_HEADS):
        pm = []
        for m in range(2):
            lo = (2 * h + m) * DIFF_QK_DIM
            pm.append(_softmax_rows(_dot_nt(col(COL_DQ, lo, lo + DIFF_QK_DIM), col(COL_DK, lo, lo + DIFF_QK_DIM))
                                    * DIFF_QK_DIM ** -0.5))
        a = (pm[0] - lam * pm[1]).astype(BF16)
        o = _dot(a, col(COL_DV, h * DIFF_V_DIM, (h + 1) * DIFF_V_DIM))
        outs.append(_diff_finish(o, g_ref[...], lam_init))
    yd_ref[0] = jnp.concatenate(outs, axis=1).astype(BF16)


def _ctx_branches(pc, ucr, uci, lp, sub_g, lam_init):
    b, l, _ = pc.shape
    c, s = _dft_cos_sin(l)
    wf = jnp.asarray(np.concatenate([c, s], axis=1), BF16)
    y_spec = pl.BlockSpec((1, l, CB), lambda bi: (bi, 0, 0))
    return pl.pallas_call(
        functools.partial(_ctx_kernel, lam_init=lam_init),
        grid=(b,),
        in_specs=[pl.BlockSpec((1, l, IN_DIM), lambda bi: (bi, 0, 0)), y_spec, y_spec,
                  pl.BlockSpec((l, 2 * l), lambda bi: (0, 0)),
                  pl.BlockSpec((4, DIFF_QK_DIM), lambda bi: (0, 0)),
                  pl.BlockSpec((1, DIFF_V_DIM), lambda bi: (0, 0))],
        out_specs=[y_spec] * 3,
        out_shape=[jax.ShapeDtypeStruct((b, l, CB), BF16)] * 3,
        compiler_params=_params("parallel"),
    )(pc, ucr, uci, wf, lp, sub_g.reshape(1, DIFF_V_DIM))


def _merge_kernel(x_ref, mod_ref, g1_ref, g2_ref, ya_ref, yb_ref, yd_ref, pb_ref, pc_ref, px_ref,
                  cp_ref, xp_ref, cn_ref, xn_ref, cw_ref, wg_ref, wb_ref, wo_ref, h2_prev,
                  xo_ref, h2_ref):
    del h2_prev
    i = pl.program_id(1)
    last = pl.num_programs(1) - 1
    tm = x_ref.shape[1]
    mod = mod_ref[0]
    x = x_ref[0]
    h = _norm_mod(x, g1_ref[...], mod[0:1], mod[1:2]).astype(BF16)

    u = pc_ref[0].astype(F32) * px_ref[0].astype(F32)
    up = cp_ref[0, BF16_SUBLANES - 1:, :].astype(F32) * xp_ref[0, BF16_SUBLANES - 1:, :].astype(F32)
    un = cn_ref[0, :1, :].astype(F32) * xn_ref[0, :1, :].astype(F32)
    up = jnp.where(i == 0, 0.0, up)
    un = jnp.where(i == last, 0.0, un)
    rid = lax.broadcasted_iota(jnp.int32, u.shape, 0)
    u_prev = jnp.where(rid == 0, up, pltpu.roll(u, 1, 0))
    u_next = jnp.where(rid == tm - 1, un, pltpu.roll(u, tm - 1, 0))
    cw = cw_ref[...]
    yc = pb_ref[0].astype(F32) * (cw[0:1] * u_prev + cw[1:2] * u + cw[2:3] * u_next)

    branches = (ya_ref[0], yb_ref[0], yc.astype(BF16), yd_ref[0])
    d = x.shape[1]
    out = None
    for n in range(d // CB):
        cols = slice(n * CB, (n + 1) * CB)
        merged = None
        for j in range(N_BRANCHES):
            t = jax.nn.sigmoid(_dot(h, wg_ref[j, :, cols])) * _dot(branches[j], wb_ref[j, :, cols])
            merged = t if merged is None else merged + t
        t = _dot(merged.astype(BF16), wo_ref[cols, :])
        out = t if out is None else out + t
    xn = x + mod[2:3] * out
    xo_ref[0] = xn
    h2 = _norm_mod(xn, g2_ref[...], mod[3:4], mod[4:5]).astype(BF16)
    h2_ref[0] = h2
    h2_ref[1] = h2


def _merge(x, mods, mod_row, g1, g2, ya, yb, yd, p, conv_w, wg, wb, wo, tm, h2_buf, tok_off):
    b, s, d = x.shape
    off = tok_off // tm
    per = s // tm
    hb = tm // BF16_SUBLANES
    n_halo = s // BF16_SUBLANES
    mod_idx = (lambda bi, i: (bi, 0, 0)) if mod_row is None else (lambda bi, i: (mod_row, 0, 0))
    seq = lambda width, col=0: pl.BlockSpec((1, tm, width), lambda bi, i: (bi, i, col))
    prev = lambda col: pl.BlockSpec((1, BF16_SUBLANES, CB), lambda bi, i: (bi, jnp.maximum(i * hb - 1, 0), col))
    nxt = lambda col: pl.BlockSpec((1, BF16_SUBLANES, CB),
                                   lambda bi, i: (bi, jnp.minimum((i + 1) * hb, n_halo - 1), col))
    const = lambda shape: pl.BlockSpec(shape, lambda bi, i: (0,) * len(shape))
    return pl.pallas_call(
        _merge_kernel,
        grid=(b, s // tm),
        in_specs=[seq(d), pl.BlockSpec((1, N_MOD, d), mod_idx), const((1, d)), const((1, d)),
                  seq(CB), seq(CB), seq(CB),
                  seq(CB, COL_CB), seq(CB, COL_CC), seq(CB, COL_CX),
                  prev(COL_CC), prev(COL_CX), nxt(COL_CC), nxt(COL_CX),
                  const((CONV_WIDTH, CB)),
                  const((N_BRANCHES, d, d)), const((N_BRANCHES, BRANCH_DIM, d)), const((d, d)),
                  pl.BlockSpec(memory_space=pl.ANY)],
        out_specs=[seq(d), pl.BlockSpec((2, tm, d), lambda bi, i: (0, off + bi * per + i, 0))],
        out_shape=[jax.ShapeDtypeStruct((b, s, d), F32),
                   jax.ShapeDtypeStruct(h2_buf.shape, BF16)],
        input_output_aliases={18: 1},
        compiler_params=_params("parallel", "arbitrary"),
    )(x, mods, g1.reshape(1, d), g2.reshape(1, d), ya, yb, yd, p, p, p, p, p, p, p,
      conv_w, wg, wb, wo, h2_buf)


def _route_kernel(h2_ref, wr_ref, bias_ref, tri_ref, ones_ref, idx_ref, w_ref, rank_ref, cnt_ref,
                  score_sc, sel_sc, carry_sc):
    i = pl.program_id(0)
    tm = h2_ref.shape[0]
    ne = wr_ref.shape[0]
    gsz = ne // N_GROUPS
    n_chunks = tm // LANES

    @pl.when(i == 0)
    def _():
        carry_sc[...] = jnp.zeros(carry_sc.shape, F32)

    score_sc[...] = jax.nn.sigmoid(_dot_nt(wr_ref[...], h2_ref[...]))

    def select(cidx, carry):
        c0 = pl.multiple_of(cidx * LANES, LANES)
        scores = score_sc[:, pl.ds(c0, LANES)]
        biased = scores + bias_ref[...]
        liota = lax.broadcasted_iota(jnp.int32, (gsz, LANES), 0)
        gs = []
        for g in range(N_GROUPS):
            v = biased[g * gsz:(g + 1) * gsz]
            m1 = jnp.max(v, axis=0, keepdims=True)
            i1 = jnp.min(jnp.where(v == m1, liota, gsz), axis=0, keepdims=True)
            m2 = jnp.max(jnp.where(liota == i1, -jnp.inf, v), axis=0, keepdims=True)
            gs.append(m1 + m2)
        gsm = jnp.concatenate(gs, axis=0)
        giota = lax.broadcasted_iota(jnp.int32, gsm.shape, 0)
        keep = jnp.zeros(gsm.shape, F32)
        for _ in range(TOPK_GROUPS):
            m = jnp.max(gsm, axis=0, keepdims=True)
            gi = jnp.min(jnp.where(gsm == m, giota, N_GROUPS), axis=0, keepdims=True)
            hit = giota == gi
            keep = jnp.where(hit, 1.0, keep)
            gsm = jnp.where(hit, -jnp.inf, gsm)
        cur = jnp.concatenate(
            [jnp.where(jnp.broadcast_to(keep[g:g + 1], (gsz, LANES)) > 0.0, biased[g * gsz:(g + 1) * gsz], -jnp.inf)
             for g in range(N_GROUPS)], axis=0)
        eiota = lax.broadcasted_iota(jnp.int32, (ne, LANES), 0)
        sel = jnp.zeros((ne, LANES), F32)
        idxs, ws = [], []
        for _ in range(TOP_K):
            m = jnp.max(cur, axis=0, keepdims=True)
            ik = jnp.min(jnp.where(cur == m, eiota, ne), axis=0, keepdims=True)
            hit = eiota == ik
            cur = jnp.where(hit, -jnp.inf, cur)
            ws.append(jnp.sum(jnp.where(hit, scores, 0.0), axis=0, keepdims=True))
            idxs.append(ik)
            sel = jnp.where(hit, 1.0, sel)
        w = jnp.concatenate(ws, axis=0)
        w_ref[:, pl.ds(c0, LANES)] = w / jnp.sum(w, axis=0, keepdims=True) * ROUTED_SCALE
        idx_ref[:, pl.ds(c0, LANES)] = jnp.concatenate(idxs, axis=0)
        sel_sc[:, pl.ds(c0, LANES)] = sel.astype(BF16)
        return carry

    lax.fori_loop(0, n_chunks, select, 0)

    sel_all = sel_sc[...]
    score_sc[...] = _dot(sel_all, tri_ref[...]) + jnp.concatenate([carry_sc[...]] * n_chunks, axis=1)

    def ranks(cidx, carry):
        c0 = pl.multiple_of(cidx * LANES, LANES)
        before = score_sc[:, pl.ds(c0, LANES)]
        idx = idx_ref[:, pl.ds(c0, LANES)]
        eiota = lax.broadcasted_iota(jnp.int32, (ne, LANES), 0)
        rows = [jnp.sum(jnp.where(eiota == idx[k:k + 1], before, 0.0), axis=0, keepdims=True)
                for k in range(TOP_K)]
        rank_ref[:, pl.ds(c0, LANES)] = jnp.concatenate(rows, axis=0).astype(jnp.int32)
        return carry

    lax.fori_loop(0, n_chunks, ranks, 0)
    carry_sc[...] += _dot(sel_all, ones_ref[...])
    cnt_ref[...] = carry_sc[...]


def _route(h2_all, n, wr_t, bias, tm):
    d = h2_all.shape[1]
    ne = wr_t.shape[0]
    tri = jnp.asarray(np.triu(np.ones((tm, tm), np.float32), 1), BF16)
    ones = jnp.ones((tm, LANES), BF16)
    bias_b = jnp.broadcast_to(bias.astype(F32)[:, None], (ne, LANES))
    const = lambda shape: pl.BlockSpec(shape, lambda i: (0,) * len(shape))
    tok = pl.BlockSpec((TOP_K, tm), lambda i: (0, i))
    return pl.pallas_call(
        _route_kernel,
        grid=(n // tm,),
        in_specs=[pl.BlockSpec((tm, d), lambda i: (i, 0)), const((ne, d)), const((ne, LANES)),
                  const((tm, tm)), const((tm, LANES))],
        out_specs=[tok, tok, tok, const((ne, LANES))],
        out_shape=[jax.ShapeDtypeStruct((TOP_K, n), jnp.int32),
                   jax.ShapeDtypeStruct((TOP_K, n), F32),
                   jax.ShapeDtypeStruct((TOP_K, n), jnp.int32),
                   jax.ShapeDtypeStruct((ne, LANES), F32)],
        scratch_shapes=[pltpu.VMEM((ne, tm), F32), pltpu.VMEM((ne, tm), BF16), pltpu.VMEM((ne, LANES), F32)],
        compiler_params=_params("arbitrary"),
    )(h2_all, wr_t, bias_b, tri, ones)


def _pos_kernel(idx_ref, rank_ref, start_ref, pos_ref):
    ne = start_ref.shape[0]
    start = start_ref[...]

    def body(cidx, carry):
        c0 = pl.multiple_of(cidx * LANES, LANES)
        idx = idx_ref[:, pl.ds(c0, LANES)]
        eiota = lax.broadcasted_iota(jnp.int32, (ne, LANES), 0)
        rows = [jnp.sum(jnp.where(eiota == idx[k:k + 1], start, 0.0), axis=0, keepdims=True)
                for k in range(TOP_K)]
        pos_ref[:, pl.ds(c0, LANES)] = jnp.concatenate(rows, axis=0).astype(jnp.int32) + rank_ref[:, pl.ds(c0, LANES)]
        return carry

    lax.fori_loop(0, idx_ref.shape[1] // LANES, body, 0)


def _positions(idx, rank, start_rows, tm):
    k, n = idx.shape
    ne = start_rows.shape[0]
    start_b = jnp.broadcast_to(start_rows.astype(F32)[:, None], (ne, LANES))
    tok = pl.BlockSpec((k, tm), lambda i: (0, i))
    return pl.pallas_call(
        _pos_kernel,
        grid=(n // tm,),
        in_specs=[tok, tok, pl.BlockSpec((ne, LANES), lambda i: (0, 0))],
        out_specs=tok,
        out_shape=jax.ShapeDtypeStruct((k, n), jnp.int32),
        compiler_params=_params("parallel"),
    )(idx, rank, start_b)


def _rowtok_kernel(start_ref, cnt_ref, pos_ref, out_ref, *, n_tokens):
    i = pl.program_id(0)
    tn = pos_ref.shape[1]
    n_rows = out_ref.shape[0]
    ne = start_ref.shape[0]

    @pl.when(i == 0)
    def _():
        def gaps(e, carry):
            lo = start_ref[e] + cnt_ref[e]
            hi = jnp.where(e + 1 < ne, start_ref[jnp.minimum(e + 1, ne - 1)], n_rows)

            def one(r, c):
                out_ref[r] = lax.rem(r, n_tokens)
                return c

            return lax.fori_loop(lo, hi, one, carry)

        lax.fori_loop(0, ne, gaps, 0)

    def body(t, carry):
        for k in range(TOP_K):
            out_ref[pos_ref[k, t]] = i * tn + t
        return carry

    lax.fori_loop(0, tn, body, 0)


def _row_tokens(pos, seg_start, counts, n_rows):
    k, n = pos.shape
    tn = max(t for t in range(LANES, 2048 + 1, LANES) if n % t == 0)
    grid_spec = pltpu.PrefetchScalarGridSpec(
        num_scalar_prefetch=2,
        grid=(n // tn,),
        in_specs=[pl.BlockSpec((k, tn), lambda i, st, ct: (0, i), memory_space=pltpu.SMEM)],
        out_specs=pl.BlockSpec((n_rows,), lambda i, st, ct: (0,), memory_space=pltpu.SMEM))
    return pl.pallas_call(
        functools.partial(_rowtok_kernel, n_tokens=n),
        grid_spec=grid_spec,
        out_shape=jax.ShapeDtypeStruct((n_rows,), jnp.int32),
        compiler_params=_params("arbitrary"),
    )(seg_start, counts, pos)


def _expert_kernel(crow_ref, crun_ref, rexp_ref, meta_ref, x_hbm, wg_hbm, wu_hbm, wd_hbm, y_hbm,
                   wg_buf, wu_buf, wd_buf, wg_bf, wu_bf, wd_bf, x_buf, y_buf, zero_buf,
                   w_sem, x_sem, y_sem, z_sem, *, layer):
    n_chunks, n_runs, tail_start = meta_ref[0], meta_ref[1], meta_ref[2]
    n_rows = x_hbm.shape[0]

    def w_copies(run):
        expert, s = rexp_ref[run], run % MOE_W_SLOTS
        return (pltpu.make_async_copy(wg_hbm.at[layer, expert], wg_buf.at[s], w_sem.at[s, 0]),
                pltpu.make_async_copy(wu_hbm.at[layer, expert], wu_buf.at[s], w_sem.at[s, 1]),
                pltpu.make_async_copy(wd_hbm.at[layer, expert], wd_buf.at[s], w_sem.at[s, 2]))

    def x_copy(c):
        rows = pl.ds(pl.multiple_of(crow_ref[c], MOE_ALIGN), MOE_ROWS)
        return pltpu.make_async_copy(x_hbm.at[rows], x_buf.at[c % MOE_X_SLOTS], x_sem.at[c % MOE_X_SLOTS])

    def y_copy(c):
        rows = pl.ds(pl.multiple_of(crow_ref[c], MOE_ALIGN), MOE_ROWS)
        return pltpu.make_async_copy(y_buf.at[c % 2], y_hbm.at[rows], y_sem.at[c % 2])

    def zero_copy(row0):
        return pltpu.make_async_copy(zero_buf, y_hbm.at[pl.ds(row0, MOE_ALIGN)], z_sem.at[0])

    for r in range(MOE_W_SLOTS - 1):
        @pl.when(r < n_runs)
        def _(r=r):
            for cp in w_copies(r):
                cp.start()

    for c in range(MOE_X_SLOTS - 1):
        @pl.when(c < n_chunks)
        def _(c=c):
            x_copy(c).start()

    def chunk(c, carry):
        run = crun_ref[c]
        first = jnp.logical_or(c == 0, crun_ref[jnp.maximum(c - 1, 0)] != run)

        @pl.when(c + MOE_X_SLOTS - 1 < n_chunks)
        def _():
            x_copy(c + MOE_X_SLOTS - 1).start()

        @pl.when(first)
        def _():
            @pl.when(run + MOE_W_SLOTS - 1 < n_runs)
            def _():
                for cp in w_copies(run + MOE_W_SLOTS - 1):
                    cp.start(priority=1)

            for cp in w_copies(run):
                cp.wait()
            s = run % MOE_W_SLOTS
            wg_bf[...] = wg_buf[s].astype(BF16)
            wu_bf[...] = wu_buf[s].astype(BF16)
            wd_bf[...] = wd_buf[s].astype(BF16)

        x_copy(c).wait()
        x = x_buf[c % MOE_X_SLOTS]
        a = (_silu(_dot(x, wg_bf[...])) * _dot(x, wu_bf[...])).astype(BF16)
        y = _dot(a, wd_bf[...]).astype(BF16)

        @pl.when(c > 0)
        def _():
            y_copy(c - 1).wait()

        y_buf[c % 2] = y
        y_copy(c).start()
        return carry

    lax.fori_loop(0, n_chunks, chunk, 0)

    @pl.when(n_chunks > 0)
    def _():
        y_copy(n_chunks - 1).wait()

    zero_buf[...] = jnp.zeros(zero_buf.shape, zero_buf.dtype)
    n_tail = (n_rows - tail_start) // MOE_ALIGN

    def fill(t, carry):
        zero_copy(pl.multiple_of(tail_start + t * MOE_ALIGN, MOE_ALIGN)).start()
        return carry

    def drain(t, carry):
        zero_copy(0).wait()
        return carry

    lax.fori_loop(0, n_tail, fill, 0)
    lax.fori_loop(0, n_tail, drain, 0)


def _experts(x_sorted, seg_start, counts, layer, w_g, w_u, w_d):
    n_rows, d = x_sorted.shape
    ne, f = w_g.shape[1], w_g.shape[3]
    nch = (counts + MOE_ROWS - 1) // MOE_ROWS
    c_end = jnp.cumsum(nch)
    max_chunks = (n_rows - MOE_ROWS) // MOE_ROWS + ne
    g = jnp.arange(max_chunks, dtype=jnp.int32)
    c_exp = jnp.minimum(jnp.sum((c_end[None, :] <= g[:, None]).astype(jnp.int32), axis=1), ne - 1)
    c_row = jnp.where(g < c_end[-1], seg_start[c_exp] + (g - (c_end - nch)[c_exp]) * MOE_ROWS, 0)
    has_rows = (nch > 0).astype(jnp.int32)
    run_end = jnp.cumsum(has_rows)
    c_run = (run_end - 1)[c_exp]
    r = jnp.arange(ne, dtype=jnp.int32)
    r_exp = jnp.minimum(jnp.sum((run_end[None, :] <= r[:, None]).astype(jnp.int32), axis=1), ne - 1)
    tail_start = jnp.max(jnp.where(nch > 0, seg_start + nch * MOE_ROWS, 0))
    meta = jnp.stack([c_end[-1], run_end[-1], tail_start]).astype(jnp.int32)
    any_spec = pl.BlockSpec(memory_space=pl.ANY)
    grid_spec = pltpu.PrefetchScalarGridSpec(
        num_scalar_prefetch=4,
        grid=(1,),
        in_specs=[any_spec] * 4,
        out_specs=any_spec,
        scratch_shapes=[pltpu.VMEM((MOE_W_SLOTS, d, f), F32), pltpu.VMEM((MOE_W_SLOTS, d, f), F32),
                        pltpu.VMEM((MOE_W_SLOTS, f, d), F32),
                        pltpu.VMEM((d, f), BF16), pltpu.VMEM((d, f), BF16), pltpu.VMEM((f, d), BF16),
                        pltpu.VMEM((MOE_X_SLOTS, MOE_ROWS, d), BF16), pltpu.VMEM((2, MOE_ROWS, d), BF16),
                        pltpu.VMEM((MOE_ALIGN, d), BF16),
                        pltpu.SemaphoreType.DMA((MOE_W_SLOTS, 3)), pltpu.SemaphoreType.DMA((MOE_X_SLOTS,)),
                        pltpu.SemaphoreType.DMA((2,)), pltpu.SemaphoreType.DMA((1,))])
    return pl.pallas_call(
        functools.partial(_expert_kernel, layer=layer),
        grid_spec=grid_spec,
        out_shape=jax.ShapeDtypeStruct((n_rows, d), BF16),
        compiler_params=_params("arbitrary"),
    )(c_row.astype(jnp.int32), c_run.astype(jnp.int32), r_exp, meta, x_sorted, w_g, w_u, w_d)


def _resid_kernel(x_ref, h2_ref, y_ref, w_ref, mod_ref, sg_ref, su_ref, sd_ref, gf_ref, o_ref, *, final):
    h2 = h2_ref[...]
    a = (_silu(_dot(h2, sg_ref[...])) * _dot(h2, su_ref[...])).astype(BF16)
    y = _dot(a, sd_ref[...])
    w = w_ref[...]
    for k in range(TOP_K):
        y = y + w[:, k:k + 1] * y_ref[k].astype(F32)
    xo = x_ref[...] + mod_ref[0][5:6] * y
    if final:
        xo = xo * lax.rsqrt(jnp.mean(xo * xo, axis=-1, keepdims=True) + EPS) * gf_ref[...]
    o_ref[...] = xo


def _resid(x_flat, h2_all, y_tok, w_tok, row_off, mods, mod_row, rows_per_mod, sg, su, sd, gf, final, tm):
    n, d = x_flat.shape
    f = sg.shape[1]
    off = row_off // tm
    per = rows_per_mod // tm
    mod_idx = (lambda i: (i // per, 0, 0)) if mod_row is None else (lambda i: (mod_row, 0, 0))
    const = lambda shape: pl.BlockSpec(shape, lambda i: (0,) * len(shape))
    return pl.pallas_call(
        functools.partial(_resid_kernel, final=final),
        grid=(n // tm,),
        in_specs=[pl.BlockSpec((tm, d), lambda i: (i, 0)),
                  pl.BlockSpec((tm, d), lambda i: (i + off, 0)),
                  pl.BlockSpec((TOP_K, tm, d), lambda i: (0, i + off, 0)),
                  pl.BlockSpec((tm, TOP_K), lambda i: (i + off, 0)),
                  pl.BlockSpec((1, N_MOD, d), mod_idx),
                  const((d, f)), const((d, f)), const((f, d)), const((1, d))],
        out_specs=pl.BlockSpec((tm, d), lambda i: (i, 0)),
        out_shape=jax.ShapeDtypeStruct((n, d), F32),
        compiler_params=_params("parallel"),
    )(x_flat, h2_all, y_tok, w_tok, mods, sg, su, sd, gf.reshape(1, d))


def _moe_routed(h2_dbl, wr_t, bias, layer, w_g, w_u, w_d):
    n, d = h2_dbl.shape[0] // 2, h2_dbl.shape[1]
    idx, wts, rank, cnt = _route(h2_dbl, n, wr_t, bias, 512)
    counts = cnt[:, 0].astype(jnp.int32)
    padded = (counts + MOE_ALIGN - 1) // MOE_ALIGN * MOE_ALIGN
    seg_start = jnp.cumsum(padded) - padded
    n_rows = -(-(n * TOP_K + N_EXPERTS * (MOE_ALIGN - 1)) // MOE_ROWS) * MOE_ROWS + MOE_ROWS
    pos2 = _positions(idx, rank, seg_start, 512)
    pos = pos2.reshape(TOP_K * n)
    row_tok = _row_tokens(pos2, seg_start, counts, n_rows)
    x_sorted = h2_dbl.at[row_tok].get(mode='promise_in_bounds')
    y_sorted = _experts(x_sorted, seg_start, counts, layer, w_g, w_u, w_d)
    return y_sorted.at[pos].get(mode='promise_in_bounds').reshape(TOP_K, n, d), wts.T


def kernel(x, c, ctx, c_ctx, ada_w, ada_b, norm1_g, w_in, conv_w, na_rel_bias, diff_lambda,
           diff_subln_g, w_branch_gate, w_branch, w_out, norm2_g, router_w, router_bias,
           expert_w_gate, expert_w_up, expert_w_down, shared_w_gate, shared_w_up, shared_w_down,
           final_norm_g):
    b, s, d = x.shape
    l_ctx = ctx.shape[1]
    rows = s // GRID_W
    ctx_row = b
    cvec = jnp.zeros((8, d), F32).at[:b].set(c).at[ctx_row].set(c_ctx)
    rope_tabs = _rope_tables(s)
    wf = _channel_dft_matrix()
    tm = 512
    xc = ctx
    for layer in range(DEPTH):
        last = layer == DEPTH - 1
        lam_init = 0.8 - 0.6 * math.exp(-0.3 * layer)
        mods = _ada(cvec, ada_w, ada_b, layer).reshape(8, N_MOD, d)
        w_in_bf = w_in[layer].astype(BF16)
        wg_bf = w_branch_gate[layer].astype(BF16)
        wb_bf = w_branch[layer].astype(BF16)
        wo_bf = w_out[layer].astype(BF16)
        wr_bf = router_w[layer].T.astype(BF16)
        lp = diff_lambda[layer]
        sub_g = diff_subln_g[layer]

        kv0 = (jnp.zeros((b, CB, s + l_ctx), BF16), jnp.zeros((b, DIFF_HEADS, s + l_ctx, LANES), BF16))
        p, ur, ui, kt, vh = _inproj(x, norm1_g[layer], mods, w_in_bf, wf, rope_tabs, None, True, tm,
                                    0, kv0)
        ctx_tabs = tuple(t[:l_ctx] for t in rope_tabs)
        pc, ucr, uci, kt, vh = _inproj(xc, norm1_g[layer], mods, w_in_bf, wf, ctx_tabs, ctx_row, False,
                                       l_ctx, s, (kt, vh))

        ya = _fourier_latent(ur, ui)
        yb = _na_latent(p, pc, _na_bias_tables(na_rel_bias[layer], rows))
        yd = _diff_latent(p, kt, vh, lp, sub_g, lam_init, 1024, _key_tile(s + l_ctx, 768))
        n_tok = b * s if last else b * (s + l_ctx)
        h2_buf = jnp.zeros((2, n_tok, d), BF16)
        x, h2_buf = _merge(x, mods, None, norm1_g[layer], norm2_g[layer], ya, yb, yd, p,
                           conv_w[layer], wg_bf, wb_bf, wo_bf, tm, h2_buf, 0)
        if not last:
            yac, ybc, ydc = _ctx_branches(pc, ucr, uci, lp, sub_g, lam_init)
            xc, h2_buf = _merge(xc, mods, ctx_row, norm1_g[layer], norm2_g[layer], yac, ybc, ydc, pc,
                                conv_w[layer], wg_bf, wb_bf, wo_bf, l_ctx, h2_buf, b * s)
        h2_all = h2_buf.reshape(2 * n_tok, d)

        y_tok, w_tok = _moe_routed(h2_all, wr_bf, router_bias[layer], layer, expert_w_gate,
                                   expert_w_up, expert_w_down)
        sg_bf = shared_w_gate[layer].astype(BF16)
        su_bf = shared_w_up[layer].astype(BF16)
        sd_bf = shared_w_down[layer].astype(BF16)
        x = _resid(x.reshape(b * s, d), h2_all, y_tok, w_tok, 0, mods, None, s, sg_bf, su_bf, sd_bf,
                   final_norm_g, last, tm).reshape(b, s, d)
        if not last:
            xc = _resid(xc.reshape(b * l_ctx, d), h2_all, y_tok, w_tok, b * s, mods, ctx_row, l_ctx,
                        sg_bf, su_bf, sd_bf, final_norm_g, False, l_ctx).reshape(b, l_ctx, d)
    return x
```

```python
import functools
import math

import numpy as np
import jax
import jax.numpy as jnp
from jax import lax
from jax.experimental import pallas as pl
from jax.experimental.pallas import tpu as pltpu

F32 = jnp.float32
BF16 = jnp.bfloat16

DEPTH = 2
GRID_W = 64
EPS = 1e-6
N_MOD = 6

FNET_GROUP_DIM = 64
NA_HEADS = 4
NA_HEAD_DIM = 64
NA_WIN_ROWS = 8
NA_WIN_COLS = 16
CONV_WIDTH = 3
DIFF_HEADS = 4
DIFF_QK_DIM = 32
DIFF_V_DIM = 64
ROPE_BASE = 10000.0
N_BRANCHES = 4
BRANCH_DIM = 256

COL_A, COL_BQ, COL_BK, COL_BV, COL_CB, COL_CC, COL_CX, COL_DQ, COL_DK, COL_DV = range(10)
N_COL_BLOCKS = 10
CB = 256
IN_DIM = N_COL_BLOCKS * CB

N_EXPERTS = 256
TOP_K = 8
N_GROUPS = 8
TOPK_GROUPS = 4
ROUTED_SCALE = 2.5
LOG2_E = 1.4426950408889634

VMEM_LIMIT_BYTES = 56 * 1024 * 1024
LANES = 128
BF16_SUBLANES = 16
FFT_N1 = 64
NA_MASKED = -1e30
NA_QROWS = 8
NA_KROWS = 16
MOE_ROWS = 256
MOE_ALIGN = BF16_SUBLANES
MOE_X_SLOTS = 4
MOE_W_SLOTS = 3


def _params(*sem):
    return pltpu.CompilerParams(dimension_semantics=sem, vmem_limit_bytes=VMEM_LIMIT_BYTES)


def _dot(a, b):
    return jnp.dot(a, b, preferred_element_type=F32)


def _dot_nt(a, b):
    return lax.dot_general(a, b, (((1,), (1,)), ((), ())), preferred_element_type=F32)


def _norm_mod(xf, g, shift, scale):
    y = xf * lax.rsqrt(jnp.mean(xf * xf, axis=-1, keepdims=True) + EPS)
    return (y * g) * (1.0 + scale) + shift


def _silu(v):
    return v * jax.nn.sigmoid(v)


def _ada_kernel(c_ref, w_ref, b_ref, o_ref):
    s = _silu(c_ref[...])
    o_ref[...] = _dot(s.astype(BF16), w_ref[0].astype(BF16)) + b_ref[0]


def _ada(cvec, w, b, layer):
    rows, d = cvec.shape
    depth, _, n = w.shape
    tn = 1536
    return pl.pallas_call(
        _ada_kernel,
        grid=(n // tn,),
        in_specs=[pl.BlockSpec((rows, d), lambda j: (0, 0)),
                  pl.BlockSpec((1, d, tn), lambda j: (layer, 0, j)),
                  pl.BlockSpec((1, 1, tn), lambda j: (layer, 0, j))],
        out_specs=pl.BlockSpec((rows, tn), lambda j: (0, j)),
        out_shape=jax.ShapeDtypeStruct((rows, n), F32),
        compiler_params=_params("arbitrary"),
    )(cvec, w, b.reshape(depth, 1, n))


def _inproj_kernel(x_ref, g_ref, mod_ref, w_ref, wf_ref, cos_ref, s1_ref, s2_ref, kt_prev, vh_prev,
                   p_ref, ur_ref, ui_ref, kt_ref, vh_ref, *, rope):
    del kt_prev, vh_prev
    mod = mod_ref[0]
    h = _norm_mod(x_ref[0], g_ref[...], mod[0:1], mod[1:2]).astype(BF16)
    for j in range(N_COL_BLOCKS):
        pj = _dot(h, w_ref[:, j * CB:(j + 1) * CB])
        if j == COL_A:
            u = _dot(pj.astype(BF16), wf_ref[...])
            ur_ref[0] = u[:, :CB].astype(BF16)
            ui_ref[0] = u[:, CB:].astype(BF16)
        if rope and j in (COL_DQ, COL_DK):
            cos = jnp.concatenate([cos_ref[...]] * 2, axis=1)
            s1 = jnp.concatenate([s1_ref[...]] * 2, axis=1)
            s2 = jnp.concatenate([s2_ref[...]] * 2, axis=1)
            pj = pj * cos + pltpu.roll(pj, CB - 8, 1) * s1 + pltpu.roll(pj, 8, 1) * s2
            if j == COL_DQ:
                pj = pj * (DIFF_QK_DIM ** -0.5 * LOG2_E)
        p_ref[0, :, j * CB:(j + 1) * CB] = pj.astype(BF16)
        if j == COL_DK:
            kt_ref[0] = pj.T.astype(BF16)
        if j == COL_DV:
            ones = jnp.ones((pj.shape[0], LANES - DIFF_V_DIM), F32)
            for hd in range(DIFF_HEADS):
                v = pj[:, hd * DIFF_V_DIM:(hd + 1) * DIFF_V_DIM]
                vh_ref[0, hd] = jnp.concatenate([v, ones], axis=1).astype(BF16)


def _inproj(x, g, mods, w_bf, wf, rope_tabs, mod_row, rope, tm, key_off, kv_prev):
    b, s, d = x.shape
    n_keys = kv_prev[0].shape[2]
    off = key_off // tm
    mod_idx = (lambda bi, i: (bi, 0, 0)) if mod_row is None else (lambda bi, i: (mod_row, 0, 0))
    tab_spec = pl.BlockSpec((tm, 128), lambda bi, i: (i, 0))
    seq_spec = lambda width: pl.BlockSpec((1, tm, width), lambda bi, i: (bi, i, 0))
    in_specs = [seq_spec(d),
                pl.BlockSpec((1, d), lambda bi, i: (0, 0)),
                pl.BlockSpec((1, N_MOD, d), mod_idx),
                pl.BlockSpec((d, IN_DIM), lambda bi, i: (0, 0)),
                pl.BlockSpec((CB, 2 * CB), lambda bi, i: (0, 0)),
                tab_spec, tab_spec, tab_spec]
    args = [x, g.reshape(1, d), mods, w_bf, wf, *rope_tabs]
    aliases = {len(args): 3, len(args) + 1: 4}
    in_specs += [pl.BlockSpec(memory_space=pl.ANY)] * 2
    args += list(kv_prev)
    return pl.pallas_call(
        functools.partial(_inproj_kernel, rope=rope),
        grid=(b, s // tm),
        in_specs=in_specs,
        out_specs=[seq_spec(IN_DIM), seq_spec(CB), seq_spec(CB),
                   pl.BlockSpec((1, CB, tm), lambda bi, i: (bi, 0, i + off)),
                   pl.BlockSpec((1, DIFF_HEADS, tm, LANES), lambda bi, i: (bi, 0, i + off, 0))],
        out_shape=[jax.ShapeDtypeStruct((b, s, IN_DIM), BF16),
                   jax.ShapeDtypeStruct((b, s, CB), BF16),
                   jax.ShapeDtypeStruct((b, s, CB), BF16),
                   jax.ShapeDtypeStruct((b, CB, n_keys), BF16),
                   jax.ShapeDtypeStruct((b, DIFF_HEADS, n_keys, LANES), BF16)],
        input_output_aliases=aliases,
        compiler_params=_params("parallel", "arbitrary"),
    )(*args)


def _channel_dft_matrix():
    c = np.arange(FNET_GROUP_DIM)
    ang = 2.0 * np.pi * ((c[:, None] * c[None, :]) % FNET_GROUP_DIM) / FNET_GROUP_DIM
    eye = np.eye(CB // FNET_GROUP_DIM)
    m = np.concatenate([np.kron(eye, np.cos(ang)), -np.kron(eye, np.sin(ang))], axis=1)
    return jnp.asarray(m, BF16)


def _rope_tables(s):
    half = DIFF_QK_DIM // 2
    inv = 1.0 / (ROPE_BASE ** (jnp.arange(0, half, 2, dtype=F32) / half))
    lane = np.arange(LANES) % DIFF_QK_DIM
    is_col = (lane // half == 1)[None, :]
    is_x2 = (lane % half // (half // 2) == 1)[None, :]
    t = jnp.arange(s)[:, None]
    pos = jnp.where(is_col, t % GRID_W, t // GRID_W).astype(F32)
    ang = pos * inv[lane % (half // 2)][None, :]
    sin = jnp.sin(ang)
    s1 = jnp.where(is_x2, 0.0, -sin)
    s2 = jnp.where(is_x2, sin, 0.0)
    return jnp.cos(ang), s1, s2


def _fft1_kernel(ur_ref, ui_ref, w_ref, ct_ref, st_ref, ar_ref, ai_ref):
    n1 = ur_ref.shape[1]
    u = jnp.concatenate([ur_ref[0], ui_ref[0]], axis=0)
    a = _dot(w_ref[...], u)
    ar, ai = a[:n1], a[n1:]
    ct, st = ct_ref[...], st_ref[...]
    ar_ref[0] = (ar * ct + ai * st).astype(BF16)
    ai_ref[0] = (ai * ct - ar * st).astype(BF16)


def _fft2_kernel(ar_ref, ai_ref, w_ref, y_ref, *, norm):
    for j in range(ar_ref.shape[1]):
        a = jnp.concatenate([ar_ref[0, j], ai_ref[0, j]], axis=0)
        y_ref[0, j] = (_dot(w_ref[...], a) * norm).astype(BF16)


def _dft_cos_sin(n):
    k = np.arange(n)
    ang = 2.0 * np.pi * ((k[:, None] * k[None, :]) % n) / n
    return np.cos(ang), np.sin(ang)


def _fourier_latent(ur, ui):
    b, s, cb = ur.shape
    n1, n2 = FFT_N1, s // FFT_N1
    c1, s1 = _dft_cos_sin(n1)
    w1 = jnp.asarray(np.block([[c1, s1], [-s1, c1]]), BF16)
    c2, s2 = _dft_cos_sin(n2)
    w2 = jnp.asarray(np.concatenate([c2, s2], axis=1), BF16)
    tw = 2.0 * np.pi * (np.arange(n1)[:, None] * np.arange(n2)[None, :]) / s
    ct = jnp.asarray(np.repeat(np.cos(tw), cb, axis=1), F32)
    st = jnp.asarray(np.repeat(np.sin(tw), cb, axis=1), F32)
    lanes = n2 * cb
    tn = min(lanes, 4096)
    u_spec = pl.BlockSpec((1, n1, tn), lambda j, bi: (bi, 0, j))
    t_spec = pl.BlockSpec((n1, tn), lambda j, bi: (0, j))
    ar, ai = pl.pallas_call(
        _fft1_kernel,
        grid=(lanes // tn, b),
        in_specs=[u_spec, u_spec, pl.BlockSpec((2 * n1, 2 * n1), lambda j, bi: (0, 0)), t_spec, t_spec],
        out_specs=[u_spec, u_spec],
        out_shape=[jax.ShapeDtypeStruct((b, n1, lanes), BF16)] * 2,
        compiler_params=_params("arbitrary", "arbitrary"),
    )(ur.reshape(b, n1, lanes), ui.reshape(b, n1, lanes), w1, ct, st)
    kc = 8
    a_spec = pl.BlockSpec((1, kc, n2, cb), lambda bi, j: (bi, j, 0, 0))
    y = pl.pallas_call(
        functools.partial(_fft2_kernel, norm=1.0 / math.sqrt(s * FNET_GROUP_DIM)),
        grid=(b, n1 // kc),
        in_specs=[a_spec, a_spec, pl.BlockSpec((n2, 2 * n2), lambda bi, j: (0, 0))],
        out_specs=a_spec,
        out_shape=jax.ShapeDtypeStruct((b, n1, n2, cb), BF16),
        compiler_params=_params("parallel", "arbitrary"),
    )(ar.reshape(b, n1, n2, cb), ai.reshape(b, n1, n2, cb), w2)
    return jnp.transpose(y, (0, 2, 1, 3)).reshape(b, s, cb)


def _na_kernel(q_ref, k_ref, v_ref, kc_ref, vc_ref, bias_ref, o_ref, *, rows):
    rb = pl.program_id(1)
    kb = jnp.clip(rb * NA_QROWS - NA_WIN_ROWS // 2, 0, rows - NA_KROWS)
    nk = NA_KROWS * GRID_W
    tok0 = pl.multiple_of(kb * GRID_W, 256)
    scale = NA_HEAD_DIM ** -0.5
    outs = []
    for h in range(NA_HEADS):
        sl = slice(h * NA_HEAD_DIM, (h + 1) * NA_HEAD_DIM)
        q = q_ref[0, :, sl]
        s = _dot_nt(q, k_ref[0, pl.ds(tok0, nk), sl]) * scale + bias_ref[0, h]
        sc = _dot_nt(q, kc_ref[0, :, sl]) * scale
        m = jnp.maximum(jnp.max(s, axis=-1, keepdims=True), jnp.max(sc, axis=-1, keepdims=True))
        e = jnp.exp(s - m)
        ec = jnp.exp(sc - m)
        l = jnp.sum(e, axis=-1, keepdims=True) + jnp.sum(ec, axis=-1, keepdims=True)
        o = _dot(e.astype(BF16), v_ref[0, pl.ds(tok0, nk), sl]) + _dot(ec.astype(BF16), vc_ref[0, :, sl])
        outs.append(o / l)
    o_ref[0] = jnp.concatenate(outs, axis=1).astype(BF16)


def _na_bias_tables(rel_bias, rows):
    n_dr, n_dc = 2 * NA_WIN_ROWS - 1, 2 * NA_WIN_COLS - 1
    cq = np.arange(GRID_W)
    col_lo = np.clip(cq - NA_WIN_COLS // 2, 0, GRID_W - NA_WIN_COLS)
    col_ok = (cq[None, :] >= col_lo[:, None]) & (cq[None, :] < col_lo[:, None] + NA_WIN_COLS)
    dc_idx = np.where(col_ok, np.clip(cq[None, :] - cq[:, None] + NA_WIN_COLS - 1, 0, n_dc - 1), n_dc)
    dr_idx = []
    for r0 in (0, NA_QROWS, rows - NA_QROWS):
        kb = int(np.clip(r0 - NA_WIN_ROWS // 2, 0, rows - NA_KROWS))
        r = r0 + np.arange(NA_QROWS)
        rk = kb + np.arange(NA_KROWS)
        start = np.clip(r - NA_WIN_ROWS // 2, 0, rows - NA_WIN_ROWS)
        row_ok = (rk[None, :] >= start[:, None]) & (rk[None, :] < start[:, None] + NA_WIN_ROWS)
        dr_idx.append(np.where(row_ok, np.clip(rk[None, :] - r[:, None] + NA_WIN_ROWS - 1, 0, n_dr - 1), n_dr))
    oh_r = jnp.asarray(np.stack(dr_idx)[..., None] == np.arange(n_dr + 1), F32)
    oh_c = jnp.asarray(dc_idx[..., None] == np.arange(n_dc + 1), F32)
    bias = jnp.pad(rel_bias.astype(F32), ((0, 0), (0, 1), (0, 1)), constant_values=NA_MASKED)
    by_row = jnp.einsum('tqka,hab->thqkb', oh_r, bias, precision=lax.Precision.HIGHEST)
    tabs = jnp.einsum('thqkb,cdb->thqckd', by_row, oh_c, precision=lax.Precision.HIGHEST)
    return tabs.reshape(3, NA_HEADS, NA_QROWS * GRID_W, NA_KROWS * GRID_W)


def _na_latent(p, pc, bias_tabs):
    b, s, _ = p.shape
    l = pc.shape[1]
    rows = s // GRID_W
    nrb = rows // NA_QROWS
    tq = NA_QROWS * GRID_W
    nk = NA_KROWS * GRID_W
    return pl.pallas_call(
        functools.partial(_na_kernel, rows=rows),
        grid=(b, nrb),
        in_specs=[pl.BlockSpec((1, tq, CB), lambda bi, i: (bi, i, COL_BQ)),
                  pl.BlockSpec((1, s, CB), lambda bi, i: (bi, 0, COL_BK)),
                  pl.BlockSpec((1, s, CB), lambda bi, i: (bi, 0, COL_BV)),
                  pl.BlockSpec((1, l, CB), lambda bi, i: (bi, 0, COL_BK)),
                  pl.BlockSpec((1, l, CB), lambda bi, i: (bi, 0, COL_BV)),
                  pl.BlockSpec((1, NA_HEADS, tq, nk),
                               lambda bi, i: (jnp.minimum(i, 1) + (i == nrb - 1).astype(jnp.int32), 0, 0, 0))],
        out_specs=pl.BlockSpec((1, tq, CB), lambda bi, i: (bi, i, 0)),
        out_shape=jax.ShapeDtypeStruct((b, s, CB), BF16),
        compiler_params=_params("parallel", "arbitrary"),
    )(p, p, p, pc, pc, bias_tabs)


def _diff_lambda(lp, lam_init):
    return (jnp.exp(jnp.sum(lp[0:1] * lp[1:2], axis=-1, keepdims=True))
            - jnp.exp(jnp.sum(lp[2:3] * lp[3:4], axis=-1, keepdims=True)) + lam_init)


def _diff_finish(o, g, lam_init):
    y = o * lax.rsqrt(jnp.mean(o * o, axis=-1, keepdims=True) + EPS)
    return y * g * (1.0 - lam_init)


def _diff_kernel(q_ref, kt_ref, v_ref, lp_ref, g_ref, o_ref, m_sc, acc_sc, e_sc, *, tk, lam_init):
    tq = q_ref.shape[1]
    nk = kt_ref.shape[2] // tk
    m_sc[...] = jnp.full(m_sc.shape, -jnp.inf, F32)
    acc_sc[...] = jnp.zeros(acc_sc.shape, F32)

    def body(c, carry):
        k0 = pl.multiple_of(c * tk, LANES)
        for h in range(DIFF_HEADS):
            for m in range(2):
                rows = slice(m * tq, (m + 1) * tq)
                dims = slice((2 * h + m) * DIFF_QK_DIM, (2 * h + m + 1) * DIFF_QK_DIM)
                s = _dot(q_ref[0, :, dims], kt_ref[0, dims, pl.ds(k0, tk)])
                m_old = m_sc[h, rows]
                m_new = jnp.maximum(m_old, jnp.max(s, axis=-1, keepdims=True))
                e_sc[h, rows] = jnp.exp2(s - m_new[:, :1]).astype(BF16)
                acc_sc[h, rows] = jnp.exp2(m_old - m_new) * acc_sc[h, rows]
                m_sc[h, rows] = m_new
            acc_sc[h] += _dot(e_sc[h], v_ref[0, h, pl.ds(k0, tk), :])
        return carry

    lax.fori_loop(0, nk, body, 0)
    lam = _diff_lambda(lp_ref[...], lam_init)
    outs = []
    for h in range(DIFF_HEADS):
        acc = acc_sc[h]
        o = acc[:, :DIFF_V_DIM] / acc[:, DIFF_V_DIM:DIFF_V_DIM + 1]
        outs.append(_diff_finish(o[:tq] - lam * o[tq:], g_ref[...], lam_init))
    o_ref[0] = jnp.concatenate(outs, axis=1).astype(BF16)


def _key_tile(nkeys, cap):
    return max(t for t in range(128, cap + 1, 128) if nkeys % t == 0)


def _diff_latent(p, kt, vh, lp, sub_g, lam_init, tq, tk):
    b, s, _ = p.shape
    nkeys = kt.shape[2]
    return pl.pallas_call(
        functools.partial(_diff_kernel, tk=tk, lam_init=lam_init),
        grid=(b, s // tq),
        in_specs=[pl.BlockSpec((1, tq, CB), lambda bi, i: (bi, i, COL_DQ)),
                  pl.BlockSpec((1, CB, nkeys), lambda bi, i: (bi, 0, 0), pipeline_mode=pl.Buffered(1)),
                  pl.BlockSpec((1, DIFF_HEADS, nkeys, LANES), lambda bi, i: (bi, 0, 0, 0),
                               pipeline_mode=pl.Buffered(1)),
                  pl.BlockSpec((4, DIFF_QK_DIM), lambda bi, i: (0, 0)),
                  pl.BlockSpec((1, DIFF_V_DIM), lambda bi, i: (0, 0))],
        out_specs=pl.BlockSpec((1, tq, CB), lambda bi, i: (bi, i, 0)),
        out_shape=jax.ShapeDtypeStruct((b, s, CB), BF16),
        scratch_shapes=[pltpu.VMEM((DIFF_HEADS, 2 * tq, LANES), F32),
                        pltpu.VMEM((DIFF_HEADS, 2 * tq, LANES), F32),
                        pltpu.VMEM((DIFF_HEADS, 2 * tq, tk), BF16)],
        compiler_params=_params("parallel", "arbitrary"),
    )(p, kt, vh, lp, sub_g.reshape(1, DIFF_V_DIM))


def _softmax_rows(s):
    e = jnp.exp(s - jnp.max(s, axis=-1, keepdims=True))
    return e / jnp.sum(e, axis=-1, keepdims=True)


def _ctx_kernel(pc_ref, ur_ref, ui_ref, wf_ref, lp_ref, g_ref, ya_ref, yb_ref, yd_ref, *, lam_init):
    l = pc_ref.shape[1]
    col = lambda j, lo, hi: pc_ref[0, :, j * CB + lo:j * CB + hi]
    u = jnp.concatenate([ur_ref[0], ui_ref[0]], axis=0)
    ya_ref[0] = (_dot(wf_ref[...], u) * (1.0 / math.sqrt(l * FNET_GROUP_DIM))).astype(BF16)
    outs = []
    for h in range(NA_HEADS):
        lo, hi = h * NA_HEAD_DIM, (h + 1) * NA_HEAD_DIM
        pr = _softmax_rows(_dot_nt(col(COL_BQ, lo, hi), col(COL_BK, lo, hi)) * NA_HEAD_DIM ** -0.5)
        outs.append(_dot(pr.astype(BF16), col(COL_BV, lo, hi)))
    yb_ref[0] = jnp.concatenate(outs, axis=1).astype(BF16)
    lam = _diff_lambda(lp_ref[...], lam_init)
    outs = []
    for h in range(DIFF_HEADS):
        pm = []
        for m in range(2):
            lo = (2 * h + m) * DIFF_QK_DIM
            pm.append(_softmax_rows(_dot_nt(col(COL_DQ, lo, lo + DIFF_QK_DIM), col(COL_DK, lo, lo + DIFF_QK_DIM))
                                    * DIFF_QK_DIM ** -0.5))
        a = (pm[0] - lam * pm[1]).astype(BF16)
        o = _dot(a, col(COL_DV, h * DIFF_V_DIM, (h + 1) * DIFF_V_DIM))
        outs.append(_diff_finish(o, g_ref[...], lam_init))
    yd_ref[0] = jnp.concatenate(outs, axis=1).astype(BF16)


def _ctx_branches(pc, ucr, uci, lp, sub_g, lam_init):
    b, l, _ = pc.shape
    c, s = _dft_cos_sin(l)
    wf = jnp.asarray(np.concatenate([c, s], axis=1), BF16)
    y_spec = pl.BlockSpec((1, l, CB), lambda bi: (bi, 0, 0))
    return pl.pallas_call(
        functools.partial(_ctx_kernel, lam_init=lam_init),
        grid=(b,),
        in_specs=[pl.BlockSpec((1, l, IN_DIM), lambda bi: (bi, 0, 0)), y_spec, y_spec,
                  pl.BlockSpec((l, 2 * l), lambda bi: (0, 0)),
                  pl.BlockSpec((4, DIFF_QK_DIM), lambda bi: (0, 0)),
                  pl.BlockSpec((1, DIFF_V_DIM), lambda bi: (0, 0))],
        out_specs=[y_spec] * 3,
        out_shape=[jax.ShapeDtypeStruct((b, l, CB), BF16)] * 3,
        compiler_params=_params("parallel"),
    )(pc, ucr, uci, wf, lp, sub_g.reshape(1, DIFF_V_DIM))


def _merge_kernel(x_ref, mod_ref, g1_ref, g2_ref, ya_ref, yb_ref, yd_ref, pb_ref, pc_ref, px_ref,
                  cp_ref, xp_ref, cn_ref, xn_ref, cw_ref, wg_ref, wb_ref, wo_ref, h2_prev,
                  xo_ref, h2_ref):
    del h2_prev
    i = pl.program_id(1)
    last = pl.num_programs(1) - 1
    tm = x_ref.shape[1]
    mod = mod_ref[0]
    x = x_ref[0]
    h = _norm_mod(x, g1_ref[...], mod[0:1], mod[1:2]).astype(BF16)

    u = pc_ref[0].astype(F32) * px_ref[0].astype(F32)
    up = cp_ref[0, BF16_SUBLANES - 1:, :].astype(F32) * xp_ref[0, BF16_SUBLANES - 1:, :].astype(F32)
    un = cn_ref[0, :1, :].astype(F32) * xn_ref[0, :1, :].astype(F32)
    up = jnp.where(i == 0, 0.0, up)
    un = jnp.where(i == last, 0.0, un)
    rid = lax.broadcasted_iota(jnp.int32, u.shape, 0)
    u_prev = jnp.where(rid == 0, up, pltpu.roll(u, 1, 0))
    u_next = jnp.where(rid == tm - 1, un, pltpu.roll(u, tm - 1, 0))
    cw = cw_ref[...]
    yc = pb_ref[0].astype(F32) * (cw[0:1] * u_prev + cw[1:2] * u + cw[2:3] * u_next)

    branches = (ya_ref[0], yb_ref[0], yc.astype(BF16), yd_ref[0])
    d = x.shape[1]
    out = None
    for n in range(d // CB):
        cols = slice(n * CB, (n + 1) * CB)
        merged = None
        for j in range(N_BRANCHES):
            t = jax.nn.sigmoid(_dot(h, wg_ref[j, :, cols])) * _dot(branches[j], wb_ref[j, :, cols])
            merged = t if merged is None else merged + t
        t = _dot(merged.astype(BF16), wo_ref[cols, :])
        out = t if out is None else out + t
    xn = x + mod[2:3] * out
    xo_ref[0] = xn
    h2 = _norm_mod(xn, g2_ref[...], mod[3:4], mod[4:5]).astype(BF16)
    h2_ref[0] = h2
    h2_ref[1] = h2


def _merge(x, mods, mod_row, g1, g2, ya, yb, yd, p, conv_w, wg, wb, wo, tm, h2_buf, tok_off):
    b, s, d = x.shape
    off = tok_off // tm
    per = s // tm
    hb = tm // BF16_SUBLANES
    n_halo = s // BF16_SUBLANES
    mod_idx = (lambda bi, i: (bi, 0, 0)) if mod_row is None else (lambda bi, i: (mod_row, 0, 0))
    seq = lambda width, col=0: pl.BlockSpec((1, tm, width), lambda bi, i: (bi, i, col))
    prev = lambda col: pl.BlockSpec((1, BF16_SUBLANES, CB), lambda bi, i: (bi, jnp.maximum(i * hb - 1, 0), col))
    nxt = lambda col: pl.BlockSpec((1, BF16_SUBLANES, CB),
                                   lambda bi, i: (bi, jnp.minimum((i + 1) * hb, n_halo - 1), col))
    const = lambda shape: pl.BlockSpec(shape, lambda bi, i: (0,) * len(shape))
    return pl.pallas_call(
        _merge_kernel,
        grid=(b, s // tm),
        in_specs=[seq(d), pl.BlockSpec((1, N_MOD, d), mod_idx), const((1, d)), const((1, d)),
                  seq(CB), seq(CB), seq(CB),
                  seq(CB, COL_CB), seq(CB, COL_CC), seq(CB, COL_CX),
                  prev(COL_CC), prev(COL_CX), nxt(COL_CC), nxt(COL_CX),
                  const((CONV_WIDTH, CB)),
                  const((N_BRANCHES, d, d)), const((N_BRANCHES, BRANCH_DIM, d)), const((d, d)),
                  pl.BlockSpec(memory_space=pl.ANY)],
        out_specs=[seq(d), pl.BlockSpec((2, tm, d), lambda bi, i: (0, off + bi * per + i, 0))],
        out_shape=[jax.ShapeDtypeStruct((b, s, d), F32),
                   jax.ShapeDtypeStruct(h2_buf.shape, BF16)],
        input_output_aliases={18: 1},
        compiler_params=_params("parallel", "arbitrary"),
    )(x, mods, g1.reshape(1, d), g2.reshape(1, d), ya, yb, yd, p, p, p, p, p, p, p,
      conv_w, wg, wb, wo, h2_buf)


def _route_kernel(h2_ref, wr_ref, bias_ref, tri_ref, ones_ref, idx_ref, w_ref, rank_ref, cnt_ref,
                  score_sc, sel_sc, carry_sc):
    i = pl.program_id(0)
    tm = h2_ref.shape[0]
    ne = wr_ref.shape[0]
    gsz = ne // N_GROUPS
    n_chunks = tm // LANES

    @pl.when(i == 0)
    def _():
        carry_sc[...] = jnp.zeros(carry_sc.shape, F32)

    score_sc[...] = jax.nn.sigmoid(_dot_nt(wr_ref[...], h2_ref[...]))

    def select(cidx, carry):
        c0 = pl.multiple_of(cidx * LANES, LANES)
        scores = score_sc[:, pl.ds(c0, LANES)]
        biased = scores + bias_ref[...]
        liota = lax.broadcasted_iota(jnp.int32, (gsz, LANES), 0)
        gs = []
        for g in range(N_GROUPS):
            v = biased[g * gsz:(g + 1) * gsz]
            m1 = jnp.max(v, axis=0, keepdims=True)
            i1 = jnp.min(jnp.where(v == m1, liota, gsz), axis=0, keepdims=True)
            m2 = jnp.max(jnp.where(liota == i1, -jnp.inf, v), axis=0, keepdims=True)
            gs.append(m1 + m2)
        gsm = jnp.concatenate(gs, axis=0)
        giota = lax.broadcasted_iota(jnp.int32, gsm.shape, 0)
        keep = jnp.zeros(gsm.shape, F32)
        for _ in range(TOPK_GROUPS):
            m = jnp.max(gsm, axis=0, keepdims=True)
            gi = jnp.min(jnp.where(gsm == m, giota, N_GROUPS), axis=0, keepdims=True)
            hit = giota == gi
            keep = jnp.where(hit, 1.0, keep)
            gsm = jnp.where(hit, -jnp.inf, gsm)
        cur = jnp.concatenate(
            [jnp.where(jnp.broadcast_to(keep[g:g + 1], (gsz, LANES)) > 0.0, biased[g * gsz:(g + 1) * gsz], -jnp.inf)
             for g in range(N_GROUPS)], axis=0)
        eiota = lax.broadcasted_iota(jnp.int32, (ne, LANES), 0)
        sel = jnp.zeros((ne, LANES), F32)
        idxs, ws = [], []
        for _ in range(TOP_K):
            m = jnp.max(cur, axis=0, keepdims=True)
            ik = jnp.min(jnp.where(cur == m, eiota, ne), axis=0, keepdims=True)
            hit = eiota == ik
            cur = jnp.where(hit, -jnp.inf, cur)
            ws.append(jnp.sum(jnp.where(hit, scores, 0.0), axis=0, keepdims=True))
            idxs.append(ik)
            sel = jnp.where(hit, 1.0, sel)
        w = jnp.concatenate(ws, axis=0)
        w_ref[:, pl.ds(c0, LANES)] = w / jnp.sum(w, axis=0, keepdims=True) * ROUTED_SCALE
        idx_ref[:, pl.ds(c0, LANES)] = jnp.concatenate(idxs, axis=0)
        sel_sc[:, pl.ds(c0, LANES)] = sel.astype(BF16)
        return carry

    lax.fori_loop(0, n_chunks, select, 0)

    sel_all = sel_sc[...]
    score_sc[...] = _dot(sel_all, tri_ref[...]) + jnp.concatenate([carry_sc[...]] * n_chunks, axis=1)

    def ranks(cidx, carry):
        c0 = pl.multiple_of(cidx * LANES, LANES)
        before = score_sc[:, pl.ds(c0, LANES)]
        idx = idx_ref[:, pl.ds(c0, LANES)]
        eiota = lax.broadcasted_iota(jnp.int32, (ne, LANES), 0)
        rows = [jnp.sum(jnp.where(eiota == idx[k:k + 1], before, 0.0), axis=0, keepdims=True)
                for k in range(TOP_K)]
        rank_ref[:, pl.ds(c0, LANES)] = jnp.concatenate(rows, axis=0).astype(jnp.int32)
        return carry

    lax.fori_loop(0, n_chunks, ranks, 0)
    carry_sc[...] += _dot(sel_all, ones_ref[...])
    cnt_ref[...] = carry_sc[...]


def _route(h2_all, n, wr_t, bias, tm):
    d = h2_all.shape[1]
    ne = wr_t.shape[0]
    tri = jnp.asarray(np.triu(np.ones((tm, tm), np.float32), 1), BF16)
    ones = jnp.ones((tm, LANES), BF16)
    bias_b = jnp.broadcast_to(bias.astype(F32)[:, None], (ne, LANES))
    const = lambda shape: pl.BlockSpec(shape, lambda i: (0,) * len(shape))
    tok = pl.BlockSpec((TOP_K, tm), lambda i: (0, i))
    return pl.pallas_call(
        _route_kernel,
        grid=(n // tm,),
        in_specs=[pl.BlockSpec((tm, d), lambda i: (i, 0)), const((ne, d)), const((ne, LANES)),
                  const((tm, tm)), const((tm, LANES))],
        out_specs=[tok, tok, tok, const((ne, LANES))],
        out_shape=[jax.ShapeDtypeStruct((TOP_K, n), jnp.int32),
                   jax.ShapeDtypeStruct((TOP_K, n), F32),
                   jax.ShapeDtypeStruct((TOP_K, n), jnp.int32),
                   jax.ShapeDtypeStruct((ne, LANES), F32)],
        scratch_shapes=[pltpu.VMEM((ne, tm), F32), pltpu.VMEM((ne, tm), BF16), pltpu.VMEM((ne, LANES), F32)],
        compiler_params=_params("arbitrary"),
    )(h2_all, wr_t, bias_b, tri, ones)


def _pos_kernel(idx_ref, rank_ref, start_ref, pos_ref):
    ne = start_ref.shape[0]
    start = start_ref[...]

    def body(cidx, carry):
        c0 = pl.multiple_of(cidx * LANES, LANES)
        idx = idx_ref[:, pl.ds(c0, LANES)]
        eiota = lax.broadcasted_iota(jnp.int32, (ne, LANES), 0)
        rows = [jnp.sum(jnp.where(eiota == idx[k:k + 1], start, 0.0), axis=0, keepdims=True)
                for k in range(TOP_K)]
        pos_ref[:, pl.ds(c0, LANES)] = jnp.concatenate(rows, axis=0).astype(jnp.int32) + rank_ref[:, pl.ds(c0, LANES)]
        return carry

    lax.fori_loop(0, idx_ref.shape[1] // LANES, body, 0)


def _positions(idx, rank, start_rows, tm):
    k, n = idx.shape
    ne = start_rows.shape[0]
    start_b = jnp.broadcast_to(start_rows.astype(F32)[:, None], (ne, LANES))
    tok = pl.BlockSpec((k, tm), lambda i: (0, i))
    return pl.pallas_call(
        _pos_kernel,
        grid=(n // tm,),
        in_specs=[tok, tok, pl.BlockSpec((ne, LANES), lambda i: (0, 0))],
        out_specs=tok,
        out_shape=jax.ShapeDtypeStruct((k, n), jnp.int32),
        compiler_params=_params("parallel"),
    )(idx, rank, start_b)


def _rowtok_kernel(start_ref, cnt_ref, pos_ref, out_ref, *, n_tokens):
    i = pl.program_id(0)
    tc = pos_ref.shape[0] // (TOP_K * LANES)
    n_rows = out_ref.shape[0]
    ne = start_ref.shape[0]

    @pl.when(i == 0)
    def _():
        def gaps(e, carry):
            lo = start_ref[e] + cnt_ref[e]
            hi = jnp.where(e + 1 < ne, start_ref[jnp.minimum(e + 1, ne - 1)], n_rows)

            def one(r, c):
                out_ref[r] = lax.rem(r, n_tokens)
                return c

            return lax.fori_loop(lo, hi, one, carry)

        lax.fori_loop(0, ne, gaps, 0)

    def tile(c, carry):
        t0 = (i * tc + c) * LANES

        def lane(l, carry2):
            src = c * (TOP_K * LANES) + l
            for k in range(TOP_K):
                out_ref[pos_ref[src + k * LANES]] = t0 + l
            return carry2

        return lax.fori_loop(0, LANES, lane, carry, unroll=4)

    lax.fori_loop(0, tc, tile, 0)


def _row_tokens(pos, seg_start, counts, n_rows):
    k, n = pos.shape
    n_tiles = n // LANES
    tc = max(t for t in range(1, 16 + 1) if n_tiles % t == 0)
    pos_tiles = jnp.transpose(pos.reshape(k, n_tiles, LANES), (1, 0, 2)).reshape(-1)
    grid_spec = pltpu.PrefetchScalarGridSpec(
        num_scalar_prefetch=2,
        grid=(n_tiles // tc,),
        in_specs=[pl.BlockSpec((tc * k * LANES,), lambda i, st, ct: (i,), memory_space=pltpu.SMEM)],
        out_specs=pl.BlockSpec((n_rows,), lambda i, st, ct: (0,), memory_space=pltpu.SMEM))
    return pl.pallas_call(
        functools.partial(_rowtok_kernel, n_tokens=n),
        grid_spec=grid_spec,
        out_shape=jax.ShapeDtypeStruct((n_rows,), jnp.int32),
        compiler_params=_params("arbitrary"),
    )(seg_start, counts, pos_tiles)


def _expert_kernel(crow_ref, crun_ref, rexp_ref, meta_ref, x_hbm, wg_hbm, wu_hbm, wd_hbm, y_hbm,
                   wg_buf, wu_buf, wd_buf, wg_bf, wu_bf, wd_bf, x_buf, y_buf, zero_buf,
                   w_sem, x_sem, y_sem, z_sem, *, layer):
    n_chunks, n_runs, tail_start = meta_ref[0], meta_ref[1], meta_ref[2]
    n_rows = x_hbm.shape[0]

    def w_copies(run):
        expert, s = rexp_ref[run], run % MOE_W_SLOTS
        return (pltpu.make_async_copy(wg_hbm.at[layer, expert], wg_buf.at[s], w_sem.at[s, 0]),
                pltpu.make_async_copy(wu_hbm.at[layer, expert], wu_buf.at[s], w_sem.at[s, 1]),
                pltpu.make_async_copy(wd_hbm.at[layer, expert], wd_buf.at[s], w_sem.at[s, 2]))

    def x_copy(c):
        rows = pl.ds(pl.multiple_of(crow_ref[c], MOE_ALIGN), MOE_ROWS)
        return pltpu.make_async_copy(x_hbm.at[rows], x_buf.at[c % MOE_X_SLOTS], x_sem.at[c % MOE_X_SLOTS])

    def y_copy(c):
        rows = pl.ds(pl.multiple_of(crow_ref[c], MOE_ALIGN), MOE_ROWS)
        return pltpu.make_async_copy(y_buf.at[c % 2], y_hbm.at[rows], y_sem.at[c % 2])

    def zero_copy(row0):
        return pltpu.make_async_copy(zero_buf, y_hbm.at[pl.ds(row0, MOE_ALIGN)], z_sem.at[0])

    for r in range(MOE_W_SLOTS - 1):
        @pl.when(r < n_runs)
        def _(r=r):
            for cp in w_copies(r):
                cp.start()

    for c in range(MOE_X_SLOTS - 1):
        @pl.when(c < n_chunks)
        def _(c=c):
            x_copy(c).start()

    def chunk(c, carry):
        run = crun_ref[c]
        first = jnp.logical_or(c == 0, crun_ref[jnp.maximum(c - 1, 0)] != run)

        @pl.when(c + MOE_X_SLOTS - 1 < n_chunks)
        def _():
            x_copy(c + MOE_X_SLOTS - 1).start()

        @pl.when(first)
        def _():
            @pl.when(run + MOE_W_SLOTS - 1 < n_runs)
            def _():
                for cp in w_copies(run + MOE_W_SLOTS - 1):
                    cp.start(priority=1)

            for cp in w_copies(run):
                cp.wait()
            s = run % MOE_W_SLOTS
            wg_bf[...] = wg_buf[s].astype(BF16)
            wu_bf[...] = wu_buf[s].astype(BF16)
            wd_bf[...] = wd_buf[s].astype(BF16)

        x_copy(c).wait()
        x = x_buf[c % MOE_X_SLOTS]
        a = (_silu(_dot(x, wg_bf[...])) * _dot(x, wu_bf[...])).astype(BF16)
        y = _dot(a, wd_bf[...]).astype(BF16)

        @pl.when(c > 0)
        def _():
            y_copy(c - 1).wait()

        y_buf[c % 2] = y
        y_copy(c).start()
        return carry

    lax.fori_loop(0, n_chunks, chunk, 0)

    @pl.when(n_chunks > 0)
    def _():
        y_copy(n_chunks - 1).wait()

    zero_buf[...] = jnp.zeros(zero_buf.shape, zero_buf.dtype)
    n_tail = (n_rows - tail_start) // MOE_ALIGN

    def fill(t, carry):
        zero_copy(pl.multiple_of(tail_start + t * MOE_ALIGN, MOE_ALIGN)).start()
        return carry

    def drain(t, carry):
        zero_copy(0).wait()
        return carry

    lax.fori_loop(0, n_tail, fill, 0)
    lax.fori_loop(0, n_tail, drain, 0)


def _experts(x_sorted, seg_start, counts, layer, w_g, w_u, w_d):
    n_rows, d = x_sorted.shape
    ne, f = w_g.shape[1], w_g.shape[3]
    nch = (counts + MOE_ROWS - 1) // MOE_ROWS
    c_end = jnp.cumsum(nch)
    max_chunks = (n_rows - MOE_ROWS) // MOE_ROWS + ne
    g = jnp.arange(max_chunks, dtype=jnp.int32)
    c_exp = jnp.minimum(jnp.sum((c_end[None, :] <= g[:, None]).astype(jnp.int32), axis=1), ne - 1)
    c_row = jnp.where(g < c_end[-1], seg_start[c_exp] + (g - (c_end - nch)[c_exp]) * MOE_ROWS, 0)
    has_rows = (nch > 0).astype(jnp.int32)
    run_end = jnp.cumsum(has_rows)
    c_run = (run_end - 1)[c_exp]
    r = jnp.arange(ne, dtype=jnp.int32)
    r_exp = jnp.minimum(jnp.sum((run_end[None, :] <= r[:, None]).astype(jnp.int32), axis=1), ne - 1)
    tail_start = jnp.max(jnp.where(nch > 0, seg_start + nch * MOE_ROWS, 0))
    meta = jnp.stack([c_end[-1], run_end[-1], tail_start]).astype(jnp.int32)
    any_spec = pl.BlockSpec(memory_space=pl.ANY)
    grid_spec = pltpu.PrefetchScalarGridSpec(
        num_scalar_prefetch=4,
        grid=(1,),
        in_specs=[any_spec] * 4,
        out_specs=any_spec,
        scratch_shapes=[pltpu.VMEM((MOE_W_SLOTS, d, f), F32), pltpu.VMEM((MOE_W_SLOTS, d, f), F32),
                        pltpu.VMEM((MOE_W_SLOTS, f, d), F32),
                        pltpu.VMEM((d, f), BF16), pltpu.VMEM((d, f), BF16), pltpu.VMEM((f, d), BF16),
                        pltpu.VMEM((MOE_X_SLOTS, MOE_ROWS, d), BF16), pltpu.VMEM((2, MOE_ROWS, d), BF16),
                        pltpu.VMEM((MOE_ALIGN, d), BF16),
                        pltpu.SemaphoreType.DMA((MOE_W_SLOTS, 3)), pltpu.SemaphoreType.DMA((MOE_X_SLOTS,)),
                        pltpu.SemaphoreType.DMA((2,)), pltpu.SemaphoreType.DMA((1,))])
    return pl.pallas_call(
        functools.partial(_expert_kernel, layer=layer),
        grid_spec=grid_spec,
        out_shape=jax.ShapeDtypeStruct((n_rows, d), BF16),
        compiler_params=_params("arbitrary"),
    )(c_row.astype(jnp.int32), c_run.astype(jnp.int32), r_exp, meta, x_sorted, w_g, w_u, w_d)


def _resid_kernel(x_ref, h2_ref, y_ref, w_ref, mod_ref, sg_ref, su_ref, sd_ref, gf_ref, o_ref, *, final):
    h2 = h2_ref[...]
    a = (_silu(_dot(h2, sg_ref[...])) * _dot(h2, su_ref[...])).astype(BF16)
    y = _dot(a, sd_ref[...])
    w = w_ref[...]
    for k in range(TOP_K):
        y = y + w[:, k:k + 1] * y_ref[k].astype(F32)
    xo = x_ref[...] + mod_ref[0][5:6] * y
    if final:
        xo = xo * lax.rsqrt(jnp.mean(xo * xo, axis=-1, keepdims=True) + EPS) * gf_ref[...]
    o_ref[...] = xo


def _resid(x_flat, h2_all, y_tok, w_tok, row_off, mods, mod_row, rows_per_mod, sg, su, sd, gf, final, tm):
    n, d = x_flat.shape
    f = sg.shape[1]
    off = row_off // tm
    per = rows_per_mod // tm
    mod_idx = (lambda i: (i // per, 0, 0)) if mod_row is None else (lambda i: (mod_row, 0, 0))
    const = lambda shape: pl.BlockSpec(shape, lambda i: (0,) * len(shape))
    return pl.pallas_call(
        functools.partial(_resid_kernel, final=final),
        grid=(n // tm,),
        in_specs=[pl.BlockSpec((tm, d), lambda i: (i, 0)),
                  pl.BlockSpec((tm, d), lambda i: (i + off, 0)),
                  pl.BlockSpec((TOP_K, tm, d), lambda i: (0, i + off, 0)),
                  pl.BlockSpec((tm, TOP_K), lambda i: (i + off, 0)),
                  pl.BlockSpec((1, N_MOD, d), mod_idx),
                  const((d, f)), const((d, f)), const((f, d)), const((1, d))],
        out_specs=pl.BlockSpec((tm, d), lambda i: (i, 0)),
        out_shape=jax.ShapeDtypeStruct((n, d), F32),
        compiler_params=_params("parallel"),
    )(x_flat, h2_all, y_tok, w_tok, mods, sg, su, sd, gf.reshape(1, d))


def _moe_routed(h2_dbl, wr_t, bias, layer, w_g, w_u, w_d):
    n, d = h2_dbl.shape[0] // 2, h2_dbl.shape[1]
    idx, wts, rank, cnt = _route(h2_dbl, n, wr_t, bias, 512)
    counts = cnt[:, 0].astype(jnp.int32)
    padded = (counts + MOE_ALIGN - 1) // MOE_ALIGN * MOE_ALIGN
    seg_start = jnp.cumsum(padded) - padded
    n_rows = -(-(n * TOP_K + N_EXPERTS * (MOE_ALIGN - 1)) // MOE_ROWS) * MOE_ROWS + MOE_ROWS
    pos2 = _positions(idx, rank, seg_start, 512)
    pos = pos2.reshape(TOP_K * n)
    row_tok = _row_tokens(pos2, seg_start, counts, n_rows)
    x_sorted = h2_dbl.at[row_tok].get(mode='promise_in_bounds')
    y_sorted = _experts(x_sorted, seg_start, counts, layer, w_g, w_u, w_d)
    return y_sorted.at[pos].get(mode='promise_in_bounds').reshape(TOP_K, n, d), wts.T


def kernel(x, c, ctx, c_ctx, ada_w, ada_b, norm1_g, w_in, conv_w, na_rel_bias, diff_lambda,
           diff_subln_g, w_branch_gate, w_branch, w_out, norm2_g, router_w, router_bias,
           expert_w_gate, expert_w_up, expert_w_down, shared_w_gate, shared_w_up, shared_w_down,
           final_norm_g):
    b, s, d = x.shape
    l_ctx = ctx.shape[1]
    rows = s // GRID_W
    ctx_row = b
    cvec = jnp.zeros((8, d), F32).at[:b].set(c).at[ctx_row].set(c_ctx)
    rope_tabs = _rope_tables(s)
    wf = _channel_dft_matrix()
    tm = 512
    xc = ctx
    for layer in range(DEPTH):
        last = layer == DEPTH - 1
        lam_init = 0.8 - 0.6 * math.exp(-0.3 * layer)
        mods = _ada(cvec, ada_w, ada_b, layer).reshape(8, N_MOD, d)
        w_in_bf = w_in[layer].astype(BF16)
        wg_bf = w_branch_gate[layer].astype(BF16)
        wb_bf = w_branch[layer].astype(BF16)
        wo_bf = w_out[layer].astype(BF16)
        wr_bf = router_w[layer].T.astype(BF16)
        lp = diff_lambda[layer]
        sub_g = diff_subln_g[layer]

        kv0 = (jnp.zeros((b, CB, s + l_ctx), BF16), jnp.zeros((b, DIFF_HEADS, s + l_ctx, LANES), BF16))
        p, ur, ui, kt, vh = _inproj(x, norm1_g[layer], mods, w_in_bf, wf, rope_tabs, None, True, tm,
                                    0, kv0)
        ctx_tabs = tuple(t[:l_ctx] for t in rope_tabs)
        pc, ucr, uci, kt, vh = _inproj(xc, norm1_g[layer], mods, w_in_bf, wf, ctx_tabs, ctx_row, False,
                                       l_ctx, s, (kt, vh))

        ya = _fourier_latent(ur, ui)
        yb = _na_latent(p, pc, _na_bias_tables(na_rel_bias[layer], rows))
        yd = _diff_latent(p, kt, vh, lp, sub_g, lam_init, 1024, _key_tile(s + l_ctx, 768))
        n_tok = b * s if last else b * (s + l_ctx)
        h2_buf = jnp.zeros((2, n_tok, d), BF16)
        x, h2_buf = _merge(x, mods, None, norm1_g[layer], norm2_g[layer], ya, yb, yd, p,
                           conv_w[layer], wg_bf, wb_bf, wo_bf, tm, h2_buf, 0)
        if not last:
            yac, ybc, ydc = _ctx_branches(pc, ucr, uci, lp, sub_g, lam_init)
            xc, h2_buf = _merge(xc, mods, ctx_row, norm1_g[layer], norm2_g[layer], yac, ybc, ydc, pc,
                                conv_w[layer], wg_bf, wb_bf, wo_bf, l_ctx, h2_buf, b * s)
        h2_all = h2_buf.reshape(2 * n_tok, d)

        y_tok, w_tok = _moe_routed(h2_all, wr_bf, router_bias[layer], layer, expert_w_gate,
                                   expert_w_up, expert_w_down)
        sg_bf = shared_w_gate[layer].astype(BF16)
        su_bf = shared_w_up[layer].astype(BF16)
        sd_bf = shared_w_down[layer].astype(BF16)
        x = _resid(x.reshape(b * s, d), h2_all, y_tok, w_tok, 0, mods, None, s, sg_bf, su_bf, sd_bf,
                   final_norm_g, last, tm).reshape(b, s, d)
        if not last:
            xc = _resid(xc.reshape(b * l_ctx, d), h2_all, y_tok, w_tok, b * s, mods, ctx_row, l_ctx,
                        sg_bf, su_bf, sd_bf, final_norm_g, False, l_ctx).reshape(b, l_ctx, d)
    return x
```

```python
import functools
import math

import numpy as np
import jax
import jax.numpy as jnp
from jax import lax
from jax.experimental import pallas as pl
from jax.experimental.pallas import tpu as pltpu

F32 = jnp.float32
BF16 = jnp.bfloat16

DEPTH = 2
GRID_W = 64
EPS = 1e-6
N_MOD = 6

FNET_GROUP_DIM = 64
NA_HEADS = 4
NA_HEAD_DIM = 64
NA_WIN_ROWS = 8
NA_WIN_COLS = 16
CONV_WIDTH = 3
DIFF_HEADS = 4
DIFF_QK_DIM = 32
DIFF_V_DIM = 64
ROPE_BASE = 10000.0
N_BRANCHES = 4
BRANCH_DIM = 256

COL_A, COL_BQ, COL_BK, COL_BV, COL_CB, COL_CC, COL_CX, COL_DQ, COL_DK, COL_DV = range(10)
N_COL_BLOCKS = 10
CB = 256
IN_DIM = N_COL_BLOCKS * CB

N_EXPERTS = 256
TOP_K = 8
N_GROUPS = 8
TOPK_GROUPS = 4
ROUTED_SCALE = 2.5
LOG2_E = 1.4426950408889634

VMEM_LIMIT_BYTES = 56 * 1024 * 1024
LANES = 128
BF16_SUBLANES = 16
TOKEN_TILE = 512
DIFF_Q_TILE = 1024
DIFF_K_TILE_MAX = 768
FFT_N1 = 64
NA_MASKED = -1e30
NA_QROWS = 8
NA_KROWS = 16
MOE_ROWS = 256
MOE_ALIGN = BF16_SUBLANES
MOE_X_SLOTS = 4
MOE_W_SLOTS = 3


def _params(*sem):
    return pltpu.CompilerParams(dimension_semantics=sem, vmem_limit_bytes=VMEM_LIMIT_BYTES)


def _dot(a, b):
    return jnp.dot(a, b, preferred_element_type=F32)


def _dot_nt(a, b):
    return lax.dot_general(a, b, (((1,), (1,)), ((), ())), preferred_element_type=F32)


def _norm_mod(xf, g, shift, scale):
    y = xf * lax.rsqrt(jnp.mean(xf * xf, axis=-1, keepdims=True) + EPS)
    return (y * g) * (1.0 + scale) + shift


def _silu(v):
    return v * jax.nn.sigmoid(v)


def _ada_kernel(c_ref, w_ref, b_ref, o_ref):
    s = _silu(c_ref[...])
    o_ref[...] = _dot(s.astype(BF16), w_ref[0].astype(BF16)) + b_ref[0]


def _ada(cvec, w, b, layer):
    rows, d = cvec.shape
    depth, _, n = w.shape
    tn = 1536
    return pl.pallas_call(
        _ada_kernel,
        grid=(n // tn,),
        in_specs=[pl.BlockSpec((rows, d), lambda j: (0, 0)),
                  pl.BlockSpec((1, d, tn), lambda j: (layer, 0, j)),
                  pl.BlockSpec((1, 1, tn), lambda j: (layer, 0, j))],
        out_specs=pl.BlockSpec((rows, tn), lambda j: (0, j)),
        out_shape=jax.ShapeDtypeStruct((rows, n), F32),
        compiler_params=_params("arbitrary"),
    )(cvec, w, b.reshape(depth, 1, n))


def _inproj_kernel(x_ref, g_ref, mod_ref, w_ref, wf_ref, cos_ref, s1_ref, s2_ref, kt_prev, vh_prev,
                   p_ref, ur_ref, ui_ref, kt_ref, vh_ref, *, rope):
    del kt_prev, vh_prev
    mod = mod_ref[0]
    h = _norm_mod(x_ref[0], g_ref[...], mod[0:1], mod[1:2]).astype(BF16)
    for j in range(N_COL_BLOCKS):
        pj = _dot(h, w_ref[:, j * CB:(j + 1) * CB])
        if j == COL_A:
            u = _dot(pj.astype(BF16), wf_ref[...])
            ur_ref[0] = u[:, :CB].astype(BF16)
            ui_ref[0] = u[:, CB:].astype(BF16)
        if rope and j in (COL_DQ, COL_DK):
            cos = jnp.concatenate([cos_ref[...]] * 2, axis=1)
            s1 = jnp.concatenate([s1_ref[...]] * 2, axis=1)
            s2 = jnp.concatenate([s2_ref[...]] * 2, axis=1)
            pj = pj * cos + pltpu.roll(pj, CB - 8, 1) * s1 + pltpu.roll(pj, 8, 1) * s2
            if j == COL_DQ:
                pj = pj * (DIFF_QK_DIM ** -0.5 * LOG2_E)
        p_ref[0, :, j * CB:(j + 1) * CB] = pj.astype(BF16)
        if j == COL_DK:
            kt_ref[0] = pj.T.astype(BF16)
        if j == COL_DV:
            ones = jnp.ones((pj.shape[0], LANES - DIFF_V_DIM), F32)
            for hd in range(DIFF_HEADS):
                v = pj[:, hd * DIFF_V_DIM:(hd + 1) * DIFF_V_DIM]
                vh_ref[0, hd] = jnp.concatenate([v, ones], axis=1).astype(BF16)


def _inproj(x, g, mods, w_bf, wf, rope_tabs, mod_row, rope, tm, key_off, kv_prev):
    b, s, d = x.shape
    n_keys = kv_prev[0].shape[2]
    off = key_off // tm
    mod_idx = (lambda bi, i: (bi, 0, 0)) if mod_row is None else (lambda bi, i: (mod_row, 0, 0))
    tab_spec = pl.BlockSpec((tm, LANES), lambda bi, i: (i, 0))
    seq_spec = lambda width: pl.BlockSpec((1, tm, width), lambda bi, i: (bi, i, 0))
    in_specs = [seq_spec(d),
                pl.BlockSpec((1, d), lambda bi, i: (0, 0)),
                pl.BlockSpec((1, N_MOD, d), mod_idx),
                pl.BlockSpec((d, IN_DIM), lambda bi, i: (0, 0)),
                pl.BlockSpec((CB, 2 * CB), lambda bi, i: (0, 0)),
                tab_spec, tab_spec, tab_spec]
    args = [x, g.reshape(1, d), mods, w_bf, wf, *rope_tabs]
    aliases = {len(args): 3, len(args) + 1: 4}
    in_specs += [pl.BlockSpec(memory_space=pl.ANY)] * 2
    args += list(kv_prev)
    return pl.pallas_call(
        functools.partial(_inproj_kernel, rope=rope),
        grid=(b, s // tm),
        in_specs=in_specs,
        out_specs=[seq_spec(IN_DIM), seq_spec(CB), seq_spec(CB),
                   pl.BlockSpec((1, CB, tm), lambda bi, i: (bi, 0, i + off)),
                   pl.BlockSpec((1, DIFF_HEADS, tm, LANES), lambda bi, i: (bi, 0, i + off, 0))],
        out_shape=[jax.ShapeDtypeStruct((b, s, IN_DIM), BF16),
                   jax.ShapeDtypeStruct((b, s, CB), BF16),
                   jax.ShapeDtypeStruct((b, s, CB), BF16),
                   jax.ShapeDtypeStruct((b, CB, n_keys), BF16),
                   jax.ShapeDtypeStruct((b, DIFF_HEADS, n_keys, LANES), BF16)],
        input_output_aliases=aliases,
        compiler_params=_params("parallel", "arbitrary"),
    )(*args)


def _channel_dft_matrix():
    c = np.arange(FNET_GROUP_DIM)
    ang = 2.0 * np.pi * ((c[:, None] * c[None, :]) % FNET_GROUP_DIM) / FNET_GROUP_DIM
    eye = np.eye(CB // FNET_GROUP_DIM)
    m = np.concatenate([np.kron(eye, np.cos(ang)), -np.kron(eye, np.sin(ang))], axis=1)
    return jnp.asarray(m, BF16)


def _rope_tables(s):
    half = DIFF_QK_DIM // 2
    inv = 1.0 / (ROPE_BASE ** (jnp.arange(0, half, 2, dtype=F32) / half))
    lane = np.arange(LANES) % DIFF_QK_DIM
    is_col = (lane // half == 1)[None, :]
    is_x2 = (lane % half // (half // 2) == 1)[None, :]
    t = jnp.arange(s)[:, None]
    pos = jnp.where(is_col, t % GRID_W, t // GRID_W).astype(F32)
    ang = pos * inv[lane % (half // 2)][None, :]
    sin = jnp.sin(ang)
    s1 = jnp.where(is_x2, 0.0, -sin)
    s2 = jnp.where(is_x2, sin, 0.0)
    return jnp.cos(ang), s1, s2


def _fft1_kernel(ur_ref, ui_ref, w_ref, ct_ref, st_ref, ar_ref, ai_ref):
    n1 = ur_ref.shape[1]
    u = jnp.concatenate([ur_ref[0], ui_ref[0]], axis=0)
    a = _dot(w_ref[...], u)
    ar, ai = a[:n1], a[n1:]
    ct, st = ct_ref[...], st_ref[...]
    ar_ref[0] = (ar * ct + ai * st).astype(BF16)
    ai_ref[0] = (ai * ct - ar * st).astype(BF16)


def _fft2_kernel(ar_ref, ai_ref, w_ref, y_ref, *, norm):
    for j in range(ar_ref.shape[1]):
        a = jnp.concatenate([ar_ref[0, j], ai_ref[0, j]], axis=0)
        y_ref[0, j] = (_dot(w_ref[...], a) * norm).astype(BF16)


def _dft_cos_sin(n):
    k = np.arange(n)
    ang = 2.0 * np.pi * ((k[:, None] * k[None, :]) % n) / n
    return np.cos(ang), np.sin(ang)


def _fourier_latent(ur, ui):
    b, s, cb = ur.shape
    n1, n2 = FFT_N1, s // FFT_N1
    c1, s1 = _dft_cos_sin(n1)
    w1 = jnp.asarray(np.block([[c1, s1], [-s1, c1]]), BF16)
    c2, s2 = _dft_cos_sin(n2)
    w2 = jnp.asarray(np.concatenate([c2, s2], axis=1), BF16)
    tw = 2.0 * np.pi * (np.arange(n1)[:, None] * np.arange(n2)[None, :]) / s
    ct = jnp.asarray(np.repeat(np.cos(tw), cb, axis=1), F32)
    st = jnp.asarray(np.repeat(np.sin(tw), cb, axis=1), F32)
    lanes = n2 * cb
    tn = min(lanes, 4096)
    u_spec = pl.BlockSpec((1, n1, tn), lambda j, bi: (bi, 0, j))
    t_spec = pl.BlockSpec((n1, tn), lambda j, bi: (0, j))
    ar, ai = pl.pallas_call(
        _fft1_kernel,
        grid=(lanes // tn, b),
        in_specs=[u_spec, u_spec, pl.BlockSpec((2 * n1, 2 * n1), lambda j, bi: (0, 0)), t_spec, t_spec],
        out_specs=[u_spec, u_spec],
        out_shape=[jax.ShapeDtypeStruct((b, n1, lanes), BF16)] * 2,
        compiler_params=_params("arbitrary", "arbitrary"),
    )(ur.reshape(b, n1, lanes), ui.reshape(b, n1, lanes), w1, ct, st)
    kc = 8
    a_spec = pl.BlockSpec((1, kc, n2, cb), lambda bi, j: (bi, j, 0, 0))
    y = pl.pallas_call(
        functools.partial(_fft2_kernel, norm=1.0 / math.sqrt(s * FNET_GROUP_DIM)),
        grid=(b, n1 // kc),
        in_specs=[a_spec, a_spec, pl.BlockSpec((n2, 2 * n2), lambda bi, j: (0, 0))],
        out_specs=a_spec,
        out_shape=jax.ShapeDtypeStruct((b, n1, n2, cb), BF16),
        compiler_params=_params("parallel", "arbitrary"),
    )(ar.reshape(b, n1, n2, cb), ai.reshape(b, n1, n2, cb), w2)
    return jnp.transpose(y, (0, 2, 1, 3)).reshape(b, s, cb)


def _na_kernel(q_ref, k_ref, v_ref, kc_ref, vc_ref, bias_ref, o_ref, *, rows):
    rb = pl.program_id(1)
    kb = jnp.clip(rb * NA_QROWS - NA_WIN_ROWS // 2, 0, rows - NA_KROWS)
    nk = NA_KROWS * GRID_W
    tok0 = pl.multiple_of(kb * GRID_W, 256)
    scale = NA_HEAD_DIM ** -0.5
    outs = []
    for h in range(NA_HEADS):
        sl = slice(h * NA_HEAD_DIM, (h + 1) * NA_HEAD_DIM)
        q = q_ref[0, :, sl]
        s = _dot_nt(q, k_ref[0, pl.ds(tok0, nk), sl]) * scale + bias_ref[0, h]
        sc = _dot_nt(q, kc_ref[0, :, sl]) * scale
        m = jnp.maximum(jnp.max(s, axis=-1, keepdims=True), jnp.max(sc, axis=-1, keepdims=True))
        e = jnp.exp(s - m)
        ec = jnp.exp(sc - m)
        l = jnp.sum(e, axis=-1, keepdims=True) + jnp.sum(ec, axis=-1, keepdims=True)
        o = _dot(e.astype(BF16), v_ref[0, pl.ds(tok0, nk), sl]) + _dot(ec.astype(BF16), vc_ref[0, :, sl])
        outs.append(o / l)
    o_ref[0] = jnp.concatenate(outs, axis=1).astype(BF16)


def _na_bias_tables(rel_bias, rows):
    n_dr, n_dc = 2 * NA_WIN_ROWS - 1, 2 * NA_WIN_COLS - 1
    cq = np.arange(GRID_W)
    col_lo = np.clip(cq - NA_WIN_COLS // 2, 0, GRID_W - NA_WIN_COLS)
    col_ok = (cq[None, :] >= col_lo[:, None]) & (cq[None, :] < col_lo[:, None] + NA_WIN_COLS)
    dc_idx = np.where(col_ok, np.clip(cq[None, :] - cq[:, None] + NA_WIN_COLS - 1, 0, n_dc - 1), n_dc)
    dr_idx = []
    for r0 in (0, NA_QROWS, rows - NA_QROWS):
        kb = int(np.clip(r0 - NA_WIN_ROWS // 2, 0, rows - NA_KROWS))
        r = r0 + np.arange(NA_QROWS)
        rk = kb + np.arange(NA_KROWS)
        start = np.clip(r - NA_WIN_ROWS // 2, 0, rows - NA_WIN_ROWS)
        row_ok = (rk[None, :] >= start[:, None]) & (rk[None, :] < start[:, None] + NA_WIN_ROWS)
        dr_idx.append(np.where(row_ok, np.clip(rk[None, :] - r[:, None] + NA_WIN_ROWS - 1, 0, n_dr - 1), n_dr))
    oh_r = jnp.asarray(np.stack(dr_idx)[..., None] == np.arange(n_dr + 1), F32)
    oh_c = jnp.asarray(dc_idx[..., None] == np.arange(n_dc + 1), F32)
    bias = jnp.pad(rel_bias.astype(F32), ((0, 0), (0, 1), (0, 1)), constant_values=NA_MASKED)
    by_row = jnp.einsum('tqka,hab->thqkb', oh_r, bias, precision=lax.Precision.HIGHEST)
    tabs = jnp.einsum('thqkb,cdb->thqckd', by_row, oh_c, precision=lax.Precision.HIGHEST)
    return tabs.reshape(3, NA_HEADS, NA_QROWS * GRID_W, NA_KROWS * GRID_W)


def _na_latent(p, pc, bias_tabs):
    b, s, _ = p.shape
    l = pc.shape[1]
    rows = s // GRID_W
    nrb = rows // NA_QROWS
    tq = NA_QROWS * GRID_W
    nk = NA_KROWS * GRID_W
    return pl.pallas_call(
        functools.partial(_na_kernel, rows=rows),
        grid=(b, nrb),
        in_specs=[pl.BlockSpec((1, tq, CB), lambda bi, i: (bi, i, COL_BQ)),
                  pl.BlockSpec((1, s, CB), lambda bi, i: (bi, 0, COL_BK)),
                  pl.BlockSpec((1, s, CB), lambda bi, i: (bi, 0, COL_BV)),
                  pl.BlockSpec((1, l, CB), lambda bi, i: (bi, 0, COL_BK)),
                  pl.BlockSpec((1, l, CB), lambda bi, i: (bi, 0, COL_BV)),
                  pl.BlockSpec((1, NA_HEADS, tq, nk),
                               lambda bi, i: (jnp.minimum(i, 1) + (i == nrb - 1).astype(jnp.int32), 0, 0, 0))],
        out_specs=pl.BlockSpec((1, tq, CB), lambda bi, i: (bi, i, 0)),
        out_shape=jax.ShapeDtypeStruct((b, s, CB), BF16),
        compiler_params=_params("parallel", "arbitrary"),
    )(p, p, p, pc, pc, bias_tabs)


def _diff_lambda(lp, lam_init):
    return (jnp.exp(jnp.sum(lp[0:1] * lp[1:2], axis=-1, keepdims=True))
            - jnp.exp(jnp.sum(lp[2:3] * lp[3:4], axis=-1, keepdims=True)) + lam_init)


def _diff_finish(o, g, lam_init):
    y = o * lax.rsqrt(jnp.mean(o * o, axis=-1, keepdims=True) + EPS)
    return y * g * (1.0 - lam_init)


def _diff_kernel(q_ref, kt_ref, v_ref, lp_ref, g_ref, o_ref, m_sc, acc_sc, e_sc, *, tk, lam_init):
    tq = q_ref.shape[1]
    nk = kt_ref.shape[2] // tk
    m_sc[...] = jnp.full(m_sc.shape, -jnp.inf, F32)
    acc_sc[...] = jnp.zeros(acc_sc.shape, F32)

    def body(c, carry):
        k0 = pl.multiple_of(c * tk, LANES)
        for h in range(DIFF_HEADS):
            for m in range(2):
                rows = slice(m * tq, (m + 1) * tq)
                dims = slice((2 * h + m) * DIFF_QK_DIM, (2 * h + m + 1) * DIFF_QK_DIM)
                s = _dot(q_ref[0, :, dims], kt_ref[0, dims, pl.ds(k0, tk)])
                m_old = m_sc[h, rows]
                m_new = jnp.maximum(m_old, jnp.max(s, axis=-1, keepdims=True))
                e_sc[h, rows] = jnp.exp2(s - m_new[:, :1]).astype(BF16)
                acc_sc[h, rows] = jnp.exp2(m_old - m_new) * acc_sc[h, rows]
                m_sc[h, rows] = m_new
            acc_sc[h] += _dot(e_sc[h], v_ref[0, h, pl.ds(k0, tk), :])
        return carry

    lax.fori_loop(0, nk, body, 0)
    lam = _diff_lambda(lp_ref[...], lam_init)
    outs = []
    for h in range(DIFF_HEADS):
        acc = acc_sc[h]
        o = acc[:, :DIFF_V_DIM] / acc[:, DIFF_V_DIM:DIFF_V_DIM + 1]
        outs.append(_diff_finish(o[:tq] - lam * o[tq:], g_ref[...], lam_init))
    o_ref[0] = jnp.concatenate(outs, axis=1).astype(BF16)


def _key_tile(nkeys, cap):
    return max(t for t in range(LANES, cap + 1, LANES) if nkeys % t == 0)


def _diff_latent(p, kt, vh, lp, sub_g, lam_init, tq, tk):
    b, s, _ = p.shape
    nkeys = kt.shape[2]
    return pl.pallas_call(
        functools.partial(_diff_kernel, tk=tk, lam_init=lam_init),
        grid=(b, s // tq),
        in_specs=[pl.BlockSpec((1, tq, CB), lambda bi, i: (bi, i, COL_DQ)),
                  pl.BlockSpec((1, CB, nkeys), lambda bi, i: (bi, 0, 0), pipeline_mode=pl.Buffered(1)),
                  pl.BlockSpec((1, DIFF_HEADS, nkeys, LANES), lambda bi, i: (bi, 0, 0, 0),
                               pipeline_mode=pl.Buffered(1)),
                  pl.BlockSpec((4, DIFF_QK_DIM), lambda bi, i: (0, 0)),
                  pl.BlockSpec((1, DIFF_V_DIM), lambda bi, i: (0, 0))],
        out_specs=pl.BlockSpec((1, tq, CB), lambda bi, i: (bi, i, 0)),
        out_shape=jax.ShapeDtypeStruct((b, s, CB), BF16),
        scratch_shapes=[pltpu.VMEM((DIFF_HEADS, 2 * tq, LANES), F32),
                        pltpu.VMEM((DIFF_HEADS, 2 * tq, LANES), F32),
                        pltpu.VMEM((DIFF_HEADS, 2 * tq, tk), BF16)],
        compiler_params=_params("parallel", "arbitrary"),
    )(p, kt, vh, lp, sub_g.reshape(1, DIFF_V_DIM))


def _softmax_rows(s):
    e = jnp.exp(s - jnp.max(s, axis=-1, keepdims=True))
    return e / jnp.sum(e, axis=-1, keepdims=True)


def _ctx_kernel(pc_ref, ur_ref, ui_ref, wf_ref, lp_ref, g_ref, ya_ref, yb_ref, yd_ref, *, lam_init):
    l = pc_ref.shape[1]
    col = lambda j, lo, hi: pc_ref[0, :, j * CB + lo:j * CB + hi]
    u = jnp.concatenate([ur_ref[0], ui_ref[0]], axis=0)
    ya_ref[0] = (_dot(wf_ref[...], u) * (1.0 / math.sqrt(l * FNET_GROUP_DIM))).astype(BF16)
    outs = []
    for h in range(NA_HEADS):
        lo, hi = h * NA_HEAD_DIM, (h + 1) * NA_HEAD_DIM
        pr = _softmax_rows(_dot_nt(col(COL_BQ, lo, hi), col(COL_BK, lo, hi)) * NA_HEAD_DIM ** -0.5)
        outs.append(_dot(pr.astype(BF16), col(COL_BV, lo, hi)))
    yb_ref[0] = jnp.concatenate(outs, axis=1).astype(BF16)
    lam = _diff_lambda(lp_ref[...], lam_init)
    outs = []
    for h in range(DIFF_HEADS):
        pm = []
        for m in range(2):
            lo = (2 * h + m) * DIFF_QK_DIM
            pm.append(_softmax_rows(_dot_nt(col(COL_DQ, lo, lo + DIFF_QK_DIM), col(COL_DK, lo, lo + DIFF_QK_DIM))
                                    * DIFF_QK_DIM ** -0.5))
        a = (pm[0] - lam * pm[1]).astype(BF16)
        o = _dot(a, col(COL_DV, h * DIFF_V_DIM, (h + 1) * DIFF_V_DIM))
        outs.append(_diff_finish(o, g_ref[...], lam_init))
    yd_ref[0] = jnp.concatenate(outs, axis=1).astype(BF16)


def _ctx_branches(pc, ucr, uci, lp, sub_g, lam_init):
    b, l, _ = pc.shape
    c, s = _dft_cos_sin(l)
    wf = jnp.asarray(np.concatenate([c, s], axis=1), BF16)
    y_spec = pl.BlockSpec((1, l, CB), lambda bi: (bi, 0, 0))
    return pl.pallas_call(
        functools.partial(_ctx_kernel, lam_init=lam_init),
        grid=(b,),
        in_specs=[pl.BlockSpec((1, l, IN_DIM), lambda bi: (bi, 0, 0)), y_spec, y_spec,
                  pl.BlockSpec((l, 2 * l), lambda bi: (0, 0)),
                  pl.BlockSpec((4, DIFF_QK_DIM), lambda bi: (0, 0)),
                  pl.BlockSpec((1, DIFF_V_DIM), lambda bi: (0, 0))],
        out_specs=[y_spec] * 3,
        out_shape=[jax.ShapeDtypeStruct((b, l, CB), BF16)] * 3,
        compiler_params=_params("parallel"),
    )(pc, ucr, uci, wf, lp, sub_g.reshape(1, DIFF_V_DIM))


def _merge_kernel(x_ref, mod_ref, g1_ref, g2_ref, ya_ref, yb_ref, yd_ref, pb_ref, pc_ref, px_ref,
                  cp_ref, xp_ref, cn_ref, xn_ref, cw_ref, wg_ref, wb_ref, wo_ref, h2_prev,
                  xo_ref, h2_ref):
    del h2_prev
    i = pl.program_id(1)
    last = pl.num_programs(1) - 1
    tm = x_ref.shape[1]
    mod = mod_ref[0]
    x = x_ref[0]
    h = _norm_mod(x, g1_ref[...], mod[0:1], mod[1:2]).astype(BF16)

    u = pc_ref[0].astype(F32) * px_ref[0].astype(F32)
    up = cp_ref[0, BF16_SUBLANES - 1:, :].astype(F32) * xp_ref[0, BF16_SUBLANES - 1:, :].astype(F32)
    un = cn_ref[0, :1, :].astype(F32) * xn_ref[0, :1, :].astype(F32)
    up = jnp.where(i == 0, 0.0, up)
    un = jnp.where(i == last, 0.0, un)
    rid = lax.broadcasted_iota(jnp.int32, u.shape, 0)
    u_prev = jnp.where(rid == 0, up, pltpu.roll(u, 1, 0))
    u_next = jnp.where(rid == tm - 1, un, pltpu.roll(u, tm - 1, 0))
    cw = cw_ref[...]
    yc = pb_ref[0].astype(F32) * (cw[0:1] * u_prev + cw[1:2] * u + cw[2:3] * u_next)

    branches = (ya_ref[0], yb_ref[0], yc.astype(BF16), yd_ref[0])
    d = x.shape[1]
    out = None
    for n in range(d // CB):
        cols = slice(n * CB, (n + 1) * CB)
        merged = None
        for j in range(N_BRANCHES):
            t = jax.nn.sigmoid(_dot(h, wg_ref[j, :, cols])) * _dot(branches[j], wb_ref[j, :, cols])
            merged = t if merged is None else merged + t
        t = _dot(merged.astype(BF16), wo_ref[cols, :])
        out = t if out is None else out + t
    xn = x + mod[2:3] * out
    xo_ref[0] = xn
    h2 = _norm_mod(xn, g2_ref[...], mod[3:4], mod[4:5]).astype(BF16)
    h2_ref[0] = h2
    h2_ref[1] = h2


def _merge(x, mods, mod_row, g1, g2, ya, yb, yd, p, conv_w, wg, wb, wo, tm, h2_buf, tok_off):
    b, s, d = x.shape
    off = tok_off // tm
    per = s // tm
    hb = tm // BF16_SUBLANES
    n_halo = s // BF16_SUBLANES
    mod_idx = (lambda bi, i: (bi, 0, 0)) if mod_row is None else (lambda bi, i: (mod_row, 0, 0))
    seq = lambda width, col=0: pl.BlockSpec((1, tm, width), lambda bi, i: (bi, i, col))
    prev = lambda col: pl.BlockSpec((1, BF16_SUBLANES, CB), lambda bi, i: (bi, jnp.maximum(i * hb - 1, 0), col))
    nxt = lambda col: pl.BlockSpec((1, BF16_SUBLANES, CB),
                                   lambda bi, i: (bi, jnp.minimum((i + 1) * hb, n_halo - 1), col))
    const = lambda shape: pl.BlockSpec(shape, lambda bi, i: (0,) * len(shape))
    return pl.pallas_call(
        _merge_kernel,
        grid=(b, s // tm),
        in_specs=[seq(d), pl.BlockSpec((1, N_MOD, d), mod_idx), const((1, d)), const((1, d)),
                  seq(CB), seq(CB), seq(CB),
                  seq(CB, COL_CB), seq(CB, COL_CC), seq(CB, COL_CX),
                  prev(COL_CC), prev(COL_CX), nxt(COL_CC), nxt(COL_CX),
                  const((CONV_WIDTH, CB)),
                  const((N_BRANCHES, d, d)), const((N_BRANCHES, BRANCH_DIM, d)), const((d, d)),
                  pl.BlockSpec(memory_space=pl.ANY)],
        out_specs=[seq(d), pl.BlockSpec((2, tm, d), lambda bi, i: (0, off + bi * per + i, 0))],
        out_shape=[jax.ShapeDtypeStruct((b, s, d), F32),
                   jax.ShapeDtypeStruct(h2_buf.shape, BF16)],
        input_output_aliases={18: 1},
        compiler_params=_params("parallel", "arbitrary"),
    )(x, mods, g1.reshape(1, d), g2.reshape(1, d), ya, yb, yd, p, p, p, p, p, p, p,
      conv_w, wg, wb, wo, h2_buf)


def _route_kernel(h2_ref, wr_ref, bias_ref, tri_ref, ones_ref, idx_ref, w_ref, rank_ref, cnt_ref,
                  score_sc, sel_sc, carry_sc):
    i = pl.program_id(0)
    tm = h2_ref.shape[0]
    ne = wr_ref.shape[0]
    gsz = ne // N_GROUPS
    n_chunks = tm // LANES

    @pl.when(i == 0)
    def _():
        carry_sc[...] = jnp.zeros(carry_sc.shape, F32)

    score_sc[...] = jax.nn.sigmoid(_dot_nt(wr_ref[...], h2_ref[...]))

    def select(cidx, carry):
        c0 = pl.multiple_of(cidx * LANES, LANES)
        scores = score_sc[:, pl.ds(c0, LANES)]
        biased = scores + bias_ref[...]
        liota = lax.broadcasted_iota(jnp.int32, (gsz, LANES), 0)
        gs = []
        for g in range(N_GROUPS):
            v = biased[g * gsz:(g + 1) * gsz]
            m1 = jnp.max(v, axis=0, keepdims=True)
            i1 = jnp.min(jnp.where(v == m1, liota, gsz), axis=0, keepdims=True)
            m2 = jnp.max(jnp.where(liota == i1, -jnp.inf, v), axis=0, keepdims=True)
            gs.append(m1 + m2)
        gsm = jnp.concatenate(gs, axis=0)
        giota = lax.broadcasted_iota(jnp.int32, gsm.shape, 0)
        keep = jnp.zeros(gsm.shape, F32)
        for _ in range(TOPK_GROUPS):
            m = jnp.max(gsm, axis=0, keepdims=True)
            gi = jnp.min(jnp.where(gsm == m, giota, N_GROUPS), axis=0, keepdims=True)
            hit = giota == gi
            keep = jnp.where(hit, 1.0, keep)
            gsm = jnp.where(hit, -jnp.inf, gsm)
        cur = jnp.concatenate(
            [jnp.where(jnp.broadcast_to(keep[g:g + 1], (gsz, LANES)) > 0.0, biased[g * gsz:(g + 1) * gsz], -jnp.inf)
             for g in range(N_GROUPS)], axis=0)
        eiota = lax.broadcasted_iota(jnp.int32, (ne, LANES), 0)
        sel = jnp.zeros((ne, LANES), F32)
        idxs, ws = [], []
        for _ in range(TOP_K):
            m = jnp.max(cur, axis=0, keepdims=True)
            ik = jnp.min(jnp.where(cur == m, eiota, ne), axis=0, keepdims=True)
            hit = eiota == ik
            cur = jnp.where(hit, -jnp.inf, cur)
            ws.append(jnp.sum(jnp.where(hit, scores, 0.0), axis=0, keepdims=True))
            idxs.append(ik)
            sel = jnp.where(hit, 1.0, sel)
        w = jnp.concatenate(ws, axis=0)
        w_ref[:, pl.ds(c0, LANES)] = w / jnp.sum(w, axis=0, keepdims=True) * ROUTED_SCALE
        idx_ref[:, pl.ds(c0, LANES)] = jnp.concatenate(idxs, axis=0)
        sel_sc[:, pl.ds(c0, LANES)] = sel.astype(BF16)
        return carry

    lax.fori_loop(0, n_chunks, select, 0)

    sel_all = sel_sc[...]
    score_sc[...] = _dot(sel_all, tri_ref[...]) + jnp.concatenate([carry_sc[...]] * n_chunks, axis=1)

    def ranks(cidx, carry):
        c0 = pl.multiple_of(cidx * LANES, LANES)
        before = score_sc[:, pl.ds(c0, LANES)]
        idx = idx_ref[:, pl.ds(c0, LANES)]
        eiota = lax.broadcasted_iota(jnp.int32, (ne, LANES), 0)
        rows = [jnp.sum(jnp.where(eiota == idx[k:k + 1], before, 0.0), axis=0, keepdims=True)
                for k in range(TOP_K)]
        rank_ref[:, pl.ds(c0, LANES)] = jnp.concatenate(rows, axis=0).astype(jnp.int32)
        return carry

    lax.fori_loop(0, n_chunks, ranks, 0)
    carry_sc[...] += _dot(sel_all, ones_ref[...])
    cnt_ref[...] = carry_sc[...]


def _route(h2_all, n, wr_t, bias, tm):
    d = h2_all.shape[1]
    ne = wr_t.shape[0]
    tri = jnp.asarray(np.triu(np.ones((tm, tm), np.float32), 1), BF16)
    ones = jnp.ones((tm, LANES), BF16)
    bias_b = jnp.broadcast_to(bias.astype(F32)[:, None], (ne, LANES))
    const = lambda shape: pl.BlockSpec(shape, lambda i: (0,) * len(shape))
    tok = pl.BlockSpec((TOP_K, tm), lambda i: (0, i))
    return pl.pallas_call(
        _route_kernel,
        grid=(n // tm,),
        in_specs=[pl.BlockSpec((tm, d), lambda i: (i, 0)), const((ne, d)), const((ne, LANES)),
                  const((tm, tm)), const((tm, LANES))],
        out_specs=[tok, tok, tok, const((ne, LANES))],
        out_shape=[jax.ShapeDtypeStruct((TOP_K, n), jnp.int32),
                   jax.ShapeDtypeStruct((TOP_K, n), F32),
                   jax.ShapeDtypeStruct((TOP_K, n), jnp.int32),
                   jax.ShapeDtypeStruct((ne, LANES), F32)],
        scratch_shapes=[pltpu.VMEM((ne, tm), F32), pltpu.VMEM((ne, tm), BF16), pltpu.VMEM((ne, LANES), F32)],
        compiler_params=_params("arbitrary"),
    )(h2_all, wr_t, bias_b, tri, ones)


def _pos_kernel(idx_ref, rank_ref, start_ref, pos_ref):
    ne = start_ref.shape[0]
    start = start_ref[...]

    def body(cidx, carry):
        c0 = pl.multiple_of(cidx * LANES, LANES)
        idx = idx_ref[:, pl.ds(c0, LANES)]
        eiota = lax.broadcasted_iota(jnp.int32, (ne, LANES), 0)
        rows = [jnp.sum(jnp.where(eiota == idx[k:k + 1], start, 0.0), axis=0, keepdims=True)
                for k in range(TOP_K)]
        pos_ref[:, pl.ds(c0, LANES)] = jnp.concatenate(rows, axis=0).astype(jnp.int32) + rank_ref[:, pl.ds(c0, LANES)]
        return carry

    lax.fori_loop(0, idx_ref.shape[1] // LANES, body, 0)


def _positions(idx, rank, start_rows, tm):
    k, n = idx.shape
    ne = start_rows.shape[0]
    start_b = jnp.broadcast_to(start_rows.astype(F32)[:, None], (ne, LANES))
    tok = pl.BlockSpec((k, tm), lambda i: (0, i))
    return pl.pallas_call(
        _pos_kernel,
        grid=(n // tm,),
        in_specs=[tok, tok, pl.BlockSpec((ne, LANES), lambda i: (0, 0))],
        out_specs=tok,
        out_shape=jax.ShapeDtypeStruct((k, n), jnp.int32),
        compiler_params=_params("parallel"),
    )(idx, rank, start_b)


def _rowtok_kernel(start_ref, cnt_ref, pos_ref, out_ref, *, n_tokens):
    i = pl.program_id(0)
    tc = pos_ref.shape[0] // (TOP_K * LANES)
    n_rows = out_ref.shape[0]
    ne = start_ref.shape[0]

    @pl.when(i == 0)
    def _():
        def gaps(e, carry):
            lo = start_ref[e] + cnt_ref[e]
            hi = jnp.where(e + 1 < ne, start_ref[jnp.minimum(e + 1, ne - 1)], n_rows)

            def one(r, c):
                out_ref[r] = lax.rem(r, n_tokens)
                return c

            return lax.fori_loop(lo, hi, one, carry)

        lax.fori_loop(0, ne, gaps, 0)

    def tile(c, carry):
        t0 = (i * tc + c) * LANES

        def lane(l, carry2):
            src = c * (TOP_K * LANES) + l
            for k in range(TOP_K):
                out_ref[pos_ref[src + k * LANES]] = t0 + l
            return carry2

        return lax.fori_loop(0, LANES, lane, carry, unroll=8)

    lax.fori_loop(0, tc, tile, 0)


def _row_tokens(pos, seg_start, counts, n_rows):
    k, n = pos.shape
    n_tiles = n // LANES
    tc = max(t for t in range(1, 16 + 1) if n_tiles % t == 0)
    pos_tiles = jnp.transpose(pos.reshape(k, n_tiles, LANES), (1, 0, 2)).reshape(-1)
    grid_spec = pltpu.PrefetchScalarGridSpec(
        num_scalar_prefetch=2,
        grid=(n_tiles // tc,),
        in_specs=[pl.BlockSpec((tc * k * LANES,), lambda i, st, ct: (i,), memory_space=pltpu.SMEM)],
        out_specs=pl.BlockSpec((n_rows,), lambda i, st, ct: (0,), memory_space=pltpu.SMEM))
    return pl.pallas_call(
        functools.partial(_rowtok_kernel, n_tokens=n),
        grid_spec=grid_spec,
        out_shape=jax.ShapeDtypeStruct((n_rows,), jnp.int32),
        compiler_params=_params("arbitrary"),
    )(seg_start, counts, pos_tiles)


def _expert_kernel(crow_ref, crun_ref, rexp_ref, meta_ref, x_hbm, wg_hbm, wu_hbm, wd_hbm, y_hbm,
                   wg_buf, wu_buf, wd_buf, wg_bf, wu_bf, wd_bf, x_buf, y_buf, zero_buf,
                   w_sem, x_sem, y_sem, z_sem, *, layer):
    n_chunks, n_runs, tail_start = meta_ref[0], meta_ref[1], meta_ref[2]
    n_rows = x_hbm.shape[0]

    def w_copies(run):
        expert, s = rexp_ref[run], run % MOE_W_SLOTS
        return (pltpu.make_async_copy(wg_hbm.at[layer, expert], wg_buf.at[s], w_sem.at[s, 0]),
                pltpu.make_async_copy(wu_hbm.at[layer, expert], wu_buf.at[s], w_sem.at[s, 1]),
                pltpu.make_async_copy(wd_hbm.at[layer, expert], wd_buf.at[s], w_sem.at[s, 2]))

    def x_copy(c):
        rows = pl.ds(pl.multiple_of(crow_ref[c], MOE_ALIGN), MOE_ROWS)
        return pltpu.make_async_copy(x_hbm.at[rows], x_buf.at[c % MOE_X_SLOTS], x_sem.at[c % MOE_X_SLOTS])

    def y_copy(c):
        rows = pl.ds(pl.multiple_of(crow_ref[c], MOE_ALIGN), MOE_ROWS)
        return pltpu.make_async_copy(y_buf.at[c % 2], y_hbm.at[rows], y_sem.at[c % 2])

    def zero_copy(row0):
        return pltpu.make_async_copy(zero_buf, y_hbm.at[pl.ds(row0, MOE_ALIGN)], z_sem.at[0])

    for r in range(MOE_W_SLOTS - 1):
        @pl.when(r < n_runs)
        def _(r=r):
            for cp in w_copies(r):
                cp.start()

    for c in range(MOE_X_SLOTS - 1):
        @pl.when(c < n_chunks)
        def _(c=c):
            x_copy(c).start()

    def chunk(c, carry):
        run = crun_ref[c]
        first = jnp.logical_or(c == 0, crun_ref[jnp.maximum(c - 1, 0)] != run)

        @pl.when(c + MOE_X_SLOTS - 1 < n_chunks)
        def _():
            x_copy(c + MOE_X_SLOTS - 1).start()

        @pl.when(first)
        def _():
            @pl.when(run + MOE_W_SLOTS - 1 < n_runs)
            def _():
                for cp in w_copies(run + MOE_W_SLOTS - 1):
                    cp.start(priority=1)

            for cp in w_copies(run):
                cp.wait()
            s = run % MOE_W_SLOTS
            wg_bf[...] = wg_buf[s].astype(BF16)
            wu_bf[...] = wu_buf[s].astype(BF16)
            wd_bf[...] = wd_buf[s].astype(BF16)

        x_copy(c).wait()
        x = x_buf[c % MOE_X_SLOTS]
        a = (_silu(_dot(x, wg_bf[...])) * _dot(x, wu_bf[...])).astype(BF16)
        y = _dot(a, wd_bf[...]).astype(BF16)

        @pl.when(c > 0)
        def _():
            y_copy(c - 1).wait()

        y_buf[c % 2] = y
        y_copy(c).start()
        return carry

    lax.fori_loop(0, n_chunks, chunk, 0)

    @pl.when(n_chunks > 0)
    def _():
        y_copy(n_chunks - 1).wait()

    zero_buf[...] = jnp.zeros(zero_buf.shape, zero_buf.dtype)
    n_tail = (n_rows - tail_start) // MOE_ALIGN

    def fill(t, carry):
        zero_copy(pl.multiple_of(tail_start + t * MOE_ALIGN, MOE_ALIGN)).start()
        return carry

    def drain(t, carry):
        zero_copy(0).wait()
        return carry

    lax.fori_loop(0, n_tail, fill, 0)
    lax.fori_loop(0, n_tail, drain, 0)


def _experts(x_sorted, seg_start, counts, layer, w_g, w_u, w_d):
    n_rows, d = x_sorted.shape
    ne, f = w_g.shape[1], w_g.shape[3]
    nch = (counts + MOE_ROWS - 1) // MOE_ROWS
    c_end = jnp.cumsum(nch)
    max_chunks = (n_rows - MOE_ROWS) // MOE_ROWS + ne
    g = jnp.arange(max_chunks, dtype=jnp.int32)
    c_exp = jnp.minimum(jnp.sum((c_end[None, :] <= g[:, None]).astype(jnp.int32), axis=1), ne - 1)
    c_row = jnp.where(g < c_end[-1], seg_start[c_exp] + (g - (c_end - nch)[c_exp]) * MOE_ROWS, 0)
    has_rows = (nch > 0).astype(jnp.int32)
    run_end = jnp.cumsum(has_rows)
    c_run = (run_end - 1)[c_exp]
    r = jnp.arange(ne, dtype=jnp.int32)
    r_exp = jnp.minimum(jnp.sum((run_end[None, :] <= r[:, None]).astype(jnp.int32), axis=1), ne - 1)
    tail_start = jnp.max(jnp.where(nch > 0, seg_start + nch * MOE_ROWS, 0))
    meta = jnp.stack([c_end[-1], run_end[-1], tail_start]).astype(jnp.int32)
    any_spec = pl.BlockSpec(memory_space=pl.ANY)
    grid_spec = pltpu.PrefetchScalarGridSpec(
        num_scalar_prefetch=4,
        grid=(1,),
        in_specs=[any_spec] * 4,
        out_specs=any_spec,
        scratch_shapes=[pltpu.VMEM((MOE_W_SLOTS, d, f), F32), pltpu.VMEM((MOE_W_SLOTS, d, f), F32),
                        pltpu.VMEM((MOE_W_SLOTS, f, d), F32),
                        pltpu.VMEM((d, f), BF16), pltpu.VMEM((d, f), BF16), pltpu.VMEM((f, d), BF16),
                        pltpu.VMEM((MOE_X_SLOTS, MOE_ROWS, d), BF16), pltpu.VMEM((2, MOE_ROWS, d), BF16),
                        pltpu.VMEM((MOE_ALIGN, d), BF16),
                        pltpu.SemaphoreType.DMA((MOE_W_SLOTS, 3)), pltpu.SemaphoreType.DMA((MOE_X_SLOTS,)),
                        pltpu.SemaphoreType.DMA((2,)), pltpu.SemaphoreType.DMA((1,))])
    return pl.pallas_call(
        functools.partial(_expert_kernel, layer=layer),
        grid_spec=grid_spec,
        out_shape=jax.ShapeDtypeStruct((n_rows, d), BF16),
        compiler_params=_params("arbitrary"),
    )(c_row.astype(jnp.int32), c_run.astype(jnp.int32), r_exp, meta, x_sorted, w_g, w_u, w_d)


def _resid_kernel(x_ref, h2_ref, y_ref, w_ref, mod_ref, sg_ref, su_ref, sd_ref, gf_ref, o_ref, *, final):
    h2 = h2_ref[...]
    a = (_silu(_dot(h2, sg_ref[...])) * _dot(h2, su_ref[...])).astype(BF16)
    y = _dot(a, sd_ref[...])
    w = w_ref[...]
    for k in range(TOP_K):
        y = y + w[:, k:k + 1] * y_ref[k].astype(F32)
    xo = x_ref[...] + mod_ref[0][5:6] * y
    if final:
        xo = xo * lax.rsqrt(jnp.mean(xo * xo, axis=-1, keepdims=True) + EPS) * gf_ref[...]
    o_ref[...] = xo


def _resid(x_flat, h2_all, y_tok, w_tok, row_off, mods, mod_row, rows_per_mod, sg, su, sd, gf, final, tm):
    n, d = x_flat.shape
    f = sg.shape[1]
    off = row_off // tm
    per = rows_per_mod // tm
    mod_idx = (lambda i: (i // per, 0, 0)) if mod_row is None else (lambda i: (mod_row, 0, 0))
    const = lambda shape: pl.BlockSpec(shape, lambda i: (0,) * len(shape))
    return pl.pallas_call(
        functools.partial(_resid_kernel, final=final),
        grid=(n // tm,),
        in_specs=[pl.BlockSpec((tm, d), lambda i: (i, 0)),
                  pl.BlockSpec((tm, d), lambda i: (i + off, 0)),
                  pl.BlockSpec((TOP_K, tm, d), lambda i: (0, i + off, 0)),
                  pl.BlockSpec((tm, TOP_K), lambda i: (i + off, 0)),
                  pl.BlockSpec((1, N_MOD, d), mod_idx),
                  const((d, f)), const((d, f)), const((f, d)), const((1, d))],
        out_specs=pl.BlockSpec((tm, d), lambda i: (i, 0)),
        out_shape=jax.ShapeDtypeStruct((n, d), F32),
        compiler_params=_params("parallel"),
    )(x_flat, h2_all, y_tok, w_tok, mods, sg, su, sd, gf.reshape(1, d))


def _moe_routed(h2_dbl, wr_t, bias, layer, w_g, w_u, w_d, later):
    n, d = h2_dbl.shape[0] // 2, h2_dbl.shape[1]
    idx, wts, rank, cnt = _route(h2_dbl, n, wr_t, bias, TOKEN_TILE)
    counts = cnt[:, 0].astype(jnp.int32)
    padded = (counts + MOE_ALIGN - 1) // MOE_ALIGN * MOE_ALIGN
    seg_start = jnp.cumsum(padded) - padded
    n_rows = -(-(n * TOP_K + N_EXPERTS * (MOE_ALIGN - 1)) // MOE_ROWS) * MOE_ROWS + MOE_ROWS
    pos2 = _positions(idx, rank, seg_start, TOKEN_TILE)
    pos = pos2.reshape(TOP_K * n)
    row_tok, later = lax.optimization_barrier((_row_tokens(pos2, seg_start, counts, n_rows), later))
    x_sorted = h2_dbl.at[row_tok].get(mode='promise_in_bounds')
    y_sorted = _experts(x_sorted, seg_start, counts, layer, w_g, w_u, w_d)
    return y_sorted.at[pos].get(mode='promise_in_bounds').reshape(TOP_K, n, d), wts.T, later


def kernel(x, c, ctx, c_ctx, ada_w, ada_b, norm1_g, w_in, conv_w, na_rel_bias, diff_lambda,
           diff_subln_g, w_branch_gate, w_branch, w_out, norm2_g, router_w, router_bias,
           expert_w_gate, expert_w_up, expert_w_down, shared_w_gate, shared_w_up, shared_w_down,
           final_norm_g):
    b, s, d = x.shape
    l_ctx = ctx.shape[1]
    rows = s // GRID_W
    ctx_row = b
    cvec = jnp.zeros((8, d), F32).at[:b].set(c).at[ctx_row].set(c_ctx)
    rope_tabs = _rope_tables(s)
    wf = _channel_dft_matrix()
    tm = TOKEN_TILE
    xc = ctx
    mods = _ada(cvec, ada_w, ada_b, 0).reshape(8, N_MOD, d)
    bias_tabs = _na_bias_tables(na_rel_bias[0], rows)
    for layer in range(DEPTH):
        last = layer == DEPTH - 1
        lam_init = 0.8 - 0.6 * math.exp(-0.3 * layer)
        w_in_bf = w_in[layer].astype(BF16)
        wg_bf = w_branch_gate[layer].astype(BF16)
        wb_bf = w_branch[layer].astype(BF16)
        wo_bf = w_out[layer].astype(BF16)
        wr_bf = router_w[layer].T.astype(BF16)
        lp = diff_lambda[layer]
        sub_g = diff_subln_g[layer]

        kv0 = (jnp.zeros((b, CB, s + l_ctx), BF16), jnp.zeros((b, DIFF_HEADS, s + l_ctx, LANES), BF16))
        p, ur, ui, kt, vh = _inproj(x, norm1_g[layer], mods, w_in_bf, wf, rope_tabs, None, True, tm,
                                    0, kv0)
        ctx_tabs = tuple(t[:l_ctx] for t in rope_tabs)
        pc, ucr, uci, kt, vh = _inproj(xc, norm1_g[layer], mods, w_in_bf, wf, ctx_tabs, ctx_row, False,
                                       l_ctx, s, (kt, vh))

        ya = _fourier_latent(ur, ui)
        yb = _na_latent(p, pc, bias_tabs)
        yd = _diff_latent(p, kt, vh, lp, sub_g, lam_init, min(s, DIFF_Q_TILE),
                          _key_tile(s + l_ctx, DIFF_K_TILE_MAX))
        n_tok = b * s if last else b * (s + l_ctx)
        h2_buf = jnp.zeros((2, n_tok, d), BF16)
        x, h2_buf = _merge(x, mods, None, norm1_g[layer], norm2_g[layer], ya, yb, yd, p,
                           conv_w[layer], wg_bf, wb_bf, wo_bf, tm, h2_buf, 0)
        if not last:
            yac, ybc, ydc = _ctx_branches(pc, ucr, uci, lp, sub_g, lam_init)
            xc, h2_buf = _merge(xc, mods, ctx_row, norm1_g[layer], norm2_g[layer], yac, ybc, ydc, pc,
                                conv_w[layer], wg_bf, wb_bf, wo_bf, l_ctx, h2_buf, b * s)
        h2_all = h2_buf.reshape(2 * n_tok, d)

        later = () if last else (cvec, na_rel_bias[layer + 1])
        y_tok, w_tok, later = _moe_routed(h2_all, wr_bf, router_bias[layer], layer, expert_w_gate,
                                          expert_w_up, expert_w_down, later)
        sg_bf = shared_w_gate[layer].astype(BF16)
        su_bf = shared_w_up[layer].astype(BF16)
        sd_bf = shared_w_down[layer].astype(BF16)
        x = _resid(x.reshape(b * s, d), h2_all, y_tok, w_tok, 0, mods, None, s, sg_bf, su_bf, sd_bf,
                   final_norm_g, last, tm).reshape(b, s, d)
        if not last:
            xc = _resid(xc.reshape(b * l_ctx, d), h2_all, y_tok, w_tok, b * s, mods, ctx_row, l_ctx,
                        sg_bf, su_bf, sd_bf, final_norm_g, False, l_ctx).reshape(b, l_ctx, d)
            mods = _ada(later[0], ada_w, ada_b, layer + 1).reshape(8, N_MOD, d)
            bias_tabs = _na_bias_tables(later[1], rows)
    return x
```

```python
import functools
import math

import numpy as np
import jax
import jax.numpy as jnp
from jax import lax
from jax.experimental import pallas as pl
from jax.experimental.pallas import tpu as pltpu

F32 = jnp.float32
BF16 = jnp.bfloat16

DEPTH = 2
GRID_W = 64
EPS = 1e-6
N_MOD = 6

FNET_GROUP_DIM = 64
NA_HEADS = 4
NA_HEAD_DIM = 64
NA_WIN_ROWS = 8
NA_WIN_COLS = 16
CONV_WIDTH = 3
DIFF_HEADS = 4
DIFF_QK_DIM = 32
DIFF_V_DIM = 64
ROPE_BASE = 10000.0
N_BRANCHES = 4
BRANCH_DIM = 256

COL_A, COL_BQ, COL_BK, COL_BV, COL_CB, COL_CC, COL_CX, COL_DQ, COL_DK, COL_DV = range(10)
N_COL_BLOCKS = 10
CB = 256
IN_DIM = N_COL_BLOCKS * CB

N_EXPERTS = 256
TOP_K = 8
N_GROUPS = 8
TOPK_GROUPS = 4
ROUTED_SCALE = 2.5
LOG2_E = 1.4426950408889634

VMEM_LIMIT_BYTES = 56 * 1024 * 1024
LANES = 128
BF16_SUBLANES = 16
TOKEN_TILE = 512
DIFF_Q_TILE = 1024
DIFF_K_TILE_MAX = 768
FFT_N1 = 64
NA_MASKED = -1e30
NA_QROWS = 8
NA_KROWS = 16
MOE_ROWS = 256
MOE_ALIGN = BF16_SUBLANES
MOE_X_SLOTS = 4
MOE_W_SLOTS = 3


def _params(*sem):
    return pltpu.CompilerParams(dimension_semantics=sem, vmem_limit_bytes=VMEM_LIMIT_BYTES)


def _dot(a, b):
    return jnp.dot(a, b, preferred_element_type=F32)


def _dot_nt(a, b):
    return lax.dot_general(a, b, (((1,), (1,)), ((), ())), preferred_element_type=F32)


def _norm_mod(xf, g, shift, scale):
    y = xf * lax.rsqrt(jnp.mean(xf * xf, axis=-1, keepdims=True) + EPS)
    return (y * g) * (1.0 + scale) + shift


def _silu(v):
    return v * jax.nn.sigmoid(v)


def _ada_kernel(c_ref, w_ref, b_ref, o_ref):
    s = _silu(c_ref[...])
    o_ref[...] = _dot(s.astype(BF16), w_ref[0].astype(BF16)) + b_ref[0]


def _ada(cvec, w, b, layer):
    rows, d = cvec.shape
    depth, _, n = w.shape
    tn = 1536
    return pl.pallas_call(
        _ada_kernel,
        grid=(n // tn,),
        in_specs=[pl.BlockSpec((rows, d), lambda j: (0, 0)),
                  pl.BlockSpec((1, d, tn), lambda j: (layer, 0, j)),
                  pl.BlockSpec((1, 1, tn), lambda j: (layer, 0, j))],
        out_specs=pl.BlockSpec((rows, tn), lambda j: (0, j)),
        out_shape=jax.ShapeDtypeStruct((rows, n), F32),
        compiler_params=_params("arbitrary"),
    )(cvec, w, b.reshape(depth, 1, n))


def _inproj_kernel(x_ref, g_ref, mod_ref, w_ref, wf_ref, cos_ref, s1_ref, s2_ref, kt_prev, vh_prev,
                   p_ref, ur_ref, ui_ref, kt_ref, vh_ref, *, rope):
    del kt_prev, vh_prev
    mod = mod_ref[0]
    h = _norm_mod(x_ref[0], g_ref[...], mod[0:1], mod[1:2]).astype(BF16)
    for j in range(N_COL_BLOCKS):
        pj = _dot(h, w_ref[:, j * CB:(j + 1) * CB])
        if j == COL_A:
            u = _dot(pj.astype(BF16), wf_ref[...])
            ur_ref[0] = u[:, :CB].astype(BF16)
            ui_ref[0] = u[:, CB:].astype(BF16)
        if rope and j in (COL_DQ, COL_DK):
            cos = jnp.concatenate([cos_ref[...]] * 2, axis=1)
            s1 = jnp.concatenate([s1_ref[...]] * 2, axis=1)
            s2 = jnp.concatenate([s2_ref[...]] * 2, axis=1)
            pj = pj * cos + pltpu.roll(pj, CB - 8, 1) * s1 + pltpu.roll(pj, 8, 1) * s2
            if j == COL_DQ:
                pj = pj * (DIFF_QK_DIM ** -0.5 * LOG2_E)
        p_ref[0, :, j * CB:(j + 1) * CB] = pj.astype(BF16)
        if j == COL_DK:
            kt_ref[0] = pj.T.astype(BF16)
        if j == COL_DV:
            ones = jnp.ones((pj.shape[0], LANES - DIFF_V_DIM), F32)
            for hd in range(DIFF_HEADS):
                v = pj[:, hd * DIFF_V_DIM:(hd + 1) * DIFF_V_DIM]
                vh_ref[0, hd] = jnp.concatenate([v, ones], axis=1).astype(BF16)


def _inproj(x, g, mods, w_bf, wf, rope_tabs, mod_row, rope, tm, key_off, kv_prev):
    b, s, d = x.shape
    n_keys = kv_prev[0].shape[2]
    off = key_off // tm
    mod_idx = (lambda bi, i: (bi, 0, 0)) if mod_row is None else (lambda bi, i: (mod_row, 0, 0))
    tab_spec = pl.BlockSpec((tm, LANES), lambda bi, i: (i, 0))
    seq_spec = lambda width: pl.BlockSpec((1, tm, width), lambda bi, i: (bi, i, 0))
    in_specs = [seq_spec(d),
                pl.BlockSpec((1, d), lambda bi, i: (0, 0)),
                pl.BlockSpec((1, N_MOD, d), mod_idx),
                pl.BlockSpec((d, IN_DIM), lambda bi, i: (0, 0)),
                pl.BlockSpec((CB, 2 * CB), lambda bi, i: (0, 0)),
                tab_spec, tab_spec, tab_spec]
    args = [x, g.reshape(1, d), mods, w_bf, wf, *rope_tabs]
    aliases = {len(args): 3, len(args) + 1: 4}
    in_specs += [pl.BlockSpec(memory_space=pl.ANY)] * 2
    args += list(kv_prev)
    return pl.pallas_call(
        functools.partial(_inproj_kernel, rope=rope),
        grid=(b, s // tm),
        in_specs=in_specs,
        out_specs=[seq_spec(IN_DIM), seq_spec(CB), seq_spec(CB),
                   pl.BlockSpec((1, CB, tm), lambda bi, i: (bi, 0, i + off)),
                   pl.BlockSpec((1, DIFF_HEADS, tm, LANES), lambda bi, i: (bi, 0, i + off, 0))],
        out_shape=[jax.ShapeDtypeStruct((b, s, IN_DIM), BF16),
                   jax.ShapeDtypeStruct((b, s, CB), BF16),
                   jax.ShapeDtypeStruct((b, s, CB), BF16),
                   jax.ShapeDtypeStruct((b, CB, n_keys), BF16),
                   jax.ShapeDtypeStruct((b, DIFF_HEADS, n_keys, LANES), BF16)],
        input_output_aliases=aliases,
        compiler_params=_params("parallel", "arbitrary"),
    )(*args)


def _channel_dft_matrix():
    c = np.arange(FNET_GROUP_DIM)
    ang = 2.0 * np.pi * ((c[:, None] * c[None, :]) % FNET_GROUP_DIM) / FNET_GROUP_DIM
    eye = np.eye(CB // FNET_GROUP_DIM)
    m = np.concatenate([np.kron(eye, np.cos(ang)), -np.kron(eye, np.sin(ang))], axis=1)
    return jnp.asarray(m, BF16)


def _rope_tables(s):
    half = DIFF_QK_DIM // 2
    inv = 1.0 / (ROPE_BASE ** (jnp.arange(0, half, 2, dtype=F32) / half))
    lane = np.arange(LANES) % DIFF_QK_DIM
    is_col = (lane // half == 1)[None, :]
    is_x2 = (lane % half // (half // 2) == 1)[None, :]
    t = jnp.arange(s)[:, None]
    pos = jnp.where(is_col, t % GRID_W, t // GRID_W).astype(F32)
    ang = pos * inv[lane % (half // 2)][None, :]
    sin = jnp.sin(ang)
    s1 = jnp.where(is_x2, 0.0, -sin)
    s2 = jnp.where(is_x2, sin, 0.0)
    return jnp.cos(ang), s1, s2


def _fft1_kernel(ur_ref, ui_ref, w_ref, ct_ref, st_ref, ar_ref, ai_ref):
    n1 = ur_ref.shape[1]
    u = jnp.concatenate([ur_ref[0], ui_ref[0]], axis=0)
    a = _dot(w_ref[...], u)
    ar, ai = a[:n1], a[n1:]
    ct, st = ct_ref[...], st_ref[...]
    ar_ref[0] = (ar * ct + ai * st).astype(BF16)
    ai_ref[0] = (ai * ct - ar * st).astype(BF16)


def _fft2_kernel(ar_ref, ai_ref, w_ref, y_ref, *, norm):
    for j in range(ar_ref.shape[1]):
        a = jnp.concatenate([ar_ref[0, j], ai_ref[0, j]], axis=0)
        y_ref[0, j] = (_dot(w_ref[...], a) * norm).astype(BF16)


def _dft_cos_sin(n):
    k = np.arange(n)
    ang = 2.0 * np.pi * ((k[:, None] * k[None, :]) % n) / n
    return np.cos(ang), np.sin(ang)


def _fourier_latent(ur, ui):
    b, s, cb = ur.shape
    n1, n2 = FFT_N1, s // FFT_N1
    c1, s1 = _dft_cos_sin(n1)
    w1 = jnp.asarray(np.block([[c1, s1], [-s1, c1]]), BF16)
    c2, s2 = _dft_cos_sin(n2)
    w2 = jnp.asarray(np.concatenate([c2, s2], axis=1), BF16)
    tw = 2.0 * np.pi * (np.arange(n1)[:, None] * np.arange(n2)[None, :]) / s
    ct = jnp.asarray(np.repeat(np.cos(tw), cb, axis=1), F32)
    st = jnp.asarray(np.repeat(np.sin(tw), cb, axis=1), F32)
    lanes = n2 * cb
    tn = min(lanes, 4096)
    u_spec = pl.BlockSpec((1, n1, tn), lambda j, bi: (bi, 0, j))
    t_spec = pl.BlockSpec((n1, tn), lambda j, bi: (0, j))
    ar, ai = pl.pallas_call(
        _fft1_kernel,
        grid=(lanes // tn, b),
        in_specs=[u_spec, u_spec, pl.BlockSpec((2 * n1, 2 * n1), lambda j, bi: (0, 0)), t_spec, t_spec],
        out_specs=[u_spec, u_spec],
        out_shape=[jax.ShapeDtypeStruct((b, n1, lanes), BF16)] * 2,
        compiler_params=_params("arbitrary", "arbitrary"),
    )(ur.reshape(b, n1, lanes), ui.reshape(b, n1, lanes), w1, ct, st)
    kc = 8
    a_spec = pl.BlockSpec((1, kc, n2, cb), lambda bi, j: (bi, j, 0, 0))
    y = pl.pallas_call(
        functools.partial(_fft2_kernel, norm=1.0 / math.sqrt(s * FNET_GROUP_DIM)),
        grid=(b, n1 // kc),
        in_specs=[a_spec, a_spec, pl.BlockSpec((n2, 2 * n2), lambda bi, j: (0, 0))],
        out_specs=a_spec,
        out_shape=jax.ShapeDtypeStruct((b, n1, n2, cb), BF16),
        compiler_params=_params("parallel", "arbitrary"),
    )(ar.reshape(b, n1, n2, cb), ai.reshape(b, n1, n2, cb), w2)
    return jnp.transpose(y, (0, 2, 1, 3)).reshape(b, s, cb)


def _na_kernel(q_ref, k_ref, v_ref, kc_ref, vc_ref, bias_ref, o_ref, *, rows):
    rb = pl.program_id(1)
    kb = jnp.clip(rb * NA_QROWS - NA_WIN_ROWS // 2, 0, rows - NA_KROWS)
    nk = NA_KROWS * GRID_W
    tok0 = pl.multiple_of(kb * GRID_W, 256)
    scale = NA_HEAD_DIM ** -0.5
    outs = []
    for h in range(NA_HEADS):
        sl = slice(h * NA_HEAD_DIM, (h + 1) * NA_HEAD_DIM)
        q = q_ref[0, :, sl]
        s = _dot_nt(q, k_ref[0, pl.ds(tok0, nk), sl]) * scale + bias_ref[0, h]
        sc = _dot_nt(q, kc_ref[0, :, sl]) * scale
        m = jnp.maximum(jnp.max(s, axis=-1, keepdims=True), jnp.max(sc, axis=-1, keepdims=True))
        e = jnp.exp(s - m)
        ec = jnp.exp(sc - m)
        l = jnp.sum(e, axis=-1, keepdims=True) + jnp.sum(ec, axis=-1, keepdims=True)
        o = _dot(e.astype(BF16), v_ref[0, pl.ds(tok0, nk), sl]) + _dot(ec.astype(BF16), vc_ref[0, :, sl])
        outs.append(o / l)
    o_ref[0] = jnp.concatenate(outs, axis=1).astype(BF16)


def _na_bias_tables(rel_bias, rows):
    n_dr, n_dc = 2 * NA_WIN_ROWS - 1, 2 * NA_WIN_COLS - 1
    cq = np.arange(GRID_W)
    col_lo = np.clip(cq - NA_WIN_COLS // 2, 0, GRID_W - NA_WIN_COLS)
    col_ok = (cq[None, :] >= col_lo[:, None]) & (cq[None, :] < col_lo[:, None] + NA_WIN_COLS)
    dc_idx = np.where(col_ok, np.clip(cq[None, :] - cq[:, None] + NA_WIN_COLS - 1, 0, n_dc - 1), n_dc)
    dr_idx = []
    for r0 in (0, NA_QROWS, rows - NA_QROWS):
        kb = int(np.clip(r0 - NA_WIN_ROWS // 2, 0, rows - NA_KROWS))
        r = r0 + np.arange(NA_QROWS)
        rk = kb + np.arange(NA_KROWS)
        start = np.clip(r - NA_WIN_ROWS // 2, 0, rows - NA_WIN_ROWS)
        row_ok = (rk[None, :] >= start[:, None]) & (rk[None, :] < start[:, None] + NA_WIN_ROWS)
        dr_idx.append(np.where(row_ok, np.clip(rk[None, :] - r[:, None] + NA_WIN_ROWS - 1, 0, n_dr - 1), n_dr))
    oh_r = jnp.asarray(np.stack(dr_idx)[..., None] == np.arange(n_dr + 1), F32)
    oh_c = jnp.asarray(dc_idx[..., None] == np.arange(n_dc + 1), F32)
    bias = jnp.pad(rel_bias.astype(F32), ((0, 0), (0, 1), (0, 1)), constant_values=NA_MASKED)
    by_row = jnp.einsum('tqka,hab->thqkb', oh_r, bias, precision=lax.Precision.HIGHEST)
    tabs = jnp.einsum('thqkb,cdb->thqckd', by_row, oh_c, precision=lax.Precision.HIGHEST)
    return tabs.reshape(3, NA_HEADS, NA_QROWS * GRID_W, NA_KROWS * GRID_W)


def _na_latent(p, pc, bias_tabs):
    b, s, _ = p.shape
    l = pc.shape[1]
    rows = s // GRID_W
    nrb = rows // NA_QROWS
    tq = NA_QROWS * GRID_W
    nk = NA_KROWS * GRID_W
    return pl.pallas_call(
        functools.partial(_na_kernel, rows=rows),
        grid=(b, nrb),
        in_specs=[pl.BlockSpec((1, tq, CB), lambda bi, i: (bi, i, COL_BQ)),
                  pl.BlockSpec((1, s, CB), lambda bi, i: (bi, 0, COL_BK)),
                  pl.BlockSpec((1, s, CB), lambda bi, i: (bi, 0, COL_BV)),
                  pl.BlockSpec((1, l, CB), lambda bi, i: (bi, 0, COL_BK)),
                  pl.BlockSpec((1, l, CB), lambda bi, i: (bi, 0, COL_BV)),
                  pl.BlockSpec((1, NA_HEADS, tq, nk),
                               lambda bi, i: (jnp.minimum(i, 1) + (i == nrb - 1).astype(jnp.int32), 0, 0, 0))],
        out_specs=pl.BlockSpec((1, tq, CB), lambda bi, i: (bi, i, 0)),
        out_shape=jax.ShapeDtypeStruct((b, s, CB), BF16),
        compiler_params=_params("parallel", "arbitrary"),
    )(p, p, p, pc, pc, bias_tabs)


def _diff_lambda(lp, lam_init):
    return (jnp.exp(jnp.sum(lp[0:1] * lp[1:2], axis=-1, keepdims=True))
            - jnp.exp(jnp.sum(lp[2:3] * lp[3:4], axis=-1, keepdims=True)) + lam_init)


def _diff_finish(o, g, lam_init):
    y = o * lax.rsqrt(jnp.mean(o * o, axis=-1, keepdims=True) + EPS)
    return y * g * (1.0 - lam_init)


def _diff_kernel(q_ref, kt_ref, v_ref, lp_ref, g_ref, o_ref, m_sc, acc_sc, e_sc, *, tk, lam_init):
    tq = q_ref.shape[1]
    nk = kt_ref.shape[2] // tk
    m_sc[...] = jnp.full(m_sc.shape, -jnp.inf, F32)
    acc_sc[...] = jnp.zeros(acc_sc.shape, F32)

    def body(c, carry):
        k0 = pl.multiple_of(c * tk, LANES)
        for h in range(DIFF_HEADS):
            for m in range(2):
                rows = slice(m * tq, (m + 1) * tq)
                dims = slice((2 * h + m) * DIFF_QK_DIM, (2 * h + m + 1) * DIFF_QK_DIM)
                s = _dot(q_ref[0, :, dims], kt_ref[0, dims, pl.ds(k0, tk)])
                m_old = m_sc[h, rows]
                m_new = jnp.maximum(m_old, jnp.max(s, axis=-1, keepdims=True))
                e_sc[h, rows] = jnp.exp2(s - m_new[:, :1]).astype(BF16)
                acc_sc[h, rows] = jnp.exp2(m_old - m_new) * acc_sc[h, rows]
                m_sc[h, rows] = m_new
            acc_sc[h] += _dot(e_sc[h], v_ref[0, h, pl.ds(k0, tk), :])
        return carry

    lax.fori_loop(0, nk, body, 0)
    lam = _diff_lambda(lp_ref[...], lam_init)
    outs = []
    for h in range(DIFF_HEADS):
        acc = acc_sc[h]
        o = acc[:, :DIFF_V_DIM] / acc[:, DIFF_V_DIM:DIFF_V_DIM + 1]
        outs.append(_diff_finish(o[:tq] - lam * o[tq:], g_ref[...], lam_init))
    o_ref[0] = jnp.concatenate(outs, axis=1).astype(BF16)


def _key_tile(nkeys, cap):
    return max(t for t in range(LANES, cap + 1, LANES) if nkeys % t == 0)


def _diff_latent(p, kt, vh, lp, sub_g, lam_init, tq, tk):
    b, s, _ = p.shape
    nkeys = kt.shape[2]
    return pl.pallas_call(
        functools.partial(_diff_kernel, tk=tk, lam_init=lam_init),
        grid=(b, s // tq),
        in_specs=[pl.BlockSpec((1, tq, CB), lambda bi, i: (bi, i, COL_DQ)),
                  pl.BlockSpec((1, CB, nkeys), lambda bi, i: (bi, 0, 0), pipeline_mode=pl.Buffered(1)),
                  pl.BlockSpec((1, DIFF_HEADS, nkeys, LANES), lambda bi, i: (bi, 0, 0, 0),
                               pipeline_mode=pl.Buffered(1)),
                  pl.BlockSpec((4, DIFF_QK_DIM), lambda bi, i: (0, 0)),
                  pl.BlockSpec((1, DIFF_V_DIM), lambda bi, i: (0, 0))],
        out_specs=pl.BlockSpec((1, tq, CB), lambda bi, i: (bi, i, 0)),
        out_shape=jax.ShapeDtypeStruct((b, s, CB), BF16),
        scratch_shapes=[pltpu.VMEM((DIFF_HEADS, 2 * tq, LANES), F32),
                        pltpu.VMEM((DIFF_HEADS, 2 * tq, LANES), F32),
                        pltpu.VMEM((DIFF_HEADS, 2 * tq, tk), BF16)],
        compiler_params=_params("parallel", "arbitrary"),
    )(p, kt, vh, lp, sub_g.reshape(1, DIFF_V_DIM))


def _softmax_rows(s):
    e = jnp.exp(s - jnp.max(s, axis=-1, keepdims=True))
    return e / jnp.sum(e, axis=-1, keepdims=True)


def _ctx_kernel(pc_ref, ur_ref, ui_ref, wf_ref, lp_ref, g_ref, ya_ref, yb_ref, yd_ref, *, lam_init):
    l = pc_ref.shape[1]
    col = lambda j, lo, hi: pc_ref[0, :, j * CB + lo:j * CB + hi]
    u = jnp.concatenate([ur_ref[0], ui_ref[0]], axis=0)
    ya_ref[0] = (_dot(wf_ref[...], u) * (1.0 / math.sqrt(l * FNET_GROUP_DIM))).astype(BF16)
    outs = []
    for h in range(NA_HEADS):
        lo, hi = h * NA_HEAD_DIM, (h + 1) * NA_HEAD_DIM
        pr = _softmax_rows(_dot_nt(col(COL_BQ, lo, hi), col(COL_BK, lo, hi)) * NA_HEAD_DIM ** -0.5)
        outs.append(_dot(pr.astype(BF16), col(COL_BV, lo, hi)))
    yb_ref[0] = jnp.concatenate(outs, axis=1).astype(BF16)
    lam = _diff_lambda(lp_ref[...], lam_init)
    outs = []
    for h in range(DIFF_HEADS):
        pm = []
        for m in range(2):
            lo = (2 * h + m) * DIFF_QK_DIM
            pm.append(_softmax_rows(_dot_nt(col(COL_DQ, lo, lo + DIFF_QK_DIM), col(COL_DK, lo, lo + DIFF_QK_DIM))
                                    * DIFF_QK_DIM ** -0.5))
        a = (pm[0] - lam * pm[1]).astype(BF16)
        o = _dot(a, col(COL_DV, h * DIFF_V_DIM, (h + 1) * DIFF_V_DIM))
        outs.append(_diff_finish(o, g_ref[...], lam_init))
    yd_ref[0] = jnp.concatenate(outs, axis=1).astype(BF16)


def _ctx_branches(pc, ucr, uci, lp, sub_g, lam_init):
    b, l, _ = pc.shape
    c, s = _dft_cos_sin(l)
    wf = jnp.asarray(np.concatenate([c, s], axis=1), BF16)
    y_spec = pl.BlockSpec((1, l, CB), lambda bi: (bi, 0, 0))
    return pl.pallas_call(
        functools.partial(_ctx_kernel, lam_init=lam_init),
        grid=(b,),
        in_specs=[pl.BlockSpec((1, l, IN_DIM), lambda bi: (bi, 0, 0)), y_spec, y_spec,
                  pl.BlockSpec((l, 2 * l), lambda bi: (0, 0)),
                  pl.BlockSpec((4, DIFF_QK_DIM), lambda bi: (0, 0)),
                  pl.BlockSpec((1, DIFF_V_DIM), lambda bi: (0, 0))],
        out_specs=[y_spec] * 3,
        out_shape=[jax.ShapeDtypeStruct((b, l, CB), BF16)] * 3,
        compiler_params=_params("parallel"),
    )(pc, ucr, uci, wf, lp, sub_g.reshape(1, DIFF_V_DIM))


def _merge_kernel(x_ref, mod_ref, g1_ref, g2_ref, ya_ref, yb_ref, yd_ref, pb_ref, pc_ref, px_ref,
                  cp_ref, xp_ref, cn_ref, xn_ref, cw_ref, wg_ref, wb_ref, wo_ref, h2_prev,
                  xo_ref, h2_ref):
    del h2_prev
    i = pl.program_id(1)
    last = pl.num_programs(1) - 1
    tm = x_ref.shape[1]
    mod = mod_ref[0]
    x = x_ref[0]
    h = _norm_mod(x, g1_ref[...], mod[0:1], mod[1:2]).astype(BF16)

    u = pc_ref[0].astype(F32) * px_ref[0].astype(F32)
    up = cp_ref[0, BF16_SUBLANES - 1:, :].astype(F32) * xp_ref[0, BF16_SUBLANES - 1:, :].astype(F32)
    un = cn_ref[0, :1, :].astype(F32) * xn_ref[0, :1, :].astype(F32)
    up = jnp.where(i == 0, 0.0, up)
    un = jnp.where(i == last, 0.0, un)
    rid = lax.broadcasted_iota(jnp.int32, u.shape, 0)
    u_prev = jnp.where(rid == 0, up, pltpu.roll(u, 1, 0))
    u_next = jnp.where(rid == tm - 1, un, pltpu.roll(u, tm - 1, 0))
    cw = cw_ref[...]
    yc = pb_ref[0].astype(F32) * (cw[0:1] * u_prev + cw[1:2] * u + cw[2:3] * u_next)

    branches = (ya_ref[0], yb_ref[0], yc.astype(BF16), yd_ref[0])
    d = x.shape[1]
    out = None
    for n in range(d // CB):
        cols = slice(n * CB, (n + 1) * CB)
        merged = None
        for j in range(N_BRANCHES):
            t = jax.nn.sigmoid(_dot(h, wg_ref[j, :, cols])) * _dot(branches[j], wb_ref[j, :, cols])
            merged = t if merged is None else merged + t
        t = _dot(merged.astype(BF16), wo_ref[cols, :])
        out = t if out is None else out + t
    xn = x + mod[2:3] * out
    xo_ref[0] = xn
    h2 = _norm_mod(xn, g2_ref[...], mod[3:4], mod[4:5]).astype(BF16)
    h2_ref[0] = h2
    h2_ref[1] = h2


def _merge(x, mods, mod_row, g1, g2, ya, yb, yd, p, conv_w, wg, wb, wo, tm, h2_buf, tok_off):
    b, s, d = x.shape
    off = tok_off // tm
    per = s // tm
    hb = tm // BF16_SUBLANES
    n_halo = s // BF16_SUBLANES
    mod_idx = (lambda bi, i: (bi, 0, 0)) if mod_row is None else (lambda bi, i: (mod_row, 0, 0))
    seq = lambda width, col=0: pl.BlockSpec((1, tm, width), lambda bi, i: (bi, i, col))
    prev = lambda col: pl.BlockSpec((1, BF16_SUBLANES, CB), lambda bi, i: (bi, jnp.maximum(i * hb - 1, 0), col))
    nxt = lambda col: pl.BlockSpec((1, BF16_SUBLANES, CB),
                                   lambda bi, i: (bi, jnp.minimum((i + 1) * hb, n_halo - 1), col))
    const = lambda shape: pl.BlockSpec(shape, lambda bi, i: (0,) * len(shape))
    return pl.pallas_call(
        _merge_kernel,
        grid=(b, s // tm),
        in_specs=[seq(d), pl.BlockSpec((1, N_MOD, d), mod_idx), const((1, d)), const((1, d)),
                  seq(CB), seq(CB), seq(CB),
                  seq(CB, COL_CB), seq(CB, COL_CC), seq(CB, COL_CX),
                  prev(COL_CC), prev(COL_CX), nxt(COL_CC), nxt(COL_CX),
                  const((CONV_WIDTH, CB)),
                  const((N_BRANCHES, d, d)), const((N_BRANCHES, BRANCH_DIM, d)), const((d, d)),
                  pl.BlockSpec(memory_space=pl.ANY)],
        out_specs=[seq(d), pl.BlockSpec((2, tm, d), lambda bi, i: (0, off + bi * per + i, 0))],
        out_shape=[jax.ShapeDtypeStruct((b, s, d), F32),
                   jax.ShapeDtypeStruct(h2_buf.shape, BF16)],
        input_output_aliases={18: 1},
        compiler_params=_params("parallel", "arbitrary"),
    )(x, mods, g1.reshape(1, d), g2.reshape(1, d), ya, yb, yd, p, p, p, p, p, p, p,
      conv_w, wg, wb, wo, h2_buf)


def _route_kernel(h2_ref, wr_ref, bias_ref, tri_ref, ones_ref, idx_ref, w_ref, rank_ref, cnt_ref,
                  score_sc, sel_sc, carry_sc):
    i = pl.program_id(0)
    tm = h2_ref.shape[0]
    ne = wr_ref.shape[0]
    gsz = ne // N_GROUPS
    n_chunks = tm // LANES

    @pl.when(i == 0)
    def _():
        carry_sc[...] = jnp.zeros(carry_sc.shape, F32)

    score_sc[...] = jax.nn.sigmoid(_dot_nt(wr_ref[...], h2_ref[...]))

    def select(cidx, carry):
        c0 = pl.multiple_of(cidx * LANES, LANES)
        scores = score_sc[:, pl.ds(c0, LANES)]
        biased = scores + bias_ref[...]
        liota = lax.broadcasted_iota(jnp.int32, (gsz, LANES), 0)
        gs = []
        for g in range(N_GROUPS):
            v = biased[g * gsz:(g + 1) * gsz]
            m1 = jnp.max(v, axis=0, keepdims=True)
            i1 = jnp.min(jnp.where(v == m1, liota, gsz), axis=0, keepdims=True)
            m2 = jnp.max(jnp.where(liota == i1, -jnp.inf, v), axis=0, keepdims=True)
            gs.append(m1 + m2)
        gsm = jnp.concatenate(gs, axis=0)
        giota = lax.broadcasted_iota(jnp.int32, gsm.shape, 0)
        keep = jnp.zeros(gsm.shape, F32)
        for _ in range(TOPK_GROUPS):
            m = jnp.max(gsm, axis=0, keepdims=True)
            gi = jnp.min(jnp.where(gsm == m, giota, N_GROUPS), axis=0, keepdims=True)
            hit = giota == gi
            keep = jnp.where(hit, 1.0, keep)
            gsm = jnp.where(hit, -jnp.inf, gsm)
        cur = jnp.concatenate(
            [jnp.where(jnp.broadcast_to(keep[g:g + 1], (gsz, LANES)) > 0.0, biased[g * gsz:(g + 1) * gsz], -jnp.inf)
             for g in range(N_GROUPS)], axis=0)
        eiota = lax.broadcasted_iota(jnp.int32, (ne, LANES), 0)
        candidates = cur
        idxs, ws = [], []
        for _ in range(TOP_K):
            m = jnp.max(cur, axis=0, keepdims=True)
            ik = jnp.min(jnp.where(cur == m, eiota, ne), axis=0, keepdims=True)
            hit = eiota == ik
            cur = jnp.where(hit, -jnp.inf, cur)
            ws.append(jnp.sum(jnp.where(hit, scores, 0.0), axis=0, keepdims=True))
            idxs.append(ik)
        sel = jnp.where(candidates != cur, 1.0, 0.0)
        w = jnp.concatenate(ws, axis=0)
        w_ref[:, pl.ds(c0, LANES)] = w / jnp.sum(w, axis=0, keepdims=True) * ROUTED_SCALE
        idx_ref[:, pl.ds(c0, LANES)] = jnp.concatenate(idxs, axis=0)
        sel_sc[:, pl.ds(c0, LANES)] = sel.astype(BF16)
        return carry

    lax.fori_loop(0, n_chunks, select, 0)

    sel_all = sel_sc[...]
    score_sc[...] = _dot(sel_all, tri_ref[...]) + jnp.concatenate([carry_sc[...]] * n_chunks, axis=1)

    def ranks(cidx, carry):
        c0 = pl.multiple_of(cidx * LANES, LANES)
        before = score_sc[:, pl.ds(c0, LANES)]
        idx = idx_ref[:, pl.ds(c0, LANES)]
        eiota = lax.broadcasted_iota(jnp.int32, (ne, LANES), 0)
        rows = [jnp.sum(jnp.where(eiota == idx[k:k + 1], before, 0.0), axis=0, keepdims=True)
                for k in range(TOP_K)]
        rank_ref[:, pl.ds(c0, LANES)] = jnp.concatenate(rows, axis=0).astype(jnp.int32)
        return carry

    lax.fori_loop(0, n_chunks, ranks, 0)
    carry_sc[...] += _dot(sel_all, ones_ref[...])
    cnt_ref[...] = carry_sc[...]


def _route(h2_all, n, wr_t, bias, tm):
    d = h2_all.shape[1]
    ne = wr_t.shape[0]
    tri = jnp.asarray(np.triu(np.ones((tm, tm), np.float32), 1), BF16)
    ones = jnp.ones((tm, LANES), BF16)
    bias_b = jnp.broadcast_to(bias.astype(F32)[:, None], (ne, LANES))
    const = lambda shape: pl.BlockSpec(shape, lambda i: (0,) * len(shape))
    tok = pl.BlockSpec((TOP_K, tm), lambda i: (0, i))
    return pl.pallas_call(
        _route_kernel,
        grid=(n // tm,),
        in_specs=[pl.BlockSpec((tm, d), lambda i: (i, 0)), const((ne, d)), const((ne, LANES)),
                  const((tm, tm)), const((tm, LANES))],
        out_specs=[tok, tok, tok, const((ne, LANES))],
        out_shape=[jax.ShapeDtypeStruct((TOP_K, n), jnp.int32),
                   jax.ShapeDtypeStruct((TOP_K, n), F32),
                   jax.ShapeDtypeStruct((TOP_K, n), jnp.int32),
                   jax.ShapeDtypeStruct((ne, LANES), F32)],
        scratch_shapes=[pltpu.VMEM((ne, tm), F32), pltpu.VMEM((ne, tm), BF16), pltpu.VMEM((ne, LANES), F32)],
        compiler_params=_params("arbitrary"),
    )(h2_all, wr_t, bias_b, tri, ones)


def _pos_kernel(idx_ref, rank_ref, start_ref, pos_ref):
    ne = start_ref.shape[0]
    start = start_ref[...]

    def body(cidx, carry):
        c0 = pl.multiple_of(cidx * LANES, LANES)
        idx = idx_ref[:, pl.ds(c0, LANES)]
        eiota = lax.broadcasted_iota(jnp.int32, (ne, LANES), 0)
        rows = [jnp.sum(jnp.where(eiota == idx[k:k + 1], start, 0.0), axis=0, keepdims=True)
                for k in range(TOP_K)]
        pos_ref[:, pl.ds(c0, LANES)] = jnp.concatenate(rows, axis=0).astype(jnp.int32) + rank_ref[:, pl.ds(c0, LANES)]
        return carry

    lax.fori_loop(0, idx_ref.shape[1] // LANES, body, 0)


def _positions(idx, rank, start_rows, tm):
    k, n = idx.shape
    ne = start_rows.shape[0]
    start_b = jnp.broadcast_to(start_rows.astype(F32)[:, None], (ne, LANES))
    tok = pl.BlockSpec((k, tm), lambda i: (0, i))
    return pl.pallas_call(
        _pos_kernel,
        grid=(n // tm,),
        in_specs=[tok, tok, pl.BlockSpec((ne, LANES), lambda i: (0, 0))],
        out_specs=tok,
        out_shape=jax.ShapeDtypeStruct((k, n), jnp.int32),
        compiler_params=_params("parallel"),
    )(idx, rank, start_b)


def _rowtok_kernel(start_ref, cnt_ref, pos_ref, out_ref, *, n_tokens):
    i = pl.program_id(0)
    tc = pos_ref.shape[0] // (TOP_K * LANES)
    n_rows = out_ref.shape[0]
    ne = start_ref.shape[0]

    @pl.when(i == 0)
    def _():
        def gaps(e, carry):
            lo = start_ref[e] + cnt_ref[e]
            hi = jnp.where(e + 1 < ne, start_ref[jnp.minimum(e + 1, ne - 1)], n_rows)

            def one(r, c):
                out_ref[r] = lax.rem(r, n_tokens)
                return c

            return lax.fori_loop(lo, hi, one, carry)

        lax.fori_loop(0, ne, gaps, 0)

    def tile(c, carry):
        t0 = (i * tc + c) * LANES

        def lane(l, carry2):
            src = c * (TOP_K * LANES) + l
            for k in range(TOP_K):
                out_ref[pos_ref[src + k * LANES]] = t0 + l
            return carry2

        return lax.fori_loop(0, LANES, lane, carry, unroll=8)

    lax.fori_loop(0, tc, tile, 0)


def _row_tokens(pos, seg_start, counts, n_rows):
    k, n = pos.shape
    n_tiles = n // LANES
    tc = max(t for t in range(1, 16 + 1) if n_tiles % t == 0)
    pos_tiles = jnp.transpose(pos.reshape(k, n_tiles, LANES), (1, 0, 2)).reshape(-1)
    grid_spec = pltpu.PrefetchScalarGridSpec(
        num_scalar_prefetch=2,
        grid=(n_tiles // tc,),
        in_specs=[pl.BlockSpec((tc * k * LANES,), lambda i, st, ct: (i,), memory_space=pltpu.SMEM)],
        out_specs=pl.BlockSpec((n_rows,), lambda i, st, ct: (0,), memory_space=pltpu.SMEM))
    return pl.pallas_call(
        functools.partial(_rowtok_kernel, n_tokens=n),
        grid_spec=grid_spec,
        out_shape=jax.ShapeDtypeStruct((n_rows,), jnp.int32),
        compiler_params=_params("arbitrary"),
    )(seg_start, counts, pos_tiles)


def _expert_kernel(crow_ref, crun_ref, rexp_ref, meta_ref, x_hbm, wg_hbm, wu_hbm, wd_hbm, y_hbm,
                   wg_buf, wu_buf, wd_buf, wg_bf, wu_bf, wd_bf, x_buf, y_buf, zero_buf,
                   w_sem, x_sem, y_sem, z_sem, *, layer):
    n_chunks, n_runs, tail_start = meta_ref[0], meta_ref[1], meta_ref[2]
    n_rows = x_hbm.shape[0]

    def w_copies(run):
        expert, s = rexp_ref[run], run % MOE_W_SLOTS
        return (pltpu.make_async_copy(wg_hbm.at[layer, expert], wg_buf.at[s], w_sem.at[s, 0]),
                pltpu.make_async_copy(wu_hbm.at[layer, expert], wu_buf.at[s], w_sem.at[s, 1]),
                pltpu.make_async_copy(wd_hbm.at[layer, expert], wd_buf.at[s], w_sem.at[s, 2]))

    def x_copy(c):
        rows = pl.ds(pl.multiple_of(crow_ref[c], MOE_ALIGN), MOE_ROWS)
        return pltpu.make_async_copy(x_hbm.at[rows], x_buf.at[c % MOE_X_SLOTS], x_sem.at[c % MOE_X_SLOTS])

    def y_copy(c):
        rows = pl.ds(pl.multiple_of(crow_ref[c], MOE_ALIGN), MOE_ROWS)
        return pltpu.make_async_copy(y_buf.at[c % 2], y_hbm.at[rows], y_sem.at[c % 2])

    def zero_copy(row0):
        return pltpu.make_async_copy(zero_buf, y_hbm.at[pl.ds(row0, MOE_ALIGN)], z_sem.at[0])

    for r in range(MOE_W_SLOTS - 1):
        @pl.when(r < n_runs)
        def _(r=r):
            for cp in w_copies(r):
                cp.start()

    for c in range(MOE_X_SLOTS - 1):
        @pl.when(c < n_chunks)
        def _(c=c):
            x_copy(c).start()

    def chunk(c, carry):
        run = crun_ref[c]
        first = jnp.logical_or(c == 0, crun_ref[jnp.maximum(c - 1, 0)] != run)

        @pl.when(c + MOE_X_SLOTS - 1 < n_chunks)
        def _():
            x_copy(c + MOE_X_SLOTS - 1).start()

        @pl.when(first)
        def _():
            @pl.when(run + MOE_W_SLOTS - 1 < n_runs)
            def _():
                for cp in w_copies(run + MOE_W_SLOTS - 1):
                    cp.start(priority=1)

            for cp in w_copies(run):
                cp.wait()
            s = run % MOE_W_SLOTS
            wg_bf[...] = wg_buf[s].astype(BF16)
            wu_bf[...] = wu_buf[s].astype(BF16)
            wd_bf[...] = wd_buf[s].astype(BF16)

        x_copy(c).wait()
        x = x_buf[c % MOE_X_SLOTS]
        a = (_silu(_dot(x, wg_bf[...])) * _dot(x, wu_bf[...])).astype(BF16)
        y = _dot(a, wd_bf[...]).astype(BF16)

        @pl.when(c > 0)
        def _():
            y_copy(c - 1).wait()

        y_buf[c % 2] = y
        y_copy(c).start()
        return carry

    lax.fori_loop(0, n_chunks, chunk, 0)

    @pl.when(n_chunks > 0)
    def _():
        y_copy(n_chunks - 1).wait()

    zero_buf[...] = jnp.zeros(zero_buf.shape, zero_buf.dtype)
    n_tail = (n_rows - tail_start) // MOE_ALIGN

    def fill(t, carry):
        zero_copy(pl.multiple_of(tail_start + t * MOE_ALIGN, MOE_ALIGN)).start()
        return carry

    def drain(t, carry):
        zero_copy(0).wait()
        return carry

    lax.fori_loop(0, n_tail, fill, 0)
    lax.fori_loop(0, n_tail, drain, 0)


def _experts(x_sorted, seg_start, counts, layer, w_g, w_u, w_d):
    n_rows, d = x_sorted.shape
    ne, f = w_g.shape[1], w_g.shape[3]
    nch = (counts + MOE_ROWS - 1) // MOE_ROWS
    c_end = jnp.cumsum(nch)
    max_chunks = (n_rows - MOE_ROWS) // MOE_ROWS + ne
    g = jnp.arange(max_chunks, dtype=jnp.int32)
    c_exp = jnp.minimum(jnp.sum((c_end[None, :] <= g[:, None]).astype(jnp.int32), axis=1), ne - 1)
    c_row = jnp.where(g < c_end[-1], seg_start[c_exp] + (g - (c_end - nch)[c_exp]) * MOE_ROWS, 0)
    has_rows = (nch > 0).astype(jnp.int32)
    run_end = jnp.cumsum(has_rows)
    c_run = (run_end - 1)[c_exp]
    r = jnp.arange(ne, dtype=jnp.int32)
    r_exp = jnp.minimum(jnp.sum((run_end[None, :] <= r[:, None]).astype(jnp.int32), axis=1), ne - 1)
    tail_start = jnp.max(jnp.where(nch > 0, seg_start + nch * MOE_ROWS, 0))
    meta = jnp.stack([c_end[-1], run_end[-1], tail_start]).astype(jnp.int32)
    any_spec = pl.BlockSpec(memory_space=pl.ANY)
    grid_spec = pltpu.PrefetchScalarGridSpec(
        num_scalar_prefetch=4,
        grid=(1,),
        in_specs=[any_spec] * 4,
        out_specs=any_spec,
        scratch_shapes=[pltpu.VMEM((MOE_W_SLOTS, d, f), F32), pltpu.VMEM((MOE_W_SLOTS, d, f), F32),
                        pltpu.VMEM((MOE_W_SLOTS, f, d), F32),
                        pltpu.VMEM((d, f), BF16), pltpu.VMEM((d, f), BF16), pltpu.VMEM((f, d), BF16),
                        pltpu.VMEM((MOE_X_SLOTS, MOE_ROWS, d), BF16), pltpu.VMEM((2, MOE_ROWS, d), BF16),
                        pltpu.VMEM((MOE_ALIGN, d), BF16),
                        pltpu.SemaphoreType.DMA((MOE_W_SLOTS, 3)), pltpu.SemaphoreType.DMA((MOE_X_SLOTS,)),
                        pltpu.SemaphoreType.DMA((2,)), pltpu.SemaphoreType.DMA((1,))])
    return pl.pallas_call(
        functools.partial(_expert_kernel, layer=layer),
        grid_spec=grid_spec,
        out_shape=jax.ShapeDtypeStruct((n_rows, d), BF16),
        compiler_params=_params("arbitrary"),
    )(c_row.astype(jnp.int32), c_run.astype(jnp.int32), r_exp, meta, x_sorted, w_g, w_u, w_d)


def _resid_kernel(x_ref, h2_ref, y_ref, w_ref, mod_ref, sg_ref, su_ref, sd_ref, gf_ref, o_ref, *, final):
    h2 = h2_ref[...]
    a = (_silu(_dot(h2, sg_ref[...])) * _dot(h2, su_ref[...])).astype(BF16)
    y = _dot(a, sd_ref[...])
    w = w_ref[...]
    for k in range(TOP_K):
        y = y + w[:, k:k + 1] * y_ref[k].astype(F32)
    xo = x_ref[...] + mod_ref[0][5:6] * y
    if final:
        xo = xo * lax.rsqrt(jnp.mean(xo * xo, axis=-1, keepdims=True) + EPS) * gf_ref[...]
    o_ref[...] = xo


def _resid(x_flat, h2_all, y_tok, w_tok, row_off, mods, mod_row, rows_per_mod, sg, su, sd, gf, final, tm):
    n, d = x_flat.shape
    f = sg.shape[1]
    off = row_off // tm
    per = rows_per_mod // tm
    mod_idx = (lambda i: (i // per, 0, 0)) if mod_row is None else (lambda i: (mod_row, 0, 0))
    const = lambda shape: pl.BlockSpec(shape, lambda i: (0,) * len(shape))
    return pl.pallas_call(
        functools.partial(_resid_kernel, final=final),
        grid=(n // tm,),
        in_specs=[pl.BlockSpec((tm, d), lambda i: (i, 0)),
                  pl.BlockSpec((tm, d), lambda i: (i + off, 0)),
                  pl.BlockSpec((TOP_K, tm, d), lambda i: (0, i + off, 0)),
                  pl.BlockSpec((tm, TOP_K), lambda i: (i + off, 0)),
                  pl.BlockSpec((1, N_MOD, d), mod_idx),
                  const((d, f)), const((d, f)), const((f, d)), const((1, d))],
        out_specs=pl.BlockSpec((tm, d), lambda i: (i, 0)),
        out_shape=jax.ShapeDtypeStruct((n, d), F32),
        compiler_params=_params("parallel"),
    )(x_flat, h2_all, y_tok, w_tok, mods, sg, su, sd, gf.reshape(1, d))


def _moe_routed(h2_dbl, wr_t, bias, layer, w_g, w_u, w_d, later):
    n, d = h2_dbl.shape[0] // 2, h2_dbl.shape[1]
    idx, wts, rank, cnt = _route(h2_dbl, n, wr_t, bias, TOKEN_TILE)
    counts = cnt[:, 0].astype(jnp.int32)
    padded = (counts + MOE_ALIGN - 1) // MOE_ALIGN * MOE_ALIGN
    seg_start = jnp.cumsum(padded) - padded
    n_rows = -(-(n * TOP_K + N_EXPERTS * (MOE_ALIGN - 1)) // MOE_ROWS) * MOE_ROWS + MOE_ROWS
    pos2 = _positions(idx, rank, seg_start, TOKEN_TILE)
    pos = pos2.reshape(TOP_K * n)
    row_tok, later = lax.optimization_barrier((_row_tokens(pos2, seg_start, counts, n_rows), later))
    x_sorted = h2_dbl.at[row_tok].get(mode='promise_in_bounds')
    y_sorted = _experts(x_sorted, seg_start, counts, layer, w_g, w_u, w_d)
    y_tok = y_sorted.at[pos].get(mode='promise_in_bounds', unique_indices=True)
    return y_tok.reshape(TOP_K, n, d), wts.T, later


def kernel(x, c, ctx, c_ctx, ada_w, ada_b, norm1_g, w_in, conv_w, na_rel_bias, diff_lambda,
           diff_subln_g, w_branch_gate, w_branch, w_out, norm2_g, router_w, router_bias,
           expert_w_gate, expert_w_up, expert_w_down, shared_w_gate, shared_w_up, shared_w_down,
           final_norm_g):
    b, s, d = x.shape
    l_ctx = ctx.shape[1]
    rows = s // GRID_W
    ctx_row = b
    cvec = jnp.zeros((8, d), F32).at[:b].set(c).at[ctx_row].set(c_ctx)
    rope_tabs = _rope_tables(s)
    wf = _channel_dft_matrix()
    tm = TOKEN_TILE
    xc = ctx
    mods = _ada(cvec, ada_w, ada_b, 0).reshape(8, N_MOD, d)
    bias_tabs = _na_bias_tables(na_rel_bias[0], rows)
    for layer in range(DEPTH):
        last = layer == DEPTH - 1
        lam_init = 0.8 - 0.6 * math.exp(-0.3 * layer)
        w_in_bf = w_in[layer].astype(BF16)
        wg_bf = w_branch_gate[layer].astype(BF16)
        wb_bf = w_branch[layer].astype(BF16)
        wo_bf = w_out[layer].astype(BF16)
        wr_bf = router_w[layer].T.astype(BF16)
        lp = diff_lambda[layer]
        sub_g = diff_subln_g[layer]

        kv0 = (jnp.zeros((b, CB, s + l_ctx), BF16), jnp.zeros((b, DIFF_HEADS, s + l_ctx, LANES), BF16))
        p, ur, ui, kt, vh = _inproj(x, norm1_g[layer], mods, w_in_bf, wf, rope_tabs, None, True, tm,
                                    0, kv0)
        ctx_tabs = tuple(t[:l_ctx] for t in rope_tabs)
        pc, ucr, uci, kt, vh = _inproj(xc, norm1_g[layer], mods, w_in_bf, wf, ctx_tabs, ctx_row, False,
                                       l_ctx, s, (kt, vh))

        ya = _fourier_latent(ur, ui)
        yb = _na_latent(p, pc, bias_tabs)
        yd = _diff_latent(p, kt, vh, lp, sub_g, lam_init, min(s, DIFF_Q_TILE),
                          _key_tile(s + l_ctx, DIFF_K_TILE_MAX))
        n_tok = b * s if last else b * (s + l_ctx)
        h2_buf = jnp.zeros((2, n_tok, d), BF16)
        x, h2_buf = _merge(x, mods, None, norm1_g[layer], norm2_g[layer], ya, yb, yd, p,
                           conv_w[layer], wg_bf, wb_bf, wo_bf, tm, h2_buf, 0)
        if not last:
            yac, ybc, ydc = _ctx_branches(pc, ucr, uci, lp, sub_g, lam_init)
            xc, h2_buf = _merge(xc, mods, ctx_row, norm1_g[layer], norm2_g[layer], yac, ybc, ydc, pc,
                                conv_w[layer], wg_bf, wb_bf, wo_bf, l_ctx, h2_buf, b * s)
        h2_all = h2_buf.reshape(2 * n_tok, d)

        later = () if last else (cvec, na_rel_bias[layer + 1])
        y_tok, w_tok, later = _moe_routed(h2_all, wr_bf, router_bias[layer], layer, expert_w_gate,
                                          expert_w_up, expert_w_down, later)
        sg_bf = shared_w_gate[layer].astype(BF16)
        su_bf = shared_w_up[layer].astype(BF16)
        sd_bf = shared_w_down[layer].astype(BF16)
        x = _resid(x.reshape(b * s, d), h2_all, y_tok, w_tok, 0, mods, None, s, sg_bf, su_bf, sd_bf,
                   final_norm_g, last, tm).reshape(b, s, d)
        if not last:
            xc = _resid(xc.reshape(b * l_ctx, d), h2_all, y_tok, w_tok, b * s, mods, ctx_row, l_ctx,
                        sg_bf, su_bf, sd_bf, final_norm_g, False, l_ctx).reshape(b, l_ctx, d)
            mods = _ada(later[0], ada_w, ada_b, layer + 1).reshape(8, N_MOD, d)
            bias_tabs = _na_bias_tables(later[1], rows)
    return x
```

```python
import functools
import math

import numpy as np
import jax
import jax.numpy as jnp
from jax import lax
from jax.experimental import pallas as pl
from jax.experimental.pallas import tpu as pltpu

F32 = jnp.float32
BF16 = jnp.bfloat16

DEPTH = 2
GRID_W = 64
EPS = 1e-6
N_MOD = 6

FNET_GROUP_DIM = 64
NA_HEADS = 4
NA_HEAD_DIM = 64
NA_WIN_ROWS = 8
NA_WIN_COLS = 16
CONV_WIDTH = 3
DIFF_HEADS = 4
DIFF_QK_DIM = 32
DIFF_V_DIM = 64
ROPE_BASE = 10000.0
N_BRANCHES = 4
BRANCH_DIM = 256

COL_A, COL_BQ, COL_BK, COL_BV, COL_CB, COL_CC, COL_CX, COL_DQ, COL_DK, COL_DV = range(10)
N_COL_BLOCKS = 10
CB = 256
IN_DIM = N_COL_BLOCKS * CB

N_EXPERTS = 256
TOP_K = 8
N_GROUPS = 8
TOPK_GROUPS = 4
ROUTED_SCALE = 2.5
LOG2_E = 1.4426950408889634

VMEM_LIMIT_BYTES = 56 * 1024 * 1024
LANES = 128
BF16_SUBLANES = 16
TOKEN_TILE = 512
DIFF_Q_TILE = 1024
DIFF_K_TILE_MAX = 768
FFT_N1 = 64
NA_MASKED = -1e30
NA_QROWS = 8
NA_KROWS = 16
MOE_ROWS = 256
MOE_ALIGN = BF16_SUBLANES
MOE_X_SLOTS = 4
MOE_W_SLOTS = 3


def _params(*sem):
    return pltpu.CompilerParams(dimension_semantics=sem, vmem_limit_bytes=VMEM_LIMIT_BYTES)


def _dot(a, b):
    return jnp.dot(a, b, preferred_element_type=F32)


def _dot_nt(a, b):
    return lax.dot_general(a, b, (((1,), (1,)), ((), ())), preferred_element_type=F32)


def _norm_mod(xf, g, shift, scale):
    y = xf * lax.rsqrt(jnp.mean(xf * xf, axis=-1, keepdims=True) + EPS)
    return (y * g) * (1.0 + scale) + shift


def _silu(v):
    return v * jax.nn.sigmoid(v)


def _ada_kernel(c_ref, w_ref, b_ref, o_ref):
    s = _silu(c_ref[...])
    o_ref[...] = _dot(s.astype(BF16), w_ref[0].astype(BF16)) + b_ref[0]


def _ada(cvec, w, b, layer):
    rows, d = cvec.shape
    depth, _, n = w.shape
    tn = 1536
    return pl.pallas_call(
        _ada_kernel,
        grid=(n // tn,),
        in_specs=[pl.BlockSpec((rows, d), lambda j: (0, 0)),
                  pl.BlockSpec((1, d, tn), lambda j: (layer, 0, j)),
                  pl.BlockSpec((1, 1, tn), lambda j: (layer, 0, j))],
        out_specs=pl.BlockSpec((rows, tn), lambda j: (0, j)),
        out_shape=jax.ShapeDtypeStruct((rows, n), F32),
        compiler_params=_params("arbitrary"),
    )(cvec, w, b.reshape(depth, 1, n))


def _inproj_kernel(x_ref, g_ref, mod_ref, w_ref, wf_ref, cos_ref, s1_ref, s2_ref, kt_prev, vh_prev,
                   p_ref, ur_ref, ui_ref, kt_ref, vh_ref, *, rope):
    del kt_prev, vh_prev
    mod = mod_ref[0]
    h = _norm_mod(x_ref[0], g_ref[...], mod[0:1], mod[1:2]).astype(BF16)
    for j in range(N_COL_BLOCKS):
        pj = _dot(h, w_ref[:, j * CB:(j + 1) * CB])
        if j == COL_A:
            u = _dot(pj.astype(BF16), wf_ref[...])
            ur_ref[0] = u[:, :CB].astype(BF16)
            ui_ref[0] = u[:, CB:].astype(BF16)
        if rope and j in (COL_DQ, COL_DK):
            cos = jnp.concatenate([cos_ref[...]] * 2, axis=1)
            s1 = jnp.concatenate([s1_ref[...]] * 2, axis=1)
            s2 = jnp.concatenate([s2_ref[...]] * 2, axis=1)
            pj = pj * cos + pltpu.roll(pj, CB - 8, 1) * s1 + pltpu.roll(pj, 8, 1) * s2
            if j == COL_DQ:
                pj = pj * (DIFF_QK_DIM ** -0.5 * LOG2_E)
        p_ref[0, :, j * CB:(j + 1) * CB] = pj.astype(BF16)
        if j == COL_DK:
            kt_ref[0] = pj.T.astype(BF16)
        if j == COL_DV:
            ones = jnp.ones((pj.shape[0], LANES - DIFF_V_DIM), F32)
            for hd in range(DIFF_HEADS):
                v = pj[:, hd * DIFF_V_DIM:(hd + 1) * DIFF_V_DIM]
                vh_ref[0, hd] = jnp.concatenate([v, ones], axis=1).astype(BF16)


def _inproj(x, g, mods, w_bf, wf, rope_tabs, mod_row, rope, tm, key_off, kv_prev):
    b, s, d = x.shape
    n_keys = kv_prev[0].shape[2]
    off = key_off // tm
    mod_idx = (lambda bi, i: (bi, 0, 0)) if mod_row is None else (lambda bi, i: (mod_row, 0, 0))
    tab_spec = pl.BlockSpec((tm, LANES), lambda bi, i: (i, 0))
    seq_spec = lambda width: pl.BlockSpec((1, tm, width), lambda bi, i: (bi, i, 0))
    in_specs = [seq_spec(d),
                pl.BlockSpec((1, d), lambda bi, i: (0, 0)),
                pl.BlockSpec((1, N_MOD, d), mod_idx),
                pl.BlockSpec((d, IN_DIM), lambda bi, i: (0, 0)),
                pl.BlockSpec((CB, 2 * CB), lambda bi, i: (0, 0)),
                tab_spec, tab_spec, tab_spec]
    args = [x, g.reshape(1, d), mods, w_bf, wf, *rope_tabs]
    aliases = {len(args): 3, len(args) + 1: 4}
    in_specs += [pl.BlockSpec(memory_space=pl.ANY)] * 2
    args += list(kv_prev)
    return pl.pallas_call(
        functools.partial(_inproj_kernel, rope=rope),
        grid=(b, s // tm),
        in_specs=in_specs,
        out_specs=[seq_spec(IN_DIM), seq_spec(CB), seq_spec(CB),
                   pl.BlockSpec((1, CB, tm), lambda bi, i: (bi, 0, i + off)),
                   pl.BlockSpec((1, DIFF_HEADS, tm, LANES), lambda bi, i: (bi, 0, i + off, 0))],
        out_shape=[jax.ShapeDtypeStruct((b, s, IN_DIM), BF16),
                   jax.ShapeDtypeStruct((b, s, CB), BF16),
                   jax.ShapeDtypeStruct((b, s, CB), BF16),
                   jax.ShapeDtypeStruct((b, CB, n_keys), BF16),
                   jax.ShapeDtypeStruct((b, DIFF_HEADS, n_keys, LANES), BF16)],
        input_output_aliases=aliases,
        compiler_params=_params("parallel", "arbitrary"),
    )(*args)


def _channel_dft_matrix():
    c = np.arange(FNET_GROUP_DIM)
    ang = 2.0 * np.pi * ((c[:, None] * c[None, :]) % FNET_GROUP_DIM) / FNET_GROUP_DIM
    eye = np.eye(CB // FNET_GROUP_DIM)
    m = np.concatenate([np.kron(eye, np.cos(ang)), -np.kron(eye, np.sin(ang))], axis=1)
    return jnp.asarray(m, BF16)


def _rope_tables(s):
    half = DIFF_QK_DIM // 2
    inv = 1.0 / (ROPE_BASE ** (jnp.arange(0, half, 2, dtype=F32) / half))
    lane = np.arange(LANES) % DIFF_QK_DIM
    is_col = (lane // half == 1)[None, :]
    is_x2 = (lane % half // (half // 2) == 1)[None, :]
    t = jnp.arange(s)[:, None]
    pos = jnp.where(is_col, t % GRID_W, t // GRID_W).astype(F32)
    ang = pos * inv[lane % (half // 2)][None, :]
    sin = jnp.sin(ang)
    s1 = jnp.where(is_x2, 0.0, -sin)
    s2 = jnp.where(is_x2, sin, 0.0)
    return jnp.cos(ang), s1, s2


def _fft1_kernel(ur_ref, ui_ref, w_ref, ct_ref, st_ref, ar_ref, ai_ref):
    n1 = ur_ref.shape[1]
    u = jnp.concatenate([ur_ref[0], ui_ref[0]], axis=0)
    a = _dot(w_ref[...], u)
    ar, ai = a[:n1], a[n1:]
    ct, st = ct_ref[...], st_ref[...]
    ar_ref[0] = (ar * ct + ai * st).astype(BF16)
    ai_ref[0] = (ai * ct - ar * st).astype(BF16)


def _fft2_kernel(ar_ref, ai_ref, w_ref, y_ref, *, norm):
    for j in range(ar_ref.shape[1]):
        a = jnp.concatenate([ar_ref[0, j], ai_ref[0, j]], axis=0)
        y_ref[0, j] = (_dot(w_ref[...], a) * norm).astype(BF16)


def _dft_cos_sin(n):
    k = np.arange(n)
    ang = 2.0 * np.pi * ((k[:, None] * k[None, :]) % n) / n
    return np.cos(ang), np.sin(ang)


def _fourier_latent(ur, ui):
    b, s, cb = ur.shape
    n1, n2 = FFT_N1, s // FFT_N1
    c1, s1 = _dft_cos_sin(n1)
    w1 = jnp.asarray(np.block([[c1, s1], [-s1, c1]]), BF16)
    c2, s2 = _dft_cos_sin(n2)
    w2 = jnp.asarray(np.concatenate([c2, s2], axis=1), BF16)
    tw = 2.0 * np.pi * (np.arange(n1)[:, None] * np.arange(n2)[None, :]) / s
    ct = jnp.asarray(np.repeat(np.cos(tw), cb, axis=1), F32)
    st = jnp.asarray(np.repeat(np.sin(tw), cb, axis=1), F32)
    lanes = n2 * cb
    tn = min(lanes, 4096)
    u_spec = pl.BlockSpec((1, n1, tn), lambda j, bi: (bi, 0, j))
    t_spec = pl.BlockSpec((n1, tn), lambda j, bi: (0, j))
    ar, ai = pl.pallas_call(
        _fft1_kernel,
        grid=(lanes // tn, b),
        in_specs=[u_spec, u_spec, pl.BlockSpec((2 * n1, 2 * n1), lambda j, bi: (0, 0)), t_spec, t_spec],
        out_specs=[u_spec, u_spec],
        out_shape=[jax.ShapeDtypeStruct((b, n1, lanes), BF16)] * 2,
        compiler_params=_params("arbitrary", "arbitrary"),
    )(ur.reshape(b, n1, lanes), ui.reshape(b, n1, lanes), w1, ct, st)
    kc = 8
    a_spec = pl.BlockSpec((1, kc, n2, cb), lambda bi, j: (bi, j, 0, 0))
    y = pl.pallas_call(
        functools.partial(_fft2_kernel, norm=1.0 / math.sqrt(s * FNET_GROUP_DIM)),
        grid=(b, n1 // kc),
        in_specs=[a_spec, a_spec, pl.BlockSpec((n2, 2 * n2), lambda bi, j: (0, 0))],
        out_specs=a_spec,
        out_shape=jax.ShapeDtypeStruct((b, n1, n2, cb), BF16),
        compiler_params=_params("parallel", "arbitrary"),
    )(ar.reshape(b, n1, n2, cb), ai.reshape(b, n1, n2, cb), w2)
    return jnp.transpose(y, (0, 2, 1, 3)).reshape(b, s, cb)


def _na_kernel(q_ref, k_ref, v_ref, kc_ref, vc_ref, bias_ref, o_ref, *, rows):
    rb = pl.program_id(1)
    kb = jnp.clip(rb * NA_QROWS - NA_WIN_ROWS // 2, 0, rows - NA_KROWS)
    nk = NA_KROWS * GRID_W
    tok0 = pl.multiple_of(kb * GRID_W, 256)
    scale = NA_HEAD_DIM ** -0.5
    outs = []
    for h in range(NA_HEADS):
        sl = slice(h * NA_HEAD_DIM, (h + 1) * NA_HEAD_DIM)
        q = q_ref[0, :, sl]
        s = _dot_nt(q, k_ref[0, pl.ds(tok0, nk), sl]) * scale + bias_ref[0, h]
        sc = _dot_nt(q, kc_ref[0, :, sl]) * scale
        m = jnp.maximum(jnp.max(s, axis=-1, keepdims=True), jnp.max(sc, axis=-1, keepdims=True))
        e = jnp.exp(s - m)
        ec = jnp.exp(sc - m)
        l = jnp.sum(e, axis=-1, keepdims=True) + jnp.sum(ec, axis=-1, keepdims=True)
        o = _dot(e.astype(BF16), v_ref[0, pl.ds(tok0, nk), sl]) + _dot(ec.astype(BF16), vc_ref[0, :, sl])
        outs.append(o / l)
    o_ref[0] = jnp.concatenate(outs, axis=1).astype(BF16)


def _na_bias_tables(rel_bias, rows):
    n_dr, n_dc = 2 * NA_WIN_ROWS - 1, 2 * NA_WIN_COLS - 1
    cq = np.arange(GRID_W)
    col_lo = np.clip(cq - NA_WIN_COLS // 2, 0, GRID_W - NA_WIN_COLS)
    col_ok = (cq[None, :] >= col_lo[:, None]) & (cq[None, :] < col_lo[:, None] + NA_WIN_COLS)
    dc_idx = np.where(col_ok, np.clip(cq[None, :] - cq[:, None] + NA_WIN_COLS - 1, 0, n_dc - 1), n_dc)
    dr_idx = []
    for r0 in (0, NA_QROWS, rows - NA_QROWS):
        kb = int(np.clip(r0 - NA_WIN_ROWS // 2, 0, rows - NA_KROWS))
        r = r0 + np.arange(NA_QROWS)
        rk = kb + np.arange(NA_KROWS)
        start = np.clip(r - NA_WIN_ROWS // 2, 0, rows - NA_WIN_ROWS)
        row_ok = (rk[None, :] >= start[:, None]) & (rk[None, :] < start[:, None] + NA_WIN_ROWS)
        dr_idx.append(np.where(row_ok, np.clip(rk[None, :] - r[:, None] + NA_WIN_ROWS - 1, 0, n_dr - 1), n_dr))
    oh_r = jnp.asarray(np.stack(dr_idx)[..., None] == np.arange(n_dr + 1), F32)
    oh_c = jnp.asarray(dc_idx[..., None] == np.arange(n_dc + 1), F32)
    bias = jnp.pad(rel_bias.astype(F32), ((0, 0), (0, 1), (0, 1)), constant_values=NA_MASKED)
    by_row = jnp.einsum('tqka,hab->thqkb', oh_r, bias, precision=lax.Precision.HIGHEST)
    tabs = jnp.einsum('thqkb,cdb->thqckd', by_row, oh_c, precision=lax.Precision.HIGHEST)
    return tabs.reshape(3, NA_HEADS, NA_QROWS * GRID_W, NA_KROWS * GRID_W)


def _na_latent(p, pc, bias_tabs):
    b, s, _ = p.shape
    l = pc.shape[1]
    rows = s // GRID_W
    nrb = rows // NA_QROWS
    tq = NA_QROWS * GRID_W
    nk = NA_KROWS * GRID_W
    return pl.pallas_call(
        functools.partial(_na_kernel, rows=rows),
        grid=(b, nrb),
        in_specs=[pl.BlockSpec((1, tq, CB), lambda bi, i: (bi, i, COL_BQ)),
                  pl.BlockSpec((1, s, CB), lambda bi, i: (bi, 0, COL_BK)),
                  pl.BlockSpec((1, s, CB), lambda bi, i: (bi, 0, COL_BV)),
                  pl.BlockSpec((1, l, CB), lambda bi, i: (bi, 0, COL_BK)),
                  pl.BlockSpec((1, l, CB), lambda bi, i: (bi, 0, COL_BV)),
                  pl.BlockSpec((1, NA_HEADS, tq, nk),
                               lambda bi, i: (jnp.minimum(i, 1) + (i == nrb - 1).astype(jnp.int32), 0, 0, 0))],
        out_specs=pl.BlockSpec((1, tq, CB), lambda bi, i: (bi, i, 0)),
        out_shape=jax.ShapeDtypeStruct((b, s, CB), BF16),
        compiler_params=_params("parallel", "arbitrary"),
    )(p, p, p, pc, pc, bias_tabs)


def _diff_lambda(lp, lam_init):
    return (jnp.exp(jnp.sum(lp[0:1] * lp[1:2], axis=-1, keepdims=True))
            - jnp.exp(jnp.sum(lp[2:3] * lp[3:4], axis=-1, keepdims=True)) + lam_init)


def _diff_finish(o, g, lam_init):
    y = o * lax.rsqrt(jnp.mean(o * o, axis=-1, keepdims=True) + EPS)
    return y * g * (1.0 - lam_init)


def _diff_kernel(q_ref, kt_ref, v_ref, lp_ref, g_ref, o_ref, m_sc, acc_sc, e_sc, *, tk, lam_init):
    tq = q_ref.shape[1]
    nk = kt_ref.shape[2] // tk
    m_sc[...] = jnp.full(m_sc.shape, -jnp.inf, F32)
    acc_sc[...] = jnp.zeros(acc_sc.shape, F32)

    def body(c, carry):
        k0 = pl.multiple_of(c * tk, LANES)
        for h in range(DIFF_HEADS):
            for m in range(2):
                rows = slice(m * tq, (m + 1) * tq)
                dims = slice((2 * h + m) * DIFF_QK_DIM, (2 * h + m + 1) * DIFF_QK_DIM)
                s = _dot(q_ref[0, :, dims], kt_ref[0, dims, pl.ds(k0, tk)])
                m_old = m_sc[h, rows]
                m_new = jnp.maximum(m_old, jnp.max(s, axis=-1, keepdims=True))
                e_sc[h, rows] = jnp.exp2(s - m_new[:, :1]).astype(BF16)
                acc_sc[h, rows] = jnp.exp2(m_old - m_new) * acc_sc[h, rows]
                m_sc[h, rows] = m_new
            acc_sc[h] += _dot(e_sc[h], v_ref[0, h, pl.ds(k0, tk), :])
        return carry

    lax.fori_loop(0, nk, body, 0)
    lam = _diff_lambda(lp_ref[...], lam_init)
    outs = []
    for h in range(DIFF_HEADS):
        acc = acc_sc[h]
        o = acc[:, :DIFF_V_DIM] / acc[:, DIFF_V_DIM:DIFF_V_DIM + 1]
        outs.append(_diff_finish(o[:tq] - lam * o[tq:], g_ref[...], lam_init))
    o_ref[0] = jnp.concatenate(outs, axis=1).astype(BF16)


def _key_tile(nkeys, cap):
    return max(t for t in range(LANES, cap + 1, LANES) if nkeys % t == 0)


def _diff_latent(p, kt, vh, lp, sub_g, lam_init, tq, tk):
    b, s, _ = p.shape
    nkeys = kt.shape[2]
    return pl.pallas_call(
        functools.partial(_diff_kernel, tk=tk, lam_init=lam_init),
        grid=(b, s // tq),
        in_specs=[pl.BlockSpec((1, tq, CB), lambda bi, i: (bi, i, COL_DQ)),
                  pl.BlockSpec((1, CB, nkeys), lambda bi, i: (bi, 0, 0), pipeline_mode=pl.Buffered(1)),
                  pl.BlockSpec((1, DIFF_HEADS, nkeys, LANES), lambda bi, i: (bi, 0, 0, 0),
                               pipeline_mode=pl.Buffered(1)),
                  pl.BlockSpec((4, DIFF_QK_DIM), lambda bi, i: (0, 0)),
                  pl.BlockSpec((1, DIFF_V_DIM), lambda bi, i: (0, 0))],
        out_specs=pl.BlockSpec((1, tq, CB), lambda bi, i: (bi, i, 0)),
        out_shape=jax.ShapeDtypeStruct((b, s, CB), BF16),
        scratch_shapes=[pltpu.VMEM((DIFF_HEADS, 2 * tq, LANES), F32),
                        pltpu.VMEM((DIFF_HEADS, 2 * tq, LANES), F32),
                        pltpu.VMEM((DIFF_HEADS, 2 * tq, tk), BF16)],
        compiler_params=_params("parallel", "arbitrary"),
    )(p, kt, vh, lp, sub_g.reshape(1, DIFF_V_DIM))


def _softmax_rows(s):
    e = jnp.exp(s - jnp.max(s, axis=-1, keepdims=True))
    return e / jnp.sum(e, axis=-1, keepdims=True)


def _ctx_kernel(pc_ref, ur_ref, ui_ref, wf_ref, lp_ref, g_ref, ya_ref, yb_ref, yd_ref, *, lam_init):
    l = pc_ref.shape[1]
    col = lambda j, lo, hi: pc_ref[0, :, j * CB + lo:j * CB + hi]
    u = jnp.concatenate([ur_ref[0], ui_ref[0]], axis=0)
    ya_ref[0] = (_dot(wf_ref[...], u) * (1.0 / math.sqrt(l * FNET_GROUP_DIM))).astype(BF16)
    outs = []
    for h in range(NA_HEADS):
        lo, hi = h * NA_HEAD_DIM, (h + 1) * NA_HEAD_DIM
        pr = _softmax_rows(_dot_nt(col(COL_BQ, lo, hi), col(COL_BK, lo, hi)) * NA_HEAD_DIM ** -0.5)
        outs.append(_dot(pr.astype(BF16), col(COL_BV, lo, hi)))
    yb_ref[0] = jnp.concatenate(outs, axis=1).astype(BF16)
    lam = _diff_lambda(lp_ref[...], lam_init)
    outs = []
    for h in range(DIFF_HEADS):
        pm = []
        for m in range(2):
            lo = (2 * h + m) * DIFF_QK_DIM
            pm.append(_softmax_rows(_dot_nt(col(COL_DQ, lo, lo + DIFF_QK_DIM), col(COL_DK, lo, lo + DIFF_QK_DIM))
                                    * DIFF_QK_DIM ** -0.5))
        a = (pm[0] - lam * pm[1]).astype(BF16)
        o = _dot(a, col(COL_DV, h * DIFF_V_DIM, (h + 1) * DIFF_V_DIM))
        outs.append(_diff_finish(o, g_ref[...], lam_init))
    yd_ref[0] = jnp.concatenate(outs, axis=1).astype(BF16)


def _ctx_branches(pc, ucr, uci, lp, sub_g, lam_init):
    b, l, _ = pc.shape
    c, s = _dft_cos_sin(l)
    wf = jnp.asarray(np.concatenate([c, s], axis=1), BF16)
    y_spec = pl.BlockSpec((1, l, CB), lambda bi: (bi, 0, 0))
    return pl.pallas_call(
        functools.partial(_ctx_kernel, lam_init=lam_init),
        grid=(b,),
        in_specs=[pl.BlockSpec((1, l, IN_DIM), lambda bi: (bi, 0, 0)), y_spec, y_spec,
                  pl.BlockSpec((l, 2 * l), lambda bi: (0, 0)),
                  pl.BlockSpec((4, DIFF_QK_DIM), lambda bi: (0, 0)),
                  pl.BlockSpec((1, DIFF_V_DIM), lambda bi: (0, 0))],
        out_specs=[y_spec] * 3,
        out_shape=[jax.ShapeDtypeStruct((b, l, CB), BF16)] * 3,
        compiler_params=_params("parallel"),
    )(pc, ucr, uci, wf, lp, sub_g.reshape(1, DIFF_V_DIM))


def _merge_kernel(x_ref, mod_ref, g1_ref, g2_ref, ya_ref, yb_ref, yd_ref, pb_ref, pc_ref, px_ref,
                  cp_ref, xp_ref, cn_ref, xn_ref, cw_ref, wg_ref, wb_ref, wo_ref, h2_prev,
                  xo_ref, h2_ref):
    del h2_prev
    i = pl.program_id(1)
    last = pl.num_programs(1) - 1
    tm = x_ref.shape[1]
    mod = mod_ref[0]
    x = x_ref[0]
    h = _norm_mod(x, g1_ref[...], mod[0:1], mod[1:2]).astype(BF16)

    u = pc_ref[0].astype(F32) * px_ref[0].astype(F32)
    up = cp_ref[0, BF16_SUBLANES - 1:, :].astype(F32) * xp_ref[0, BF16_SUBLANES - 1:, :].astype(F32)
    un = cn_ref[0, :1, :].astype(F32) * xn_ref[0, :1, :].astype(F32)
    up = jnp.where(i == 0, 0.0, up)
    un = jnp.where(i == last, 0.0, un)
    rid = lax.broadcasted_iota(jnp.int32, u.shape, 0)
    u_prev = jnp.where(rid == 0, up, pltpu.roll(u, 1, 0))
    u_next = jnp.where(rid == tm - 1, un, pltpu.roll(u, tm - 1, 0))
    cw = cw_ref[...]
    yc = pb_ref[0].astype(F32) * (cw[0:1] * u_prev + cw[1:2] * u + cw[2:3] * u_next)

    branches = (ya_ref[0], yb_ref[0], yc.astype(BF16), yd_ref[0])
    d = x.shape[1]
    out = None
    for n in range(d // CB):
        cols = slice(n * CB, (n + 1) * CB)
        merged = None
        for j in range(N_BRANCHES):
            t = jax.nn.sigmoid(_dot(h, wg_ref[j, :, cols])) * _dot(branches[j], wb_ref[j, :, cols])
            merged = t if merged is None else merged + t
        t = _dot(merged.astype(BF16), wo_ref[cols, :])
        out = t if out is None else out + t
    xn = x + mod[2:3] * out
    xo_ref[0] = xn
    h2 = _norm_mod(xn, g2_ref[...], mod[3:4], mod[4:5]).astype(BF16)
    h2_ref[0] = h2
    h2_ref[1] = h2


def _merge(x, mods, mod_row, g1, g2, ya, yb, yd, p, conv_w, wg, wb, wo, tm, h2_buf, tok_off):
    b, s, d = x.shape
    off = tok_off // tm
    per = s // tm
    hb = tm // BF16_SUBLANES
    n_halo = s // BF16_SUBLANES
    mod_idx = (lambda bi, i: (bi, 0, 0)) if mod_row is None else (lambda bi, i: (mod_row, 0, 0))
    seq = lambda width, col=0: pl.BlockSpec((1, tm, width), lambda bi, i: (bi, i, col))
    prev = lambda col: pl.BlockSpec((1, BF16_SUBLANES, CB), lambda bi, i: (bi, jnp.maximum(i * hb - 1, 0), col))
    nxt = lambda col: pl.BlockSpec((1, BF16_SUBLANES, CB),
                                   lambda bi, i: (bi, jnp.minimum((i + 1) * hb, n_halo - 1), col))
    const = lambda shape: pl.BlockSpec(shape, lambda bi, i: (0,) * len(shape))
    return pl.pallas_call(
        _merge_kernel,
        grid=(b, s // tm),
        in_specs=[seq(d), pl.BlockSpec((1, N_MOD, d), mod_idx), const((1, d)), const((1, d)),
                  seq(CB), seq(CB), seq(CB),
                  seq(CB, COL_CB), seq(CB, COL_CC), seq(CB, COL_CX),
                  prev(COL_CC), prev(COL_CX), nxt(COL_CC), nxt(COL_CX),
                  const((CONV_WIDTH, CB)),
                  const((N_BRANCHES, d, d)), const((N_BRANCHES, BRANCH_DIM, d)), const((d, d)),
                  pl.BlockSpec(memory_space=pl.ANY)],
        out_specs=[seq(d), pl.BlockSpec((2, tm, d), lambda bi, i: (0, off + bi * per + i, 0))],
        out_shape=[jax.ShapeDtypeStruct((b, s, d), F32),
                   jax.ShapeDtypeStruct(h2_buf.shape, BF16)],
        input_output_aliases={18: 1},
        compiler_params=_params("parallel", "arbitrary"),
    )(x, mods, g1.reshape(1, d), g2.reshape(1, d), ya, yb, yd, p, p, p, p, p, p, p,
      conv_w, wg, wb, wo, h2_buf)


def _route_kernel(h2_ref, wr_ref, bias_ref, tri_ref, ones_ref, idx_ref, w_ref, rank_ref, cnt_ref,
                  score_sc, sel_sc, carry_sc):
    i = pl.program_id(0)
    tm = h2_ref.shape[0]
    ne = wr_ref.shape[0]
    gsz = ne // N_GROUPS
    n_chunks = tm // LANES

    @pl.when(i == 0)
    def _():
        carry_sc[...] = jnp.zeros(carry_sc.shape, F32)

    score_sc[...] = jax.nn.sigmoid(_dot_nt(wr_ref[...], h2_ref[...]))

    def select(cidx, carry):
        c0 = pl.multiple_of(cidx * LANES, LANES)
        scores = score_sc[:, pl.ds(c0, LANES)]
        biased = scores + bias_ref[...]
        liota = lax.broadcasted_iota(jnp.int32, (gsz, LANES), 0)
        gs = []
        for g in range(N_GROUPS):
            v = biased[g * gsz:(g + 1) * gsz]
            m1 = jnp.max(v, axis=0, keepdims=True)
            i1 = jnp.min(jnp.where(v == m1, liota, gsz), axis=0, keepdims=True)
            m2 = jnp.max(jnp.where(liota == i1, -jnp.inf, v), axis=0, keepdims=True)
            gs.append(m1 + m2)
        gsm = jnp.concatenate(gs, axis=0)
        giota = lax.broadcasted_iota(jnp.int32, gsm.shape, 0)
        keep = jnp.zeros(gsm.shape, F32)
        for _ in range(TOPK_GROUPS):
            m = jnp.max(gsm, axis=0, keepdims=True)
            gi = jnp.min(jnp.where(gsm == m, giota, N_GROUPS), axis=0, keepdims=True)
            hit = giota == gi
            keep = jnp.where(hit, 1.0, keep)
            gsm = jnp.where(hit, -jnp.inf, gsm)
        cur = jnp.concatenate(
            [jnp.where(jnp.broadcast_to(keep[g:g + 1], (gsz, LANES)) > 0.0, biased[g * gsz:(g + 1) * gsz], -jnp.inf)
             for g in range(N_GROUPS)], axis=0)
        eiota = lax.broadcasted_iota(jnp.int32, (ne, LANES), 0)
        candidates = cur
        idxs, ws = [], []
        for _ in range(TOP_K):
            m = jnp.max(cur, axis=0, keepdims=True)
            ik = jnp.min(jnp.where(cur == m, eiota, ne), axis=0, keepdims=True)
            hit = eiota == ik
            cur = jnp.where(hit, -jnp.inf, cur)
            ws.append(jnp.sum(jnp.where(hit, scores, 0.0), axis=0, keepdims=True))
            idxs.append(ik)
        sel = jnp.where(candidates != cur, 1.0, 0.0)
        w = jnp.concatenate(ws, axis=0)
        w_ref[:, pl.ds(c0, LANES)] = w / jnp.sum(w, axis=0, keepdims=True) * ROUTED_SCALE
        idx_ref[:, pl.ds(c0, LANES)] = jnp.concatenate(idxs, axis=0)
        sel_sc[:, pl.ds(c0, LANES)] = sel.astype(BF16)
        return carry

    lax.fori_loop(0, n_chunks, select, 0)

    sel_all = sel_sc[...]
    score_sc[...] = _dot(sel_all, tri_ref[...]) + jnp.concatenate([carry_sc[...]] * n_chunks, axis=1)

    def ranks(cidx, carry):
        c0 = pl.multiple_of(cidx * LANES, LANES)
        before = score_sc[:, pl.ds(c0, LANES)]
        idx = idx_ref[:, pl.ds(c0, LANES)]
        eiota = lax.broadcasted_iota(jnp.int32, (ne, LANES), 0)
        rows = [jnp.sum(jnp.where(eiota == idx[k:k + 1], before, 0.0), axis=0, keepdims=True)
                for k in range(TOP_K)]
        rank_ref[:, pl.ds(c0, LANES)] = jnp.concatenate(rows, axis=0).astype(jnp.int32)
        return carry

    lax.fori_loop(0, n_chunks, ranks, 0)
    carry_sc[...] += _dot(sel_all, ones_ref[...])
    cnt_ref[...] = carry_sc[...]


def _route(h2_all, n, wr_t, bias, tm):
    d = h2_all.shape[1]
    ne = wr_t.shape[0]
    tri = jnp.asarray(np.triu(np.ones((tm, tm), np.float32), 1), BF16)
    ones = jnp.ones((tm, LANES), BF16)
    bias_b = jnp.broadcast_to(bias.astype(F32)[:, None], (ne, LANES))
    const = lambda shape: pl.BlockSpec(shape, lambda i: (0,) * len(shape))
    tok = pl.BlockSpec((TOP_K, tm), lambda i: (0, i))
    return pl.pallas_call(
        _route_kernel,
        grid=(n // tm,),
        in_specs=[pl.BlockSpec((tm, d), lambda i: (i, 0)), const((ne, d)), const((ne, LANES)),
                  const((tm, tm)), const((tm, LANES))],
        out_specs=[tok, tok, tok, const((ne, LANES))],
        out_shape=[jax.ShapeDtypeStruct((TOP_K, n), jnp.int32),
                   jax.ShapeDtypeStruct((TOP_K, n), F32),
                   jax.ShapeDtypeStruct((TOP_K, n), jnp.int32),
                   jax.ShapeDtypeStruct((ne, LANES), F32)],
        scratch_shapes=[pltpu.VMEM((ne, tm), F32), pltpu.VMEM((ne, tm), BF16), pltpu.VMEM((ne, LANES), F32)],
        compiler_params=_params("arbitrary"),
    )(h2_all, wr_t, bias_b, tri, ones)


def _pos_kernel(idx_ref, rank_ref, start_ref, pos_ref):
    ne = start_ref.shape[0]
    start = start_ref[...]

    def body(cidx, carry):
        c0 = pl.multiple_of(cidx * LANES, LANES)
        idx = idx_ref[:, pl.ds(c0, LANES)]
        eiota = lax.broadcasted_iota(jnp.int32, (ne, LANES), 0)
        rows = [jnp.sum(jnp.where(eiota == idx[k:k + 1], start, 0.0), axis=0, keepdims=True)
                for k in range(TOP_K)]
        pos_ref[:, pl.ds(c0, LANES)] = jnp.concatenate(rows, axis=0).astype(jnp.int32) + rank_ref[:, pl.ds(c0, LANES)]
        return carry

    lax.fori_loop(0, idx_ref.shape[1] // LANES, body, 0)


def _positions(idx, rank, start_rows, tm):
    k, n = idx.shape
    ne = start_rows.shape[0]
    start_b = jnp.broadcast_to(start_rows.astype(F32)[:, None], (ne, LANES))
    tok = pl.BlockSpec((k, tm), lambda i: (0, i))
    return pl.pallas_call(
        _pos_kernel,
        grid=(n // tm,),
        in_specs=[tok, tok, pl.BlockSpec((ne, LANES), lambda i: (0, 0))],
        out_specs=tok,
        out_shape=jax.ShapeDtypeStruct((k, n), jnp.int32),
        compiler_params=_params("parallel"),
    )(idx, rank, start_b)


def _rowtok_kernel(start_ref, cnt_ref, pos_ref, out_ref, *, n_tokens):
    i = pl.program_id(0)
    tc = pos_ref.shape[0] // (TOP_K * LANES)
    n_rows = out_ref.shape[0]
    ne = start_ref.shape[0]

    @pl.when(i == 0)
    def _():
        def gaps(e, carry):
            lo = start_ref[e] + cnt_ref[e]
            hi = jnp.where(e + 1 < ne, start_ref[jnp.minimum(e + 1, ne - 1)], n_rows)

            def one(r, c):
                out_ref[r] = lax.rem(r, n_tokens)
                return c

            return lax.fori_loop(lo, hi, one, carry)

        lax.fori_loop(0, ne, gaps, 0)

    def tile(c, carry):
        t0 = (i * tc + c) * LANES

        def lane(l, carry2):
            src = c * (TOP_K * LANES) + l
            for k in range(TOP_K):
                out_ref[pos_ref[src + k * LANES]] = t0 + l
            return carry2

        return lax.fori_loop(0, LANES, lane, carry, unroll=8)

    lax.fori_loop(0, tc, tile, 0)


def _row_tokens(pos, seg_start, counts, n_rows):
    k, n = pos.shape
    n_tiles = n // LANES
    tc = max(t for t in range(1, 16 + 1) if n_tiles % t == 0)
    pos_tiles = jnp.transpose(pos.reshape(k, n_tiles, LANES), (1, 0, 2)).reshape(-1)
    grid_spec = pltpu.PrefetchScalarGridSpec(
        num_scalar_prefetch=2,
        grid=(n_tiles // tc,),
        in_specs=[pl.BlockSpec((tc * k * LANES,), lambda i, st, ct: (i,), memory_space=pltpu.SMEM)],
        out_specs=pl.BlockSpec((n_rows,), lambda i, st, ct: (0,), memory_space=pltpu.SMEM))
    return pl.pallas_call(
        functools.partial(_rowtok_kernel, n_tokens=n),
        grid_spec=grid_spec,
        out_shape=jax.ShapeDtypeStruct((n_rows,), jnp.int32),
        compiler_params=_params("arbitrary"),
    )(seg_start, counts, pos_tiles)


def _expert_kernel(crow_ref, crun_ref, rexp_ref, meta_ref, x_hbm, wg_hbm, wu_hbm, wd_hbm, y_hbm,
                   wg_buf, wu_buf, wd_buf, wg_bf, wu_bf, wd_bf, x_buf, y_buf, zero_buf,
                   w_sem, x_sem, y_sem, z_sem, *, layer):
    n_chunks, n_runs, tail_start = meta_ref[0], meta_ref[1], meta_ref[2]
    n_rows = x_hbm.shape[0]

    def w_copies(run):
        expert, s = rexp_ref[run], run % MOE_W_SLOTS
        return (pltpu.make_async_copy(wg_hbm.at[layer, expert], wg_buf.at[s], w_sem.at[s, 0]),
                pltpu.make_async_copy(wu_hbm.at[layer, expert], wu_buf.at[s], w_sem.at[s, 1]),
                pltpu.make_async_copy(wd_hbm.at[layer, expert], wd_buf.at[s], w_sem.at[s, 2]))

    def x_copy(c):
        rows = pl.ds(pl.multiple_of(crow_ref[c], MOE_ALIGN), MOE_ROWS)
        return pltpu.make_async_copy(x_hbm.at[rows], x_buf.at[c % MOE_X_SLOTS], x_sem.at[c % MOE_X_SLOTS])

    def y_copy(c):
        rows = pl.ds(pl.multiple_of(crow_ref[c], MOE_ALIGN), MOE_ROWS)
        return pltpu.make_async_copy(y_buf.at[c % 2], y_hbm.at[rows], y_sem.at[c % 2])

    def zero_copy(row0):
        return pltpu.make_async_copy(zero_buf, y_hbm.at[pl.ds(row0, MOE_ALIGN)], z_sem.at[0])

    for r in range(MOE_W_SLOTS - 1):
        @pl.when(r < n_runs)
        def _(r=r):
            for cp in w_copies(r):
                cp.start()

    for c in range(MOE_X_SLOTS - 1):
        @pl.when(c < n_chunks)
        def _(c=c):
            x_copy(c).start()

    def chunk(c, carry):
        run = crun_ref[c]
        first = jnp.logical_or(c == 0, crun_ref[jnp.maximum(c - 1, 0)] != run)

        @pl.when(c + MOE_X_SLOTS - 1 < n_chunks)
        def _():
            x_copy(c + MOE_X_SLOTS - 1).start()

        @pl.when(first)
        def _():
            @pl.when(run + MOE_W_SLOTS - 1 < n_runs)
            def _():
                for cp in w_copies(run + MOE_W_SLOTS - 1):
                    cp.start(priority=1)

            for cp in w_copies(run):
                cp.wait()
            s = run % MOE_W_SLOTS
            wg_bf[...] = wg_buf[s].astype(BF16)
            wu_bf[...] = wu_buf[s].astype(BF16)
            wd_bf[...] = wd_buf[s].astype(BF16)

        x_copy(c).wait()
        x = x_buf[c % MOE_X_SLOTS]
        a = (_silu(_dot(x, wg_bf[...])) * _dot(x, wu_bf[...])).astype(BF16)
        y = _dot(a, wd_bf[...]).astype(BF16)

        @pl.when(c > 0)
        def _():
            y_copy(c - 1).wait()

        y_buf[c % 2] = y
        y_copy(c).start()
        return carry

    lax.fori_loop(0, n_chunks, chunk, 0)

    @pl.when(n_chunks > 0)
    def _():
        y_copy(n_chunks - 1).wait()

    zero_buf[...] = jnp.zeros(zero_buf.shape, zero_buf.dtype)
    n_tail = (n_rows - tail_start) // MOE_ALIGN

    def fill(t, carry):
        zero_copy(pl.multiple_of(tail_start + t * MOE_ALIGN, MOE_ALIGN)).start()
        return carry

    def drain(t, carry):
        zero_copy(0).wait()
        return carry

    lax.fori_loop(0, n_tail, fill, 0)
    lax.fori_loop(0, n_tail, drain, 0)


def _experts(x_sorted, seg_start, counts, layer, w_g, w_u, w_d):
    n_rows, d = x_sorted.shape
    ne, f = w_g.shape[1], w_g.shape[3]
    nch = (counts + MOE_ROWS - 1) // MOE_ROWS
    c_end = jnp.cumsum(nch)
    max_chunks = (n_rows - MOE_ROWS) // MOE_ROWS + ne
    g = jnp.arange(max_chunks, dtype=jnp.int32)
    c_exp = jnp.minimum(jnp.sum((c_end[None, :] <= g[:, None]).astype(jnp.int32), axis=1), ne - 1)
    c_row = jnp.where(g < c_end[-1], seg_start[c_exp] + (g - (c_end - nch)[c_exp]) * MOE_ROWS, 0)
    has_rows = (nch > 0).astype(jnp.int32)
    run_end = jnp.cumsum(has_rows)
    c_run = (run_end - 1)[c_exp]
    r = jnp.arange(ne, dtype=jnp.int32)
    r_exp = jnp.minimum(jnp.sum((run_end[None, :] <= r[:, None]).astype(jnp.int32), axis=1), ne - 1)
    tail_start = jnp.max(jnp.where(nch > 0, seg_start + nch * MOE_ROWS, 0))
    meta = jnp.stack([c_end[-1], run_end[-1], tail_start]).astype(jnp.int32)
    any_spec = pl.BlockSpec(memory_space=pl.ANY)
    grid_spec = pltpu.PrefetchScalarGridSpec(
        num_scalar_prefetch=4,
        grid=(1,),
        in_specs=[any_spec] * 4,
        out_specs=any_spec,
        scratch_shapes=[pltpu.VMEM((MOE_W_SLOTS, d, f), F32), pltpu.VMEM((MOE_W_SLOTS, d, f), F32),
                        pltpu.VMEM((MOE_W_SLOTS, f, d), F32),
                        pltpu.VMEM((d, f), BF16), pltpu.VMEM((d, f), BF16), pltpu.VMEM((f, d), BF16),
                        pltpu.VMEM((MOE_X_SLOTS, MOE_ROWS, d), BF16), pltpu.VMEM((2, MOE_ROWS, d), BF16),
                        pltpu.VMEM((MOE_ALIGN, d), BF16),
                        pltpu.SemaphoreType.DMA((MOE_W_SLOTS, 3)), pltpu.SemaphoreType.DMA((MOE_X_SLOTS,)),
                        pltpu.SemaphoreType.DMA((2,)), pltpu.SemaphoreType.DMA((1,))])
    return pl.pallas_call(
        functools.partial(_expert_kernel, layer=layer),
        grid_spec=grid_spec,
        out_shape=jax.ShapeDtypeStruct((n_rows, d), BF16),
        compiler_params=_params("arbitrary"),
    )(c_row.astype(jnp.int32), c_run.astype(jnp.int32), r_exp, meta, x_sorted, w_g, w_u, w_d)


def _shared_kernel(x_ref, h2_ref, mod_ref, sg_ref, su_ref, sd_ref, o_ref):
    h2 = h2_ref[...]
    a = (_silu(_dot(h2, sg_ref[...])) * _dot(h2, su_ref[...])).astype(BF16)
    o_ref[...] = x_ref[...] + mod_ref[0][5:6] * _dot(a, sd_ref[...])


def _combine_kernel(x_ref, y_ref, w_ref, mod_ref, gf_ref, o_ref, *, final):
    w = w_ref[...]
    y = w[:, 0:1] * y_ref[0].astype(F32)
    for k in range(1, TOP_K):
        y = y + w[:, k:k + 1] * y_ref[k].astype(F32)
    xo = x_ref[...] + mod_ref[0][5:6] * y
    if final:
        xo = xo * lax.rsqrt(jnp.mean(xo * xo, axis=-1, keepdims=True) + EPS) * gf_ref[...]
    o_ref[...] = xo


def _shared(x_flat, h2_all, row_off, mods, mod_row, rows_per_mod, sg, su, sd, tm):
    n, d = x_flat.shape
    f = sg.shape[1]
    off = row_off // tm
    per = rows_per_mod // tm
    mod_idx = (lambda i: (i // per, 0, 0)) if mod_row is None else (lambda i: (mod_row, 0, 0))
    const = lambda shape: pl.BlockSpec(shape, lambda i: (0,) * len(shape))
    return pl.pallas_call(
        _shared_kernel,
        grid=(n // tm,),
        in_specs=[pl.BlockSpec((tm, d), lambda i: (i, 0)),
                  pl.BlockSpec((tm, d), lambda i: (i + off, 0)),
                  pl.BlockSpec((1, N_MOD, d), mod_idx),
                  const((d, f)), const((d, f)), const((f, d))],
        out_specs=pl.BlockSpec((tm, d), lambda i: (i, 0)),
        out_shape=jax.ShapeDtypeStruct((n, d), F32),
        compiler_params=_params("parallel"),
    )(x_flat, h2_all, mods, sg, su, sd)


def _combine(x_flat, y_tok, w_tok, row_off, mods, mod_row, rows_per_mod, gf, final, tm):
    n, d = x_flat.shape
    off = row_off // tm
    per = rows_per_mod // tm
    mod_idx = (lambda i: (i // per, 0, 0)) if mod_row is None else (lambda i: (mod_row, 0, 0))
    return pl.pallas_call(
        functools.partial(_combine_kernel, final=final),
        grid=(n // tm,),
        in_specs=[pl.BlockSpec((tm, d), lambda i: (i, 0)),
                  pl.BlockSpec((TOP_K, tm, d), lambda i: (0, i + off, 0)),
                  pl.BlockSpec((tm, TOP_K), lambda i: (i + off, 0)),
                  pl.BlockSpec((1, N_MOD, d), mod_idx),
                  pl.BlockSpec((1, d), lambda i: (0, 0))],
        out_specs=pl.BlockSpec((tm, d), lambda i: (i, 0)),
        out_shape=jax.ShapeDtypeStruct((n, d), F32),
        compiler_params=_params("parallel"),
    )(x_flat, y_tok, w_tok, mods, gf.reshape(1, d))


def _moe_routed(h2_dbl, wr_t, bias, layer, w_g, w_u, w_d, later):
    n, d = h2_dbl.shape[0] // 2, h2_dbl.shape[1]
    idx, wts, rank, cnt = _route(h2_dbl, n, wr_t, bias, TOKEN_TILE)
    counts = cnt[:, 0].astype(jnp.int32)
    padded = (counts + MOE_ALIGN - 1) // MOE_ALIGN * MOE_ALIGN
    seg_start = jnp.cumsum(padded) - padded
    n_rows = -(-(n * TOP_K + N_EXPERTS * (MOE_ALIGN - 1)) // MOE_ROWS) * MOE_ROWS + MOE_ROWS
    pos2 = _positions(idx, rank, seg_start, TOKEN_TILE)
    pos = pos2.reshape(TOP_K * n)
    row_tok, later = lax.optimization_barrier((_row_tokens(pos2, seg_start, counts, n_rows), later))
    x_sorted = h2_dbl.at[row_tok].get(mode='promise_in_bounds')
    y_sorted = _experts(x_sorted, seg_start, counts, layer, w_g, w_u, w_d)
    y_tok = y_sorted.at[pos].get(mode='promise_in_bounds', unique_indices=True)
    return y_tok.reshape(TOP_K, n, d), wts.T, later


def kernel(x, c, ctx, c_ctx, ada_w, ada_b, norm1_g, w_in, conv_w, na_rel_bias, diff_lambda,
           diff_subln_g, w_branch_gate, w_branch, w_out, norm2_g, router_w, router_bias,
           expert_w_gate, expert_w_up, expert_w_down, shared_w_gate, shared_w_up, shared_w_down,
           final_norm_g):
    b, s, d = x.shape
    l_ctx = ctx.shape[1]
    rows = s // GRID_W
    ctx_row = b
    cvec = jnp.zeros((8, d), F32).at[:b].set(c).at[ctx_row].set(c_ctx)
    rope_tabs = _rope_tables(s)
    wf = _channel_dft_matrix()
    tm = TOKEN_TILE
    xc = ctx
    mods = _ada(cvec, ada_w, ada_b, 0).reshape(8, N_MOD, d)
    bias_tabs = _na_bias_tables(na_rel_bias[0], rows)
    for layer in range(DEPTH):
        last = layer == DEPTH - 1
        lam_init = 0.8 - 0.6 * math.exp(-0.3 * layer)
        w_in_bf = w_in[layer].astype(BF16)
        wg_bf = w_branch_gate[layer].astype(BF16)
        wb_bf = w_branch[layer].astype(BF16)
        wo_bf = w_out[layer].astype(BF16)
        wr_bf = router_w[layer].T.astype(BF16)
        lp = diff_lambda[layer]
        sub_g = diff_subln_g[layer]

        kv0 = (jnp.zeros((b, CB, s + l_ctx), BF16), jnp.zeros((b, DIFF_HEADS, s + l_ctx, LANES), BF16))
        p, ur, ui, kt, vh = _inproj(x, norm1_g[layer], mods, w_in_bf, wf, rope_tabs, None, True, tm,
                                    0, kv0)
        ctx_tabs = tuple(t[:l_ctx] for t in rope_tabs)
        pc, ucr, uci, kt, vh = _inproj(xc, norm1_g[layer], mods, w_in_bf, wf, ctx_tabs, ctx_row, False,
                                       l_ctx, s, (kt, vh))

        ya = _fourier_latent(ur, ui)
        yb = _na_latent(p, pc, bias_tabs)
        yd = _diff_latent(p, kt, vh, lp, sub_g, lam_init, min(s, DIFF_Q_TILE),
                          _key_tile(s + l_ctx, DIFF_K_TILE_MAX))
        n_tok = b * s if last else b * (s + l_ctx)
        h2_buf = jnp.zeros((2, n_tok, d), BF16)
        x, h2_buf = _merge(x, mods, None, norm1_g[layer], norm2_g[layer], ya, yb, yd, p,
                           conv_w[layer], wg_bf, wb_bf, wo_bf, tm, h2_buf, 0)
        if not last:
            yac, ybc, ydc = _ctx_branches(pc, ucr, uci, lp, sub_g, lam_init)
            xc, h2_buf = _merge(xc, mods, ctx_row, norm1_g[layer], norm2_g[layer], yac, ybc, ydc, pc,
                                conv_w[layer], wg_bf, wb_bf, wo_bf, l_ctx, h2_buf, b * s)
        h2_all = h2_buf.reshape(2 * n_tok, d)

        later = (x, xc) + (() if last else (cvec, na_rel_bias[layer + 1]))
        y_tok, w_tok, later = _moe_routed(h2_all, wr_bf, router_bias[layer], layer, expert_w_gate,
                                          expert_w_up, expert_w_down, later)
        sg_bf = shared_w_gate[layer].astype(BF16)
        su_bf = shared_w_up[layer].astype(BF16)
        sd_bf = shared_w_down[layer].astype(BF16)
        x = _shared(later[0].reshape(b * s, d), h2_all, 0, mods, None, s, sg_bf, su_bf, sd_bf, tm)
        x = _combine(x, y_tok, w_tok, 0, mods, None, s, final_norm_g, last, tm).reshape(b, s, d)
        if not last:
            xc = _shared(later[1].reshape(b * l_ctx, d), h2_all, b * s, mods, ctx_row, l_ctx,
                         sg_bf, su_bf, sd_bf, l_ctx)
            xc = _combine(xc, y_tok, w_tok, b * s, mods, ctx_row, l_ctx, final_norm_g, False,
                          l_ctx).reshape(b, l_ctx, d)
            mods = _ada(later[2], ada_w, ada_b, layer + 1).reshape(8, N_MOD, d)
            bias_tabs = _na_bias_tables(later[3], rows)
    return x
```
